```python
import math
import jax
import jax.numpy as jnp
from jax import lax
import numpy as np

D_MODEL = 1024
BATCH = 8
SEQ = 4096
DEPTH = 2

RWKV_HEADS = 8
RWKV_HEAD_DIM = 64
RWKV_DIM = RWKV_HEADS * RWKV_HEAD_DIM
RWKV_W_LORA = 64
RWKV_A_LORA = 64
RWKV_G_LORA = 128
RWKV_GN_EPS = 64e-5
MLA_HEADS = 8
MLA_Q_LORA = 256
MLA_KV_LORA = 128
MLA_NOPE_DIM = 64
MLA_ROPE_DIM = 32
MLA_V_DIM = 64
MLA_DIM = MLA_HEADS * MLA_V_DIM
ROPE_BASE = 10000.0
DIFF_HEADS = 4
DIFF_QK_DIM = 64
DIFF_V_DIM = 2 * DIFF_QK_DIM
DIFF_DIM = DIFF_HEADS * DIFF_V_DIM
REL_BUCKETS = 32
REL_MAX_DISTANCE = 128
D_FF = 2816
CONV_WIDTH = 3
N_BRANCHES = 3
Q_BLOCK = 128
NORM_EPS = 1e-6

RWKV_COLS = 3 * RWKV_DIM + RWKV_W_LORA + RWKV_A_LORA + RWKV_G_LORA
MLA_COLS = MLA_Q_LORA + MLA_KV_LORA + MLA_ROPE_DIM
DIFF_COLS = 2 * (DIFF_HEADS * 2 * DIFF_QK_DIM) + DIFF_DIM
GATE_COLS = N_BRANCHES * D_MODEL
IN_COLS = RWKV_COLS + MLA_COLS + DIFF_COLS + GATE_COLS
IN_SPLITS = (RWKV_COLS, RWKV_COLS + MLA_COLS, RWKV_COLS + MLA_COLS + DIFF_COLS)
RWKV_SPLITS = (RWKV_DIM, 2 * RWKV_DIM, 3 * RWKV_DIM, 3 * RWKV_DIM + RWKV_W_LORA,
               3 * RWKV_DIM + RWKV_W_LORA + RWKV_A_LORA)

kernel_name = 'hybrid_rwkv7_mla_diffattn_convffn'


def _rmsnorm(x, g, eps=NORM_EPS):
    xf = x.astype(jnp.float32)
    y = xf * lax.rsqrt(jnp.mean(xf * xf, axis=-1, keepdims=True) + eps)
    return (y * g.astype(jnp.float32)).astype(x.dtype)


def _token_shift(p):
    return jnp.pad(p, ((0, 0), (1, 0), (0, 0)))[:, :-1]


def _split_heads(t, n_heads):
    return t.reshape(t.shape[0], t.shape[1], n_heads, t.shape[-1] // n_heads)


def _rope_tables(positions, dtype):
    inv_freq = ROPE_BASE ** (-jnp.arange(0, MLA_ROPE_DIM, 2, dtype=jnp.float32) / MLA_ROPE_DIM)
    ang = positions.astype(jnp.float32)[..., None] * inv_freq
    return jnp.cos(ang).astype(dtype), jnp.sin(ang).astype(dtype)


def _apply_rope(x, cos, sin):
    x1, x2 = jnp.split(x, 2, axis=-1)
    return jnp.concatenate([x1 * cos - x2 * sin, x1 * sin + x2 * cos], axis=-1)


def _t5_bucket(dist):
    n = jnp.maximum(dist, 0)
    max_exact = REL_BUCKETS // 2
    nf = jnp.maximum(n, 1).astype(jnp.float32)
    large = max_exact + (jnp.log(nf / max_exact) / math.log(REL_MAX_DISTANCE / max_exact)
                         * (REL_BUCKETS - max_exact)).astype(jnp.int32)
    large = jnp.minimum(large, REL_BUCKETS - 1)
    return jnp.where(n < max_exact, n, large)


def _masked_softmax(logits, mask):
    logits = jnp.where(mask, logits.astype(jnp.float32), jnp.finfo(jnp.float32).min)
    return jax.nn.softmax(logits, axis=-1)


def _causal_blocks(block_fn, seq_len):
    outs = []
    for i in range(seq_len // Q_BLOCK):
        q0, q1 = i * Q_BLOCK, (i + 1) * Q_BLOCK
        mask = jnp.arange(q0, q1)[:, None] >= jnp.arange(q1)[None, :]
        outs.append(block_fn(q0, q1, mask))
    return jnp.concatenate(outs, axis=2)


def _rwkv7_mix(p, mu, w0, w2, a0, a2, g2, k_k, k_a, r_k, ln_w, ln_b):
    B, S, _ = p.shape
    f32 = jnp.float32
    p = p + (_token_shift(p) - p) * mu
    r, k, v, pw, pa, pg = jnp.split(p, RWKV_SPLITS, axis=-1)
    w_log = -jax.nn.softplus(-(w0 + jnp.tanh(pw) @ w2)) - 0.5
    decay = jnp.exp(-jnp.exp(w_log.astype(f32)))
    a = jax.nn.sigmoid(a0 + pa @ a2)
    g = jax.nn.sigmoid(pg) @ g2
    kk = _split_heads(k * k_k, RWKV_HEADS).astype(f32)
    kk = kk / jnp.maximum(jnp.linalg.norm(kk, axis=-1, keepdims=True), 1e-12)
    k = k * (1 + (a - 1) * k_a)
    r_h, k_h, v_h = _split_heads(r, RWKV_HEADS), _split_heads(k, RWKV_HEADS), _split_heads(v, RWKV_HEADS)
    b_h = kk * _split_heads(a, RWKV_HEADS).astype(f32)
    xs = tuple(jnp.moveaxis(t.astype(f32), 1, 0)
               for t in (r_h, _split_heads(decay, RWKV_HEADS), k_h, v_h, kk, b_h))

    def step(state, inp):
        r_t, w_t, k_t, v_t, kk_t, b_t = inp
        sa = jnp.einsum('bhij,bhj->bhi', state, -kk_t)
        state = (state * w_t[:, :, None, :] + sa[..., None] * b_t[:, :, None, :]
                 + v_t[..., None] * k_t[:, :, None, :])
        return state, jnp.einsum('bhij,bhj->bhi', state, r_t)

    state0 = jnp.zeros((B, RWKV_HEADS, RWKV_HEAD_DIM, RWKV_HEAD_DIM), f32)
    _, y = lax.scan(step, state0, xs)
    y = jnp.moveaxis(y, 0, 1)
    mean = jnp.mean(y, axis=-1, keepdims=True)
    var = jnp.mean(jnp.square(y - mean), axis=-1, keepdims=True)
    y = ((y - mean) * lax.rsqrt(var + RWKV_GN_EPS)).reshape(B, S, RWKV_DIM) * ln_w + ln_b
    bonus = (jnp.sum(r_h * k_h * r_k, axis=-1, keepdims=True) * v_h).reshape(B, S, RWKV_DIM)
    y = (y + bonus.astype(f32)) * g.astype(f32)
    return y.astype(p.dtype)


def _mla_mix(p, cos, sin, q_norm, w_uq, kv_norm, w_ukv):
    B, S, _ = p.shape
    c_q, c_kv, k_rope = jnp.split(p, (MLA_Q_LORA, MLA_Q_LORA + MLA_KV_LORA), axis=-1)
    q = (_rmsnorm(c_q, q_norm) @ w_uq).reshape(B, S, MLA_HEADS, MLA_NOPE_DIM + MLA_ROPE_DIM)
    q = q.transpose(0, 2, 1, 3)
    q_nope = q[..., :MLA_NOPE_DIM]
    q_rope = _apply_rope(q[..., MLA_NOPE_DIM:], cos[:, None], sin[:, None])
    kv = (_rmsnorm(c_kv, kv_norm) @ w_ukv).reshape(B, S, MLA_HEADS, MLA_NOPE_DIM + MLA_V_DIM)
    kv = kv.transpose(0, 2, 1, 3)
    k_nope, v = kv[..., :MLA_NOPE_DIM], kv[..., MLA_NOPE_DIM:]
    k_rope = _apply_rope(k_rope, cos, sin)
    scale = (MLA_NOPE_DIM + MLA_ROPE_DIM) ** -0.5

    def block(q0, q1, mask):
        logits = (jnp.einsum('bhqd,bhkd->bhqk', q_nope[:, :, q0:q1], k_nope[:, :, :q1])
                  + jnp.einsum('bhqd,bkd->bhqk', q_rope[:, :, q0:q1], k_rope[:, :q1]))
        probs = _masked_softmax(logits * scale, mask)
        return jnp.einsum('bhqk,bhkd->bhqd', probs.astype(v.dtype), v[:, :, :q1])

    o = _causal_blocks(block, S)
    return o.transpose(0, 2, 1, 3).reshape(B, S, MLA_DIM)


def _diff_mix(p, positions, rel_bias, lam, subln, layer_idx):
    B, S, _ = p.shape
    qk_w = DIFF_HEADS * 2 * DIFF_QK_DIM
    q, k, v = jnp.split(p, (qk_w, 2 * qk_w), axis=-1)
    q = q.reshape(B, S, DIFF_HEADS, 2, DIFF_QK_DIM).transpose(0, 2, 3, 1, 4)
    k = k.reshape(B, S, DIFF_HEADS, 2, DIFF_QK_DIM).transpose(0, 2, 3, 1, 4)
    v = v.reshape(B, S, DIFF_HEADS, DIFF_V_DIM).transpose(0, 2, 1, 3)
    lambda_init = 0.8 - 0.6 * math.exp(-0.3 * layer_idx)
    lam = lam.astype(jnp.float32)
    lam_full = jnp.exp(jnp.sum(lam[0] * lam[1])) - jnp.exp(jnp.sum(lam[2] * lam[3])) + lambda_init
    scale = DIFF_QK_DIM ** -0.5

    def block(q0, q1, mask):
        dist = positions[:, q0:q1, None] - positions[:, None, :q1]
        bias = rel_bias[_t5_bucket(dist)].transpose(0, 3, 1, 2)[:, :, None]
        logits = jnp.einsum('bhmqd,bhmkd->bhmqk', q[:, :, :, q0:q1], k[:, :, :, :q1]) * scale + bias
        probs = _masked_softmax(logits, mask)
        attn = probs[:, :, 0] - lam_full * probs[:, :, 1]
        return jnp.einsum('bhqk,bhkd->bhqd', attn.astype(v.dtype), v[:, :, :q1])

    o = _causal_blocks(block, S)
    o = _rmsnorm(o, subln, eps=1e-5) * (1 - lambda_init)
    return o.transpose(0, 2, 1, 3).reshape(B, S, DIFF_DIM)


def _conv_ffn(h, w_up, conv_w, conv_b, w_down):
    S = h.shape[1]
    u = h @ w_up
    up = jnp.pad(u, ((0, 0), (CONV_WIDTH - 1, 0), (0, 0)))
    u = sum(conv_w[j] * up[:, j:j + S] for j in range(CONV_WIDTH)) + conv_b
    gate, val = jnp.split(u, 2, axis=-1)
    return (jax.nn.silu(gate) * val) @ w_down


def setup_inputs(seed: int = 0) -> dict:
    key = jax.random.key(seed)
    ks = iter(jax.random.split(key, 40))
    f32 = jnp.float32
    L = DEPTH

    def nrm(shape, scale):
        return jax.random.normal(next(ks), shape, f32) * scale

    def gain(shape):
        return 1.0 + nrm(shape, 0.05)

    offset = jax.random.randint(next(ks), (BATCH, 1), 0, 1024, dtype=jnp.int32)
    positions = offset + jnp.arange(SEQ, dtype=jnp.int32)[None, :]
    return {
        'x': nrm((BATCH, SEQ, D_MODEL), 1.0),
        'positions': positions,
        'rel_bias': nrm((REL_BUCKETS, DIFF_HEADS), 0.5),
        'norm_mix': gain((L, D_MODEL)),
        'w_in': nrm((L, D_MODEL, IN_COLS), D_MODEL ** -0.5),
        'b_gate': nrm((L, GATE_COLS), 0.1),
        'rwkv_mu': jax.random.uniform(next(ks), (L, RWKV_COLS), f32),
        'rwkv_w0': jax.random.uniform(next(ks), (L, RWKV_DIM), f32, minval=-6.0, maxval=-1.0),
        'rwkv_w2': nrm((L, RWKV_W_LORA, RWKV_DIM), 0.1 * RWKV_W_LORA ** -0.5),
        'rwkv_a0': nrm((L, RWKV_DIM), 0.1),
        'rwkv_a2': nrm((L, RWKV_A_LORA, RWKV_DIM), RWKV_A_LORA ** -0.5),
        'rwkv_g2': nrm((L, RWKV_G_LORA, RWKV_DIM), RWKV_G_LORA ** -0.5),
        'rwkv_k_k': 0.85 + nrm((L, RWKV_DIM), 0.05),
        'rwkv_k_a': gain((L, RWKV_DIM)),
        'rwkv_r_k': nrm((L, RWKV_HEADS, RWKV_HEAD_DIM), 0.1),
        'rwkv_ln_w': gain((L, RWKV_DIM)),
        'rwkv_ln_b': nrm((L, RWKV_DIM), 0.02),
        'mla_q_norm': gain((L, MLA_Q_LORA)),
        'mla_w_uq': nrm((L, MLA_Q_LORA, MLA_HEADS * (MLA_NOPE_DIM + MLA_ROPE_DIM)), MLA_Q_LORA ** -0.5),
        'mla_kv_norm': gain((L, MLA_KV_LORA)),
        'mla_w_ukv': nrm((L, MLA_KV_LORA, MLA_HEADS * (MLA_NOPE_DIM + MLA_V_DIM)), MLA_KV_LORA ** -0.5),
        'diff_lambda': nrm((L, 4, DIFF_QK_DIM), 0.1),
        'diff_subln': gain((L, DIFF_V_DIM)),
        'w_branch_rwkv': nrm((L, RWKV_DIM, D_MODEL), RWKV_DIM ** -0.5),
        'w_branch_mla': nrm((L, MLA_DIM, D_MODEL), MLA_DIM ** -0.5),
        'w_branch_diff': nrm((L, DIFF_DIM, D_MODEL), DIFF_DIM ** -0.5),
        'w_o': nrm((L, D_MODEL, D_MODEL), D_MODEL ** -0.5),
        'norm_ffn': gain((L, D_MODEL)),
        'ffn_w_up': nrm((L, D_MODEL, 2 * D_FF), D_MODEL ** -0.5),
        'ffn_conv_w': nrm((L, CONV_WIDTH, 2 * D_FF), CONV_WIDTH ** -0.5),
        'ffn_conv_b': nrm((L, 2 * D_FF), 0.02),
        'ffn_w_down': nrm((L, D_FF, D_MODEL), D_FF ** -0.5),
        'norm_final': gain((D_MODEL,)),
    }


def reference(x, positions, rel_bias, norm_mix, w_in, b_gate, rwkv_mu, rwkv_w0, rwkv_w2, rwkv_a0,
              rwkv_a2, rwkv_g2, rwkv_k_k, rwkv_k_a, rwkv_r_k, rwkv_ln_w, rwkv_ln_b, mla_q_norm,
              mla_w_uq, mla_kv_norm, mla_w_ukv, diff_lambda, diff_subln, w_branch_rwkv, w_branch_mla,
              w_branch_diff, w_o, norm_ffn, ffn_w_up, ffn_conv_w, ffn_conv_b, ffn_w_down, norm_final):
    B, S, _ = x.shape
    cos, sin = _rope_tables(positions, x.dtype)
    for l in range(DEPTH):
        h = _rmsnorm(x, norm_mix[l])
        p = h @ w_in[l]
        p_rwkv, p_mla, p_diff, p_gate = jnp.split(p, IN_SPLITS, axis=-1)
        o_rwkv = _rwkv7_mix(p_rwkv, rwkv_mu[l], rwkv_w0[l], rwkv_w2[l], rwkv_a0[l], rwkv_a2[l],
                            rwkv_g2[l], rwkv_k_k[l], rwkv_k_a[l], rwkv_r_k[l], rwkv_ln_w[l], rwkv_ln_b[l])
        o_mla = _mla_mix(p_mla, cos, sin, mla_q_norm[l], mla_w_uq[l], mla_kv_norm[l], mla_w_ukv[l])
        o_diff = _diff_mix(p_diff, positions, rel_bias, diff_lambda[l], diff_subln[l], l)
        gates = jax.nn.sigmoid(p_gate + b_gate[l]).reshape(B, S, N_BRANCHES, D_MODEL)
        merged = (gates[:, :, 0] * (o_rwkv @ w_branch_rwkv[l])
                  + gates[:, :, 1] * (o_mla @ w_branch_mla[l])
                  + gates[:, :, 2] * (o_diff @ w_branch_diff[l]))
        x = x + merged @ w_o[l]
        x = x + _conv_ffn(_rmsnorm(x, norm_ffn[l]), ffn_w_up[l], ffn_conv_w[l], ffn_conv_b[l], ffn_w_down[l])
    return _rmsnorm(x, norm_final)
```

```python
import functools
import math

import numpy as np
import jax
import jax.numpy as jnp
from jax import lax
from jax.experimental import pallas as pl
from jax.experimental.pallas import tpu as pltpu

F32 = jnp.float32
BF16 = jnp.bfloat16

D_MODEL = 1024
RWKV_HEADS = 8
RWKV_N = 64
RWKV_DIM = 512
RWKV_COLS = 1792
RWKV_GN_EPS = 64e-5
MLA_HEADS = 8
MLA_Q_LORA = 256
MLA_KV_LORA = 128
MLA_NOPE = 64
MLA_ROPE = 32
MLA_V = 64
MLA_COLS = 416
MLA_PAD = 512
ROPE_BASE = 10000.0
DIFF_HEADS = 4
DIFF_QK = 64
DIFF_V = 128
DIFF_COLS = 1536
REL_BUCKETS = 32
REL_MAX_DISTANCE = 128
D_FF = 2816
GATE_COLS = 3072
NORM_EPS = 1e-6
SUBLN_EPS = 1e-5

LANES = 128
VMEM_LIMIT = 56 * 1024 * 1024
TM = 256
CHUNK = 64
TQ = 256
FF_CHUNK = 256
NEG = -1e30

_NT = (((1,), (1,)), ((), ()))
_TN = (((0,), (0,)), ((), ()))


def _dot(a, b):
    return jnp.dot(a, b, preferred_element_type=F32)


def _dot_nt(a, b):
    return lax.dot_general(a, b, _NT, preferred_element_type=F32)


def _dot_tn(a, b):
    return lax.dot_general(a, b, _TN, preferred_element_type=F32)


def _rms(x, g, eps):
    return x * lax.rsqrt(jnp.mean(x * x, axis=-1, keepdims=True) + eps) * g


def _params():
    return pltpu.CompilerParams(dimension_semantics=("arbitrary",), vmem_limit_bytes=VMEM_LIMIT)


def _params2():
    return pltpu.CompilerParams(dimension_semantics=("arbitrary", "arbitrary"), vmem_limit_bytes=VMEM_LIMIT)


def _const_spec(shape):
    return pl.BlockSpec(shape, lambda *_: (0,) * len(shape), pipeline_mode=pl.Buffered(1))


def _row_spec(tm, cols, col_block=0):
    return pl.BlockSpec((tm, cols), lambda i: (i, col_block))


def _in_proj_kernel(x_ref, g_ref, w_ref, pr_ref, pm_ref, pd_ref, pg_ref):
    h = _rms(x_ref[...], g_ref[...], NORM_EPS).astype(BF16)
    c0 = 0
    pr_ref[...] = _dot(h, w_ref[:, c0:c0 + RWKV_COLS])
    c0 += RWKV_COLS
    pm_ref[...] = _dot(h, w_ref[:, c0:c0 + MLA_PAD])
    c0 += MLA_PAD
    pd_ref[...] = _dot(h, w_ref[:, c0:c0 + DIFF_COLS]).astype(BF16)
    c0 += DIFF_COLS
    pg_ref[...] = _dot(h, w_ref[:, c0:c0 + GATE_COLS])


def _in_proj(x, g, w_all):
    t = x.shape[0]
    n_all = w_all.shape[1]
    return pl.pallas_call(
        _in_proj_kernel,
        grid=(t // TM,),
        in_specs=[_row_spec(TM, D_MODEL), _const_spec((1, D_MODEL)), _const_spec((D_MODEL, n_all))],
        out_specs=[_row_spec(TM, RWKV_COLS), _row_spec(TM, MLA_PAD), _row_spec(TM, DIFF_COLS), _row_spec(TM, GATE_COLS)],
        out_shape=[jax.ShapeDtypeStruct((t, RWKV_COLS), F32), jax.ShapeDtypeStruct((t, MLA_PAD), F32),
                   jax.ShapeDtypeStruct((t, DIFF_COLS), BF16), jax.ShapeDtypeStruct((t, GATE_COLS), F32)],
        compiler_params=_params(),
        name="in_proj",
    )(x, g, w_all)


def _split3(a):
    hi = a.astype(BF16)
    r1 = a - hi.astype(F32)
    mid = r1.astype(BF16)
    lo = (r1 - mid.astype(F32)).astype(BF16)
    return hi, mid, lo


def _softplus(z):
    return jnp.maximum(z, 0.0) + jnp.log(1.0 + jnp.exp(-jnp.abs(z)))


def _rwkv_kernel(p_ref, mu_ref, w0_ref, w2_ref, a0_ref, a2_ref, g2_ref, kk_ref, ka_ref, rk_ref, lnw_ref, lnb_ref,
                 o_ref, carry_ref, state_ref, o_scr, *, chunks_per_seq):
    c = CHUNK
    n = RWKV_N

    @pl.when(pl.program_id(0) % chunks_per_seq == 0)
    def _():
        carry_ref[...] = jnp.zeros_like(carry_ref)
        state_ref[...] = jnp.zeros_like(state_ref)

    p = p_ref[...]
    row = lax.broadcasted_iota(jnp.int32, (c, 1), 0)
    shifted = jnp.where(row == 0, carry_ref[...], pltpu.roll(p, 1, 0))
    carry_ref[...] = p[c - 1:c, :]
    pm = p + (shifted - p) * mu_ref[...]

    r = pm[:, 0:512]
    k = pm[:, 512:1024]
    v = pm[:, 1024:1536]
    pw = pm[:, 1536:1600]
    pa = pm[:, 1600:1664]
    pg = pm[:, 1664:1792]

    w_log = -_softplus(-(w0_ref[...] + _dot(jnp.tanh(pw).astype(BF16), w2_ref[...]))) - 0.5
    logd = -jnp.exp(w_log)
    a = jax.nn.sigmoid(a0_ref[...] + _dot(pa.astype(BF16), a2_ref[...]))
    g = _dot(jax.nn.sigmoid(pg).astype(BF16), g2_ref[...])

    tri = (lax.broadcasted_iota(jnp.int32, (c, c), 0) >= lax.broadcasted_iota(jnp.int32, (c, c), 1)).astype(BF16)
    hi, mid, lo = _split3(logd)
    cs = _dot(tri, hi) + _dot(tri, mid) + _dot(tri, lo)
    total = cs[c - 1:c, :]
    e_in = jnp.exp(cs)
    e_ex = jnp.exp(cs - logd)
    e_inv = jnp.exp(-cs)
    e_end = jnp.exp(total - cs)
    g_end = jnp.exp(total)

    k2 = k * (1.0 + (a - 1.0) * ka_ref[...])
    kku = k * kk_ref[...]
    rt = r * e_in
    rkk = r * k2 * rk_ref[...]

    ri = lax.broadcasted_iota(jnp.int32, (c, 2 * c), 0)
    ci = lax.broadcasted_iota(jnp.int32, (c, 2 * c), 1)
    cm = jnp.where(ci >= c, ci - c, ci)
    strict = ri > cm
    incl = ri >= cm
    right = ci >= c
    eye = (lax.broadcasted_iota(jnp.int32, (c, c), 0) == lax.broadcasted_iota(jnp.int32, (c, c), 1)).astype(F32)

    for h in range(RWKV_HEADS):
        hs = slice(h * n, (h + 1) * n)
        kk_h = kku[:, hs]
        kk_h = kk_h / jnp.maximum(jnp.sqrt(jnp.sum(kk_h * kk_h, axis=-1, keepdims=True)), 1e-12)
        b_h = kk_h * a[:, hs]
        k2_h = k2[:, hs]
        v_h = v[:, hs]
        at = (-kk_h * e_ex[:, hs]).astype(BF16)
        rt_h = rt[:, hs].astype(BF16)
        bt = b_h * e_inv[:, hs]
        kt = k2_h * e_inv[:, hs]
        lhs = jnp.concatenate([at, rt_h], axis=0)
        rhs = jnp.concatenate([bt, kt], axis=0).astype(BF16)
        big = _dot_nt(lhs, rhs)
        top = jnp.where(strict, big[:c, :], 0.0)
        bot = jnp.where(incl, big[c:, :], 0.0).astype(BF16)

        x = top[:, :c]
        tinv = eye + x
        xb = x.astype(BF16)
        steps = int(math.log2(c)) - 1
        for s in range(steps):
            x = _dot(xb, xb)
            xb = x.astype(BF16)
            tinv = tinv + _dot(tinv.astype(BF16), xb)
        tinv_b = tinv.astype(BF16)

        v_b = v_h.astype(BF16)
        v_stack = jnp.concatenate([v_b, v_b], axis=0)
        lak_v = _dot(jnp.where(right, top, 0.0).astype(BF16), v_stack)
        a_bar = _dot(tinv_b, at)
        v_bar = _dot(tinv_b, lak_v.astype(BF16))

        st = state_ref[h]
        st_b = st.astype(BF16)
        proj = _dot_nt(jnp.concatenate([a_bar.astype(BF16), rt_h], axis=0), st_b)
        u = proj[:c, :] + v_bar
        uv = jnp.concatenate([u.astype(BF16), v_b], axis=0)
        y = proj[c:, :] + _dot(bot, uv)
        ends = jnp.concatenate([b_h * e_end[:, hs], k2_h * e_end[:, hs]], axis=0).astype(BF16)
        state_ref[h] = st * g_end[:, hs] + _dot_tn(uv, ends)

        mean = jnp.mean(y, axis=-1, keepdims=True)
        yc = y - mean
        var = jnp.mean(yc * yc, axis=-1, keepdims=True)
        yn = yc * lax.rsqrt(var + RWKV_GN_EPS) * lnw_ref[:, hs] + lnb_ref[:, hs]
        bonus = jnp.sum(rkk[:, hs], axis=-1, keepdims=True) * v_h
        o_scr[:, hs] = (yn + bonus) * g[:, hs]

    o_ref[...] = o_scr[...].astype(BF16)


def _rwkv(p_rwkv, seq, mu, w0, w2, a0, a2, g2, k_k, k_a, r_k, ln_w, ln_b):
    t = p_rwkv.shape[0]
    kern = functools.partial(_rwkv_kernel, chunks_per_seq=seq // CHUNK)
    vec = lambda: _const_spec((1, RWKV_DIM))
    return pl.pallas_call(
        kern,
        grid=(t // CHUNK,),
        in_specs=[_row_spec(CHUNK, RWKV_COLS), _const_spec((1, RWKV_COLS)), vec(), _const_spec((64, RWKV_DIM)), vec(),
                  _const_spec((64, RWKV_DIM)), _const_spec((128, RWKV_DIM)), vec(), vec(), vec(), vec(), vec()],
        out_specs=_row_spec(CHUNK, RWKV_DIM),
        out_shape=jax.ShapeDtypeStruct((t, RWKV_DIM), BF16),
        scratch_shapes=[pltpu.VMEM((1, RWKV_COLS), F32), pltpu.VMEM((RWKV_HEADS, RWKV_N, RWKV_N), F32),
                        pltpu.VMEM((CHUNK, RWKV_DIM), F32)],
        compiler_params=_params(),
        name="rwkv7",
    )(p_rwkv, mu, w0, w2, a0, a2, g2, k_k, k_a, r_k, ln_w, ln_b)


def _mla_prep_kernel(p_ref, pos_ref, freq_ref, qn_ref, kvn_ref, wq_ref, wqr_ref, wk_ref, wv_ref, q_out, k_out, v_out):
    p = p_ref[...]
    hq = _rms(p[:, 0:MLA_Q_LORA], qn_ref[...], NORM_EPS).astype(BF16)
    hkv = _rms(p[:, MLA_Q_LORA:MLA_Q_LORA + MLA_KV_LORA], kvn_ref[...], NORM_EPS).astype(BF16)
    blk = p[:, 384:512]
    ang = pos_ref[...].astype(F32) * freq_ref[...]
    cos = jnp.cos(ang)
    sin = jnp.sin(ang)
    scale = (MLA_NOPE + MLA_ROPE) ** -0.5
    qa = _dot(hq, wq_ref[...])
    qr = _dot(hq, wqr_ref[...])
    lane = lax.broadcasted_iota(jnp.int32, blk.shape, 1)
    kr = pltpu.roll(blk, 64, 1)
    rot = jnp.where(lane < 80, -pltpu.roll(blk, 48, 1), pltpu.roll(blk, 80, 1))
    rot = jnp.where((lane >= 64) & (lane < 96), rot, 0.0)
    krope = kr * cos + rot * sin
    kn = _dot(hkv, wk_ref[...])
    for h in range(MLA_HEADS):
        hs = slice(h * LANES, (h + 1) * LANES)
        q_out[:, hs] = ((qa[:, hs] * cos + qr[:, hs] * sin) * scale).astype(BF16)
        k_out[:, hs] = (kn[:, hs] + krope).astype(BF16)
    v_out[...] = _dot(hkv, wv_ref[...]).astype(BF16)


def _mla_prep(p_mla, pos_col, freq, q_norm, kv_norm, wq, wqr, wk, wv):
    t = p_mla.shape[0]
    hw = MLA_HEADS * LANES
    return pl.pallas_call(
        _mla_prep_kernel,
        grid=(t // TM,),
        in_specs=[_row_spec(TM, MLA_PAD), _row_spec(TM, 1), _const_spec((1, LANES)), _const_spec((1, MLA_Q_LORA)),
                  _const_spec((1, MLA_KV_LORA)), _const_spec((MLA_Q_LORA, hw)), _const_spec((MLA_Q_LORA, hw)),
                  _const_spec((MLA_KV_LORA, hw)), _const_spec((MLA_KV_LORA, MLA_HEADS * MLA_V))],
        out_specs=[_row_spec(TM, hw), _row_spec(TM, hw), _row_spec(TM, MLA_HEADS * MLA_V)],
        out_shape=[jax.ShapeDtypeStruct((t, hw), BF16), jax.ShapeDtypeStruct((t, hw), BF16),
                   jax.ShapeDtypeStruct((t, MLA_HEADS * MLA_V), BF16)],
        compiler_params=_params(),
        name="mla_prep",
    )(p_mla, pos_col, freq, q_norm, kv_norm, wq, wqr, wk, wv)


def _softmax_step(s, vb, m, l, acc):
    m_new = jnp.maximum(m, jnp.max(s, axis=-1, keepdims=True))
    p = jnp.exp(s - m_new)
    alpha = jnp.exp(m - m_new)
    l = alpha * l + jnp.sum(p, axis=-1, keepdims=True)
    acc = alpha * acc + _dot(p.astype(BF16), vb)
    return m_new, l, acc


def _mla_attn_kernel(q_ref, k_ref, v_ref, o_ref):
    tq = TQ
    i = pl.program_id(1)
    causal = lax.broadcasted_iota(jnp.int32, (tq, tq), 0) >= lax.broadcasted_iota(jnp.int32, (tq, tq), 1)
    lane = lax.broadcasted_iota(jnp.int32, (tq, LANES), 1)
    init = (jnp.full((tq, 1), NEG, F32), jnp.zeros((tq, 1), F32), jnp.zeros((tq, LANES), F32))
    diag = pl.multiple_of(i * tq, tq)
    for pr in range(MLA_HEADS // 2):
        vs = slice(pr * LANES, (pr + 1) * LANES)
        outs = []
        for hh in range(2):
            hs = slice((2 * pr + hh) * LANES, (2 * pr + hh + 1) * LANES)
            q = q_ref[:, hs]

            def body(j, carry, q=q, hs=hs, vs=vs):
                off = pl.multiple_of(j * tq, tq)
                s = _dot_nt(q, k_ref[pl.ds(off, tq), hs])
                return _softmax_step(s, v_ref[pl.ds(off, tq), vs], *carry)

            carry = lax.fori_loop(0, i, body, init)
            s = jnp.where(causal, _dot_nt(q, k_ref[pl.ds(diag, tq), hs]), NEG)
            _, l, acc = _softmax_step(s, v_ref[pl.ds(diag, tq), vs], *carry)
            outs.append(acc / l)
        o_ref[:, vs] = jnp.where(lane < MLA_V, outs[0], outs[1]).astype(BF16)


def _mla_attn(q, k, v, batch, seq):
    t = q.shape[0]
    nq = seq // TQ
    hw = MLA_HEADS * LANES
    vw = MLA_HEADS * MLA_V
    return pl.pallas_call(
        _mla_attn_kernel,
        grid=(batch, nq),
        in_specs=[pl.BlockSpec((TQ, hw), lambda b, i: (b * nq + i, 0)),
                  pl.BlockSpec((seq, hw), lambda b, i: (b, 0)),
                  pl.BlockSpec((seq, vw), lambda b, i: (b, 0))],
        out_specs=pl.BlockSpec((TQ, vw), lambda b, i: (b * nq + i, 0)),
        out_shape=jax.ShapeDtypeStruct((t, vw), BF16),
        compiler_params=_params2(),
        name="mla_attn",
    )(q, k, v)


def _t5_bucket_table():
    n = np.arange(0, REL_MAX_DISTANCE + 1)
    max_exact = REL_BUCKETS // 2
    nf = np.maximum(n, 1).astype(np.float32)
    ratio = np.log(nf / np.float32(max_exact)) / np.float32(math.log(REL_MAX_DISTANCE / max_exact))
    large = max_exact + (ratio * np.float32(REL_BUCKETS - max_exact)).astype(np.int32)
    large = np.minimum(large, REL_BUCKETS - 1)
    return np.where(n < max_exact, n, large)


_BUCKETS = _t5_bucket_table()
_FAR_BUCKET = int(_BUCKETS[REL_MAX_DISTANCE])
assert _FAR_BUCKET == REL_BUCKETS - 1 and np.all(np.diff(_BUCKETS) >= 0)
_BUCKET_STARTS = [int(np.argmax(_BUCKETS >= b)) for b in range(REL_BUCKETS // 2 + 1, REL_BUCKETS)]


def _diff_attn_kernel(rb_ref, lam_ref, subln_ref, q_ref, k_ref, v_ref, o_ref, bias_scr, *, lambda_init):
    tq = TQ
    i = pl.program_id(1)
    r_i = lax.broadcasted_iota(jnp.int32, (tq, tq), 0)
    c_i = lax.broadcasted_iota(jnp.int32, (tq, tq), 1)
    causal = r_i >= c_i

    @pl.when((pl.program_id(0) == 0) & (i == 0))
    def _():
        for t_idx, delta in enumerate((0, tq)):
            d = jnp.maximum(r_i - c_i + delta, 0)
            log_b = REL_BUCKETS // 2
            for start in _BUCKET_STARTS:
                log_b = log_b + (d >= start).astype(jnp.int32)
            bucket = jnp.where(d < REL_BUCKETS // 2, d, log_b)
            for h in range(DIFF_HEADS):
                bias = jnp.zeros((tq, tq), F32)
                for b in range(REL_BUCKETS):
                    bias = jnp.where(bucket == b, rb_ref[b, h], bias)
                bias_scr[t_idx, h] = bias - rb_ref[_FAR_BUCKET, h]

    lam = lam_ref[...]
    lam_full = (jnp.exp(jnp.sum(lam[0:1] * lam[1:2], axis=-1, keepdims=True))
                - jnp.exp(jnp.sum(lam[2:3] * lam[3:4], axis=-1, keepdims=True)) + lambda_init)
    lane = lax.broadcasted_iota(jnp.int32, (tq, LANES), 1)
    init1 = (jnp.full((tq, 1), NEG, F32), jnp.zeros((tq, 1), F32), jnp.zeros((tq, DIFF_V), F32))
    diag = pl.multiple_of(i * tq, tq)
    n_far = jnp.maximum(i - 1, 0)

    for h in range(DIFF_HEADS):
        hs = slice(h * LANES, (h + 1) * LANES)
        qp = q_ref[:, hs]
        q0 = jnp.where(lane < DIFF_QK, qp, jnp.zeros_like(qp))
        q1 = jnp.where(lane >= DIFF_QK, qp, jnp.zeros_like(qp))

        def step(off, carry, bias, mask, q0=q0, q1=q1, hs=hs):
            kb = k_ref[pl.ds(off, tq), hs]
            vb = v_ref[pl.ds(off, tq), hs]
            s0 = _dot_nt(q0, kb)
            s1 = _dot_nt(q1, kb)
            if bias is not None:
                s0 = s0 + bias
                s1 = s1 + bias
            if mask:
                s0 = jnp.where(causal, s0, NEG)
                s1 = jnp.where(causal, s1, NEG)
            c0 = _softmax_step(s0, vb, *carry[0])
            c1 = _softmax_step(s1, vb, *carry[1])
            return (c0, c1)

        def far_body(j, carry, step=step):
            return step(pl.multiple_of(j * tq, tq), carry, None, False)

        def near_body(j, carry, step=step, h=h):
            return step(pl.multiple_of(j * tq, tq), carry, bias_scr[1, h], False)

        carry = lax.fori_loop(0, n_far, far_body, (init1, init1))
        carry = lax.fori_loop(n_far, i, near_body, carry)
        (_, l0, a0), (_, l1, a1) = step(diag, carry, bias_scr[0, h], True)
        o = a0 / l0 - lam_full * (a1 / l1)
        o = _rms(o, subln_ref[...], SUBLN_EPS) * (1.0 - lambda_init)
        o_ref[:, hs] = o.astype(BF16)


def _diff_attn(p_diff, rel_bias, lam, subln, batch, seq, layer_idx):
    t = p_diff.shape[0]
    nq = seq // TQ
    w = DIFF_HEADS * LANES
    lambda_init = 0.8 - 0.6 * math.exp(-0.3 * layer_idx)
    kern = functools.partial(_diff_attn_kernel, lambda_init=lambda_init)
    const2 = lambda shape: pl.BlockSpec(shape, lambda b, i: (0, 0))
    return pl.pallas_call(
        kern,
        grid=(batch, nq),
        in_specs=[pl.BlockSpec(memory_space=pltpu.SMEM),
                  const2((4, DIFF_QK)), const2((1, DIFF_V)),
                  pl.BlockSpec((TQ, w), lambda b, i: (b * nq + i, 0)),
                  pl.BlockSpec((seq, w), lambda b, i: (b, 1)),
                  pl.BlockSpec((seq, w), lambda b, i: (b, 2))],
        out_specs=pl.BlockSpec((TQ, w), lambda b, i: (b * nq + i, 0)),
        out_shape=jax.ShapeDtypeStruct((t, w), BF16),
        scratch_shapes=[pltpu.VMEM((2, DIFF_HEADS, TQ, TQ), F32)],
        compiler_params=_params2(),
        name="diff_attn",
    )(rel_bias, lam, subln, p_diff, p_diff, p_diff)


def _merge_kernel(x_ref, pg_ref, bg_ref, or_ref, om_ref, od_ref, wr_ref, wm_ref, wd_ref, wo_ref, gf_ref, x_out, h_out):
    def gate(idx):
        cs = slice(idx * D_MODEL, (idx + 1) * D_MODEL)
        return jax.nn.sigmoid(pg_ref[:, cs] + bg_ref[:, cs])

    merged = gate(0) * _dot(or_ref[...], wr_ref[...])
    merged = merged + gate(1) * _dot(om_ref[...], wm_ref[...])
    merged = merged + gate(2) * _dot(od_ref[...], wd_ref[...])
    x1 = x_ref[...] + _dot(merged.astype(BF16), wo_ref[...])
    x_out[...] = x1
    h_out[...] = _rms(x1, gf_ref[...], NORM_EPS).astype(BF16)


def _merge(x, p_gate, b_gate, o_r, o_m, o_d, w_r, w_m, w_d, w_o, g_ffn):
    t = x.shape[0]
    return pl.pallas_call(
        _merge_kernel,
        grid=(t // TM,),
        in_specs=[_row_spec(TM, D_MODEL), _row_spec(TM, GATE_COLS), _const_spec((1, GATE_COLS)),
                  _row_spec(TM, 512), _row_spec(TM, 512), _row_spec(TM, 512),
                  _const_spec((512, D_MODEL)), _const_spec((512, D_MODEL)), _const_spec((512, D_MODEL)),
                  _const_spec((D_MODEL, D_MODEL)), _const_spec((1, D_MODEL))],
        out_specs=[_row_spec(TM, D_MODEL), _row_spec(TM, D_MODEL)],
        out_shape=[jax.ShapeDtypeStruct((t, D_MODEL), F32), jax.ShapeDtypeStruct((t, D_MODEL), BF16)],
        compiler_params=_params(),
        name="merge",
    )(x, p_gate, b_gate, o_r, o_m, o_d, w_r, w_m, w_d, w_o, g_ffn)


def _ffn_kernel(x_ref, h_ref, wup_ref, cw_ref, cb_ref, wdn_ref, gfin_ref, o_ref, carry_ref, *, tiles_per_seq, final_norm):
    tm = TM

    @pl.when(pl.program_id(0) % tiles_per_seq == 0)
    def _():
        carry_ref[...] = jnp.zeros_like(carry_ref)

    h = h_ref[...]
    row = lax.broadcasted_iota(jnp.int32, (tm, 1), 0)

    def conv(u, cols):
        prev = carry_ref[:, cols]
        u1 = jnp.where(row == 0, prev[7:8, :], pltpu.roll(u, 1, 0))
        u2 = jnp.where(row == 0, prev[6:7, :], jnp.where(row == 1, prev[7:8, :], pltpu.roll(u, 2, 0)))
        carry_ref[:, cols] = u[tm - 8:tm, :]
        return cw_ref[0:1, cols] * u2 + cw_ref[1:2, cols] * u1 + cw_ref[2:3, cols] * u + cb_ref[:, cols]

    acc = x_ref[...]
    for ck in range(D_FF // FF_CHUNK):
        gc = slice(ck * FF_CHUNK, (ck + 1) * FF_CHUNK)
        vc = slice(D_FF + ck * FF_CHUNK, D_FF + (ck + 1) * FF_CHUNK)
        gate = conv(_dot(h, wup_ref[:, gc]), gc)
        val = conv(_dot(h, wup_ref[:, vc]), vc)
        act = (gate * jax.nn.sigmoid(gate) * val).astype(BF16)
        acc = acc + _dot(act, wdn_ref[gc, :])
    if final_norm:
        acc = _rms(acc, gfin_ref[...], NORM_EPS)
    o_ref[...] = acc


def _ffn(x1, h2, w_up, conv_w, conv_b, w_down, g_final, seq, final_norm):
    t = x1.shape[0]
    kern = functools.partial(_ffn_kernel, tiles_per_seq=seq // TM, final_norm=final_norm)
    return pl.pallas_call(
        kern,
        grid=(t // TM,),
        in_specs=[_row_spec(TM, D_MODEL), _row_spec(TM, D_MODEL), _const_spec((D_MODEL, 2 * D_FF)),
                  _const_spec((3, 2 * D_FF)), _const_spec((1, 2 * D_FF)), _const_spec((D_FF, D_MODEL)),
                  _const_spec((1, D_MODEL))],
        out_specs=_row_spec(TM, D_MODEL),
        out_shape=jax.ShapeDtypeStruct((t, D_MODEL), F32),
        scratch_shapes=[pltpu.VMEM((8, 2 * D_FF), F32)],
        compiler_params=_params(),
        name="conv_ffn",
    )(x1, h2, w_up, conv_w, conv_b, w_down, g_final)


def _mla_weights(w_uq, w_ukv):
    qd = MLA_NOPE + MLA_ROPE
    half = MLA_ROPE // 2
    wq = w_uq.reshape(MLA_Q_LORA, MLA_HEADS, qd)
    zq = jnp.zeros((MLA_Q_LORA, MLA_HEADS, LANES - qd), F32)
    wq_main = jnp.concatenate([wq, zq], axis=-1)
    x1 = wq[:, :, MLA_NOPE:MLA_NOPE + half]
    x2 = wq[:, :, MLA_NOPE + half:]
    wq_rot = jnp.concatenate([jnp.zeros((MLA_Q_LORA, MLA_HEADS, MLA_NOPE), F32), -x2, x1, zq], axis=-1)
    wkv = w_ukv.reshape(MLA_KV_LORA, MLA_HEADS, MLA_NOPE + MLA_V)
    wk = jnp.concatenate([wkv[:, :, :MLA_NOPE], jnp.zeros((MLA_KV_LORA, MLA_HEADS, LANES - MLA_NOPE), F32)], axis=-1)
    wv = wkv[:, :, MLA_NOPE:]
    flat = lambda w: w.reshape(w.shape[0], -1).astype(BF16)
    return flat(wq_main), flat(wq_rot), flat(wk), flat(wv)


def kernel(x, positions, rel_bias, norm_mix, w_in, b_gate, rwkv_mu, rwkv_w0, rwkv_w2, rwkv_a0, rwkv_a2, rwkv_g2, rwkv_k_k, rwkv_k_a, rwkv_r_k, rwkv_ln_w, rwkv_ln_b, mla_q_norm, mla_w_uq, mla_kv_norm, mla_w_ukv, diff_lambda, diff_subln, w_branch_rwkv, w_branch_mla, w_branch_diff, w_o, norm_ffn, ffn_w_up, ffn_conv_w, ffn_conv_b, ffn_w_down, norm_final):
    batch, seq, _ = x.shape
    depth = w_in.shape[0]
    t = batch * seq
    assert seq % TQ == 0 and seq % TM == 0 and seq % CHUNK == 0
    xf = x.reshape(t, D_MODEL)
    pos_col = positions.reshape(t, 1)
    inv_freq = ROPE_BASE ** (-jnp.arange(0, MLA_ROPE, 2, dtype=F32) / MLA_ROPE)
    freq = jnp.concatenate([jnp.zeros((MLA_NOPE,), F32), inv_freq, inv_freq,
                            jnp.zeros((LANES - MLA_NOPE - MLA_ROPE,), F32)]).reshape(1, LANES)
    row = lambda v: v.reshape(1, -1)
    diff_scale = jnp.concatenate([jnp.full((512,), DIFF_QK ** -0.5, F32), jnp.ones((DIFF_COLS - 512,), F32)])

    for l in range(depth):
        s0, s1, s2 = RWKV_COLS, RWKV_COLS + MLA_COLS, RWKV_COLS + MLA_COLS + DIFF_COLS
        w = w_in[l]
        w_all = jnp.concatenate([w[:, :s0], w[:, s0:s1], jnp.zeros((D_MODEL, MLA_PAD - MLA_COLS), F32),
                                 w[:, s1:s2] * diff_scale, w[:, s2:]], axis=1).astype(BF16)
        p_rwkv, p_mla, p_diff, p_gate = _in_proj(xf, row(norm_mix[l]), w_all)

        o_r = _rwkv(p_rwkv, seq, row(rwkv_mu[l]), row(rwkv_w0[l]), rwkv_w2[l].astype(BF16), row(rwkv_a0[l]),
                    rwkv_a2[l].astype(BF16), rwkv_g2[l].astype(BF16), row(rwkv_k_k[l]), row(rwkv_k_a[l]),
                    row(rwkv_r_k[l]), row(rwkv_ln_w[l]), row(rwkv_ln_b[l]))

        wq, wqr, wk, wv = _mla_weights(mla_w_uq[l], mla_w_ukv[l])
        q_m, k_m, v_m = _mla_prep(p_mla, pos_col, freq, row(mla_q_norm[l]), row(mla_kv_norm[l]), wq, wqr, wk, wv)
        o_m = _mla_attn(q_m, k_m, v_m, batch, seq)

        o_d = _diff_attn(p_diff, rel_bias, diff_lambda[l], row(diff_subln[l]), batch, seq, l)

        x1, h2 = _merge(xf, p_gate, row(b_gate[l]), o_r, o_m, o_d, w_branch_rwkv[l].astype(BF16),
                        w_branch_mla[l].astype(BF16), w_branch_diff[l].astype(BF16), w_o[l].astype(BF16),
                        row(norm_ffn[l]))
        xf = _ffn(x1, h2, ffn_w_up[l].astype(BF16), ffn_conv_w[l], row(ffn_conv_b[l]), ffn_w_down[l].astype(BF16),
                  row(norm_final), seq, final_norm=(l == depth - 1))
    return xf.reshape(batch, seq, D_MODEL)
```

```python
import functools
import math

import numpy as np
import jax
import jax.numpy as jnp
from jax import lax
from jax.experimental import pallas as pl
from jax.experimental.pallas import tpu as pltpu

F32 = jnp.float32
BF16 = jnp.bfloat16

D_MODEL = 1024
RWKV_HEADS = 8
RWKV_N = 64
RWKV_DIM = 512
RWKV_COLS = 1792
RWKV_GN_EPS = 64e-5
MLA_HEADS = 8
MLA_Q_LORA = 256
MLA_KV_LORA = 128
MLA_NOPE = 64
MLA_ROPE = 32
MLA_V = 64
MLA_COLS = 416
MLA_PAD = 512
ROPE_BASE = 10000.0
DIFF_HEADS = 4
DIFF_QK = 64
DIFF_V = 128
DIFF_COLS = 1536
REL_BUCKETS = 32
REL_MAX_DISTANCE = 128
D_FF = 2816
GATE_COLS = 3072
NORM_EPS = 1e-6
SUBLN_EPS = 1e-5

LANES = 128
VMEM_LIMIT = 56 * 1024 * 1024
TM = 256
CHUNK = 64
RWKV_TILE = 128
TQ = 256
FF_CHUNK = 256
NEG = -1e30

_NT = (((1,), (1,)), ((), ()))
_TN = (((0,), (0,)), ((), ()))


def _dot(a, b):
    return jnp.dot(a, b, preferred_element_type=F32)


def _dot_nt(a, b):
    return lax.dot_general(a, b, _NT, preferred_element_type=F32)


def _dot_tn(a, b):
    return lax.dot_general(a, b, _TN, preferred_element_type=F32)


def _rms(x, g, eps):
    return x * lax.rsqrt(jnp.mean(x * x, axis=-1, keepdims=True) + eps) * g


def _params():
    return pltpu.CompilerParams(dimension_semantics=("arbitrary",), vmem_limit_bytes=VMEM_LIMIT)


def _params2():
    return pltpu.CompilerParams(dimension_semantics=("arbitrary", "arbitrary"), vmem_limit_bytes=VMEM_LIMIT)


def _const_spec(shape):
    return pl.BlockSpec(shape, lambda *_: (0,) * len(shape), pipeline_mode=pl.Buffered(1))


def _row_spec(tm, cols, col_block=0):
    return pl.BlockSpec((tm, cols), lambda i: (i, col_block))


def _in_proj_kernel(x_ref, g_ref, w_ref, pr_ref, pm_ref, pd_ref, pg_ref):
    h = _rms(x_ref[...], g_ref[...], NORM_EPS).astype(BF16)
    c0 = 0
    pr_ref[...] = _dot(h, w_ref[:, c0:c0 + RWKV_COLS])
    c0 += RWKV_COLS
    pm_ref[...] = _dot(h, w_ref[:, c0:c0 + MLA_PAD])
    c0 += MLA_PAD
    pd_ref[...] = _dot(h, w_ref[:, c0:c0 + DIFF_COLS]).astype(BF16)
    c0 += DIFF_COLS
    pg_ref[...] = _dot(h, w_ref[:, c0:c0 + GATE_COLS])


def _in_proj(x, g, w_all):
    t = x.shape[0]
    n_all = w_all.shape[1]
    return pl.pallas_call(
        _in_proj_kernel,
        grid=(t // TM,),
        in_specs=[_row_spec(TM, D_MODEL), _const_spec((1, D_MODEL)), _const_spec((D_MODEL, n_all))],
        out_specs=[_row_spec(TM, RWKV_COLS), _row_spec(TM, MLA_PAD), _row_spec(TM, DIFF_COLS), _row_spec(TM, GATE_COLS)],
        out_shape=[jax.ShapeDtypeStruct((t, RWKV_COLS), F32), jax.ShapeDtypeStruct((t, MLA_PAD), F32),
                   jax.ShapeDtypeStruct((t, DIFF_COLS), BF16), jax.ShapeDtypeStruct((t, GATE_COLS), F32)],
        compiler_params=_params(),
        name="in_proj",
    )(x, g, w_all)


def _split3(a):
    hi = a.astype(BF16)
    r1 = a - hi.astype(F32)
    mid = r1.astype(BF16)
    lo = (r1 - mid.astype(F32)).astype(BF16)
    return hi, mid, lo


def _softplus(z):
    return jnp.maximum(z, 0.0) + jnp.log(1.0 + jnp.exp(-jnp.abs(z)))


def _rwkv_kernel(p_ref, mu_ref, w0_ref, w2_ref, a0_ref, a2_ref, g2_ref, kk_ref, ka_ref, rk_ref, lnw_ref, lnb_ref,
                 o_ref, carry_ref, state_ref, o_scr, *, tiles_per_seq):
    c = CHUNK
    n = RWKV_N
    ts = RWKV_TILE
    nc = ts // c

    @pl.when(pl.program_id(0) % tiles_per_seq == 0)
    def _():
        carry_ref[...] = jnp.zeros_like(carry_ref)
        state_ref[...] = jnp.zeros_like(state_ref)

    p = p_ref[...]
    row = lax.broadcasted_iota(jnp.int32, (ts, 1), 0)
    shifted = jnp.where(row == 0, carry_ref[...], pltpu.roll(p, 1, 0))
    carry_ref[...] = p[ts - 1:ts, :]
    pm = p + (shifted - p) * mu_ref[...]

    r = pm[:, 0:512]
    k = pm[:, 512:1024]
    v = pm[:, 1024:1536]
    pw = pm[:, 1536:1600]
    pa = pm[:, 1600:1664]
    pg = pm[:, 1664:1792]

    w_log = -_softplus(-(w0_ref[...] + _dot(jnp.tanh(pw).astype(BF16), w2_ref[...]))) - 0.5
    logd = -jnp.exp(w_log)
    a = jax.nn.sigmoid(a0_ref[...] + _dot(pa.astype(BF16), a2_ref[...]))
    g = _dot(jax.nn.sigmoid(pg).astype(BF16), g2_ref[...])

    tr = lax.broadcasted_iota(jnp.int32, (ts, ts), 0)
    tc = lax.broadcasted_iota(jnp.int32, (ts, ts), 1)
    tri = ((tr >= tc) & (tr // c == tc // c)).astype(BF16)
    hi, mid, lo = _split3(logd)
    cs = _dot(tri, hi) + _dot(tri, mid) + _dot(tri, lo)
    total = jnp.concatenate([jnp.broadcast_to(cs[(m + 1) * c - 1:(m + 1) * c, :], (c, RWKV_DIM)) for m in range(nc)], axis=0)
    e_in = jnp.exp(cs)
    e_ex = jnp.exp(cs - logd)
    e_inv = jnp.exp(-cs)
    e_end = jnp.exp(total - cs)
    g_end = jnp.exp(total)

    k2 = k * (1.0 + (a - 1.0) * ka_ref[...])
    kku = k * kk_ref[...]
    rt = r * e_in
    rkk = r * k2 * rk_ref[...]

    ri = lax.broadcasted_iota(jnp.int32, (c, 2 * c), 0)
    ci = lax.broadcasted_iota(jnp.int32, (c, 2 * c), 1)
    cm = jnp.where(ci >= c, ci - c, ci)
    strict = ri > cm
    incl = ri >= cm
    right = ci >= c
    eye = (lax.broadcasted_iota(jnp.int32, (c, c), 0) == lax.broadcasted_iota(jnp.int32, (c, c), 1)).astype(F32)

    items = [(m, h) for m in range(nc) for h in range(RWKV_HEADS)]
    sl = lambda x, m, h: x[m * c:(m + 1) * c, h * n:(h + 1) * n]
    kk_l = []
    for m, h in items:
        kk_h = sl(kku, m, h)
        kk_l.append(kk_h / jnp.maximum(jnp.sqrt(jnp.sum(kk_h * kk_h, axis=-1, keepdims=True)), 1e-12))
    b_l = [kk * sl(a, m, h) for kk, (m, h) in zip(kk_l, items)]
    at_l = [(-kk * sl(e_ex, m, h)).astype(BF16) for kk, (m, h) in zip(kk_l, items)]
    rt_l = [sl(rt, m, h).astype(BF16) for m, h in items]
    vb_l = [sl(v, m, h).astype(BF16) for m, h in items]
    rhs_l = [jnp.concatenate([b * sl(e_inv, m, h), sl(k2, m, h) * sl(e_inv, m, h)], axis=0).astype(BF16)
             for b, (m, h) in zip(b_l, items)]
    ends_l = [jnp.concatenate([b * sl(e_end, m, h), sl(k2, m, h) * sl(e_end, m, h)], axis=0).astype(BF16)
              for b, (m, h) in zip(b_l, items)]
    big_l = [_dot_nt(jnp.concatenate([at, rtb], axis=0), rhs) for at, rtb, rhs in zip(at_l, rt_l, rhs_l)]
    top_l = [jnp.where(strict, big[:c, :], 0.0) for big in big_l]
    bot_l = [jnp.where(incl, big[c:, :], 0.0).astype(BF16) for big in big_l]
    lakv_l = [_dot(jnp.where(right, top, 0.0).astype(BF16), jnp.concatenate([vb, vb], axis=0))
              for top, vb in zip(top_l, vb_l)]

    x_l = [top[:, :c] for top in top_l]
    tinv_l = [eye + x for x in x_l]
    xb_l = [x.astype(BF16) for x in x_l]
    for _ in range(int(math.log2(c)) - 1):
        x_l = [_dot(xb, xb) for xb in xb_l]
        xb_l = [x.astype(BF16) for x in x_l]
        tinv_l = [tinv + _dot(tinv.astype(BF16), xb) for tinv, xb in zip(tinv_l, xb_l)]
    tinvb_l = [tinv.astype(BF16) for tinv in tinv_l]
    abar_l = [_dot(tb, at).astype(BF16) for tb, at in zip(tinvb_l, at_l)]
    vbar_l = [_dot(tb, lv.astype(BF16)).astype(BF16) for tb, lv in zip(tinvb_l, lakv_l)]
    rhat_l = [(rtb.astype(F32) + _dot(bot[:, :c], ab)).astype(BF16) for rtb, bot, ab in zip(rt_l, bot_l, abar_l)]
    y0_l = [_dot(bot, jnp.concatenate([vbar, vb], axis=0)) for bot, vbar, vb in zip(bot_l, vbar_l, vb_l)]
    p_l = [_dot_tn(ab, ends[:c, :]).astype(BF16) for ab, ends in zip(abar_l, ends_l)]
    q_l = [_dot_tn(jnp.concatenate([vbar, vb], axis=0), ends) for vbar, vb, ends in zip(vbar_l, vb_l, ends_l)]

    st_l = [state_ref[h] for h in range(RWKV_HEADS)]
    y_l = []
    for idx, (m, h) in enumerate(items):
        st = st_l[h]
        st_b = st.astype(BF16)
        y_l.append(_dot_nt(rhat_l[idx], st_b) + y0_l[idx])
        st_l[h] = st * g_end[(m + 1) * c - 1:(m + 1) * c, h * n:(h + 1) * n] + _dot(st_b, p_l[idx]) + q_l[idx]
    for h in range(RWKV_HEADS):
        state_ref[h] = st_l[h]

    for y, (m, h) in zip(y_l, items):
        hs = slice(h * n, (h + 1) * n)
        mean = jnp.mean(y, axis=-1, keepdims=True)
        yc = y - mean
        var = jnp.mean(yc * yc, axis=-1, keepdims=True)
        yn = yc * lax.rsqrt(var + RWKV_GN_EPS) * lnw_ref[:, hs] + lnb_ref[:, hs]
        bonus = jnp.sum(sl(rkk, m, h), axis=-1, keepdims=True) * sl(v, m, h)
        o_scr[m * c:(m + 1) * c, hs] = (yn + bonus) * sl(g, m, h)

    o_ref[...] = o_scr[...].astype(BF16)


def _rwkv(p_rwkv, seq, mu, w0, w2, a0, a2, g2, k_k, k_a, r_k, ln_w, ln_b):
    t = p_rwkv.shape[0]
    ts = RWKV_TILE
    kern = functools.partial(_rwkv_kernel, tiles_per_seq=seq // ts)
    vec = lambda: _const_spec((1, RWKV_DIM))
    return pl.pallas_call(
        kern,
        grid=(t // ts,),
        in_specs=[_row_spec(ts, RWKV_COLS), _const_spec((1, RWKV_COLS)), vec(), _const_spec((64, RWKV_DIM)), vec(),
                  _const_spec((64, RWKV_DIM)), _const_spec((128, RWKV_DIM)), vec(), vec(), vec(), vec(), vec()],
        out_specs=_row_spec(ts, RWKV_DIM),
        out_shape=jax.ShapeDtypeStruct((t, RWKV_DIM), BF16),
        scratch_shapes=[pltpu.VMEM((1, RWKV_COLS), F32), pltpu.VMEM((RWKV_HEADS, RWKV_N, RWKV_N), F32),
                        pltpu.VMEM((ts, RWKV_DIM), F32)],
        compiler_params=_params(),
        name="rwkv7",
    )(p_rwkv, mu, w0, w2, a0, a2, g2, k_k, k_a, r_k, ln_w, ln_b)


def _mla_prep_kernel(p_ref, pos_ref, freq_ref, qn_ref, kvn_ref, wq_ref, wqr_ref, wk_ref, wv_ref, q_out, k_out, v_out):
    p = p_ref[...]
    hq = _rms(p[:, 0:MLA_Q_LORA], qn_ref[...], NORM_EPS).astype(BF16)
    hkv = _rms(p[:, MLA_Q_LORA:MLA_Q_LORA + MLA_KV_LORA], kvn_ref[...], NORM_EPS).astype(BF16)
    blk = p[:, 384:512]
    ang = pos_ref[...].astype(F32) * freq_ref[...]
    cos = jnp.cos(ang)
    sin = jnp.sin(ang)
    scale = (MLA_NOPE + MLA_ROPE) ** -0.5
    qa = _dot(hq, wq_ref[...])
    qr = _dot(hq, wqr_ref[...])
    lane = lax.broadcasted_iota(jnp.int32, blk.shape, 1)
    kr = pltpu.roll(blk, 64, 1)
    rot = jnp.where(lane < 80, -pltpu.roll(blk, 48, 1), pltpu.roll(blk, 80, 1))
    rot = jnp.where((lane >= 64) & (lane < 96), rot, 0.0)
    krope = kr * cos + rot * sin
    kn = _dot(hkv, wk_ref[...])
    for h in range(MLA_HEADS):
        hs = slice(h * LANES, (h + 1) * LANES)
        q_out[:, hs] = ((qa[:, hs] * cos + qr[:, hs] * sin) * scale).astype(BF16)
        k_out[:, hs] = (kn[:, hs] + krope).astype(BF16)
    vv = _dot(hkv, wv_ref[...])
    ones_col = (lane == MLA_V).astype(F32)
    for h in range(MLA_HEADS):
        hs = slice(h * LANES, (h + 1) * LANES)
        v_out[:, hs] = (vv[:, hs] + ones_col).astype(BF16)


def _mla_prep(p_mla, pos_col, freq, q_norm, kv_norm, wq, wqr, wk, wv):
    t = p_mla.shape[0]
    hw = MLA_HEADS * LANES
    return pl.pallas_call(
        _mla_prep_kernel,
        grid=(t // TM,),
        in_specs=[_row_spec(TM, MLA_PAD), _row_spec(TM, 1), _const_spec((1, LANES)), _const_spec((1, MLA_Q_LORA)),
                  _const_spec((1, MLA_KV_LORA)), _const_spec((MLA_Q_LORA, hw)), _const_spec((MLA_Q_LORA, hw)),
                  _const_spec((MLA_KV_LORA, hw)), _const_spec((MLA_KV_LORA, hw))],
        out_specs=[_row_spec(TM, hw), _row_spec(TM, hw), _row_spec(TM, hw)],
        out_shape=[jax.ShapeDtypeStruct((t, hw), BF16), jax.ShapeDtypeStruct((t, hw), BF16),
                   jax.ShapeDtypeStruct((t, hw), BF16)],
        compiler_params=_params(),
        name="mla_prep",
    )(p_mla, pos_col, freq, q_norm, kv_norm, wq, wqr, wk, wv)


def _softmax_stage(s_l, v_l, m_scr, acc_scr):
    n = len(s_l)
    tk = s_l[0].shape[-1]
    m_old = [m_scr[c] for c in range(n)]
    m_new = [jnp.maximum(m, jnp.max(s, axis=-1, keepdims=True)) for m, s in zip(m_old, s_l)]
    p_l = [jnp.concatenate([jnp.exp(s[:, o:o + LANES] - m) for o in range(0, tk, LANES)], axis=-1).astype(BF16)
           for s, m in zip(s_l, m_new)]
    alpha = [jnp.exp(mo - mn) for mo, mn in zip(m_old, m_new)]
    pv_l = [_dot(p, v) for p, v in zip(p_l, v_l)]
    for c in range(n):
        m_scr[c] = m_new[c]
        w = acc_scr.shape[-1]
        a = alpha[c] if w == LANES else jnp.concatenate([alpha[c]] * (w // LANES), axis=-1)
        acc_scr[c] = a * acc_scr[c] + pv_l[c]


def _mla_attn_kernel(q_ref, k_ref, v_ref, o_ref, m_scr, acc_scr):
    tq = TQ
    i = pl.program_id(1)
    causal = lax.broadcasted_iota(jnp.int32, (tq, tq), 0) >= lax.broadcasted_iota(jnp.int32, (tq, tq), 1)
    lane = lax.broadcasted_iota(jnp.int32, (tq, LANES), 1)
    m_scr[...] = jnp.full(m_scr.shape, NEG, F32)
    acc_scr[...] = jnp.zeros(acc_scr.shape, F32)
    hsl = [slice(h * LANES, (h + 1) * LANES) for h in range(MLA_HEADS)]

    def tile(off, mask):
        s_l = [_dot_nt(q_ref[:, hs], k_ref[pl.ds(off, tq), hs]) for hs in hsl]
        if mask:
            s_l = [jnp.where(causal, s, NEG) for s in s_l]
        _softmax_stage(s_l, [v_ref[pl.ds(off, tq), hs] for hs in hsl], m_scr, acc_scr)

    def body(j, carry):
        tile(pl.multiple_of(j * tq, tq), False)
        return carry

    lax.fori_loop(0, i, body, 0)
    tile(pl.multiple_of(i * tq, tq), True)
    for pr in range(MLA_HEADS // 2):
        o0 = acc_scr[2 * pr] / acc_scr[2 * pr][:, MLA_V:MLA_V + 1]
        o1 = acc_scr[2 * pr + 1] / acc_scr[2 * pr + 1][:, MLA_V:MLA_V + 1]
        o_ref[:, pr * LANES:(pr + 1) * LANES] = jnp.where(lane < MLA_V, o0, pltpu.roll(o1, MLA_V, 1)).astype(BF16)


def _mla_attn(q, k, v, batch, seq):
    t = q.shape[0]
    nq = seq // TQ
    hw = MLA_HEADS * LANES
    vw = MLA_HEADS * MLA_V
    seq_spec = pl.BlockSpec((seq, hw), lambda b, i: (b, 0), pipeline_mode=pl.Buffered(1))
    return pl.pallas_call(
        _mla_attn_kernel,
        grid=(batch, nq),
        in_specs=[pl.BlockSpec((TQ, hw), lambda b, i: (b * nq + i, 0)), seq_spec, seq_spec],
        out_specs=pl.BlockSpec((TQ, vw), lambda b, i: (b * nq + i, 0)),
        out_shape=jax.ShapeDtypeStruct((t, vw), BF16),
        scratch_shapes=[pltpu.VMEM((MLA_HEADS, TQ, LANES), F32), pltpu.VMEM((MLA_HEADS, TQ, LANES), F32)],
        compiler_params=_params2(),
        name="mla_attn",
    )(q, k, v)


def _t5_bucket_table():
    n = np.arange(0, REL_MAX_DISTANCE + 1)
    max_exact = REL_BUCKETS // 2
    nf = np.maximum(n, 1).astype(np.float32)
    ratio = np.log(nf / np.float32(max_exact)) / np.float32(math.log(REL_MAX_DISTANCE / max_exact))
    large = max_exact + (ratio * np.float32(REL_BUCKETS - max_exact)).astype(np.int32)
    large = np.minimum(large, REL_BUCKETS - 1)
    return np.where(n < max_exact, n, large)


_BUCKETS = _t5_bucket_table()
_FAR_BUCKET = int(_BUCKETS[REL_MAX_DISTANCE])
assert _FAR_BUCKET == REL_BUCKETS - 1 and np.all(np.diff(_BUCKETS) >= 0)
_BUCKET_STARTS = [int(np.argmax(_BUCKETS >= b)) for b in range(REL_BUCKETS // 2 + 1, REL_BUCKETS)]


def _diff_attn_kernel(rb_ref, lam_ref, subln_ref, q_ref, k_ref, v_ref, o_ref, bias_scr, m_scr, acc_scr, *, lambda_init):
    tq = TQ
    i = pl.program_id(1)
    r_i = lax.broadcasted_iota(jnp.int32, (tq, tq), 0)
    c_i = lax.broadcasted_iota(jnp.int32, (tq, tq), 1)
    causal = r_i >= c_i

    @pl.when((pl.program_id(0) == 0) & (i == 0))
    def _():
        for t_idx, delta in enumerate((0, tq)):
            d = jnp.maximum(r_i - c_i + delta, 0)
            log_b = REL_BUCKETS // 2
            for start in _BUCKET_STARTS:
                log_b = log_b + (d >= start).astype(jnp.int32)
            bucket = jnp.where(d < REL_BUCKETS // 2, d, log_b)
            for h in range(DIFF_HEADS):
                bias = jnp.zeros((tq, tq), F32)
                for b in range(REL_BUCKETS):
                    bias = jnp.where(bucket == b, rb_ref[b, h], bias)
                bias_scr[t_idx, h] = bias - rb_ref[_FAR_BUCKET, h]

    m_scr[...] = jnp.full(m_scr.shape, NEG, F32)
    acc_scr[...] = jnp.zeros(acc_scr.shape, F32)
    lane = lax.broadcasted_iota(jnp.int32, (tq, LANES), 1)
    ones_blk = (lane == 0).astype(BF16)
    hsl = [slice(h * LANES, (h + 1) * LANES) for h in range(DIFF_HEADS)]
    q_l = []
    for hs in hsl:
        qp = q_ref[:, hs]
        q_l.append(jnp.concatenate([jnp.where(lane < DIFF_QK, qp, jnp.zeros_like(qp)),
                                    jnp.where(lane >= DIFF_QK, qp, jnp.zeros_like(qp))], axis=0))

    def tile(off, bias_idx, mask):
        s_l = [_dot_nt(q_l[h], k_ref[pl.ds(off, tq), hsl[h]]) for h in range(DIFF_HEADS)]
        if bias_idx is not None:
            s_l = [s + jnp.concatenate([bias_scr[bias_idx, h]] * 2, axis=0) for h, s in enumerate(s_l)]
        if mask:
            mask2 = jnp.concatenate([causal, causal], axis=0)
            s_l = [jnp.where(mask2, s, NEG) for s in s_l]
        v_l = [jnp.concatenate([v_ref[pl.ds(off, tq), hs], ones_blk], axis=-1) for hs in hsl]
        _softmax_stage(s_l, v_l, m_scr, acc_scr)

    def far_body(j, carry):
        tile(pl.multiple_of(j * tq, tq), None, False)
        return carry

    def near_body(j, carry):
        tile(pl.multiple_of(j * tq, tq), 1, False)
        return carry

    n_far = jnp.maximum(i - 1, 0)
    lax.fori_loop(0, n_far, far_body, 0)
    lax.fori_loop(n_far, i, near_body, 0)
    tile(pl.multiple_of(i * tq, tq), 0, True)

    lam = lam_ref[...]
    lam_full = (jnp.exp(jnp.sum(lam[0:1] * lam[1:2], axis=-1, keepdims=True))
                - jnp.exp(jnp.sum(lam[2:3] * lam[3:4], axis=-1, keepdims=True)) + lambda_init)
    for h in range(DIFF_HEADS):
        acc = acc_scr[h]
        on = acc[:, :DIFF_V] / acc[:, DIFF_V:DIFF_V + 1]
        o = on[:tq] - lam_full * on[tq:]
        o = _rms(o, subln_ref[...], SUBLN_EPS) * (1.0 - lambda_init)
        o_ref[:, hsl[h]] = o.astype(BF16)


def _diff_attn(p_diff, rel_bias, lam, subln, batch, seq, layer_idx):
    t = p_diff.shape[0]
    nq = seq // TQ
    w = DIFF_HEADS * LANES
    lambda_init = 0.8 - 0.6 * math.exp(-0.3 * layer_idx)
    kern = functools.partial(_diff_attn_kernel, lambda_init=lambda_init)
    const2 = lambda shape: pl.BlockSpec(shape, lambda b, i: (0, 0))
    return pl.pallas_call(
        kern,
        grid=(batch, nq),
        in_specs=[pl.BlockSpec(memory_space=pltpu.SMEM),
                  const2((4, DIFF_QK)), const2((1, DIFF_V)),
                  pl.BlockSpec((TQ, w), lambda b, i: (b * nq + i, 0)),
                  pl.BlockSpec((seq, w), lambda b, i: (b, 1), pipeline_mode=pl.Buffered(1)),
                  pl.BlockSpec((seq, w), lambda b, i: (b, 2), pipeline_mode=pl.Buffered(1))],
        out_specs=pl.BlockSpec((TQ, w), lambda b, i: (b * nq + i, 0)),
        out_shape=jax.ShapeDtypeStruct((t, w), BF16),
        scratch_shapes=[pltpu.VMEM((2, DIFF_HEADS, TQ, TQ), F32), pltpu.VMEM((DIFF_HEADS, 2 * TQ, LANES), F32),
                        pltpu.VMEM((DIFF_HEADS, 2 * TQ, 2 * LANES), F32)],
        compiler_params=_params2(),
        name="diff_attn",
    )(rel_bias, lam, subln, p_diff, p_diff, p_diff)


def _merge_kernel(x_ref, pg_ref, bg_ref, or_ref, om_ref, od_ref, wr_ref, wm_ref, wd_ref, wo_ref, gf_ref, x_out, h_out):
    def gate(idx):
        cs = slice(idx * D_MODEL, (idx + 1) * D_MODEL)
        return jax.nn.sigmoid(pg_ref[:, cs] + bg_ref[:, cs])

    merged = gate(0) * _dot(or_ref[...], wr_ref[...])
    merged = merged + gate(1) * _dot(om_ref[...], wm_ref[...])
    merged = merged + gate(2) * _dot(od_ref[...], wd_ref[...])
    x1 = x_ref[...] + _dot(merged.astype(BF16), wo_ref[...])
    x_out[...] = x1
    h_out[...] = _rms(x1, gf_ref[...], NORM_EPS).astype(BF16)


def _merge(x, p_gate, b_gate, o_r, o_m, o_d, w_r, w_m, w_d, w_o, g_ffn):
    t = x.shape[0]
    return pl.pallas_call(
        _merge_kernel,
        grid=(t // TM,),
        in_specs=[_row_spec(TM, D_MODEL), _row_spec(TM, GATE_COLS), _const_spec((1, GATE_COLS)),
                  _row_spec(TM, 512), _row_spec(TM, 512), _row_spec(TM, 512),
                  _const_spec((512, D_MODEL)), _const_spec((512, D_MODEL)), _const_spec((512, D_MODEL)),
                  _const_spec((D_MODEL, D_MODEL)), _const_spec((1, D_MODEL))],
        out_specs=[_row_spec(TM, D_MODEL), _row_spec(TM, D_MODEL)],
        out_shape=[jax.ShapeDtypeStruct((t, D_MODEL), F32), jax.ShapeDtypeStruct((t, D_MODEL), BF16)],
        compiler_params=_params(),
        name="merge",
    )(x, p_gate, b_gate, o_r, o_m, o_d, w_r, w_m, w_d, w_o, g_ffn)


def _ffn_kernel(x_ref, h_ref, wup_ref, cw_ref, cb_ref, wdn_ref, gfin_ref, o_ref, carry_ref, *, tiles_per_seq, final_norm):
    tm = TM

    @pl.when(pl.program_id(0) % tiles_per_seq == 0)
    def _():
        carry_ref[...] = jnp.zeros_like(carry_ref)

    h = h_ref[...]
    row = lax.broadcasted_iota(jnp.int32, (tm, 1), 0)

    def conv(u, cols):
        prev = carry_ref[:, cols]
        u1 = jnp.where(row == 0, prev[7:8, :], pltpu.roll(u, 1, 0))
        u2 = jnp.where(row == 0, prev[6:7, :], jnp.where(row == 1, prev[7:8, :], pltpu.roll(u, 2, 0)))
        carry_ref[:, cols] = u[tm - 8:tm, :]
        return cw_ref[0:1, cols] * u2 + cw_ref[1:2, cols] * u1 + cw_ref[2:3, cols] * u + cb_ref[:, cols]

    acc = x_ref[...]
    for ck in range(D_FF // FF_CHUNK):
        gc = slice(ck * FF_CHUNK, (ck + 1) * FF_CHUNK)
        vc = slice(D_FF + ck * FF_CHUNK, D_FF + (ck + 1) * FF_CHUNK)
        gate = conv(_dot(h, wup_ref[:, gc]), gc)
        val = conv(_dot(h, wup_ref[:, vc]), vc)
        act = (gate * jax.nn.sigmoid(gate) * val).astype(BF16)
        acc = acc + _dot(act, wdn_ref[gc, :])
    if final_norm:
        acc = _rms(acc, gfin_ref[...], NORM_EPS)
    o_ref[...] = acc


def _ffn(x1, h2, w_up, conv_w, conv_b, w_down, g_final, seq, final_norm):
    t = x1.shape[0]
    kern = functools.partial(_ffn_kernel, tiles_per_seq=seq // TM, final_norm=final_norm)
    return pl.pallas_call(
        kern,
        grid=(t // TM,),
        in_specs=[_row_spec(TM, D_MODEL), _row_spec(TM, D_MODEL), _const_spec((D_MODEL, 2 * D_FF)),
                  _const_spec((3, 2 * D_FF)), _const_spec((1, 2 * D_FF)), _const_spec((D_FF, D_MODEL)),
                  _const_spec((1, D_MODEL))],
        out_specs=_row_spec(TM, D_MODEL),
        out_shape=jax.ShapeDtypeStruct((t, D_MODEL), F32),
        scratch_shapes=[pltpu.VMEM((8, 2 * D_FF), F32)],
        compiler_params=_params(),
        name="conv_ffn",
    )(x1, h2, w_up, conv_w, conv_b, w_down, g_final)


def _mla_weights(w_uq, w_ukv):
    qd = MLA_NOPE + MLA_ROPE
    half = MLA_ROPE // 2
    wq = w_uq.reshape(MLA_Q_LORA, MLA_HEADS, qd)
    zq = jnp.zeros((MLA_Q_LORA, MLA_HEADS, LANES - qd), F32)
    wq_main = jnp.concatenate([wq, zq], axis=-1)
    x1 = wq[:, :, MLA_NOPE:MLA_NOPE + half]
    x2 = wq[:, :, MLA_NOPE + half:]
    wq_rot = jnp.concatenate([jnp.zeros((MLA_Q_LORA, MLA_HEADS, MLA_NOPE), F32), -x2, x1, zq], axis=-1)
    wkv = w_ukv.reshape(MLA_KV_LORA, MLA_HEADS, MLA_NOPE + MLA_V)
    zkv = jnp.zeros((MLA_KV_LORA, MLA_HEADS, LANES - MLA_NOPE), F32)
    wk = jnp.concatenate([wkv[:, :, :MLA_NOPE], zkv], axis=-1)
    wv = jnp.concatenate([wkv[:, :, MLA_NOPE:], zkv], axis=-1)
    flat = lambda w: w.reshape(w.shape[0], -1).astype(BF16)
    return flat(wq_main), flat(wq_rot), flat(wk), flat(wv)


def kernel(x, positions, rel_bias, norm_mix, w_in, b_gate, rwkv_mu, rwkv_w0, rwkv_w2, rwkv_a0, rwkv_a2, rwkv_g2, rwkv_k_k, rwkv_k_a, rwkv_r_k, rwkv_ln_w, rwkv_ln_b, mla_q_norm, mla_w_uq, mla_kv_norm, mla_w_ukv, diff_lambda, diff_subln, w_branch_rwkv, w_branch_mla, w_branch_diff, w_o, norm_ffn, ffn_w_up, ffn_conv_w, ffn_conv_b, ffn_w_down, norm_final):
    batch, seq, _ = x.shape
    depth = w_in.shape[0]
    t = batch * seq
    assert seq % TQ == 0 and seq % TM == 0 and seq % RWKV_TILE == 0 and RWKV_TILE % CHUNK == 0
    xf = x.reshape(t, D_MODEL)
    pos_col = positions.reshape(t, 1)
    inv_freq = ROPE_BASE ** (-jnp.arange(0, MLA_ROPE, 2, dtype=F32) / MLA_ROPE)
    freq = jnp.concatenate([jnp.zeros((MLA_NOPE,), F32), inv_freq, inv_freq,
                            jnp.zeros((LANES - MLA_NOPE - MLA_ROPE,), F32)]).reshape(1, LANES)
    row = lambda v: v.reshape(1, -1)
    diff_scale = jnp.concatenate([jnp.full((512,), DIFF_QK ** -0.5, F32), jnp.ones((DIFF_COLS - 512,), F32)])

    for l in range(depth):
        s0, s1, s2 = RWKV_COLS, RWKV_COLS + MLA_COLS, RWKV_COLS + MLA_COLS + DIFF_COLS
        w = w_in[l]
        w_all = jnp.concatenate([w[:, :s0], w[:, s0:s1], jnp.zeros((D_MODEL, MLA_PAD - MLA_COLS), F32),
                                 w[:, s1:s2] * diff_scale, w[:, s2:]], axis=1).astype(BF16)
        p_rwkv, p_mla, p_diff, p_gate = _in_proj(xf, row(norm_mix[l]), w_all)

        o_r = _rwkv(p_rwkv, seq, row(rwkv_mu[l]), row(rwkv_w0[l]), rwkv_w2[l].astype(BF16), row(rwkv_a0[l]),
                    rwkv_a2[l].astype(BF16), rwkv_g2[l].astype(BF16), row(rwkv_k_k[l]), row(rwkv_k_a[l]),
                    row(rwkv_r_k[l]), row(rwkv_ln_w[l]), row(rwkv_ln_b[l]))

        wq, wqr, wk, wv = _mla_weights(mla_w_uq[l], mla_w_ukv[l])
        q_m, k_m, v_m = _mla_prep(p_mla, pos_col, freq, row(mla_q_norm[l]), row(mla_kv_norm[l]), wq, wqr, wk, wv)
        o_m = _mla_attn(q_m, k_m, v_m, batch, seq)

        o_d = _diff_attn(p_diff, rel_bias, diff_lambda[l], row(diff_subln[l]), batch, seq, l)

        x1, h2 = _merge(xf, p_gate, row(b_gate[l]), o_r, o_m, o_d, w_branch_rwkv[l].astype(BF16),
                        w_branch_mla[l].astype(BF16), w_branch_diff[l].astype(BF16), w_o[l].astype(BF16),
                        row(norm_ffn[l]))
        xf = _ffn(x1, h2, ffn_w_up[l].astype(BF16), ffn_conv_w[l], row(ffn_conv_b[l]), ffn_w_down[l].astype(BF16),
                  row(norm_final), seq, final_norm=(l == depth - 1))
    return xf.reshape(batch, seq, D_MODEL)
```

```python
import functools
import math

import numpy as np
import jax
import jax.numpy as jnp
from jax import lax
from jax.experimental import pallas as pl
from jax.experimental.pallas import tpu as pltpu

F32 = jnp.float32
BF16 = jnp.bfloat16

D_MODEL = 1024
RWKV_HEADS = 8
RWKV_N = 64
RWKV_DIM = 512
RWKV_COLS = 1792
RWKV_GN_EPS = 64e-5
MLA_HEADS = 8
MLA_Q_LORA = 256
MLA_KV_LORA = 128
MLA_NOPE = 64
MLA_ROPE = 32
MLA_V = 64
MLA_COLS = 416
MLA_PAD = 512
ROPE_BASE = 10000.0
DIFF_HEADS = 4
DIFF_QK = 64
DIFF_V = 128
DIFF_COLS = 1536
REL_BUCKETS = 32
REL_MAX_DISTANCE = 128
D_FF = 2816
GATE_COLS = 3072
NORM_EPS = 1e-6
SUBLN_EPS = 1e-5

LANES = 128
VMEM_LIMIT = 56 * 1024 * 1024
TM = 256
CHUNK = 64
RWKV_TILE = 128
TQ = 256
FF_CHUNK = 256
NEG = -1e30
LOG2E = 1.4426950408889634

_NT = (((1,), (1,)), ((), ()))
_TN = (((0,), (0,)), ((), ()))


def _dot(a, b):
    return jnp.dot(a, b, preferred_element_type=F32)


def _dot_nt(a, b):
    return lax.dot_general(a, b, _NT, preferred_element_type=F32)


def _dot_tn(a, b):
    return lax.dot_general(a, b, _TN, preferred_element_type=F32)


def _rms(x, g, eps):
    return x * lax.rsqrt(jnp.mean(x * x, axis=-1, keepdims=True) + eps) * g


def _params():
    return pltpu.CompilerParams(dimension_semantics=("arbitrary",), vmem_limit_bytes=VMEM_LIMIT)


def _params2():
    return pltpu.CompilerParams(dimension_semantics=("arbitrary", "arbitrary"), vmem_limit_bytes=VMEM_LIMIT)


def _const_spec(shape):
    return pl.BlockSpec(shape, lambda *_: (0,) * len(shape), pipeline_mode=pl.Buffered(1))


def _row_spec(tm, cols, col_block=0):
    return pl.BlockSpec((tm, cols), lambda i: (i, col_block))


def _in_proj_kernel(x_ref, g_ref, w_ref, pr_ref, pm_ref, pd_ref, pg_ref):
    h = _rms(x_ref[...], g_ref[...], NORM_EPS).astype(BF16)
    c0 = 0
    pr_ref[...] = _dot(h, w_ref[:, c0:c0 + RWKV_COLS])
    c0 += RWKV_COLS
    pm_ref[...] = _dot(h, w_ref[:, c0:c0 + MLA_PAD])
    c0 += MLA_PAD
    pd_ref[...] = _dot(h, w_ref[:, c0:c0 + DIFF_COLS]).astype(BF16)
    c0 += DIFF_COLS
    pg_ref[...] = _dot(h, w_ref[:, c0:c0 + GATE_COLS])


def _in_proj(x, g, w_all):
    t = x.shape[0]
    n_all = w_all.shape[1]
    return pl.pallas_call(
        _in_proj_kernel,
        grid=(t // TM,),
        in_specs=[_row_spec(TM, D_MODEL), _const_spec((1, D_MODEL)), _const_spec((D_MODEL, n_all))],
        out_specs=[_row_spec(TM, RWKV_COLS), _row_spec(TM, MLA_PAD), _row_spec(TM, DIFF_COLS), _row_spec(TM, GATE_COLS)],
        out_shape=[jax.ShapeDtypeStruct((t, RWKV_COLS), F32), jax.ShapeDtypeStruct((t, MLA_PAD), F32),
                   jax.ShapeDtypeStruct((t, DIFF_COLS), BF16), jax.ShapeDtypeStruct((t, GATE_COLS), F32)],
        compiler_params=_params(),
        name="in_proj",
    )(x, g, w_all)


def _split3(a):
    hi = a.astype(BF16)
    r1 = a - hi.astype(F32)
    mid = r1.astype(BF16)
    lo = (r1 - mid.astype(F32)).astype(BF16)
    return hi, mid, lo


def _softplus(z):
    return jnp.maximum(z, 0.0) + jnp.log(1.0 + jnp.exp(-jnp.abs(z)))


def _rwkv_kernel(p_ref, mu_ref, w0_ref, w2_ref, a0_ref, a2_ref, g2_ref, kk_ref, ka_ref, rk_ref, lnw_ref, lnb_ref,
                 o_ref, carry_ref, state_ref, o_scr, *, tiles_per_seq):
    c = CHUNK
    n = RWKV_N
    ts = RWKV_TILE
    nc = ts // c

    @pl.when(pl.program_id(0) % tiles_per_seq == 0)
    def _():
        carry_ref[...] = jnp.zeros_like(carry_ref)
        state_ref[...] = jnp.zeros_like(state_ref)

    p = p_ref[...]
    row = lax.broadcasted_iota(jnp.int32, (ts, 1), 0)
    shifted = jnp.where(row == 0, carry_ref[...], pltpu.roll(p, 1, 0))
    carry_ref[...] = p[ts - 1:ts, :]
    pm = p + (shifted - p) * mu_ref[...]

    r = pm[:, 0:512]
    k = pm[:, 512:1024]
    v = pm[:, 1024:1536]
    pw = pm[:, 1536:1600]
    pa = pm[:, 1600:1664]
    pg = pm[:, 1664:1792]

    w_log = -_softplus(-(w0_ref[...] + _dot(jnp.tanh(pw).astype(BF16), w2_ref[...]))) - 0.5
    logd = -jnp.exp(w_log)
    a = jax.nn.sigmoid(a0_ref[...] + _dot(pa.astype(BF16), a2_ref[...]))
    g = _dot(jax.nn.sigmoid(pg).astype(BF16), g2_ref[...])

    tr = lax.broadcasted_iota(jnp.int32, (ts, ts), 0)
    tc = lax.broadcasted_iota(jnp.int32, (ts, ts), 1)
    tri = ((tr >= tc) & (tr // c == tc // c)).astype(BF16)
    hi, mid, lo = _split3(logd)
    cs = _dot(tri, hi) + _dot(tri, mid) + _dot(tri, lo)
    total = jnp.concatenate([jnp.broadcast_to(cs[(m + 1) * c - 1:(m + 1) * c, :], (c, RWKV_DIM)) for m in range(nc)], axis=0)
    e_in = jnp.exp(cs)
    e_ex = jnp.exp(cs - logd)
    e_inv = jnp.exp(-cs)
    e_end = jnp.exp(total - cs)
    g_end = jnp.exp(total)

    k2 = k * (1.0 + (a - 1.0) * ka_ref[...])
    kku = k * kk_ref[...]
    rt = r * e_in
    rkk = r * k2 * rk_ref[...]

    ri = lax.broadcasted_iota(jnp.int32, (c, 2 * c), 0)
    ci = lax.broadcasted_iota(jnp.int32, (c, 2 * c), 1)
    cm = jnp.where(ci >= c, ci - c, ci)
    strict = ri > cm
    incl = ri >= cm
    right = ci >= c
    eye = (lax.broadcasted_iota(jnp.int32, (c, c), 0) == lax.broadcasted_iota(jnp.int32, (c, c), 1)).astype(F32)

    items = [(m, h) for m in range(nc) for h in range(RWKV_HEADS)]
    sl = lambda x, m, h: x[m * c:(m + 1) * c, h * n:(h + 1) * n]
    kk_l = []
    for m, h in items:
        kk_h = sl(kku, m, h)
        kk_l.append(kk_h / jnp.maximum(jnp.sqrt(jnp.sum(kk_h * kk_h, axis=-1, keepdims=True)), 1e-12))
    b_l = [kk * sl(a, m, h) for kk, (m, h) in zip(kk_l, items)]
    at_l = [(-kk * sl(e_ex, m, h)).astype(BF16) for kk, (m, h) in zip(kk_l, items)]
    rt_l = [sl(rt, m, h).astype(BF16) for m, h in items]
    vb_l = [sl(v, m, h).astype(BF16) for m, h in items]
    rhs_l = [jnp.concatenate([b * sl(e_inv, m, h), sl(k2, m, h) * sl(e_inv, m, h)], axis=0).astype(BF16)
             for b, (m, h) in zip(b_l, items)]
    ends_l = [jnp.concatenate([b * sl(e_end, m, h), sl(k2, m, h) * sl(e_end, m, h)], axis=0).astype(BF16)
              for b, (m, h) in zip(b_l, items)]
    big_l = [_dot_nt(jnp.concatenate([at, rtb], axis=0), rhs) for at, rtb, rhs in zip(at_l, rt_l, rhs_l)]
    top_l = [jnp.where(strict, big[:c, :], 0.0) for big in big_l]
    bot_l = [jnp.where(incl, big[c:, :], 0.0).astype(BF16) for big in big_l]
    lakv_l = [_dot(jnp.where(right, top, 0.0).astype(BF16), jnp.concatenate([vb, vb], axis=0))
              for top, vb in zip(top_l, vb_l)]

    x_l = [top[:, :c] for top in top_l]
    tinv_l = [eye + x for x in x_l]
    xb_l = [x.astype(BF16) for x in x_l]
    for _ in range(int(math.log2(c)) - 1):
        x_l = [_dot(xb, xb) for xb in xb_l]
        xb_l = [x.astype(BF16) for x in x_l]
        tinv_l = [tinv + _dot(tinv.astype(BF16), xb) for tinv, xb in zip(tinv_l, xb_l)]
    tinvb_l = [tinv.astype(BF16) for tinv in tinv_l]
    abar_l = [_dot(tb, at).astype(BF16) for tb, at in zip(tinvb_l, at_l)]
    vbar_l = [_dot(tb, lv.astype(BF16)).astype(BF16) for tb, lv in zip(tinvb_l, lakv_l)]
    rhat_l = [(rtb.astype(F32) + _dot(bot[:, :c], ab)).astype(BF16) for rtb, bot, ab in zip(rt_l, bot_l, abar_l)]
    y0_l = [_dot(bot, jnp.concatenate([vbar, vb], axis=0)) for bot, vbar, vb in zip(bot_l, vbar_l, vb_l)]
    p_l = [_dot_tn(ab, ends[:c, :]).astype(BF16) for ab, ends in zip(abar_l, ends_l)]
    q_l = [_dot_tn(jnp.concatenate([vbar, vb], axis=0), ends) for vbar, vb, ends in zip(vbar_l, vb_l, ends_l)]

    st_l = [state_ref[h] for h in range(RWKV_HEADS)]
    y_l = []
    for idx, (m, h) in enumerate(items):
        st = st_l[h]
        st_b = st.astype(BF16)
        y_l.append(_dot_nt(rhat_l[idx], st_b) + y0_l[idx])
        st_l[h] = st * g_end[(m + 1) * c - 1:(m + 1) * c, h * n:(h + 1) * n] + _dot(st_b, p_l[idx]) + q_l[idx]
    for h in range(RWKV_HEADS):
        state_ref[h] = st_l[h]

    for y, (m, h) in zip(y_l, items):
        hs = slice(h * n, (h + 1) * n)
        mean = jnp.mean(y, axis=-1, keepdims=True)
        yc = y - mean
        var = jnp.mean(yc * yc, axis=-1, keepdims=True)
        yn = yc * lax.rsqrt(var + RWKV_GN_EPS) * lnw_ref[:, hs] + lnb_ref[:, hs]
        bonus = jnp.sum(sl(rkk, m, h), axis=-1, keepdims=True) * sl(v, m, h)
        o_scr[m * c:(m + 1) * c, hs] = (yn + bonus) * sl(g, m, h)

    o_ref[...] = o_scr[...].astype(BF16)


def _rwkv(p_rwkv, seq, mu, w0, w2, a0, a2, g2, k_k, k_a, r_k, ln_w, ln_b):
    t = p_rwkv.shape[0]
    ts = RWKV_TILE
    kern = functools.partial(_rwkv_kernel, tiles_per_seq=seq // ts)
    vec = lambda: _const_spec((1, RWKV_DIM))
    return pl.pallas_call(
        kern,
        grid=(t // ts,),
        in_specs=[_row_spec(ts, RWKV_COLS), _const_spec((1, RWKV_COLS)), vec(), _const_spec((64, RWKV_DIM)), vec(),
                  _const_spec((64, RWKV_DIM)), _const_spec((128, RWKV_DIM)), vec(), vec(), vec(), vec(), vec()],
        out_specs=_row_spec(ts, RWKV_DIM),
        out_shape=jax.ShapeDtypeStruct((t, RWKV_DIM), BF16),
        scratch_shapes=[pltpu.VMEM((1, RWKV_COLS), F32), pltpu.VMEM((RWKV_HEADS, RWKV_N, RWKV_N), F32),
                        pltpu.VMEM((ts, RWKV_DIM), F32)],
        compiler_params=_params(),
        name="rwkv7",
    )(p_rwkv, mu, w0, w2, a0, a2, g2, k_k, k_a, r_k, ln_w, ln_b)


def _mla_prep_kernel(p_ref, pos_ref, freq_ref, qn_ref, kvn_ref, wq_ref, wqr_ref, wk_ref, wv_ref, q_out, k_out, v_out):
    p = p_ref[...]
    hq = _rms(p[:, 0:MLA_Q_LORA], qn_ref[...], NORM_EPS).astype(BF16)
    hkv = _rms(p[:, MLA_Q_LORA:MLA_Q_LORA + MLA_KV_LORA], kvn_ref[...], NORM_EPS).astype(BF16)
    blk = p[:, 384:512]
    ang = pos_ref[...].astype(F32) * freq_ref[...]
    cos = jnp.cos(ang)
    sin = jnp.sin(ang)
    scale = (MLA_NOPE + MLA_ROPE) ** -0.5 * LOG2E
    qa = _dot(hq, wq_ref[...])
    qr = _dot(hq, wqr_ref[...])
    lane = lax.broadcasted_iota(jnp.int32, blk.shape, 1)
    kr = pltpu.roll(blk, 64, 1)
    rot = jnp.where(lane < 80, -pltpu.roll(blk, 48, 1), pltpu.roll(blk, 80, 1))
    rot = jnp.where((lane >= 64) & (lane < 96), rot, 0.0)
    krope = kr * cos + rot * sin
    kn = _dot(hkv, wk_ref[...])
    for h in range(MLA_HEADS):
        hs = slice(h * LANES, (h + 1) * LANES)
        q_out[:, hs] = ((qa[:, hs] * cos + qr[:, hs] * sin) * scale).astype(BF16)
        k_out[:, hs] = (kn[:, hs] + krope).astype(BF16)
    vv = _dot(hkv, wv_ref[...])
    ones_col = (lane == MLA_V).astype(F32)
    for h in range(MLA_HEADS):
        hs = slice(h * LANES, (h + 1) * LANES)
        v_out[:, hs] = (vv[:, hs] + ones_col).astype(BF16)


def _mla_prep(p_mla, pos_col, freq, q_norm, kv_norm, wq, wqr, wk, wv):
    t = p_mla.shape[0]
    hw = MLA_HEADS * LANES
    return pl.pallas_call(
        _mla_prep_kernel,
        grid=(t // TM,),
        in_specs=[_row_spec(TM, MLA_PAD), _row_spec(TM, 1), _const_spec((1, LANES)), _const_spec((1, MLA_Q_LORA)),
                  _const_spec((1, MLA_KV_LORA)), _const_spec((MLA_Q_LORA, hw)), _const_spec((MLA_Q_LORA, hw)),
                  _const_spec((MLA_KV_LORA, hw)), _const_spec((MLA_KV_LORA, hw))],
        out_specs=[_row_spec(TM, hw), _row_spec(TM, hw), _row_spec(TM, hw)],
        out_shape=[jax.ShapeDtypeStruct((t, hw), BF16), jax.ShapeDtypeStruct((t, hw), BF16),
                   jax.ShapeDtypeStruct((t, hw), BF16)],
        compiler_params=_params(),
        name="mla_prep",
    )(p_mla, pos_col, freq, q_norm, kv_norm, wq, wqr, wk, wv)


def _softmax_stage(s_l, v_l, m_scr, acc_scr):
    n = len(s_l)
    tk = s_l[0].shape[-1]
    m_old = [m_scr[c] for c in range(n)]
    m_new = [jnp.maximum(m, jnp.max(s, axis=-1, keepdims=True)) for m, s in zip(m_old, s_l)]
    p_l = [jnp.concatenate([jnp.exp2(s[:, o:o + LANES] - m) for o in range(0, tk, LANES)], axis=-1).astype(BF16)
           for s, m in zip(s_l, m_new)]
    alpha = [jnp.exp2(mo - mn) for mo, mn in zip(m_old, m_new)]
    pv_l = [_dot(p, v) for p, v in zip(p_l, v_l)]
    for c in range(n):
        m_scr[c] = m_new[c]
        w = acc_scr.shape[-1]
        a = alpha[c] if w == LANES else jnp.concatenate([alpha[c]] * (w // LANES), axis=-1)
        acc_scr[c] = a * acc_scr[c] + pv_l[c]


def _mla_attn_kernel(q_ref, k_ref, v_ref, o_ref, m_scr, acc_scr):
    tq = TQ
    i = pl.program_id(1)
    causal = lax.broadcasted_iota(jnp.int32, (tq, tq), 0) >= lax.broadcasted_iota(jnp.int32, (tq, tq), 1)
    lane = lax.broadcasted_iota(jnp.int32, (tq, LANES), 1)
    m_scr[...] = jnp.full(m_scr.shape, NEG, F32)
    acc_scr[...] = jnp.zeros(acc_scr.shape, F32)
    hsl = [slice(h * LANES, (h + 1) * LANES) for h in range(MLA_HEADS)]

    def tile(off, mask):
        s_l = [_dot_nt(q_ref[:, hs], k_ref[pl.ds(off, tq), hs]) for hs in hsl]
        if mask:
            s_l = [jnp.where(causal, s, NEG) for s in s_l]
        _softmax_stage(s_l, [v_ref[pl.ds(off, tq), hs] for hs in hsl], m_scr, acc_scr)

    def body(j, carry):
        tile(pl.multiple_of(j * tq, tq), False)
        return carry

    lax.fori_loop(0, i, body, 0)
    tile(pl.multiple_of(i * tq, tq), True)
    for pr in range(MLA_HEADS // 2):
        o0 = acc_scr[2 * pr] / acc_scr[2 * pr][:, MLA_V:MLA_V + 1]
        o1 = acc_scr[2 * pr + 1] / acc_scr[2 * pr + 1][:, MLA_V:MLA_V + 1]
        o_ref[:, pr * LANES:(pr + 1) * LANES] = jnp.where(lane < MLA_V, o0, pltpu.roll(o1, MLA_V, 1)).astype(BF16)


def _mla_attn(q, k, v, batch, seq):
    t = q.shape[0]
    nq = seq // TQ
    hw = MLA_HEADS * LANES
    vw = MLA_HEADS * MLA_V
    seq_spec = pl.BlockSpec((seq, hw), lambda b, i: (b, 0), pipeline_mode=pl.Buffered(1))
    return pl.pallas_call(
        _mla_attn_kernel,
        grid=(batch, nq),
        in_specs=[pl.BlockSpec((TQ, hw), lambda b, i: (b * nq + i, 0)), seq_spec, seq_spec],
        out_specs=pl.BlockSpec((TQ, vw), lambda b, i: (b * nq + i, 0)),
        out_shape=jax.ShapeDtypeStruct((t, vw), BF16),
        scratch_shapes=[pltpu.VMEM((MLA_HEADS, TQ, LANES), F32), pltpu.VMEM((MLA_HEADS, TQ, LANES), F32)],
        compiler_params=_params2(),
        name="mla_attn",
    )(q, k, v)


def _t5_bucket_table():
    n = np.arange(0, REL_MAX_DISTANCE + 1)
    max_exact = REL_BUCKETS // 2
    nf = np.maximum(n, 1).astype(np.float32)
    ratio = np.log(nf / np.float32(max_exact)) / np.float32(math.log(REL_MAX_DISTANCE / max_exact))
    large = max_exact + (ratio * np.float32(REL_BUCKETS - max_exact)).astype(np.int32)
    large = np.minimum(large, REL_BUCKETS - 1)
    return np.where(n < max_exact, n, large)


_BUCKETS = _t5_bucket_table()
_FAR_BUCKET = int(_BUCKETS[REL_MAX_DISTANCE])
assert _FAR_BUCKET == REL_BUCKETS - 1 and np.all(np.diff(_BUCKETS) >= 0)
_BUCKET_STARTS = [int(np.argmax(_BUCKETS >= b)) for b in range(REL_BUCKETS // 2 + 1, REL_BUCKETS)]


def _diff_attn_kernel(rb_ref, lam_ref, subln_ref, q_ref, k_ref, v_ref, o_ref, bias_scr, m_scr, acc_scr, *, lambda_init):
    tq = TQ
    i = pl.program_id(1)
    r_i = lax.broadcasted_iota(jnp.int32, (tq, tq), 0)
    c_i = lax.broadcasted_iota(jnp.int32, (tq, tq), 1)
    causal = r_i >= c_i

    @pl.when((pl.program_id(0) == 0) & (i == 0))
    def _():
        for t_idx, delta in enumerate((0, tq)):
            d = jnp.maximum(r_i - c_i + delta, 0)
            log_b = REL_BUCKETS // 2
            for start in _BUCKET_STARTS:
                log_b = log_b + (d >= start).astype(jnp.int32)
            bucket = jnp.where(d < REL_BUCKETS // 2, d, log_b)
            for h in range(DIFF_HEADS):
                bias = jnp.zeros((tq, tq), F32)
                for b in range(REL_BUCKETS):
                    bias = jnp.where(bucket == b, rb_ref[b, h], bias)
                bias_scr[t_idx, h] = (bias - rb_ref[_FAR_BUCKET, h]) * LOG2E

    m_scr[...] = jnp.full(m_scr.shape, NEG, F32)
    acc_scr[...] = jnp.zeros(acc_scr.shape, F32)
    lane = lax.broadcasted_iota(jnp.int32, (tq, LANES), 1)
    ones_blk = (lane == 0).astype(BF16)
    hsl = [slice(h * LANES, (h + 1) * LANES) for h in range(DIFF_HEADS)]
    q_l = []
    for hs in hsl:
        qp = q_ref[:, hs]
        q_l.append(jnp.concatenate([jnp.where(lane < DIFF_QK, qp, jnp.zeros_like(qp)),
                                    jnp.where(lane >= DIFF_QK, qp, jnp.zeros_like(qp))], axis=0))

    def tile(off, bias_idx, mask):
        s_l = [_dot_nt(q_l[h], k_ref[pl.ds(off, tq), hsl[h]]) for h in range(DIFF_HEADS)]
        if bias_idx is not None:
            s_l = [s + jnp.concatenate([bias_scr[bias_idx, h]] * 2, axis=0) for h, s in enumerate(s_l)]
        if mask:
            mask2 = jnp.concatenate([causal, causal], axis=0)
            s_l = [jnp.where(mask2, s, NEG) for s in s_l]
        v_l = [jnp.concatenate([v_ref[pl.ds(off, tq), hs], ones_blk], axis=-1) for hs in hsl]
        _softmax_stage(s_l, v_l, m_scr, acc_scr)

    def far_body(j, carry):
        tile(pl.multiple_of(j * tq, tq), None, False)
        return carry

    def near_body(j, carry):
        tile(pl.multiple_of(j * tq, tq), 1, False)
        return carry

    n_far = jnp.maximum(i - 1, 0)
    lax.fori_loop(0, n_far, far_body, 0)
    lax.fori_loop(n_far, i, near_body, 0)
    tile(pl.multiple_of(i * tq, tq), 0, True)

    lam = lam_ref[...]
    lam_full = (jnp.exp(jnp.sum(lam[0:1] * lam[1:2], axis=-1, keepdims=True))
                - jnp.exp(jnp.sum(lam[2:3] * lam[3:4], axis=-1, keepdims=True)) + lambda_init)
    for h in range(DIFF_HEADS):
        acc = acc_scr[h]
        on = acc[:, :DIFF_V] / acc[:, DIFF_V:DIFF_V + 1]
        o = on[:tq] - lam_full * on[tq:]
        o = _rms(o, subln_ref[...], SUBLN_EPS) * (1.0 - lambda_init)
        o_ref[:, hsl[h]] = o.astype(BF16)


def _diff_attn(p_diff, rel_bias, lam, subln, batch, seq, layer_idx):
    t = p_diff.shape[0]
    nq = seq // TQ
    w = DIFF_HEADS * LANES
    lambda_init = 0.8 - 0.6 * math.exp(-0.3 * layer_idx)
    kern = functools.partial(_diff_attn_kernel, lambda_init=lambda_init)
    const2 = lambda shape: pl.BlockSpec(shape, lambda b, i: (0, 0))
    return pl.pallas_call(
        kern,
        grid=(batch, nq),
        in_specs=[pl.BlockSpec(memory_space=pltpu.SMEM),
                  const2((4, DIFF_QK)), const2((1, DIFF_V)),
                  pl.BlockSpec((TQ, w), lambda b, i: (b * nq + i, 0)),
                  pl.BlockSpec((seq, w), lambda b, i: (b, 1), pipeline_mode=pl.Buffered(1)),
                  pl.BlockSpec((seq, w), lambda b, i: (b, 2), pipeline_mode=pl.Buffered(1))],
        out_specs=pl.BlockSpec((TQ, w), lambda b, i: (b * nq + i, 0)),
        out_shape=jax.ShapeDtypeStruct((t, w), BF16),
        scratch_shapes=[pltpu.VMEM((2, DIFF_HEADS, TQ, TQ), F32), pltpu.VMEM((DIFF_HEADS, 2 * TQ, LANES), F32),
                        pltpu.VMEM((DIFF_HEADS, 2 * TQ, 2 * LANES), F32)],
        compiler_params=_params2(),
        name="diff_attn",
    )(rel_bias, lam, subln, p_diff, p_diff, p_diff)


def _merge_kernel(x_ref, pg_ref, bg_ref, or_ref, om_ref, od_ref, wr_ref, wm_ref, wd_ref, wo_ref, gf_ref, x_out, h_out):
    def gate(idx):
        cs = slice(idx * D_MODEL, (idx + 1) * D_MODEL)
        return jax.nn.sigmoid(pg_ref[:, cs] + bg_ref[:, cs])

    merged = gate(0) * _dot(or_ref[...], wr_ref[...])
    merged = merged + gate(1) * _dot(om_ref[...], wm_ref[...])
    merged = merged + gate(2) * _dot(od_ref[...], wd_ref[...])
    x1 = x_ref[...] + _dot(merged.astype(BF16), wo_ref[...])
    x_out[...] = x1
    h_out[...] = _rms(x1, gf_ref[...], NORM_EPS).astype(BF16)


def _merge(x, p_gate, b_gate, o_r, o_m, o_d, w_r, w_m, w_d, w_o, g_ffn):
    t = x.shape[0]
    return pl.pallas_call(
        _merge_kernel,
        grid=(t // TM,),
        in_specs=[_row_spec(TM, D_MODEL), _row_spec(TM, GATE_COLS), _const_spec((1, GATE_COLS)),
                  _row_spec(TM, 512), _row_spec(TM, 512), _row_spec(TM, 512),
                  _const_spec((512, D_MODEL)), _const_spec((512, D_MODEL)), _const_spec((512, D_MODEL)),
                  _const_spec((D_MODEL, D_MODEL)), _const_spec((1, D_MODEL))],
        out_specs=[_row_spec(TM, D_MODEL), _row_spec(TM, D_MODEL)],
        out_shape=[jax.ShapeDtypeStruct((t, D_MODEL), F32), jax.ShapeDtypeStruct((t, D_MODEL), BF16)],
        compiler_params=_params(),
        name="merge",
    )(x, p_gate, b_gate, o_r, o_m, o_d, w_r, w_m, w_d, w_o, g_ffn)


def _ffn_kernel(x_ref, h_ref, wup_ref, cw_ref, cb_ref, wdn_ref, gfin_ref, o_ref, carry_ref, *, tiles_per_seq, final_norm):
    tm = TM

    @pl.when(pl.program_id(0) % tiles_per_seq == 0)
    def _():
        carry_ref[...] = jnp.zeros_like(carry_ref)

    h = h_ref[...]

    def conv(u, cols):
        ext = jnp.concatenate([carry_ref[:, cols], u], axis=0)
        carry_ref[:, cols] = u[tm - 8:tm, :]
        b = cw_ref[1:2, cols] * ext + pltpu.roll(cw_ref[0:1, cols] * ext, 1, 0)
        out = cw_ref[2:3, cols] * ext + pltpu.roll(b, 1, 0) + cb_ref[:, cols]
        return out[8:, :]

    def up(ck):
        gc = slice(ck * FF_CHUNK, (ck + 1) * FF_CHUNK)
        vc = slice(D_FF + ck * FF_CHUNK, D_FF + (ck + 1) * FF_CHUNK)
        return _dot(h, wup_ref[:, gc]), _dot(h, wup_ref[:, vc])

    n_chunks = D_FF // FF_CHUNK
    acc = x_ref[...]
    nxt = up(0)
    for ck in range(n_chunks):
        ug, uv = nxt
        if ck + 1 < n_chunks:
            nxt = up(ck + 1)
        gc = slice(ck * FF_CHUNK, (ck + 1) * FF_CHUNK)
        vc = slice(D_FF + ck * FF_CHUNK, D_FF + (ck + 1) * FF_CHUNK)
        gate = conv(ug, gc)
        val = conv(uv, vc)
        act = (gate * jax.nn.sigmoid(gate) * val).astype(BF16)
        acc = acc + _dot(act, wdn_ref[gc, :])
    if final_norm:
        acc = _rms(acc, gfin_ref[...], NORM_EPS)
    o_ref[...] = acc


def _ffn(x1, h2, w_up, conv_w, conv_b, w_down, g_final, seq, final_norm):
    t = x1.shape[0]
    kern = functools.partial(_ffn_kernel, tiles_per_seq=seq // TM, final_norm=final_norm)
    return pl.pallas_call(
        kern,
        grid=(t // TM,),
        in_specs=[_row_spec(TM, D_MODEL), _row_spec(TM, D_MODEL), _const_spec((D_MODEL, 2 * D_FF)),
                  _const_spec((3, 2 * D_FF)), _const_spec((1, 2 * D_FF)), _const_spec((D_FF, D_MODEL)),
                  _const_spec((1, D_MODEL))],
        out_specs=_row_spec(TM, D_MODEL),
        out_shape=jax.ShapeDtypeStruct((t, D_MODEL), F32),
        scratch_shapes=[pltpu.VMEM((8, 2 * D_FF), F32)],
        compiler_params=_params(),
        name="conv_ffn",
    )(x1, h2, w_up, conv_w, conv_b, w_down, g_final)


def _mla_weights(w_uq, w_ukv):
    qd = MLA_NOPE + MLA_ROPE
    half = MLA_ROPE // 2
    wq = w_uq.reshape(MLA_Q_LORA, MLA_HEADS, qd)
    zq = jnp.zeros((MLA_Q_LORA, MLA_HEADS, LANES - qd), F32)
    wq_main = jnp.concatenate([wq, zq], axis=-1)
    x1 = wq[:, :, MLA_NOPE:MLA_NOPE + half]
    x2 = wq[:, :, MLA_NOPE + half:]
    wq_rot = jnp.concatenate([jnp.zeros((MLA_Q_LORA, MLA_HEADS, MLA_NOPE), F32), -x2, x1, zq], axis=-1)
    wkv = w_ukv.reshape(MLA_KV_LORA, MLA_HEADS, MLA_NOPE + MLA_V)
    zkv = jnp.zeros((MLA_KV_LORA, MLA_HEADS, LANES - MLA_NOPE), F32)
    wk = jnp.concatenate([wkv[:, :, :MLA_NOPE], zkv], axis=-1)
    wv = jnp.concatenate([wkv[:, :, MLA_NOPE:], zkv], axis=-1)
    flat = lambda w: w.reshape(w.shape[0], -1).astype(BF16)
    return flat(wq_main), flat(wq_rot), flat(wk), flat(wv)


def kernel(x, positions, rel_bias, norm_mix, w_in, b_gate, rwkv_mu, rwkv_w0, rwkv_w2, rwkv_a0, rwkv_a2, rwkv_g2, rwkv_k_k, rwkv_k_a, rwkv_r_k, rwkv_ln_w, rwkv_ln_b, mla_q_norm, mla_w_uq, mla_kv_norm, mla_w_ukv, diff_lambda, diff_subln, w_branch_rwkv, w_branch_mla, w_branch_diff, w_o, norm_ffn, ffn_w_up, ffn_conv_w, ffn_conv_b, ffn_w_down, norm_final):
    batch, seq, _ = x.shape
    depth = w_in.shape[0]
    t = batch * seq
    assert seq % TQ == 0 and seq % TM == 0 and seq % RWKV_TILE == 0 and RWKV_TILE % CHUNK == 0
    xf = x.reshape(t, D_MODEL)
    pos_col = positions.reshape(t, 1)
    inv_freq = ROPE_BASE ** (-jnp.arange(0, MLA_ROPE, 2, dtype=F32) / MLA_ROPE)
    freq = jnp.concatenate([jnp.zeros((MLA_NOPE,), F32), inv_freq, inv_freq,
                            jnp.zeros((LANES - MLA_NOPE - MLA_ROPE,), F32)]).reshape(1, LANES)
    row = lambda v: v.reshape(1, -1)
    diff_scale = jnp.concatenate([jnp.full((512,), DIFF_QK ** -0.5 * LOG2E, F32), jnp.ones((DIFF_COLS - 512,), F32)])

    for l in range(depth):
        s0, s1, s2 = RWKV_COLS, RWKV_COLS + MLA_COLS, RWKV_COLS + MLA_COLS + DIFF_COLS
        w = w_in[l]
        w_all = jnp.concatenate([w[:, :s0], w[:, s0:s1], jnp.zeros((D_MODEL, MLA_PAD - MLA_COLS), F32),
                                 w[:, s1:s2] * diff_scale, w[:, s2:]], axis=1).astype(BF16)
        p_rwkv, p_mla, p_diff, p_gate = _in_proj(xf, row(norm_mix[l]), w_all)

        o_r = _rwkv(p_rwkv, seq, row(rwkv_mu[l]), row(rwkv_w0[l]), rwkv_w2[l].astype(BF16), row(rwkv_a0[l]),
                    rwkv_a2[l].astype(BF16), rwkv_g2[l].astype(BF16), row(rwkv_k_k[l]), row(rwkv_k_a[l]),
                    row(rwkv_r_k[l]), row(rwkv_ln_w[l]), row(rwkv_ln_b[l]))

        wq, wqr, wk, wv = _mla_weights(mla_w_uq[l], mla_w_ukv[l])
        q_m, k_m, v_m = _mla_prep(p_mla, pos_col, freq, row(mla_q_norm[l]), row(mla_kv_norm[l]), wq, wqr, wk, wv)
        o_m = _mla_attn(q_m, k_m, v_m, batch, seq)

        o_d = _diff_attn(p_diff, rel_bias, diff_lambda[l], row(diff_subln[l]), batch, seq, l)

        x1, h2 = _merge(xf, p_gate, row(b_gate[l]), o_r, o_m, o_d, w_branch_rwkv[l].astype(BF16),
                        w_branch_mla[l].astype(BF16), w_branch_diff[l].astype(BF16), w_o[l].astype(BF16),
                        row(norm_ffn[l]))
        xf = _ffn(x1, h2, ffn_w_up[l].astype(BF16), ffn_conv_w[l], row(ffn_conv_b[l]), ffn_w_down[l].astype(BF16),
                  row(norm_final), seq, final_norm=(l == depth - 1))
    return xf.reshape(batch, seq, D_MODEL)
```

```python
import functools
import math

import numpy as np
import jax
import jax.numpy as jnp
from jax import lax
from jax.experimental import pallas as pl
from jax.experimental.pallas import tpu as pltpu

F32 = jnp.float32
BF16 = jnp.bfloat16

D_MODEL = 1024
RWKV_HEADS = 8
RWKV_N = 64
RWKV_DIM = 512
RWKV_COLS = 1792
RWKV_GN_EPS = 64e-5
MLA_HEADS = 8
MLA_Q_LORA = 256
MLA_KV_LORA = 128
MLA_NOPE = 64
MLA_ROPE = 32
MLA_V = 64
MLA_COLS = 416
MLA_PAD = 512
ROPE_BASE = 10000.0
DIFF_HEADS = 4
DIFF_QK = 64
DIFF_V = 128
DIFF_COLS = 1536
REL_BUCKETS = 32
REL_MAX_DISTANCE = 128
D_FF = 2816
GATE_COLS = 3072
NORM_EPS = 1e-6
SUBLN_EPS = 1e-5

LANES = 128
VMEM_LIMIT = 56 * 1024 * 1024
TM = 256
CHUNK = 64
RWKV_TILE = 256
TQ = 256
ATTN_GROUP = 8
FF_CHUNK = 256
NEG = -1e30
LOG2E = 1.4426950408889634

_NT = (((1,), (1,)), ((), ()))
_TN = (((0,), (0,)), ((), ()))


def _dot(a, b):
    return jnp.dot(a, b, preferred_element_type=F32)


def _dot_nt(a, b):
    return lax.dot_general(a, b, _NT, preferred_element_type=F32)


def _dot_tn(a, b):
    return lax.dot_general(a, b, _TN, preferred_element_type=F32)


def _rms(x, g, eps):
    return x * lax.rsqrt(jnp.mean(x * x, axis=-1, keepdims=True) + eps) * g


def _params():
    return pltpu.CompilerParams(dimension_semantics=("arbitrary",), vmem_limit_bytes=VMEM_LIMIT)


def _params2():
    return pltpu.CompilerParams(dimension_semantics=("arbitrary", "arbitrary"), vmem_limit_bytes=VMEM_LIMIT)


def _const_spec(shape):
    return pl.BlockSpec(shape, lambda *_: (0,) * len(shape), pipeline_mode=pl.Buffered(1))


def _row_spec(tm, cols, col_block=0):
    return pl.BlockSpec((tm, cols), lambda i: (i, col_block))


def _in_proj_kernel(x_ref, g_ref, w_ref, pr_ref, pm_ref, pd_ref, pg_ref):
    h = _rms(x_ref[...], g_ref[...], NORM_EPS).astype(BF16)
    c0 = 0
    pr_ref[...] = _dot(h, w_ref[:, c0:c0 + RWKV_COLS])
    c0 += RWKV_COLS
    pm_ref[...] = _dot(h, w_ref[:, c0:c0 + MLA_PAD])
    c0 += MLA_PAD
    pd_ref[...] = _dot(h, w_ref[:, c0:c0 + DIFF_COLS]).astype(BF16)
    c0 += DIFF_COLS
    pg_ref[...] = _dot(h, w_ref[:, c0:c0 + GATE_COLS])


def _in_proj(x, g, w_all):
    t = x.shape[0]
    n_all = w_all.shape[1]
    return pl.pallas_call(
        _in_proj_kernel,
        grid=(t // TM,),
        in_specs=[_row_spec(TM, D_MODEL), _const_spec((1, D_MODEL)), _const_spec((D_MODEL, n_all))],
        out_specs=[_row_spec(TM, RWKV_COLS), _row_spec(TM, MLA_PAD), _row_spec(TM, DIFF_COLS), _row_spec(TM, GATE_COLS)],
        out_shape=[jax.ShapeDtypeStruct((t, RWKV_COLS), F32), jax.ShapeDtypeStruct((t, MLA_PAD), F32),
                   jax.ShapeDtypeStruct((t, DIFF_COLS), BF16), jax.ShapeDtypeStruct((t, GATE_COLS), F32)],
        compiler_params=_params(),
        name="in_proj",
    )(x, g, w_all)


def _split3(a):
    hi = a.astype(BF16)
    r1 = a - hi.astype(F32)
    mid = r1.astype(BF16)
    lo = (r1 - mid.astype(F32)).astype(BF16)
    return hi, mid, lo


def _softplus(z):
    return jnp.maximum(z, 0.0) + jnp.log(1.0 + jnp.exp(-jnp.abs(z)))


def _rwkv_kernel(p_ref, mu_ref, w0_ref, w2_ref, a0_ref, a2_ref, g2_ref, kk_ref, ka_ref, rk_ref, lnw_ref, lnb_ref,
                 o_ref, carry_ref, state_ref, *, tiles_per_seq):
    c = CHUNK
    n = RWKV_N
    ts = RWKV_TILE
    nc = ts // c

    @pl.when(pl.program_id(0) % tiles_per_seq == 0)
    def _():
        carry_ref[...] = jnp.zeros_like(carry_ref)
        state_ref[...] = jnp.zeros_like(state_ref)

    p = p_ref[...]
    row = lax.broadcasted_iota(jnp.int32, (ts, 1), 0)
    shifted = jnp.where(row == 0, carry_ref[...], pltpu.roll(p, 1, 0))
    carry_ref[...] = p[ts - 1:ts, :]
    pm = p + (shifted - p) * mu_ref[...]

    r = pm[:, 0:512]
    k = pm[:, 512:1024]
    v = pm[:, 1024:1536]
    pw = pm[:, 1536:1600]
    pa = pm[:, 1600:1664]
    pg = pm[:, 1664:1792]

    w_log = -_softplus(-(w0_ref[...] + _dot(jnp.tanh(pw).astype(BF16), w2_ref[...]))) - 0.5
    logd = -jnp.exp(w_log)
    a = jax.nn.sigmoid(a0_ref[...] + _dot(pa.astype(BF16), a2_ref[...]))
    g = _dot(jax.nn.sigmoid(pg).astype(BF16), g2_ref[...])

    tr = lax.broadcasted_iota(jnp.int32, (ts, ts), 0)
    tc = lax.broadcasted_iota(jnp.int32, (ts, ts), 1)
    tri = ((tr >= tc) & (tr // c == tc // c)).astype(BF16)
    hi, mid, lo = _split3(logd)
    cs = _dot(tri, hi) + _dot(tri, mid) + _dot(tri, lo)
    total = jnp.concatenate([jnp.broadcast_to(cs[(m + 1) * c - 1:(m + 1) * c, :], (c, RWKV_DIM)) for m in range(nc)], axis=0)
    e_in = jnp.exp(cs)
    e_ex = jnp.exp(cs - logd)
    e_inv = jnp.exp(-cs)
    e_end = jnp.exp(total - cs)
    g_end = jnp.exp(total)

    n_pairs = RWKV_HEADS // 2
    lo_t = lax.broadcasted_iota(jnp.int32, (ts, LANES), 1) < n
    lo_c = lax.broadcasted_iota(jnp.int32, (c, LANES), 1) < n

    def head_sum(x):
        blocks = []
        for q in range(n_pairs):
            xb = x[:, q * LANES:(q + 1) * LANES]
            s_lo = jnp.sum(jnp.where(lo_t, xb, 0.0), axis=-1, keepdims=True)
            s_hi = jnp.sum(jnp.where(lo_t, 0.0, xb), axis=-1, keepdims=True)
            blocks.append(jnp.where(lo_t, s_lo, s_hi))
        return jnp.concatenate(blocks, axis=-1)

    k2 = k * (1.0 + (a - 1.0) * ka_ref[...])
    kku = k * kk_ref[...]
    kkn = kku / jnp.maximum(jnp.sqrt(head_sum(kku * kku)), 1e-12)
    b = kkn * a
    full = {"at": -kkn * e_ex, "rt": r * e_in, "bt": b * e_inv, "kt": k2 * e_inv, "v": v,
            "be": b * e_end, "ke": k2 * e_end}

    ti = lax.broadcasted_iota(jnp.int32, (2 * c, 4 * c), 0) % c
    si = lax.broadcasted_iota(jnp.int32, (2 * c, 4 * c), 1) % c
    strict = ti > si
    incl = ti >= si
    eye = (lax.broadcasted_iota(jnp.int32, (2 * c, 2 * c), 0)
           == lax.broadcasted_iota(jnp.int32, (2 * c, 2 * c), 1)).astype(F32)

    items = [(m, q) for m in range(nc) for q in range(n_pairs)]

    def stacked(name, m, q):
        xb = full[name][m * c:(m + 1) * c, q * LANES:(q + 1) * LANES]
        return jnp.concatenate([jnp.where(lo_c, xb, 0.0), jnp.where(lo_c, 0.0, xb)], axis=0).astype(BF16)

    st_ops = {name: [stacked(name, m, q) for m, q in items] for name in full}
    big_l = [_dot_nt(jnp.concatenate([at, rtb], axis=0), jnp.concatenate([bt, kt], axis=0))
             for at, rtb, bt, kt in zip(st_ops["at"], st_ops["rt"], st_ops["bt"], st_ops["kt"])]
    top_l = [jnp.where(strict, big[:2 * c, :], 0.0) for big in big_l]
    bot_l = [jnp.where(incl, big[2 * c:, :], 0.0).astype(BF16) for big in big_l]
    lakv_l = [_dot(top[:, 2 * c:].astype(BF16), vb) for top, vb in zip(top_l, st_ops["v"])]

    x_l = [top[:, :2 * c] for top in top_l]
    tinv_l = [eye + x for x in x_l]
    xb_l = [x.astype(BF16) for x in x_l]
    for _ in range(int(math.log2(c)) - 1):
        x_l = [_dot(xb, xb) for xb in xb_l]
        xb_l = [x.astype(BF16) for x in x_l]
        tinv_l = [tinv + _dot(tinv.astype(BF16), xb) for tinv, xb in zip(tinv_l, xb_l)]
    tinvb_l = [tinv.astype(BF16) for tinv in tinv_l]
    abar_l = [_dot(tb, at).astype(BF16) for tb, at in zip(tinvb_l, st_ops["at"])]
    vbar_l = [_dot(tb, lv.astype(BF16)).astype(BF16) for tb, lv in zip(tinvb_l, lakv_l)]
    rhat_l = [(rtb.astype(F32) + _dot(bot[:, :2 * c], ab)).astype(BF16)
              for rtb, bot, ab in zip(st_ops["rt"], bot_l, abar_l)]
    uv_l = [jnp.concatenate([vbar, vb], axis=0) for vbar, vb in zip(vbar_l, st_ops["v"])]
    y0_l = [_dot(bot, uv) for bot, uv in zip(bot_l, uv_l)]
    p_l = [_dot_tn(ab, be).astype(BF16) for ab, be in zip(abar_l, st_ops["be"])]
    q_l = [_dot_tn(uv, jnp.concatenate([be, ke], axis=0)) for uv, be, ke in zip(uv_l, st_ops["be"], st_ops["ke"])]

    st_l = [state_ref[q] for q in range(n_pairs)]
    y_rows = []
    for m in range(nc):
        y_blocks = []
        for q in range(n_pairs):
            idx = m * n_pairs + q
            st = st_l[q]
            st_b = st.astype(BF16)
            y_s = _dot_nt(rhat_l[idx], st_b) + y0_l[idx]
            y_blocks.append(y_s[:c] + y_s[c:])
            decay = g_end[(m + 1) * c - 1:(m + 1) * c, q * LANES:(q + 1) * LANES]
            st_l[q] = st * decay + _dot(st_b, p_l[idx]) + q_l[idx]
        y_rows.append(jnp.concatenate(y_blocks, axis=-1))
    for q in range(n_pairs):
        state_ref[q] = st_l[q]
    y = jnp.concatenate(y_rows, axis=0)

    inv_n = 1.0 / n
    yc = y - head_sum(y) * inv_n
    var = head_sum(yc * yc) * inv_n
    yn = yc * lax.rsqrt(var + RWKV_GN_EPS) * lnw_ref[...] + lnb_ref[...]
    bonus = head_sum(r * k2 * rk_ref[...]) * v
    o_ref[...] = ((yn + bonus) * g).astype(BF16)


def _rwkv(p_rwkv, seq, mu, w0, w2, a0, a2, g2, k_k, k_a, r_k, ln_w, ln_b):
    t = p_rwkv.shape[0]
    ts = RWKV_TILE
    kern = functools.partial(_rwkv_kernel, tiles_per_seq=seq // ts)
    vec = lambda: _const_spec((1, RWKV_DIM))
    return pl.pallas_call(
        kern,
        grid=(t // ts,),
        in_specs=[_row_spec(ts, RWKV_COLS), _const_spec((1, RWKV_COLS)), vec(), _const_spec((64, RWKV_DIM)), vec(),
                  _const_spec((64, RWKV_DIM)), _const_spec((128, RWKV_DIM)), vec(), vec(), vec(), vec(), vec()],
        out_specs=_row_spec(ts, RWKV_DIM),
        out_shape=jax.ShapeDtypeStruct((t, RWKV_DIM), BF16),
        scratch_shapes=[pltpu.VMEM((1, RWKV_COLS), F32), pltpu.VMEM((RWKV_HEADS // 2, LANES, LANES), F32)],
        compiler_params=_params(),
        name="rwkv7",
    )(p_rwkv, mu, w0, w2, a0, a2, g2, k_k, k_a, r_k, ln_w, ln_b)


def _mla_prep_kernel(p_ref, pos_ref, freq_ref, qn_ref, kvn_ref, wq_ref, wqr_ref, wk_ref, wv_ref, q_out, k_out, v_out):
    p = p_ref[...]
    hq = _rms(p[:, 0:MLA_Q_LORA], qn_ref[...], NORM_EPS).astype(BF16)
    hkv = _rms(p[:, MLA_Q_LORA:MLA_Q_LORA + MLA_KV_LORA], kvn_ref[...], NORM_EPS).astype(BF16)
    blk = p[:, 384:512]
    ang = pos_ref[...].astype(F32) * freq_ref[...]
    cos = jnp.cos(ang)
    sin = jnp.sin(ang)
    scale = (MLA_NOPE + MLA_ROPE) ** -0.5 * LOG2E
    qa = _dot(hq, wq_ref[...])
    qr = _dot(hq, wqr_ref[...])
    lane = lax.broadcasted_iota(jnp.int32, blk.shape, 1)
    kr = pltpu.roll(blk, 64, 1)
    rot = jnp.where(lane < 80, -pltpu.roll(blk, 48, 1), pltpu.roll(blk, 80, 1))
    rot = jnp.where((lane >= 64) & (lane < 96), rot, 0.0)
    krope = kr * cos + rot * sin
    kn = _dot(hkv, wk_ref[...])
    for h in range(MLA_HEADS):
        hs = slice(h * LANES, (h + 1) * LANES)
        q_out[:, hs] = ((qa[:, hs] * cos + qr[:, hs] * sin) * scale).astype(BF16)
        k_out[:, hs] = (kn[:, hs] + krope).astype(BF16)
    vv = _dot(hkv, wv_ref[...])
    ones_col = (lane == MLA_V).astype(F32)
    for h in range(MLA_HEADS):
        hs = slice(h * LANES, (h + 1) * LANES)
        v_out[:, hs] = (vv[:, hs] + ones_col).astype(BF16)


def _mla_prep(p_mla, pos_col, freq, q_norm, kv_norm, wq, wqr, wk, wv):
    t = p_mla.shape[0]
    hw = MLA_HEADS * LANES
    return pl.pallas_call(
        _mla_prep_kernel,
        grid=(t // TM,),
        in_specs=[_row_spec(TM, MLA_PAD), _row_spec(TM, 1), _const_spec((1, LANES)), _const_spec((1, MLA_Q_LORA)),
                  _const_spec((1, MLA_KV_LORA)), _const_spec((MLA_Q_LORA, hw)), _const_spec((MLA_Q_LORA, hw)),
                  _const_spec((MLA_KV_LORA, hw)), _const_spec((MLA_KV_LORA, hw))],
        out_specs=[_row_spec(TM, hw), _row_spec(TM, hw), _row_spec(TM, hw)],
        out_shape=[jax.ShapeDtypeStruct((t, hw), BF16), jax.ShapeDtypeStruct((t, hw), BF16),
                   jax.ShapeDtypeStruct((t, hw), BF16)],
        compiler_params=_params(),
        name="mla_prep",
    )(p_mla, pos_col, freq, q_norm, kv_norm, wq, wqr, wk, wv)


def _softmax_stage(s_l, v_l, m_scr, acc_scr, idx):
    n = len(s_l)
    tk = s_l[0].shape[-1]
    m_old = [m_scr[c] for c in idx]
    m_new = [jnp.maximum(m, jnp.max(s, axis=-1, keepdims=True)) for m, s in zip(m_old, s_l)]
    p_l = [jnp.concatenate([jnp.exp2(s[:, o:o + LANES] - m) for o in range(0, tk, LANES)], axis=-1).astype(BF16)
           for s, m in zip(s_l, m_new)]
    alpha = [jnp.exp2(mo - mn) for mo, mn in zip(m_old, m_new)]
    pv_l = [_dot(p, v) for p, v in zip(p_l, v_l)]
    for c in range(n):
        m_scr[idx[c]] = m_new[c]
        w = acc_scr.shape[-1]
        a = alpha[c] if w == LANES else jnp.concatenate([alpha[c]] * (w // LANES), axis=-1)
        acc_scr[idx[c]] = a * acc_scr[idx[c]] + pv_l[c]


def _mla_attn_kernel(q_ref, k_ref, v_ref, o_ref, m_scr, acc_scr):
    tq = TQ
    i = pl.program_id(1)
    causal = lax.broadcasted_iota(jnp.int32, (tq, tq), 0) >= lax.broadcasted_iota(jnp.int32, (tq, tq), 1)
    lane = lax.broadcasted_iota(jnp.int32, (tq, LANES), 1)
    m_scr[...] = jnp.full(m_scr.shape, NEG, F32)
    acc_scr[...] = jnp.zeros(acc_scr.shape, F32)
    hsl = [slice(h * LANES, (h + 1) * LANES) for h in range(MLA_HEADS)]

    groups = [list(range(g, g + ATTN_GROUP)) for g in range(0, MLA_HEADS, ATTN_GROUP)]

    def tile(off, mask):
        def scores(grp):
            s_l = [_dot_nt(q_ref[:, hsl[h]], k_ref[pl.ds(off, tq), hsl[h]]) for h in grp]
            return [jnp.where(causal, s, NEG) for s in s_l] if mask else s_l

        nxt = scores(groups[0])
        for gi, grp in enumerate(groups):
            s_l = nxt
            if gi + 1 < len(groups):
                nxt = scores(groups[gi + 1])
            _softmax_stage(s_l, [v_ref[pl.ds(off, tq), hsl[h]] for h in grp], m_scr, acc_scr, grp)

    def body(j, carry):
        tile(pl.multiple_of(j * tq, tq), False)
        return carry

    lax.fori_loop(0, i, body, 0)
    tile(pl.multiple_of(i * tq, tq), True)
    for pr in range(MLA_HEADS // 2):
        o0 = acc_scr[2 * pr] / acc_scr[2 * pr][:, MLA_V:MLA_V + 1]
        o1 = acc_scr[2 * pr + 1] / acc_scr[2 * pr + 1][:, MLA_V:MLA_V + 1]
        o_ref[:, pr * LANES:(pr + 1) * LANES] = jnp.where(lane < MLA_V, o0, pltpu.roll(o1, MLA_V, 1)).astype(BF16)


def _mla_attn(q, k, v, batch, seq):
    t = q.shape[0]
    nq = seq // TQ
    hw = MLA_HEADS * LANES
    vw = MLA_HEADS * MLA_V
    seq_spec = pl.BlockSpec((seq, hw), lambda b, i: (b, 0), pipeline_mode=pl.Buffered(1))
    return pl.pallas_call(
        _mla_attn_kernel,
        grid=(batch, nq),
        in_specs=[pl.BlockSpec((TQ, hw), lambda b, i: (b * nq + i, 0)), seq_spec, seq_spec],
        out_specs=pl.BlockSpec((TQ, vw), lambda b, i: (b * nq + i, 0)),
        out_shape=jax.ShapeDtypeStruct((t, vw), BF16),
        scratch_shapes=[pltpu.VMEM((MLA_HEADS, TQ, LANES), F32), pltpu.VMEM((MLA_HEADS, TQ, LANES), F32)],
        compiler_params=_params2(),
        name="mla_attn",
    )(q, k, v)


def _t5_bucket_table():
    n = np.arange(0, REL_MAX_DISTANCE + 1)
    max_exact = REL_BUCKETS // 2
    nf = np.maximum(n, 1).astype(np.float32)
    ratio = np.log(nf / np.float32(max_exact)) / np.float32(math.log(REL_MAX_DISTANCE / max_exact))
    large = max_exact + (ratio * np.float32(REL_BUCKETS - max_exact)).astype(np.int32)
    large = np.minimum(large, REL_BUCKETS - 1)
    return np.where(n < max_exact, n, large)


_BUCKETS = _t5_bucket_table()
_FAR_BUCKET = int(_BUCKETS[REL_MAX_DISTANCE])
assert _FAR_BUCKET == REL_BUCKETS - 1 and np.all(np.diff(_BUCKETS) >= 0)
_BUCKET_STARTS = [int(np.argmax(_BUCKETS >= b)) for b in range(REL_BUCKETS // 2 + 1, REL_BUCKETS)]


def _diff_attn_kernel(rb_ref, lam_ref, subln_ref, q_ref, k_ref, v_ref, o_ref, bias_scr, m_scr, acc_scr, *, lambda_init):
    tq = TQ
    i = pl.program_id(1)
    r_i = lax.broadcasted_iota(jnp.int32, (tq, tq), 0)
    c_i = lax.broadcasted_iota(jnp.int32, (tq, tq), 1)
    causal = r_i >= c_i

    @pl.when((pl.program_id(0) == 0) & (i == 0))
    def _():
        for t_idx, delta in enumerate((0, tq)):
            d = jnp.maximum(r_i - c_i + delta, 0)
            log_b = REL_BUCKETS // 2
            for start in _BUCKET_STARTS:
                log_b = log_b + (d >= start).astype(jnp.int32)
            bucket = jnp.where(d < REL_BUCKETS // 2, d, log_b)
            for h in range(DIFF_HEADS):
                bias = jnp.zeros((tq, tq), F32)
                for b in range(REL_BUCKETS):
                    bias = jnp.where(bucket == b, rb_ref[b, h], bias)
                bias_scr[t_idx, h] = (bias - rb_ref[_FAR_BUCKET, h]) * LOG2E

    m_scr[...] = jnp.full(m_scr.shape, NEG, F32)
    acc_scr[...] = jnp.zeros(acc_scr.shape, F32)
    lane = lax.broadcasted_iota(jnp.int32, (tq, LANES), 1)
    ones_blk = (lane == 0).astype(BF16)
    hsl = [slice(h * LANES, (h + 1) * LANES) for h in range(DIFF_HEADS)]
    q_l = []
    for hs in hsl:
        qp = q_ref[:, hs]
        q_l.append(jnp.concatenate([jnp.where(lane < DIFF_QK, qp, jnp.zeros_like(qp)),
                                    jnp.where(lane >= DIFF_QK, qp, jnp.zeros_like(qp))], axis=0))

    def tile(off, bias_idx, mask):
        s_l = [_dot_nt(q_l[h], k_ref[pl.ds(off, tq), hsl[h]]) for h in range(DIFF_HEADS)]
        if bias_idx is not None:
            s_l = [s + jnp.concatenate([bias_scr[bias_idx, h]] * 2, axis=0) for h, s in enumerate(s_l)]
        if mask:
            mask2 = jnp.concatenate([causal, causal], axis=0)
            s_l = [jnp.where(mask2, s, NEG) for s in s_l]
        v_l = [jnp.concatenate([v_ref[pl.ds(off, tq), hs], ones_blk], axis=-1) for hs in hsl]
        _softmax_stage(s_l, v_l, m_scr, acc_scr, list(range(DIFF_HEADS)))

    def far_body(j, carry):
        tile(pl.multiple_of(j * tq, tq), None, False)
        return carry

    def near_body(j, carry):
        tile(pl.multiple_of(j * tq, tq), 1, False)
        return carry

    n_far = jnp.maximum(i - 1, 0)
    lax.fori_loop(0, n_far, far_body, 0)
    lax.fori_loop(n_far, i, near_body, 0)
    tile(pl.multiple_of(i * tq, tq), 0, True)

    lam = lam_ref[...]
    lam_full = (jnp.exp(jnp.sum(lam[0:1] * lam[1:2], axis=-1, keepdims=True))
                - jnp.exp(jnp.sum(lam[2:3] * lam[3:4], axis=-1, keepdims=True)) + lambda_init)
    for h in range(DIFF_HEADS):
        acc = acc_scr[h]
        on = acc[:, :DIFF_V] / acc[:, DIFF_V:DIFF_V + 1]
        o = on[:tq] - lam_full * on[tq:]
        o = _rms(o, subln_ref[...], SUBLN_EPS) * (1.0 - lambda_init)
        o_ref[:, hsl[h]] = o.astype(BF16)


def _diff_attn(p_diff, rel_bias, lam, subln, batch, seq, layer_idx):
    t = p_diff.shape[0]
    nq = seq // TQ
    w = DIFF_HEADS * LANES
    lambda_init = 0.8 - 0.6 * math.exp(-0.3 * layer_idx)
    kern = functools.partial(_diff_attn_kernel, lambda_init=lambda_init)
    const2 = lambda shape: pl.BlockSpec(shape, lambda b, i: (0, 0))
    return pl.pallas_call(
        kern,
        grid=(batch, nq),
        in_specs=[pl.BlockSpec(memory_space=pltpu.SMEM),
                  const2((4, DIFF_QK)), const2((1, DIFF_V)),
                  pl.BlockSpec((TQ, w), lambda b, i: (b * nq + i, 0)),
                  pl.BlockSpec((seq, w), lambda b, i: (b, 1), pipeline_mode=pl.Buffered(1)),
                  pl.BlockSpec((seq, w), lambda b, i: (b, 2), pipeline_mode=pl.Buffered(1))],
        out_specs=pl.BlockSpec((TQ, w), lambda b, i: (b * nq + i, 0)),
        out_shape=jax.ShapeDtypeStruct((t, w), BF16),
        scratch_shapes=[pltpu.VMEM((2, DIFF_HEADS, TQ, TQ), F32), pltpu.VMEM((DIFF_HEADS, 2 * TQ, LANES), F32),
                        pltpu.VMEM((DIFF_HEADS, 2 * TQ, 2 * LANES), F32)],
        compiler_params=_params2(),
        name="diff_attn",
    )(rel_bias, lam, subln, p_diff, p_diff, p_diff)


def _merge_kernel(x_ref, pg_ref, bg_ref, or_ref, om_ref, od_ref, wr_ref, wm_ref, wd_ref, wo_ref, gf_ref, x_out, h_out):
    def gate(idx):
        cs = slice(idx * D_MODEL, (idx + 1) * D_MODEL)
        return jax.nn.sigmoid(pg_ref[:, cs] + bg_ref[:, cs])

    merged = gate(0) * _dot(or_ref[...], wr_ref[...])
    merged = merged + gate(1) * _dot(om_ref[...], wm_ref[...])
    merged = merged + gate(2) * _dot(od_ref[...], wd_ref[...])
    x1 = x_ref[...] + _dot(merged.astype(BF16), wo_ref[...])
    x_out[...] = x1
    h_out[...] = _rms(x1, gf_ref[...], NORM_EPS).astype(BF16)


def _merge(x, p_gate, b_gate, o_r, o_m, o_d, w_r, w_m, w_d, w_o, g_ffn):
    t = x.shape[0]
    return pl.pallas_call(
        _merge_kernel,
        grid=(t // TM,),
        in_specs=[_row_spec(TM, D_MODEL), _row_spec(TM, GATE_COLS), _const_spec((1, GATE_COLS)),
                  _row_spec(TM, 512), _row_spec(TM, 512), _row_spec(TM, 512),
                  _const_spec((512, D_MODEL)), _const_spec((512, D_MODEL)), _const_spec((512, D_MODEL)),
                  _const_spec((D_MODEL, D_MODEL)), _const_spec((1, D_MODEL))],
        out_specs=[_row_spec(TM, D_MODEL), _row_spec(TM, D_MODEL)],
        out_shape=[jax.ShapeDtypeStruct((t, D_MODEL), F32), jax.ShapeDtypeStruct((t, D_MODEL), BF16)],
        compiler_params=_params(),
        name="merge",
    )(x, p_gate, b_gate, o_r, o_m, o_d, w_r, w_m, w_d, w_o, g_ffn)


def _ffn_kernel(x_ref, h_ref, wup_ref, cw_ref, cb_ref, wdn_ref, gfin_ref, o_ref, carry_ref, *, tiles_per_seq, final_norm):
    tm = TM

    @pl.when(pl.program_id(0) % tiles_per_seq == 0)
    def _():
        carry_ref[...] = jnp.zeros_like(carry_ref)

    h = h_ref[...]

    def conv(u, cols):
        ext = jnp.concatenate([carry_ref[:, cols], u], axis=0)
        carry_ref[:, cols] = u[tm - 8:tm, :]
        b = cw_ref[1:2, cols] * ext + pltpu.roll(cw_ref[0:1, cols] * ext, 1, 0)
        out = cw_ref[2:3, cols] * ext + pltpu.roll(b, 1, 0) + cb_ref[:, cols]
        return out[8:, :]

    def up(ck):
        gc = slice(ck * FF_CHUNK, (ck + 1) * FF_CHUNK)
        vc = slice(D_FF + ck * FF_CHUNK, D_FF + (ck + 1) * FF_CHUNK)
        return _dot(h, wup_ref[:, gc]), _dot(h, wup_ref[:, vc])

    n_chunks = D_FF // FF_CHUNK
    acc = x_ref[...]
    nxt = up(0)
    for ck in range(n_chunks):
        ug, uv = nxt
        if ck + 1 < n_chunks:
            nxt = up(ck + 1)
        gc = slice(ck * FF_CHUNK, (ck + 1) * FF_CHUNK)
        vc = slice(D_FF + ck * FF_CHUNK, D_FF + (ck + 1) * FF_CHUNK)
        gate = conv(ug, gc)
        val = conv(uv, vc)
        act = (gate * jax.nn.sigmoid(gate) * val).astype(BF16)
        acc = acc + _dot(act, wdn_ref[gc, :])
    if final_norm:
        acc = _rms(acc, gfin_ref[...], NORM_EPS)
    o_ref[...] = acc


def _ffn(x1, h2, w_up, conv_w, conv_b, w_down, g_final, seq, final_norm):
    t = x1.shape[0]
    kern = functools.partial(_ffn_kernel, tiles_per_seq=seq // TM, final_norm=final_norm)
    return pl.pallas_call(
        kern,
        grid=(t // TM,),
        in_specs=[_row_spec(TM, D_MODEL), _row_spec(TM, D_MODEL), _const_spec((D_MODEL, 2 * D_FF)),
                  _const_spec((3, 2 * D_FF)), _const_spec((1, 2 * D_FF)), _const_spec((D_FF, D_MODEL)),
                  _const_spec((1, D_MODEL))],
        out_specs=_row_spec(TM, D_MODEL),
        out_shape=jax.ShapeDtypeStruct((t, D_MODEL), F32),
        scratch_shapes=[pltpu.VMEM((8, 2 * D_FF), F32)],
        compiler_params=_params(),
        name="conv_ffn",
    )(x1, h2, w_up, conv_w, conv_b, w_down, g_final)


def _mla_weights(w_uq, w_ukv):
    qd = MLA_NOPE + MLA_ROPE
    half = MLA_ROPE // 2
    wq = w_uq.reshape(MLA_Q_LORA, MLA_HEADS, qd)
    zq = jnp.zeros((MLA_Q_LORA, MLA_HEADS, LANES - qd), F32)
    wq_main = jnp.concatenate([wq, zq], axis=-1)
    x1 = wq[:, :, MLA_NOPE:MLA_NOPE + half]
    x2 = wq[:, :, MLA_NOPE + half:]
    wq_rot = jnp.concatenate([jnp.zeros((MLA_Q_LORA, MLA_HEADS, MLA_NOPE), F32), -x2, x1, zq], axis=-1)
    wkv = w_ukv.reshape(MLA_KV_LORA, MLA_HEADS, MLA_NOPE + MLA_V)
    zkv = jnp.zeros((MLA_KV_LORA, MLA_HEADS, LANES - MLA_NOPE), F32)
    wk = jnp.concatenate([wkv[:, :, :MLA_NOPE], zkv], axis=-1)
    wv = jnp.concatenate([wkv[:, :, MLA_NOPE:], zkv], axis=-1)
    flat = lambda w: w.reshape(w.shape[0], -1).astype(BF16)
    return flat(wq_main), flat(wq_rot), flat(wk), flat(wv)


def kernel(x, positions, rel_bias, norm_mix, w_in, b_gate, rwkv_mu, rwkv_w0, rwkv_w2, rwkv_a0, rwkv_a2, rwkv_g2, rwkv_k_k, rwkv_k_a, rwkv_r_k, rwkv_ln_w, rwkv_ln_b, mla_q_norm, mla_w_uq, mla_kv_norm, mla_w_ukv, diff_lambda, diff_subln, w_branch_rwkv, w_branch_mla, w_branch_diff, w_o, norm_ffn, ffn_w_up, ffn_conv_w, ffn_conv_b, ffn_w_down, norm_final):
    batch, seq, _ = x.shape
    depth = w_in.shape[0]
    t = batch * seq
    assert seq % TQ == 0 and seq % TM == 0 and seq % RWKV_TILE == 0 and RWKV_TILE % CHUNK == 0
    xf = x.reshape(t, D_MODEL)
    pos_col = positions.reshape(t, 1)
    inv_freq = ROPE_BASE ** (-jnp.arange(0, MLA_ROPE, 2, dtype=F32) / MLA_ROPE)
    freq = jnp.concatenate([jnp.zeros((MLA_NOPE,), F32), inv_freq, inv_freq,
                            jnp.zeros((LANES - MLA_NOPE - MLA_ROPE,), F32)]).reshape(1, LANES)
    row = lambda v: v.reshape(1, -1)
    diff_scale = jnp.concatenate([jnp.full((512,), DIFF_QK ** -0.5 * LOG2E, F32), jnp.ones((DIFF_COLS - 512,), F32)])

    for l in range(depth):
        s0, s1, s2 = RWKV_COLS, RWKV_COLS + MLA_COLS, RWKV_COLS + MLA_COLS + DIFF_COLS
        w = w_in[l]
        w_all = jnp.concatenate([w[:, :s0], w[:, s0:s1], jnp.zeros((D_MODEL, MLA_PAD - MLA_COLS), F32),
                                 w[:, s1:s2] * diff_scale, w[:, s2:]], axis=1).astype(BF16)
        p_rwkv, p_mla, p_diff, p_gate = _in_proj(xf, row(norm_mix[l]), w_all)

        o_r = _rwkv(p_rwkv, seq, row(rwkv_mu[l]), row(rwkv_w0[l]), rwkv_w2[l].astype(BF16), row(rwkv_a0[l]),
                    rwkv_a2[l].astype(BF16), rwkv_g2[l].astype(BF16), row(rwkv_k_k[l]), row(rwkv_k_a[l]),
                    row(rwkv_r_k[l]), row(rwkv_ln_w[l]), row(rwkv_ln_b[l]))

        wq, wqr, wk, wv = _mla_weights(mla_w_uq[l], mla_w_ukv[l])
        q_m, k_m, v_m = _mla_prep(p_mla, pos_col, freq, row(mla_q_norm[l]), row(mla_kv_norm[l]), wq, wqr, wk, wv)
        o_m = _mla_attn(q_m, k_m, v_m, batch, seq)

        o_d = _diff_attn(p_diff, rel_bias, diff_lambda[l], row(diff_subln[l]), batch, seq, l)

        x1, h2 = _merge(xf, p_gate, row(b_gate[l]), o_r, o_m, o_d, w_branch_rwkv[l].astype(BF16),
                        w_branch_mla[l].astype(BF16), w_branch_diff[l].astype(BF16), w_o[l].astype(BF16),
                        row(norm_ffn[l]))
        xf = _ffn(x1, h2, ffn_w_up[l].astype(BF16), ffn_conv_w[l], row(ffn_conv_b[l]), ffn_w_down[l].astype(BF16),
                  row(norm_final), seq, final_norm=(l == depth - 1))
    return xf.reshape(batch, seq, D_MODEL)
```

```python
import functools
import math

import numpy as np
import jax
import jax.numpy as jnp
from jax import lax
from jax.experimental import pallas as pl
from jax.experimental.pallas import tpu as pltpu

F32 = jnp.float32
BF16 = jnp.bfloat16

D_MODEL = 1024
RWKV_HEADS = 8
RWKV_N = 64
RWKV_DIM = 512
RWKV_COLS = 1792
RWKV_GN_EPS = 64e-5
MLA_HEADS = 8
MLA_Q_LORA = 256
MLA_KV_LORA = 128
MLA_NOPE = 64
MLA_ROPE = 32
MLA_V = 64
MLA_COLS = 416
MLA_PAD = 512
ROPE_BASE = 10000.0
DIFF_HEADS = 4
DIFF_QK = 64
DIFF_V = 128
DIFF_COLS = 1536
DIFF_QK_COLS = 1024
DIFF_VT_ROWS = DIFF_V + 16
REL_BUCKETS = 32
REL_MAX_DISTANCE = 128
D_FF = 2816
GATE_COLS = 3072
NORM_EPS = 1e-6
SUBLN_EPS = 1e-5

LANES = 128
VMEM_LIMIT = 56 * 1024 * 1024
TM = 256
CHUNK = 64
RWKV_TILE = 256
TQ = 256
FF_CHUNK = 256
NEG = -1e30
LOG2E = 1.4426950408889634

_NT = (((1,), (1,)), ((), ()))
_TN = (((0,), (0,)), ((), ()))


def _dot(a, b):
    return jnp.dot(a, b, preferred_element_type=F32)


def _dot_nt(a, b):
    return lax.dot_general(a, b, _NT, preferred_element_type=F32)


def _dot_tn(a, b):
    return lax.dot_general(a, b, _TN, preferred_element_type=F32)


def _rms(x, g, eps):
    return x * lax.rsqrt(jnp.mean(x * x, axis=-1, keepdims=True) + eps) * g


def _params():
    return pltpu.CompilerParams(dimension_semantics=("arbitrary",), vmem_limit_bytes=VMEM_LIMIT)


def _params2():
    return pltpu.CompilerParams(dimension_semantics=("arbitrary", "arbitrary"), vmem_limit_bytes=VMEM_LIMIT)


def _const_spec(shape):
    return pl.BlockSpec(shape, lambda *_: (0,) * len(shape), pipeline_mode=pl.Buffered(1))


def _row_spec(tm, cols, col_block=0):
    return pl.BlockSpec((tm, cols), lambda i: (i, col_block))


def _in_proj_kernel(x_ref, g_ref, w_ref, wvt_ref, pr_ref, pm_ref, pqk_ref, pvt_ref, pg_ref):
    h = _rms(x_ref[...], g_ref[...], NORM_EPS).astype(BF16)
    c0 = 0
    pr_ref[...] = _dot(h, w_ref[:, c0:c0 + RWKV_COLS])
    c0 += RWKV_COLS
    pm_ref[...] = _dot(h, w_ref[:, c0:c0 + MLA_PAD])
    c0 += MLA_PAD
    pqk_ref[...] = _dot(h, w_ref[:, c0:c0 + DIFF_QK_COLS]).astype(BF16)
    c0 += DIFF_QK_COLS
    pg_ref[...] = _dot(h, w_ref[:, c0:c0 + GATE_COLS])
    pvt_ref[...] = _dot_nt(wvt_ref[...], h).astype(BF16)


def _in_proj(x, g, w_all, w_vt):
    t = x.shape[0]
    n_all = w_all.shape[1]
    vw = DIFF_HEADS * DIFF_V
    return pl.pallas_call(
        _in_proj_kernel,
        grid=(t // TM,),
        in_specs=[_row_spec(TM, D_MODEL), _const_spec((1, D_MODEL)), _const_spec((D_MODEL, n_all)),
                  _const_spec((vw, D_MODEL))],
        out_specs=[_row_spec(TM, RWKV_COLS), _row_spec(TM, MLA_PAD), _row_spec(TM, DIFF_QK_COLS),
                   pl.BlockSpec((vw, TM), lambda i: (0, i)), _row_spec(TM, GATE_COLS)],
        out_shape=[jax.ShapeDtypeStruct((t, RWKV_COLS), F32), jax.ShapeDtypeStruct((t, MLA_PAD), F32),
                   jax.ShapeDtypeStruct((t, DIFF_QK_COLS), BF16), jax.ShapeDtypeStruct((vw, t), BF16),
                   jax.ShapeDtypeStruct((t, GATE_COLS), F32)],
        compiler_params=_params(),
        name="in_proj",
    )(x, g, w_all, w_vt)


def _split3(a):
    hi = a.astype(BF16)
    r1 = a - hi.astype(F32)
    mid = r1.astype(BF16)
    lo = (r1 - mid.astype(F32)).astype(BF16)
    return hi, mid, lo


def _softplus(z):
    return jnp.maximum(z, 0.0) + jnp.log(1.0 + jnp.exp(-jnp.abs(z)))


def _rwkv_kernel(p_ref, mu_ref, w0_ref, w2_ref, a0_ref, a2_ref, g2_ref, kk_ref, ka_ref, rk_ref, lnw_ref, lnb_ref,
                 o_ref, carry_ref, state_ref, *, tiles_per_seq):
    c = CHUNK
    n = RWKV_N
    ts = RWKV_TILE
    nc = ts // c

    @pl.when(pl.program_id(0) % tiles_per_seq == 0)
    def _():
        carry_ref[...] = jnp.zeros_like(carry_ref)
        state_ref[...] = jnp.zeros_like(state_ref)

    p = p_ref[...]
    row = lax.broadcasted_iota(jnp.int32, (ts, 1), 0)
    shifted = jnp.where(row == 0, carry_ref[...], pltpu.roll(p, 1, 0))
    carry_ref[...] = p[ts - 1:ts, :]
    pm = p + (shifted - p) * mu_ref[...]

    r = pm[:, 0:512]
    k = pm[:, 512:1024]
    v = pm[:, 1024:1536]
    pw = pm[:, 1536:1600]
    pa = pm[:, 1600:1664]
    pg = pm[:, 1664:1792]

    w_log = -_softplus(-(w0_ref[...] + _dot(jnp.tanh(pw).astype(BF16), w2_ref[...]))) - 0.5
    logd = -jnp.exp(w_log)
    a = jax.nn.sigmoid(a0_ref[...] + _dot(pa.astype(BF16), a2_ref[...]))
    g = _dot(jax.nn.sigmoid(pg).astype(BF16), g2_ref[...])

    tr = lax.broadcasted_iota(jnp.int32, (ts, ts), 0)
    tc = lax.broadcasted_iota(jnp.int32, (ts, ts), 1)
    tri = ((tr >= tc) & (tr // c == tc // c)).astype(BF16)
    hi, mid, lo = _split3(logd)
    cs = _dot(tri, hi) + _dot(tri, mid) + _dot(tri, lo)
    total = jnp.concatenate([jnp.broadcast_to(cs[(m + 1) * c - 1:(m + 1) * c, :], (c, RWKV_DIM)) for m in range(nc)], axis=0)
    e_in = jnp.exp(cs)
    e_ex = jnp.exp(cs - logd)
    e_inv = jnp.exp(-cs)
    e_end = jnp.exp(total - cs)
    g_end = jnp.exp(total)

    n_pairs = RWKV_HEADS // 2
    lo_t = lax.broadcasted_iota(jnp.int32, (ts, LANES), 1) < n
    lo_c = lax.broadcasted_iota(jnp.int32, (c, LANES), 1) < n

    def head_sum(x):
        blocks = []
        for q in range(n_pairs):
            xb = x[:, q * LANES:(q + 1) * LANES]
            s_lo = jnp.sum(jnp.where(lo_t, xb, 0.0), axis=-1, keepdims=True)
            s_hi = jnp.sum(jnp.where(lo_t, 0.0, xb), axis=-1, keepdims=True)
            blocks.append(jnp.where(lo_t, s_lo, s_hi))
        return jnp.concatenate(blocks, axis=-1)

    k2 = k * (1.0 + (a - 1.0) * ka_ref[...])
    kku = k * kk_ref[...]
    kkn = kku / jnp.maximum(jnp.sqrt(head_sum(kku * kku)), 1e-12)
    b = kkn * a
    full = {"at": -kkn * e_ex, "rt": r * e_in, "bt": b * e_inv, "kt": k2 * e_inv, "v": v,
            "be": b * e_end, "ke": k2 * e_end}

    ti = lax.broadcasted_iota(jnp.int32, (2 * c, 4 * c), 0) % c
    si = lax.broadcasted_iota(jnp.int32, (2 * c, 4 * c), 1) % c
    strict = ti > si
    incl = ti >= si
    eye = (lax.broadcasted_iota(jnp.int32, (2 * c, 2 * c), 0)
           == lax.broadcasted_iota(jnp.int32, (2 * c, 2 * c), 1)).astype(F32)

    items = [(m, q) for m in range(nc) for q in range(n_pairs)]

    def stacked(name, m, q):
        xb = full[name][m * c:(m + 1) * c, q * LANES:(q + 1) * LANES]
        return jnp.concatenate([jnp.where(lo_c, xb, 0.0), jnp.where(lo_c, 0.0, xb)], axis=0).astype(BF16)

    st_ops = {name: [stacked(name, m, q) for m, q in items] for name in full}
    big_l = [_dot_nt(jnp.concatenate([at, rtb], axis=0), jnp.concatenate([bt, kt], axis=0))
             for at, rtb, bt, kt in zip(st_ops["at"], st_ops["rt"], st_ops["bt"], st_ops["kt"])]
    top_l = [jnp.where(strict, big[:2 * c, :], 0.0) for big in big_l]
    bot_l = [jnp.where(incl, big[2 * c:, :], 0.0).astype(BF16) for big in big_l]
    lakv_l = [_dot(top[:, 2 * c:].astype(BF16), vb) for top, vb in zip(top_l, st_ops["v"])]

    x_l = [top[:, :2 * c] for top in top_l]
    tinv_l = [eye + x for x in x_l]
    xb_l = [x.astype(BF16) for x in x_l]
    for _ in range(int(math.log2(c)) - 1):
        x_l = [_dot(xb, xb) for xb in xb_l]
        xb_l = [x.astype(BF16) for x in x_l]
        tinv_l = [tinv + _dot(tinv.astype(BF16), xb) for tinv, xb in zip(tinv_l, xb_l)]
    tinvb_l = [tinv.astype(BF16) for tinv in tinv_l]
    abar_l = [_dot(tb, at).astype(BF16) for tb, at in zip(tinvb_l, st_ops["at"])]
    vbar_l = [_dot(tb, lv.astype(BF16)).astype(BF16) for tb, lv in zip(tinvb_l, lakv_l)]
    rhat_l = [(rtb.astype(F32) + _dot(bot[:, :2 * c], ab)).astype(BF16)
              for rtb, bot, ab in zip(st_ops["rt"], bot_l, abar_l)]
    uv_l = [jnp.concatenate([vbar, vb], axis=0) for vbar, vb in zip(vbar_l, st_ops["v"])]
    y0_l = [_dot(bot, uv) for bot, uv in zip(bot_l, uv_l)]
    p_l = [_dot_tn(ab, be).astype(BF16) for ab, be in zip(abar_l, st_ops["be"])]
    q_l = [_dot_tn(uv, jnp.concatenate([be, ke], axis=0)) for uv, be, ke in zip(uv_l, st_ops["be"], st_ops["ke"])]

    st_l = [state_ref[q] for q in range(n_pairs)]
    y_rows = []
    for m in range(nc):
        y_blocks = []
        for q in range(n_pairs):
            idx = m * n_pairs + q
            st = st_l[q]
            st_b = st.astype(BF16)
            y_s = _dot_nt(rhat_l[idx], st_b) + y0_l[idx]
            y_blocks.append(y_s[:c] + y_s[c:])
            decay = g_end[(m + 1) * c - 1:(m + 1) * c, q * LANES:(q + 1) * LANES]
            st_l[q] = st * decay + _dot(st_b, p_l[idx]) + q_l[idx]
        y_rows.append(jnp.concatenate(y_blocks, axis=-1))
    for q in range(n_pairs):
        state_ref[q] = st_l[q]
    y = jnp.concatenate(y_rows, axis=0)

    inv_n = 1.0 / n
    yc = y - head_sum(y) * inv_n
    var = head_sum(yc * yc) * inv_n
    yn = yc * lax.rsqrt(var + RWKV_GN_EPS) * lnw_ref[...] + lnb_ref[...]
    bonus = head_sum(r * k2 * rk_ref[...]) * v
    o_ref[...] = ((yn + bonus) * g).astype(BF16)


def _rwkv(p_rwkv, seq, mu, w0, w2, a0, a2, g2, k_k, k_a, r_k, ln_w, ln_b):
    t = p_rwkv.shape[0]
    ts = RWKV_TILE
    kern = functools.partial(_rwkv_kernel, tiles_per_seq=seq // ts)
    vec = lambda: _const_spec((1, RWKV_DIM))
    return pl.pallas_call(
        kern,
        grid=(t // ts,),
        in_specs=[_row_spec(ts, RWKV_COLS), _const_spec((1, RWKV_COLS)), vec(), _const_spec((64, RWKV_DIM)), vec(),
                  _const_spec((64, RWKV_DIM)), _const_spec((128, RWKV_DIM)), vec(), vec(), vec(), vec(), vec()],
        out_specs=_row_spec(ts, RWKV_DIM),
        out_shape=jax.ShapeDtypeStruct((t, RWKV_DIM), BF16),
        scratch_shapes=[pltpu.VMEM((1, RWKV_COLS), F32), pltpu.VMEM((RWKV_HEADS // 2, LANES, LANES), F32)],
        compiler_params=_params(),
        name="rwkv7",
    )(p_rwkv, mu, w0, w2, a0, a2, g2, k_k, k_a, r_k, ln_w, ln_b)


def _mla_prep_kernel(p_ref, pos_ref, freq_ref, qn_ref, kvn_ref, wq_ref, wqr_ref, wk_ref, wv_ref, q_out, k_out, v_out):
    p = p_ref[...]
    hq = _rms(p[:, 0:MLA_Q_LORA], qn_ref[...], NORM_EPS).astype(BF16)
    hkv = _rms(p[:, MLA_Q_LORA:MLA_Q_LORA + MLA_KV_LORA], kvn_ref[...], NORM_EPS).astype(BF16)
    blk = p[:, 384:512]
    ang = pos_ref[...].astype(F32) * freq_ref[...]
    cos = jnp.cos(ang)
    sin = jnp.sin(ang)
    scale = (MLA_NOPE + MLA_ROPE) ** -0.5 * LOG2E
    qa = _dot(hq, wq_ref[...])
    qr = _dot(hq, wqr_ref[...])
    lane = lax.broadcasted_iota(jnp.int32, blk.shape, 1)
    kr = pltpu.roll(blk, 64, 1)
    rot = jnp.where(lane < 80, -pltpu.roll(blk, 48, 1), pltpu.roll(blk, 80, 1))
    rot = jnp.where((lane >= 64) & (lane < 96), rot, 0.0)
    krope = kr * cos + rot * sin
    kn = _dot(hkv, wk_ref[...])
    for h in range(MLA_HEADS):
        hs = slice(h * LANES, (h + 1) * LANES)
        q_out[:, hs] = ((qa[:, hs] * cos + qr[:, hs] * sin) * scale).astype(BF16)
        k_out[:, hs] = (kn[:, hs] + krope).astype(BF16)
    vt = _dot_nt(wv_ref[...], hkv)
    rowi = lax.broadcasted_iota(jnp.int32, vt.shape, 0)
    v_out[...] = (vt + (rowi % LANES == MLA_V).astype(F32)).astype(BF16)


def _mla_prep(p_mla, pos_col, freq, q_norm, kv_norm, wq, wqr, wk, wv):
    t = p_mla.shape[0]
    hw = MLA_HEADS * LANES
    return pl.pallas_call(
        _mla_prep_kernel,
        grid=(t // TM,),
        in_specs=[_row_spec(TM, MLA_PAD), _row_spec(TM, 1), _const_spec((1, LANES)), _const_spec((1, MLA_Q_LORA)),
                  _const_spec((1, MLA_KV_LORA)), _const_spec((MLA_Q_LORA, hw)), _const_spec((MLA_Q_LORA, hw)),
                  _const_spec((MLA_KV_LORA, hw)), _const_spec((hw, MLA_KV_LORA))],
        out_specs=[_row_spec(TM, hw), _row_spec(TM, hw), pl.BlockSpec((hw, TM), lambda i: (0, i))],
        out_shape=[jax.ShapeDtypeStruct((t, hw), BF16), jax.ShapeDtypeStruct((t, hw), BF16),
                   jax.ShapeDtypeStruct((hw, t), BF16)],
        compiler_params=_params(),
        name="mla_prep",
    )(p_mla, pos_col, freq, q_norm, kv_norm, wq, wqr, wk, wv)


def _softmax_stage_t(s_l, pv_fn, m_scr, acc_scr, idx):
    n = len(s_l)
    m_old = [m_scr[c][0:1, :] for c in idx]
    m_new = [jnp.maximum(m, jnp.max(s, axis=0, keepdims=True)) for m, s in zip(m_old, s_l)]
    p_l = [jnp.exp2(s - m).astype(BF16) for s, m in zip(s_l, m_new)]
    alpha = [jnp.exp2(mo - mn) for mo, mn in zip(m_old, m_new)]
    pv_l = [pv_fn(c, p_l[c]) for c in range(n)]
    for c in range(n):
        m_scr[idx[c]] = jnp.broadcast_to(m_new[c], m_scr.shape[1:])
        acc_scr[idx[c]] = alpha[c] * acc_scr[idx[c]] + pv_l[c]


def _mla_attn_kernel(q_ref, k_ref, vt_ref, o_ref, m_scr, acc_scr):
    tq = TQ
    i = pl.program_id(1)
    causal = lax.broadcasted_iota(jnp.int32, (tq, tq), 0) <= lax.broadcasted_iota(jnp.int32, (tq, tq), 1)
    lane = lax.broadcasted_iota(jnp.int32, (tq, LANES), 1)
    m_scr[...] = jnp.full(m_scr.shape, NEG, F32)
    acc_scr[...] = jnp.zeros(acc_scr.shape, F32)
    hsl = [slice(h * LANES, (h + 1) * LANES) for h in range(MLA_HEADS)]
    heads = list(range(MLA_HEADS))

    def tile(off, mask):
        s_l = [_dot_nt(k_ref[pl.ds(off, tq), hs], q_ref[:, hs]) for hs in hsl]
        if mask:
            s_l = [jnp.where(causal, s, NEG) for s in s_l]
        _softmax_stage_t(s_l, lambda c, p: _dot(vt_ref[hsl[c], pl.ds(off, tq)], p), m_scr, acc_scr, heads)

    def body(j, carry):
        tile(pl.multiple_of(j * tq, tq), False)
        return carry

    lax.fori_loop(0, i, body, 0)
    tile(pl.multiple_of(i * tq, tq), True)
    for pr in range(MLA_HEADS // 2):
        outs = []
        for hh in range(2):
            acc = acc_scr[2 * pr + hh]
            outs.append(jnp.transpose(acc / acc[MLA_V:MLA_V + 1, :]))
        o_ref[:, pr * LANES:(pr + 1) * LANES] = jnp.where(lane < MLA_V, outs[0], pltpu.roll(outs[1], MLA_V, 1)).astype(BF16)


def _mla_attn(q, k, vt, batch, seq):
    t = q.shape[0]
    nq = seq // TQ
    hw = MLA_HEADS * LANES
    vw = MLA_HEADS * MLA_V
    return pl.pallas_call(
        _mla_attn_kernel,
        grid=(batch, nq),
        in_specs=[pl.BlockSpec((TQ, hw), lambda b, i: (b * nq + i, 0)),
                  pl.BlockSpec((seq, hw), lambda b, i: (b, 0), pipeline_mode=pl.Buffered(1)),
                  pl.BlockSpec((hw, seq), lambda b, i: (0, b), pipeline_mode=pl.Buffered(1))],
        out_specs=pl.BlockSpec((TQ, vw), lambda b, i: (b * nq + i, 0)),
        out_shape=jax.ShapeDtypeStruct((t, vw), BF16),
        scratch_shapes=[pltpu.VMEM((MLA_HEADS, 8, TQ), F32), pltpu.VMEM((MLA_HEADS, LANES, TQ), F32)],
        compiler_params=_params2(),
        name="mla_attn",
    )(q, k, vt)


def _t5_bucket_table():
    n = np.arange(0, REL_MAX_DISTANCE + 1)
    max_exact = REL_BUCKETS // 2
    nf = np.maximum(n, 1).astype(np.float32)
    ratio = np.log(nf / np.float32(max_exact)) / np.float32(math.log(REL_MAX_DISTANCE / max_exact))
    large = max_exact + (ratio * np.float32(REL_BUCKETS - max_exact)).astype(np.int32)
    large = np.minimum(large, REL_BUCKETS - 1)
    return np.where(n < max_exact, n, large)


_BUCKETS = _t5_bucket_table()
_FAR_BUCKET = int(_BUCKETS[REL_MAX_DISTANCE])
assert _FAR_BUCKET == REL_BUCKETS - 1 and np.all(np.diff(_BUCKETS) >= 0)
_BUCKET_STARTS = [int(np.argmax(_BUCKETS >= b)) for b in range(REL_BUCKETS // 2 + 1, REL_BUCKETS)]


def _diff_attn_kernel(rb_ref, lam_ref, subln_ref, q_ref, k_ref, vt_ref, o_ref, bias_scr, m_scr, acc_scr, *, lambda_init):
    tq = TQ
    i = pl.program_id(1)
    r_i = lax.broadcasted_iota(jnp.int32, (tq, tq), 0)
    c_i = lax.broadcasted_iota(jnp.int32, (tq, tq), 1)
    causal = r_i <= c_i

    @pl.when((pl.program_id(0) == 0) & (i == 0))
    def _():
        for t_idx, delta in enumerate((0, tq)):
            d = jnp.maximum(c_i - r_i + delta, 0)
            log_b = REL_BUCKETS // 2
            for start in _BUCKET_STARTS:
                log_b = log_b + (d >= start).astype(jnp.int32)
            bucket = jnp.where(d < REL_BUCKETS // 2, d, log_b)
            for h in range(DIFF_HEADS):
                bias = jnp.zeros((tq, tq), F32)
                for b in range(REL_BUCKETS):
                    bias = jnp.where(bucket == b, rb_ref[b, h], bias)
                bias_scr[t_idx, h] = (bias - rb_ref[_FAR_BUCKET, h]) * LOG2E

    m_scr[...] = jnp.full(m_scr.shape, NEG, F32)
    acc_scr[...] = jnp.zeros(acc_scr.shape, F32)
    lane = lax.broadcasted_iota(jnp.int32, (tq, LANES), 1)
    ones_rows = (lax.broadcasted_iota(jnp.int32, (DIFF_VT_ROWS - DIFF_V, tq), 0) == 0).astype(BF16)
    hsl = [slice(h * LANES, (h + 1) * LANES) for h in range(DIFF_HEADS)]
    heads = list(range(DIFF_HEADS))
    q_l = []
    for hs in hsl:
        qp = q_ref[:, hs]
        q_l.append(jnp.concatenate([jnp.where(lane < DIFF_QK, qp, jnp.zeros_like(qp)),
                                    jnp.where(lane >= DIFF_QK, qp, jnp.zeros_like(qp))], axis=0))

    def tile(off, bias_idx, mask):
        s_l = [_dot_nt(k_ref[pl.ds(off, tq), hsl[h]], q_l[h]) for h in heads]
        if bias_idx is not None:
            s_l = [s + jnp.concatenate([bias_scr[bias_idx, h]] * 2, axis=1) for h, s in enumerate(s_l)]
        if mask:
            mask2 = jnp.concatenate([causal, causal], axis=1)
            s_l = [jnp.where(mask2, s, NEG) for s in s_l]

        def pv(c, p):
            vt = jnp.concatenate([vt_ref[hsl[c], pl.ds(off, tq)], ones_rows], axis=0)
            return jnp.concatenate([_dot(vt, p[:, :tq]), _dot(vt, p[:, tq:])], axis=1)

        _softmax_stage_t(s_l, pv, m_scr, acc_scr, heads)

    def far_body(j, carry):
        tile(pl.multiple_of(j * tq, tq), None, False)
        return carry

    def near_body(j, carry):
        tile(pl.multiple_of(j * tq, tq), 1, False)
        return carry

    n_far = jnp.maximum(i - 1, 0)
    lax.fori_loop(0, n_far, far_body, 0)
    lax.fori_loop(n_far, i, near_body, 0)
    tile(pl.multiple_of(i * tq, tq), 0, True)

    lam = lam_ref[...]
    lam_full = (jnp.exp(jnp.sum(lam[0:1] * lam[1:2], axis=-1, keepdims=True))
                - jnp.exp(jnp.sum(lam[2:3] * lam[3:4], axis=-1, keepdims=True)) + lambda_init)
    for h in heads:
        acc = acc_scr[h]
        on = acc[:DIFF_V, :] / acc[DIFF_V:DIFF_V + 1, :]
        ot = on[:, :tq] - lam_full * on[:, tq:]
        ot = ot * lax.rsqrt(jnp.mean(ot * ot, axis=0, keepdims=True) + SUBLN_EPS)
        o_ref[:, hsl[h]] = (jnp.transpose(ot) * subln_ref[...] * (1.0 - lambda_init)).astype(BF16)


def _diff_attn(p_qk, p_vt, rel_bias, lam, subln, batch, seq, layer_idx):
    t = p_qk.shape[0]
    nq = seq // TQ
    w = DIFF_HEADS * LANES
    lambda_init = 0.8 - 0.6 * math.exp(-0.3 * layer_idx)
    kern = functools.partial(_diff_attn_kernel, lambda_init=lambda_init)
    const2 = lambda shape: pl.BlockSpec(shape, lambda b, i: (0, 0))
    return pl.pallas_call(
        kern,
        grid=(batch, nq),
        in_specs=[pl.BlockSpec(memory_space=pltpu.SMEM),
                  const2((4, DIFF_QK)), const2((1, DIFF_V)),
                  pl.BlockSpec((TQ, w), lambda b, i: (b * nq + i, 0)),
                  pl.BlockSpec((seq, w), lambda b, i: (b, 1), pipeline_mode=pl.Buffered(1)),
                  pl.BlockSpec((w, seq), lambda b, i: (0, b), pipeline_mode=pl.Buffered(1))],
        out_specs=pl.BlockSpec((TQ, w), lambda b, i: (b * nq + i, 0)),
        out_shape=jax.ShapeDtypeStruct((t, w), BF16),
        scratch_shapes=[pltpu.VMEM((2, DIFF_HEADS, TQ, TQ), F32), pltpu.VMEM((DIFF_HEADS, 8, 2 * TQ), F32),
                        pltpu.VMEM((DIFF_HEADS, DIFF_VT_ROWS, 2 * TQ), F32)],
        compiler_params=_params2(),
        name="diff_attn",
    )(rel_bias, lam, subln, p_qk, p_qk, p_vt)


def _merge_kernel(x_ref, pg_ref, bg_ref, or_ref, om_ref, od_ref, wr_ref, wm_ref, wd_ref, wo_ref, gf_ref, x_out, h_out):
    def gate(idx):
        cs = slice(idx * D_MODEL, (idx + 1) * D_MODEL)
        return jax.nn.sigmoid(pg_ref[:, cs] + bg_ref[:, cs])

    merged = gate(0) * _dot(or_ref[...], wr_ref[...])
    merged = merged + gate(1) * _dot(om_ref[...], wm_ref[...])
    merged = merged + gate(2) * _dot(od_ref[...], wd_ref[...])
    x1 = x_ref[...] + _dot(merged.astype(BF16), wo_ref[...])
    x_out[...] = x1
    h_out[...] = _rms(x1, gf_ref[...], NORM_EPS).astype(BF16)


def _merge(x, p_gate, b_gate, o_r, o_m, o_d, w_r, w_m, w_d, w_o, g_ffn):
    t = x.shape[0]
    return pl.pallas_call(
        _merge_kernel,
        grid=(t // TM,),
        in_specs=[_row_spec(TM, D_MODEL), _row_spec(TM, GATE_COLS), _const_spec((1, GATE_COLS)),
                  _row_spec(TM, 512), _row_spec(TM, 512), _row_spec(TM, 512),
                  _const_spec((512, D_MODEL)), _const_spec((512, D_MODEL)), _const_spec((512, D_MODEL)),
                  _const_spec((D_MODEL, D_MODEL)), _const_spec((1, D_MODEL))],
        out_specs=[_row_spec(TM, D_MODEL), _row_spec(TM, D_MODEL)],
        out_shape=[jax.ShapeDtypeStruct((t, D_MODEL), F32), jax.ShapeDtypeStruct((t, D_MODEL), BF16)],
        compiler_params=_params(),
        name="merge",
    )(x, p_gate, b_gate, o_r, o_m, o_d, w_r, w_m, w_d, w_o, g_ffn)


def _ffn_kernel(x_ref, h_ref, wup_ref, cw_ref, cb_ref, wdn_ref, gfin_ref, o_ref, carry_ref, *, tiles_per_seq, final_norm):
    tm = TM

    @pl.when(pl.program_id(0) % tiles_per_seq == 0)
    def _():
        carry_ref[...] = jnp.zeros_like(carry_ref)

    h = h_ref[...]

    def conv(u, cols):
        ext = jnp.concatenate([carry_ref[:, cols], u], axis=0)
        carry_ref[:, cols] = u[tm - 8:tm, :]
        b = cw_ref[1:2, cols] * ext + pltpu.roll(cw_ref[0:1, cols] * ext, 1, 0)
        out = cw_ref[2:3, cols] * ext + pltpu.roll(b, 1, 0) + cb_ref[:, cols]
        return out[8:, :]

    def up(ck):
        gc = slice(ck * FF_CHUNK, (ck + 1) * FF_CHUNK)
        vc = slice(D_FF + ck * FF_CHUNK, D_FF + (ck + 1) * FF_CHUNK)
        return _dot(h, wup_ref[:, gc]), _dot(h, wup_ref[:, vc])

    n_chunks = D_FF // FF_CHUNK
    acc = x_ref[...]
    nxt = up(0)
    for ck in range(n_chunks):
        ug, uv = nxt
        if ck + 1 < n_chunks:
            nxt = up(ck + 1)
        gc = slice(ck * FF_CHUNK, (ck + 1) * FF_CHUNK)
        vc = slice(D_FF + ck * FF_CHUNK, D_FF + (ck + 1) * FF_CHUNK)
        gate = conv(ug, gc)
        val = conv(uv, vc)
        act = (gate * jax.nn.sigmoid(gate) * val).astype(BF16)
        acc = acc + _dot(act, wdn_ref[gc, :])
    if final_norm:
        acc = _rms(acc, gfin_ref[...], NORM_EPS)
    o_ref[...] = acc


def _ffn(x1, h2, w_up, conv_w, conv_b, w_down, g_final, seq, final_norm):
    t = x1.shape[0]
    kern = functools.partial(_ffn_kernel, tiles_per_seq=seq // TM, final_norm=final_norm)
    return pl.pallas_call(
        kern,
        grid=(t // TM,),
        in_specs=[_row_spec(TM, D_MODEL), _row_spec(TM, D_MODEL), _const_spec((D_MODEL, 2 * D_FF)),
                  _const_spec((3, 2 * D_FF)), _const_spec((1, 2 * D_FF)), _const_spec((D_FF, D_MODEL)),
                  _const_spec((1, D_MODEL))],
        out_specs=_row_spec(TM, D_MODEL),
        out_shape=jax.ShapeDtypeStruct((t, D_MODEL), F32),
        scratch_shapes=[pltpu.VMEM((8, 2 * D_FF), F32)],
        compiler_params=_params(),
        name="conv_ffn",
    )(x1, h2, w_up, conv_w, conv_b, w_down, g_final)


def _mla_weights(w_uq, w_ukv):
    qd = MLA_NOPE + MLA_ROPE
    half = MLA_ROPE // 2
    wq = w_uq.reshape(MLA_Q_LORA, MLA_HEADS, qd)
    zq = jnp.zeros((MLA_Q_LORA, MLA_HEADS, LANES - qd), F32)
    wq_main = jnp.concatenate([wq, zq], axis=-1)
    x1 = wq[:, :, MLA_NOPE:MLA_NOPE + half]
    x2 = wq[:, :, MLA_NOPE + half:]
    wq_rot = jnp.concatenate([jnp.zeros((MLA_Q_LORA, MLA_HEADS, MLA_NOPE), F32), -x2, x1, zq], axis=-1)
    wkv = w_ukv.reshape(MLA_KV_LORA, MLA_HEADS, MLA_NOPE + MLA_V)
    zkv = jnp.zeros((MLA_KV_LORA, MLA_HEADS, LANES - MLA_NOPE), F32)
    wk = jnp.concatenate([wkv[:, :, :MLA_NOPE], zkv], axis=-1)
    wv = jnp.concatenate([wkv[:, :, MLA_NOPE:], zkv], axis=-1)
    flat = lambda w: w.reshape(w.shape[0], -1).astype(BF16)
    return flat(wq_main), flat(wq_rot), flat(wk), flat(wv).T


def kernel(x, positions, rel_bias, norm_mix, w_in, b_gate, rwkv_mu, rwkv_w0, rwkv_w2, rwkv_a0, rwkv_a2, rwkv_g2, rwkv_k_k, rwkv_k_a, rwkv_r_k, rwkv_ln_w, rwkv_ln_b, mla_q_norm, mla_w_uq, mla_kv_norm, mla_w_ukv, diff_lambda, diff_subln, w_branch_rwkv, w_branch_mla, w_branch_diff, w_o, norm_ffn, ffn_w_up, ffn_conv_w, ffn_conv_b, ffn_w_down, norm_final):
    batch, seq, _ = x.shape
    depth = w_in.shape[0]
    t = batch * seq
    assert seq % TQ == 0 and seq % TM == 0 and seq % RWKV_TILE == 0 and RWKV_TILE % CHUNK == 0
    xf = x.reshape(t, D_MODEL)
    pos_col = positions.reshape(t, 1)
    inv_freq = ROPE_BASE ** (-jnp.arange(0, MLA_ROPE, 2, dtype=F32) / MLA_ROPE)
    freq = jnp.concatenate([jnp.zeros((MLA_NOPE,), F32), inv_freq, inv_freq,
                            jnp.zeros((LANES - MLA_NOPE - MLA_ROPE,), F32)]).reshape(1, LANES)
    row = lambda v: v.reshape(1, -1)
    diff_scale = jnp.concatenate([jnp.full((512,), DIFF_QK ** -0.5 * LOG2E, F32), jnp.ones((512,), F32)])

    for l in range(depth):
        s0, s1, s2 = RWKV_COLS, RWKV_COLS + MLA_COLS, RWKV_COLS + MLA_COLS + DIFF_COLS
        sv = s1 + DIFF_QK_COLS
        w = w_in[l]
        w_all = jnp.concatenate([w[:, :s0], w[:, s0:s1], jnp.zeros((D_MODEL, MLA_PAD - MLA_COLS), F32),
                                 w[:, s1:sv] * diff_scale, w[:, s2:]], axis=1).astype(BF16)
        w_vt = w[:, sv:s2].T.astype(BF16)
        p_rwkv, p_mla, p_qk, p_vt, p_gate = _in_proj(xf, row(norm_mix[l]), w_all, w_vt)

        o_r = _rwkv(p_rwkv, seq, row(rwkv_mu[l]), row(rwkv_w0[l]), rwkv_w2[l].astype(BF16), row(rwkv_a0[l]),
                    rwkv_a2[l].astype(BF16), rwkv_g2[l].astype(BF16), row(rwkv_k_k[l]), row(rwkv_k_a[l]),
                    row(rwkv_r_k[l]), row(rwkv_ln_w[l]), row(rwkv_ln_b[l]))

        wq, wqr, wk, wv = _mla_weights(mla_w_uq[l], mla_w_ukv[l])
        q_m, k_m, v_m = _mla_prep(p_mla, pos_col, freq, row(mla_q_norm[l]), row(mla_kv_norm[l]), wq, wqr, wk, wv)
        o_m = _mla_attn(q_m, k_m, v_m, batch, seq)

        o_d = _diff_attn(p_qk, p_vt, rel_bias, diff_lambda[l], row(diff_subln[l]), batch, seq, l)

        x1, h2 = _merge(xf, p_gate, row(b_gate[l]), o_r, o_m, o_d, w_branch_rwkv[l].astype(BF16),
                        w_branch_mla[l].astype(BF16), w_branch_diff[l].astype(BF16), w_o[l].astype(BF16),
                        row(norm_ffn[l]))
        xf = _ffn(x1, h2, ffn_w_up[l].astype(BF16), ffn_conv_w[l], row(ffn_conv_b[l]), ffn_w_down[l].astype(BF16),
                  row(norm_final), seq, final_norm=(l == depth - 1))
    return xf.reshape(batch, seq, D_MODEL)
```

```python
import functools
import math

import numpy as np
import jax
import jax.numpy as jnp
from jax import lax
from jax.experimental import pallas as pl
from jax.experimental.pallas import tpu as pltpu

F32 = jnp.float32
BF16 = jnp.bfloat16

D_MODEL = 1024
RWKV_HEADS = 8
RWKV_N = 64
RWKV_DIM = 512
RWKV_COLS = 1792
RWKV_GN_EPS = 64e-5
MLA_HEADS = 8
MLA_Q_LORA = 256
MLA_KV_LORA = 128
MLA_NOPE = 64
MLA_ROPE = 32
MLA_V = 64
MLA_COLS = 416
MLA_PAD = 512
ROPE_BASE = 10000.0
DIFF_HEADS = 4
DIFF_QK = 64
DIFF_V = 128
DIFF_COLS = 1536
DIFF_QK_COLS = 1024
DIFF_VT_ROWS = DIFF_V + 16
REL_BUCKETS = 32
REL_MAX_DISTANCE = 128
D_FF = 2816
GATE_COLS = 3072
NORM_EPS = 1e-6
SUBLN_EPS = 1e-5

LANES = 128
VMEM_LIMIT = 56 * 1024 * 1024
TM = 256
TM_FFN = 256
CHUNK = 64
RWKV_TILE = 256
TQ = 256
FF_CHUNK = 256
NEG = -1e30
LOG2E = 1.4426950408889634

_NT = (((1,), (1,)), ((), ()))
_TN = (((0,), (0,)), ((), ()))


def _dot(a, b):
    return jnp.dot(a, b, preferred_element_type=F32)


def _dot_nt(a, b):
    return lax.dot_general(a, b, _NT, preferred_element_type=F32)


def _dot_tn(a, b):
    return lax.dot_general(a, b, _TN, preferred_element_type=F32)


def _rms(x, g, eps):
    return x * lax.rsqrt(jnp.mean(x * x, axis=-1, keepdims=True) + eps) * g


def _params():
    return pltpu.CompilerParams(dimension_semantics=("arbitrary",), vmem_limit_bytes=VMEM_LIMIT)


def _params2():
    return pltpu.CompilerParams(dimension_semantics=("arbitrary", "arbitrary"), vmem_limit_bytes=VMEM_LIMIT)


def _const_spec(shape):
    return pl.BlockSpec(shape, lambda *_: (0,) * len(shape), pipeline_mode=pl.Buffered(1))


def _row_spec(tm, cols, col_block=0):
    return pl.BlockSpec((tm, cols), lambda i: (i, col_block))


def _in_proj_kernel(x_ref, g_ref, w_ref, wvt_ref, pr_ref, pm_ref, pqk_ref, pvt_ref, pg_ref):
    h = _rms(x_ref[...], g_ref[...], NORM_EPS).astype(BF16)
    c0 = 0
    pr_ref[...] = _dot(h, w_ref[:, c0:c0 + RWKV_COLS])
    c0 += RWKV_COLS
    pm_ref[...] = _dot(h, w_ref[:, c0:c0 + MLA_PAD])
    c0 += MLA_PAD
    pqk_ref[...] = _dot(h, w_ref[:, c0:c0 + DIFF_QK_COLS]).astype(BF16)
    c0 += DIFF_QK_COLS
    pg_ref[...] = _dot(h, w_ref[:, c0:c0 + GATE_COLS])
    pvt_ref[...] = _dot_nt(wvt_ref[...], h).astype(BF16)


def _in_proj(x, g, w_all, w_vt):
    t = x.shape[0]
    n_all = w_all.shape[1]
    vw = DIFF_HEADS * DIFF_V
    return pl.pallas_call(
        _in_proj_kernel,
        grid=(t // TM,),
        in_specs=[_row_spec(TM, D_MODEL), _const_spec((1, D_MODEL)), _const_spec((D_MODEL, n_all)),
                  _const_spec((vw, D_MODEL))],
        out_specs=[_row_spec(TM, RWKV_COLS), _row_spec(TM, MLA_PAD), _row_spec(TM, DIFF_QK_COLS),
                   pl.BlockSpec((vw, TM), lambda i: (0, i)), _row_spec(TM, GATE_COLS)],
        out_shape=[jax.ShapeDtypeStruct((t, RWKV_COLS), F32), jax.ShapeDtypeStruct((t, MLA_PAD), F32),
                   jax.ShapeDtypeStruct((t, DIFF_QK_COLS), BF16), jax.ShapeDtypeStruct((vw, t), BF16),
                   jax.ShapeDtypeStruct((t, GATE_COLS), F32)],
        compiler_params=_params(),
        name="in_proj",
    )(x, g, w_all, w_vt)


def _split3(a):
    hi = a.astype(BF16)
    r1 = a - hi.astype(F32)
    mid = r1.astype(BF16)
    lo = (r1 - mid.astype(F32)).astype(BF16)
    return hi, mid, lo


def _softplus(z):
    return jnp.maximum(z, 0.0) + jnp.log(1.0 + jnp.exp(-jnp.abs(z)))


def _rwkv_kernel(p_ref, mu_ref, w0_ref, w2_ref, a0_ref, a2_ref, g2_ref, kk_ref, ka_ref, rk_ref, lnw_ref, lnb_ref,
                 o_ref, carry_ref, state_ref, *, tiles_per_seq):
    c = CHUNK
    n = RWKV_N
    ts = RWKV_TILE
    nc = ts // c

    @pl.when(pl.program_id(0) % tiles_per_seq == 0)
    def _():
        carry_ref[...] = jnp.zeros_like(carry_ref)
        state_ref[...] = jnp.zeros_like(state_ref)

    p = p_ref[...]
    row = lax.broadcasted_iota(jnp.int32, (ts, 1), 0)
    shifted = jnp.where(row == 0, carry_ref[...], pltpu.roll(p, 1, 0))
    carry_ref[...] = p[ts - 1:ts, :]
    pm = p + (shifted - p) * mu_ref[...]

    r = pm[:, 0:512]
    k = pm[:, 512:1024]
    v = pm[:, 1024:1536]
    pw = pm[:, 1536:1600]
    pa = pm[:, 1600:1664]
    pg = pm[:, 1664:1792]

    w_log = -_softplus(-(w0_ref[...] + _dot(jnp.tanh(pw).astype(BF16), w2_ref[...]))) - 0.5
    logd = -jnp.exp(w_log)
    a = jax.nn.sigmoid(a0_ref[...] + _dot(pa.astype(BF16), a2_ref[...]))
    g = _dot(jax.nn.sigmoid(pg).astype(BF16), g2_ref[...])

    tr = lax.broadcasted_iota(jnp.int32, (ts, ts), 0)
    tc = lax.broadcasted_iota(jnp.int32, (ts, ts), 1)
    tri = ((tr >= tc) & (tr // c == tc // c)).astype(BF16)
    hi, mid, lo = _split3(logd)
    cs = _dot(tri, hi) + _dot(tri, mid) + _dot(tri, lo)
    total = jnp.concatenate([jnp.broadcast_to(cs[(m + 1) * c - 1:(m + 1) * c, :], (c, RWKV_DIM)) for m in range(nc)], axis=0)
    e_in = jnp.exp(cs)
    e_ex = jnp.exp(cs - logd)
    e_inv = jnp.exp(-cs)
    e_end = jnp.exp(total - cs)
    g_end = jnp.exp(total)

    n_pairs = RWKV_HEADS // 2
    lo_t = lax.broadcasted_iota(jnp.int32, (ts, LANES), 1) < n
    lo_c = lax.broadcasted_iota(jnp.int32, (c, LANES), 1) < n

    def head_sum(x):
        blocks = []
        for q in range(n_pairs):
            xb = x[:, q * LANES:(q + 1) * LANES]
            s_lo = jnp.sum(jnp.where(lo_t, xb, 0.0), axis=-1, keepdims=True)
            s_hi = jnp.sum(jnp.where(lo_t, 0.0, xb), axis=-1, keepdims=True)
            blocks.append(jnp.where(lo_t, s_lo, s_hi))
        return jnp.concatenate(blocks, axis=-1)

    k2 = k * (1.0 + (a - 1.0) * ka_ref[...])
    kku = k * kk_ref[...]
    kkn = kku / jnp.maximum(jnp.sqrt(head_sum(kku * kku)), 1e-12)
    b = kkn * a
    full = {"at": -kkn * e_ex, "rt": r * e_in, "bt": b * e_inv, "kt": k2 * e_inv, "v": v,
            "be": b * e_end, "ke": k2 * e_end}

    ti = lax.broadcasted_iota(jnp.int32, (2 * c, 4 * c), 0) % c
    si = lax.broadcasted_iota(jnp.int32, (2 * c, 4 * c), 1) % c
    strict = ti > si
    incl = ti >= si
    eye = (lax.broadcasted_iota(jnp.int32, (2 * c, 2 * c), 0)
           == lax.broadcasted_iota(jnp.int32, (2 * c, 2 * c), 1)).astype(F32)

    items = [(m, q) for m in range(nc) for q in range(n_pairs)]

    def stacked(name, m, q):
        xb = full[name][m * c:(m + 1) * c, q * LANES:(q + 1) * LANES]
        return jnp.concatenate([jnp.where(lo_c, xb, 0.0), jnp.where(lo_c, 0.0, xb)], axis=0).astype(BF16)

    st_ops = {name: [stacked(name, m, q) for m, q in items] for name in full}
    big_l = [_dot_nt(jnp.concatenate([at, rtb], axis=0), jnp.concatenate([bt, kt], axis=0))
             for at, rtb, bt, kt in zip(st_ops["at"], st_ops["rt"], st_ops["bt"], st_ops["kt"])]
    top_l = [jnp.where(strict, big[:2 * c, :], 0.0) for big in big_l]
    bot_l = [jnp.where(incl, big[2 * c:, :], 0.0).astype(BF16) for big in big_l]
    lakv_l = [_dot(top[:, 2 * c:].astype(BF16), vb) for top, vb in zip(top_l, st_ops["v"])]

    x_l = [top[:, :2 * c] for top in top_l]
    tinv_l = [eye + x for x in x_l]
    xb_l = [x.astype(BF16) for x in x_l]
    for _ in range(int(math.log2(c)) - 1):
        x_l = [_dot(xb, xb) for xb in xb_l]
        xb_l = [x.astype(BF16) for x in x_l]
        tinv_l = [tinv + _dot(tinv.astype(BF16), xb) for tinv, xb in zip(tinv_l, xb_l)]
    tinvb_l = [tinv.astype(BF16) for tinv in tinv_l]
    abar_l = [_dot(tb, at).astype(BF16) for tb, at in zip(tinvb_l, st_ops["at"])]
    vbar_l = [_dot(tb, lv.astype(BF16)).astype(BF16) for tb, lv in zip(tinvb_l, lakv_l)]
    rhat_l = [(rtb.astype(F32) + _dot(bot[:, :2 * c], ab)).astype(BF16)
              for rtb, bot, ab in zip(st_ops["rt"], bot_l, abar_l)]
    uv_l = [jnp.concatenate([vbar, vb], axis=0) for vbar, vb in zip(vbar_l, st_ops["v"])]
    y0_l = [_dot(bot, uv) for bot, uv in zip(bot_l, uv_l)]
    p_l = [_dot_tn(ab, be).astype(BF16) for ab, be in zip(abar_l, st_ops["be"])]
    q_l = [_dot_tn(uv, jnp.concatenate([be, ke], axis=0)) for uv, be, ke in zip(uv_l, st_ops["be"], st_ops["ke"])]

    st_l = [state_ref[q] for q in range(n_pairs)]
    y_rows = []
    for m in range(nc):
        y_blocks = []
        for q in range(n_pairs):
            idx = m * n_pairs + q
            st = st_l[q]
            st_b = st.astype(BF16)
            y_s = _dot_nt(rhat_l[idx], st_b) + y0_l[idx]
            y_blocks.append(y_s[:c] + y_s[c:])
            decay = g_end[(m + 1) * c - 1:(m + 1) * c, q * LANES:(q + 1) * LANES]
            st_l[q] = st * decay + _dot(st_b, p_l[idx]) + q_l[idx]
        y_rows.append(jnp.concatenate(y_blocks, axis=-1))
    for q in range(n_pairs):
        state_ref[q] = st_l[q]
    y = jnp.concatenate(y_rows, axis=0)

    inv_n = 1.0 / n
    yc = y - head_sum(y) * inv_n
    var = head_sum(yc * yc) * inv_n
    yn = yc * lax.rsqrt(var + RWKV_GN_EPS) * lnw_ref[...] + lnb_ref[...]
    bonus = head_sum(r * k2 * rk_ref[...]) * v
    o_ref[...] = ((yn + bonus) * g).astype(BF16)


def _rwkv(p_rwkv, seq, mu, w0, w2, a0, a2, g2, k_k, k_a, r_k, ln_w, ln_b):
    t = p_rwkv.shape[0]
    ts = RWKV_TILE
    kern = functools.partial(_rwkv_kernel, tiles_per_seq=seq // ts)
    vec = lambda: _const_spec((1, RWKV_DIM))
    return pl.pallas_call(
        kern,
        grid=(t // ts,),
        in_specs=[_row_spec(ts, RWKV_COLS), _const_spec((1, RWKV_COLS)), vec(), _const_spec((64, RWKV_DIM)), vec(),
                  _const_spec((64, RWKV_DIM)), _const_spec((128, RWKV_DIM)), vec(), vec(), vec(), vec(), vec()],
        out_specs=_row_spec(ts, RWKV_DIM),
        out_shape=jax.ShapeDtypeStruct((t, RWKV_DIM), BF16),
        scratch_shapes=[pltpu.VMEM((1, RWKV_COLS), F32), pltpu.VMEM((RWKV_HEADS // 2, LANES, LANES), F32)],
        compiler_params=_params(),
        name="rwkv7",
    )(p_rwkv, mu, w0, w2, a0, a2, g2, k_k, k_a, r_k, ln_w, ln_b)


def _mla_prep_kernel(p_ref, pos_ref, freq_ref, qn_ref, kvn_ref, wq_ref, wqr_ref, wk_ref, wv_ref, q_out, k_out, v_out):
    p = p_ref[...]
    hq = _rms(p[:, 0:MLA_Q_LORA], qn_ref[...], NORM_EPS).astype(BF16)
    hkv = _rms(p[:, MLA_Q_LORA:MLA_Q_LORA + MLA_KV_LORA], kvn_ref[...], NORM_EPS).astype(BF16)
    blk = p[:, 384:512]
    ang = pos_ref[...].astype(F32) * freq_ref[...]
    cos = jnp.cos(ang)
    sin = jnp.sin(ang)
    scale = (MLA_NOPE + MLA_ROPE) ** -0.5 * LOG2E
    qa = _dot(hq, wq_ref[...])
    qr = _dot(hq, wqr_ref[...])
    lane = lax.broadcasted_iota(jnp.int32, blk.shape, 1)
    kr = pltpu.roll(blk, 64, 1)
    rot = jnp.where(lane < 80, -pltpu.roll(blk, 48, 1), pltpu.roll(blk, 80, 1))
    rot = jnp.where((lane >= 64) & (lane < 96), rot, 0.0)
    krope = kr * cos + rot * sin
    kn = _dot(hkv, wk_ref[...])
    for h in range(MLA_HEADS):
        hs = slice(h * LANES, (h + 1) * LANES)
        q_out[:, hs] = ((qa[:, hs] * cos + qr[:, hs] * sin) * scale).astype(BF16)
        k_out[:, hs] = (kn[:, hs] + krope).astype(BF16)
    vt = _dot_nt(wv_ref[...], hkv)
    rowi = lax.broadcasted_iota(jnp.int32, vt.shape, 0)
    v_out[...] = (vt + (rowi % LANES == MLA_V).astype(F32)).astype(BF16)


def _mla_prep(p_mla, pos_col, freq, q_norm, kv_norm, wq, wqr, wk, wv):
    t = p_mla.shape[0]
    hw = MLA_HEADS * LANES
    return pl.pallas_call(
        _mla_prep_kernel,
        grid=(t // TM,),
        in_specs=[_row_spec(TM, MLA_PAD), _row_spec(TM, 1), _const_spec((1, LANES)), _const_spec((1, MLA_Q_LORA)),
                  _const_spec((1, MLA_KV_LORA)), _const_spec((MLA_Q_LORA, hw)), _const_spec((MLA_Q_LORA, hw)),
                  _const_spec((MLA_KV_LORA, hw)), _const_spec((hw, MLA_KV_LORA))],
        out_specs=[_row_spec(TM, hw), _row_spec(TM, hw), pl.BlockSpec((hw, TM), lambda i: (0, i))],
        out_shape=[jax.ShapeDtypeStruct((t, hw), BF16), jax.ShapeDtypeStruct((t, hw), BF16),
                   jax.ShapeDtypeStruct((hw, t), BF16)],
        compiler_params=_params(),
        name="mla_prep",
    )(p_mla, pos_col, freq, q_norm, kv_norm, wq, wqr, wk, wv)


def _softmax_stage_t(s_l, pv_fn, m_scr, acc_scr, idx):
    n = len(s_l)
    m_old = [m_scr[c][0:1, :] for c in idx]
    m_new = [jnp.maximum(m, jnp.max(s, axis=0, keepdims=True)) for m, s in zip(m_old, s_l)]
    p_l = [jnp.exp2(s - m).astype(BF16) for s, m in zip(s_l, m_new)]
    alpha = [jnp.exp2(mo - mn) for mo, mn in zip(m_old, m_new)]
    pv_l = [pv_fn(c, p_l[c]) for c in range(n)]
    for c in range(n):
        m_scr[idx[c]] = jnp.broadcast_to(m_new[c], m_scr.shape[1:])
        acc_scr[idx[c]] = alpha[c] * acc_scr[idx[c]] + pv_l[c]


def _mla_attn_kernel(q_ref, k_ref, vt_ref, o_ref, m_scr, acc_scr):
    tq = TQ
    i = pl.program_id(1)
    causal = lax.broadcasted_iota(jnp.int32, (tq, tq), 0) <= lax.broadcasted_iota(jnp.int32, (tq, tq), 1)
    lane = lax.broadcasted_iota(jnp.int32, (tq, LANES), 1)
    m_scr[...] = jnp.full(m_scr.shape, NEG, F32)
    acc_scr[...] = jnp.zeros(acc_scr.shape, F32)
    hsl = [slice(h * LANES, (h + 1) * LANES) for h in range(MLA_HEADS)]
    heads = list(range(MLA_HEADS))

    def tile(off, tk, mask):
        s_l = [_dot_nt(k_ref[pl.ds(off, tk), hs], q_ref[:, hs]) for hs in hsl]
        if mask:
            s_l = [jnp.where(causal, s, NEG) for s in s_l]
        _softmax_stage_t(s_l, lambda c, p: _dot(vt_ref[hsl[c], pl.ds(off, tk)], p), m_scr, acc_scr, heads)

    def pair_body(j, carry):
        tile(pl.multiple_of(j * 2 * tq, 2 * tq), 2 * tq, False)
        return carry

    def single_body(j, carry):
        tile(pl.multiple_of(j * tq, tq), tq, False)
        return carry

    n_pairs = i // 2
    lax.fori_loop(0, n_pairs, pair_body, 0)
    lax.fori_loop(2 * n_pairs, i, single_body, 0)
    tile(pl.multiple_of(i * tq, tq), tq, True)
    for pr in range(MLA_HEADS // 2):
        outs = []
        for hh in range(2):
            acc = acc_scr[2 * pr + hh]
            outs.append(jnp.transpose(acc / acc[MLA_V:MLA_V + 1, :]))
        o_ref[:, pr * LANES:(pr + 1) * LANES] = jnp.where(lane < MLA_V, outs[0], pltpu.roll(outs[1], MLA_V, 1)).astype(BF16)


def _mla_attn(q, k, vt, batch, seq):
    t = q.shape[0]
    nq = seq // TQ
    hw = MLA_HEADS * LANES
    vw = MLA_HEADS * MLA_V
    return pl.pallas_call(
        _mla_attn_kernel,
        grid=(batch, nq),
        in_specs=[pl.BlockSpec((TQ, hw), lambda b, i: (b * nq + i, 0)),
                  pl.BlockSpec((seq, hw), lambda b, i: (b, 0), pipeline_mode=pl.Buffered(1)),
                  pl.BlockSpec((hw, seq), lambda b, i: (0, b), pipeline_mode=pl.Buffered(1))],
        out_specs=pl.BlockSpec((TQ, vw), lambda b, i: (b * nq + i, 0)),
        out_shape=jax.ShapeDtypeStruct((t, vw), BF16),
        scratch_shapes=[pltpu.VMEM((MLA_HEADS, 8, TQ), F32), pltpu.VMEM((MLA_HEADS, LANES, TQ), F32)],
        compiler_params=_params2(),
        name="mla_attn",
    )(q, k, vt)


def _t5_bucket_table():
    n = np.arange(0, REL_MAX_DISTANCE + 1)
    max_exact = REL_BUCKETS // 2
    nf = np.maximum(n, 1).astype(np.float32)
    ratio = np.log(nf / np.float32(max_exact)) / np.float32(math.log(REL_MAX_DISTANCE / max_exact))
    large = max_exact + (ratio * np.float32(REL_BUCKETS - max_exact)).astype(np.int32)
    large = np.minimum(large, REL_BUCKETS - 1)
    return np.where(n < max_exact, n, large)


_BUCKETS = _t5_bucket_table()
_FAR_BUCKET = int(_BUCKETS[REL_MAX_DISTANCE])
assert _FAR_BUCKET == REL_BUCKETS - 1 and np.all(np.diff(_BUCKETS) >= 0)
_BUCKET_STARTS = [int(np.argmax(_BUCKETS >= b)) for b in range(REL_BUCKETS // 2 + 1, REL_BUCKETS)]


def _diff_attn_kernel(rb_ref, lam_ref, subln_ref, q_ref, k_ref, vt_ref, o_ref, bias_scr, m_scr, acc_scr, *, lambda_init):
    tq = TQ
    i = pl.program_id(1)
    r_i = lax.broadcasted_iota(jnp.int32, (tq, tq), 0)
    c_i = lax.broadcasted_iota(jnp.int32, (tq, tq), 1)
    causal = r_i <= c_i

    @pl.when((pl.program_id(0) == 0) & (i == 0))
    def _():
        for t_idx, delta in enumerate((0, tq)):
            d = jnp.maximum(c_i - r_i + delta, 0)
            log_b = REL_BUCKETS // 2
            for start in _BUCKET_STARTS:
                log_b = log_b + (d >= start).astype(jnp.int32)
            bucket = jnp.where(d < REL_BUCKETS // 2, d, log_b)
            for h in range(DIFF_HEADS):
                bias = jnp.zeros((tq, tq), F32)
                for b in range(REL_BUCKETS):
                    bias = jnp.where(bucket == b, rb_ref[b, h], bias)
                bias_scr[t_idx, h] = (bias - rb_ref[_FAR_BUCKET, h]) * LOG2E

    m_scr[...] = jnp.full(m_scr.shape, NEG, F32)
    acc_scr[...] = jnp.zeros(acc_scr.shape, F32)
    lane = lax.broadcasted_iota(jnp.int32, (tq, LANES), 1)
    ones_rows = {tk: (lax.broadcasted_iota(jnp.int32, (DIFF_VT_ROWS - DIFF_V, tk), 0) == 0).astype(BF16)
                 for tk in (tq, 2 * tq)}
    hsl = [slice(h * LANES, (h + 1) * LANES) for h in range(DIFF_HEADS)]
    heads = list(range(DIFF_HEADS))
    q_l = []
    for hs in hsl:
        qp = q_ref[:, hs]
        q_l.append(jnp.concatenate([jnp.where(lane < DIFF_QK, qp, jnp.zeros_like(qp)),
                                    jnp.where(lane >= DIFF_QK, qp, jnp.zeros_like(qp))], axis=0))

    def tile(off, tk, bias_idx, mask):
        s_l = [_dot_nt(k_ref[pl.ds(off, tk), hsl[h]], q_l[h]) for h in heads]
        if bias_idx is not None:
            s_l = [s + jnp.concatenate([bias_scr[bias_idx, h]] * 2, axis=1) for h, s in enumerate(s_l)]
        if mask:
            mask2 = jnp.concatenate([causal, causal], axis=1)
            s_l = [jnp.where(mask2, s, NEG) for s in s_l]

        def pv(c, p):
            vt = jnp.concatenate([vt_ref[hsl[c], pl.ds(off, tk)], ones_rows[tk]], axis=0)
            return jnp.concatenate([_dot(vt, p[:, :tq]), _dot(vt, p[:, tq:])], axis=1)

        _softmax_stage_t(s_l, pv, m_scr, acc_scr, heads)

    def far_pair_body(j, carry):
        tile(pl.multiple_of(j * 2 * tq, 2 * tq), 2 * tq, None, False)
        return carry

    def far_body(j, carry):
        tile(pl.multiple_of(j * tq, tq), tq, None, False)
        return carry

    def near_body(j, carry):
        tile(pl.multiple_of(j * tq, tq), tq, 1, False)
        return carry

    n_far = jnp.maximum(i - 1, 0)
    n_pairs = n_far // 2
    lax.fori_loop(0, n_pairs, far_pair_body, 0)
    lax.fori_loop(2 * n_pairs, n_far, far_body, 0)
    lax.fori_loop(n_far, i, near_body, 0)
    tile(pl.multiple_of(i * tq, tq), tq, 0, True)

    lam = lam_ref[...]
    lam_full = (jnp.exp(jnp.sum(lam[0:1] * lam[1:2], axis=-1, keepdims=True))
                - jnp.exp(jnp.sum(lam[2:3] * lam[3:4], axis=-1, keepdims=True)) + lambda_init)
    for h in heads:
        acc = acc_scr[h]
        on = acc[:DIFF_V, :] / acc[DIFF_V:DIFF_V + 1, :]
        ot = on[:, :tq] - lam_full * on[:, tq:]
        ot = ot * lax.rsqrt(jnp.mean(ot * ot, axis=0, keepdims=True) + SUBLN_EPS)
        o_ref[:, hsl[h]] = (jnp.transpose(ot) * subln_ref[...] * (1.0 - lambda_init)).astype(BF16)


def _diff_attn(p_qk, p_vt, rel_bias, lam, subln, batch, seq, layer_idx):
    t = p_qk.shape[0]
    nq = seq // TQ
    w = DIFF_HEADS * LANES
    lambda_init = 0.8 - 0.6 * math.exp(-0.3 * layer_idx)
    kern = functools.partial(_diff_attn_kernel, lambda_init=lambda_init)
    const2 = lambda shape: pl.BlockSpec(shape, lambda b, i: (0, 0))
    return pl.pallas_call(
        kern,
        grid=(batch, nq),
        in_specs=[pl.BlockSpec(memory_space=pltpu.SMEM),
                  const2((4, DIFF_QK)), const2((1, DIFF_V)),
                  pl.BlockSpec((TQ, w), lambda b, i: (b * nq + i, 0)),
                  pl.BlockSpec((seq, w), lambda b, i: (b, 1), pipeline_mode=pl.Buffered(1)),
                  pl.BlockSpec((w, seq), lambda b, i: (0, b), pipeline_mode=pl.Buffered(1))],
        out_specs=pl.BlockSpec((TQ, w), lambda b, i: (b * nq + i, 0)),
        out_shape=jax.ShapeDtypeStruct((t, w), BF16),
        scratch_shapes=[pltpu.VMEM((2, DIFF_HEADS, TQ, TQ), F32), pltpu.VMEM((DIFF_HEADS, 8, 2 * TQ), F32),
                        pltpu.VMEM((DIFF_HEADS, DIFF_VT_ROWS, 2 * TQ), F32)],
        compiler_params=_params2(),
        name="diff_attn",
    )(rel_bias, lam, subln, p_qk, p_qk, p_vt)


def _merge_kernel(x_ref, pg_ref, bg_ref, or_ref, om_ref, od_ref, wr_ref, wm_ref, wd_ref, wo_ref, gf_ref, x_out, h_out):
    def gate(idx):
        cs = slice(idx * D_MODEL, (idx + 1) * D_MODEL)
        return jax.nn.sigmoid(pg_ref[:, cs] + bg_ref[:, cs])

    merged = gate(0) * _dot(or_ref[...], wr_ref[...])
    merged = merged + gate(1) * _dot(om_ref[...], wm_ref[...])
    merged = merged + gate(2) * _dot(od_ref[...], wd_ref[...])
    x1 = x_ref[...] + _dot(merged.astype(BF16), wo_ref[...])
    x_out[...] = x1
    h_out[...] = _rms(x1, gf_ref[...], NORM_EPS).astype(BF16)


def _merge(x, p_gate, b_gate, o_r, o_m, o_d, w_r, w_m, w_d, w_o, g_ffn):
    t = x.shape[0]
    return pl.pallas_call(
        _merge_kernel,
        grid=(t // TM,),
        in_specs=[_row_spec(TM, D_MODEL), _row_spec(TM, GATE_COLS), _const_spec((1, GATE_COLS)),
                  _row_spec(TM, 512), _row_spec(TM, 512), _row_spec(TM, 512),
                  _const_spec((512, D_MODEL)), _const_spec((512, D_MODEL)), _const_spec((512, D_MODEL)),
                  _const_spec((D_MODEL, D_MODEL)), _const_spec((1, D_MODEL))],
        out_specs=[_row_spec(TM, D_MODEL), _row_spec(TM, D_MODEL)],
        out_shape=[jax.ShapeDtypeStruct((t, D_MODEL), F32), jax.ShapeDtypeStruct((t, D_MODEL), BF16)],
        compiler_params=_params(),
        name="merge",
    )(x, p_gate, b_gate, o_r, o_m, o_d, w_r, w_m, w_d, w_o, g_ffn)


def _ffn_kernel(x_ref, h_ref, wup_ref, cw_ref, cb_ref, wdn_ref, gfin_ref, o_ref, carry_ref, *, tiles_per_seq, final_norm):
    tm = TM_FFN

    @pl.when(pl.program_id(0) % tiles_per_seq == 0)
    def _():
        carry_ref[0:8, :] = jnp.zeros((8, carry_ref.shape[1]), F32)

    h = h_ref[...]

    def conv(u, cols):
        carry_ref[8:, cols] = u
        u1 = carry_ref[7:7 + tm, cols]
        u2 = carry_ref[6:6 + tm, cols]
        carry_ref[0:8, cols] = u[tm - 8:tm, :]
        return cw_ref[0:1, cols] * u2 + cw_ref[1:2, cols] * u1 + cw_ref[2:3, cols] * u + cb_ref[:, cols]

    def up(ck):
        gc = slice(ck * FF_CHUNK, (ck + 1) * FF_CHUNK)
        vc = slice(D_FF + ck * FF_CHUNK, D_FF + (ck + 1) * FF_CHUNK)
        return _dot(h, wup_ref[:, gc]), _dot(h, wup_ref[:, vc])

    n_chunks = D_FF // FF_CHUNK
    acc = x_ref[...]
    nxt = up(0)
    for ck in range(n_chunks):
        ug, uv = nxt
        if ck + 1 < n_chunks:
            nxt = up(ck + 1)
        gc = slice(ck * FF_CHUNK, (ck + 1) * FF_CHUNK)
        vc = slice(D_FF + ck * FF_CHUNK, D_FF + (ck + 1) * FF_CHUNK)
        gate = conv(ug, gc)
        val = conv(uv, vc)
        act = (gate * jax.nn.sigmoid(gate) * val).astype(BF16)
        acc = acc + _dot(act, wdn_ref[gc, :])
    if final_norm:
        acc = _rms(acc, gfin_ref[...], NORM_EPS)
    o_ref[...] = acc


def _ffn(x1, h2, w_up, conv_w, conv_b, w_down, g_final, seq, final_norm):
    t = x1.shape[0]
    tm = TM_FFN
    kern = functools.partial(_ffn_kernel, tiles_per_seq=seq // tm, final_norm=final_norm)
    return pl.pallas_call(
        kern,
        grid=(t // tm,),
        in_specs=[_row_spec(tm, D_MODEL), _row_spec(tm, D_MODEL), _const_spec((D_MODEL, 2 * D_FF)),
                  _const_spec((3, 2 * D_FF)), _const_spec((1, 2 * D_FF)), _const_spec((D_FF, D_MODEL)),
                  _const_spec((1, D_MODEL))],
        out_specs=_row_spec(tm, D_MODEL),
        out_shape=jax.ShapeDtypeStruct((t, D_MODEL), F32),
        scratch_shapes=[pltpu.VMEM((8 + tm, 2 * D_FF), F32)],
        compiler_params=_params(),
        name="conv_ffn",
    )(x1, h2, w_up, conv_w, conv_b, w_down, g_final)


def _mla_weights(w_uq, w_ukv):
    qd = MLA_NOPE + MLA_ROPE
    half = MLA_ROPE // 2
    wq = w_uq.reshape(MLA_Q_LORA, MLA_HEADS, qd)
    zq = jnp.zeros((MLA_Q_LORA, MLA_HEADS, LANES - qd), F32)
    wq_main = jnp.concatenate([wq, zq], axis=-1)
    x1 = wq[:, :, MLA_NOPE:MLA_NOPE + half]
    x2 = wq[:, :, MLA_NOPE + half:]
    wq_rot = jnp.concatenate([jnp.zeros((MLA_Q_LORA, MLA_HEADS, MLA_NOPE), F32), -x2, x1, zq], axis=-1)
    wkv = w_ukv.reshape(MLA_KV_LORA, MLA_HEADS, MLA_NOPE + MLA_V)
    zkv = jnp.zeros((MLA_KV_LORA, MLA_HEADS, LANES - MLA_NOPE), F32)
    wk = jnp.concatenate([wkv[:, :, :MLA_NOPE], zkv], axis=-1)
    wv = jnp.concatenate([wkv[:, :, MLA_NOPE:], zkv], axis=-1)
    flat = lambda w: w.reshape(w.shape[0], -1).astype(BF16)
    return flat(wq_main), flat(wq_rot), flat(wk), flat(wv).T


def kernel(x, positions, rel_bias, norm_mix, w_in, b_gate, rwkv_mu, rwkv_w0, rwkv_w2, rwkv_a0, rwkv_a2, rwkv_g2, rwkv_k_k, rwkv_k_a, rwkv_r_k, rwkv_ln_w, rwkv_ln_b, mla_q_norm, mla_w_uq, mla_kv_norm, mla_w_ukv, diff_lambda, diff_subln, w_branch_rwkv, w_branch_mla, w_branch_diff, w_o, norm_ffn, ffn_w_up, ffn_conv_w, ffn_conv_b, ffn_w_down, norm_final):
    batch, seq, _ = x.shape
    depth = w_in.shape[0]
    t = batch * seq
    assert seq % TQ == 0 and seq % TM == 0 and seq % TM_FFN == 0 and seq % RWKV_TILE == 0 and RWKV_TILE % CHUNK == 0
    xf = x.reshape(t, D_MODEL)
    pos_col = positions.reshape(t, 1)
    inv_freq = ROPE_BASE ** (-jnp.arange(0, MLA_ROPE, 2, dtype=F32) / MLA_ROPE)
    freq = jnp.concatenate([jnp.zeros((MLA_NOPE,), F32), inv_freq, inv_freq,
                            jnp.zeros((LANES - MLA_NOPE - MLA_ROPE,), F32)]).reshape(1, LANES)
    row = lambda v: v.reshape(1, -1)
    diff_scale = jnp.concatenate([jnp.full((512,), DIFF_QK ** -0.5 * LOG2E, F32), jnp.ones((512,), F32)])

    for l in range(depth):
        s0, s1, s2 = RWKV_COLS, RWKV_COLS + MLA_COLS, RWKV_COLS + MLA_COLS + DIFF_COLS
        sv = s1 + DIFF_QK_COLS
        w = w_in[l]
        w_all = jnp.concatenate([w[:, :s0], w[:, s0:s1], jnp.zeros((D_MODEL, MLA_PAD - MLA_COLS), F32),
                                 w[:, s1:sv] * diff_scale, w[:, s2:]], axis=1).astype(BF16)
        w_vt = w[:, sv:s2].T.astype(BF16)
        p_rwkv, p_mla, p_qk, p_vt, p_gate = _in_proj(xf, row(norm_mix[l]), w_all, w_vt)

        o_r = _rwkv(p_rwkv, seq, row(rwkv_mu[l]), row(rwkv_w0[l]), rwkv_w2[l].astype(BF16), row(rwkv_a0[l]),
                    rwkv_a2[l].astype(BF16), rwkv_g2[l].astype(BF16), row(rwkv_k_k[l]), row(rwkv_k_a[l]),
                    row(rwkv_r_k[l]), row(rwkv_ln_w[l]), row(rwkv_ln_b[l]))

        wq, wqr, wk, wv = _mla_weights(mla_w_uq[l], mla_w_ukv[l])
        q_m, k_m, v_m = _mla_prep(p_mla, pos_col, freq, row(mla_q_norm[l]), row(mla_kv_norm[l]), wq, wqr, wk, wv)
        o_m = _mla_attn(q_m, k_m, v_m, batch, seq)

        o_d = _diff_attn(p_qk, p_vt, rel_bias, diff_lambda[l], row(diff_subln[l]), batch, seq, l)

        x1, h2 = _merge(xf, p_gate, row(b_gate[l]), o_r, o_m, o_d, w_branch_rwkv[l].astype(BF16),
                        w_branch_mla[l].astype(BF16), w_branch_diff[l].astype(BF16), w_o[l].astype(BF16),
                        row(norm_ffn[l]))
        xf = _ffn(x1, h2, ffn_w_up[l].astype(BF16), ffn_conv_w[l], row(ffn_conv_b[l]), ffn_w_down[l].astype(BF16),
                  row(norm_final), seq, final_norm=(l == depth - 1))
    return xf.reshape(batch, seq, D_MODEL)
```

```python
import functools
import math

import numpy as np
import jax
import jax.numpy as jnp
from jax import lax
from jax.experimental import pallas as pl
from jax.experimental.pallas import tpu as pltpu

F32 = jnp.float32
BF16 = jnp.bfloat16

D_MODEL = 1024
RWKV_HEADS = 8
RWKV_N = 64
RWKV_DIM = 512
RWKV_COLS = 1792
RWKV_GN_EPS = 64e-5
MLA_HEADS = 8
MLA_Q_LORA = 256
MLA_KV_LORA = 128
MLA_NOPE = 64
MLA_ROPE = 32
MLA_V = 64
MLA_COLS = 416
MLA_PAD = 512
ROPE_BASE = 10000.0
DIFF_HEADS = 4
DIFF_QK = 64
DIFF_V = 128
DIFF_COLS = 1536
DIFF_QK_COLS = 1024
DIFF_VT_ROWS = DIFF_V + 16
REL_BUCKETS = 32
REL_MAX_DISTANCE = 128
D_FF = 2816
GATE_COLS = 3072
NORM_EPS = 1e-6
SUBLN_EPS = 1e-5

LANES = 128
VMEM_LIMIT = 56 * 1024 * 1024
TM = 256
TM_FFN = 256
CHUNK = 64
RWKV_TILE = 256
TQ = 256
FF_CHUNK = 256
FF_DOWN_GROUP = 11
NEG = -1e30
LOG2E = 1.4426950408889634

_NT = (((1,), (1,)), ((), ()))
_TN = (((0,), (0,)), ((), ()))


def _dot(a, b):
    return jnp.dot(a, b, preferred_element_type=F32)


def _dot_nt(a, b):
    return lax.dot_general(a, b, _NT, preferred_element_type=F32)


def _dot_tn(a, b):
    return lax.dot_general(a, b, _TN, preferred_element_type=F32)


def _rms(x, g, eps):
    return x * lax.rsqrt(jnp.mean(x * x, axis=-1, keepdims=True) + eps) * g


def _params():
    return pltpu.CompilerParams(dimension_semantics=("arbitrary",), vmem_limit_bytes=VMEM_LIMIT)


def _params2():
    return pltpu.CompilerParams(dimension_semantics=("arbitrary", "arbitrary"), vmem_limit_bytes=VMEM_LIMIT)


def _const_spec(shape):
    return pl.BlockSpec(shape, lambda *_: (0,) * len(shape), pipeline_mode=pl.Buffered(1))


def _row_spec(tm, cols, col_block=0):
    return pl.BlockSpec((tm, cols), lambda i: (i, col_block))


def _in_proj_kernel(x_ref, g_ref, w_ref, wvt_ref, pr_ref, pm_ref, pqk_ref, pvt_ref, pg_ref):
    h = _rms(x_ref[...], g_ref[...], NORM_EPS).astype(BF16)
    c0 = 0
    pr_ref[...] = _dot(h, w_ref[:, c0:c0 + RWKV_COLS])
    c0 += RWKV_COLS
    pm_ref[...] = _dot(h, w_ref[:, c0:c0 + MLA_PAD])
    c0 += MLA_PAD
    pqk_ref[...] = _dot(h, w_ref[:, c0:c0 + DIFF_QK_COLS]).astype(BF16)
    c0 += DIFF_QK_COLS
    pg_ref[...] = _dot(h, w_ref[:, c0:c0 + GATE_COLS]).astype(BF16)
    pvt_ref[...] = _dot_nt(wvt_ref[...], h).astype(BF16)


def _in_proj(x, g, w_all, w_vt):
    t = x.shape[0]
    n_all = w_all.shape[1]
    vw = DIFF_HEADS * DIFF_V
    return pl.pallas_call(
        _in_proj_kernel,
        grid=(t // TM,),
        in_specs=[_row_spec(TM, D_MODEL), _const_spec((1, D_MODEL)), _const_spec((D_MODEL, n_all)),
                  _const_spec((vw, D_MODEL))],
        out_specs=[_row_spec(TM, RWKV_COLS), _row_spec(TM, MLA_PAD), _row_spec(TM, DIFF_QK_COLS),
                   pl.BlockSpec((vw, TM), lambda i: (0, i)), _row_spec(TM, GATE_COLS)],
        out_shape=[jax.ShapeDtypeStruct((t, RWKV_COLS), F32), jax.ShapeDtypeStruct((t, MLA_PAD), F32),
                   jax.ShapeDtypeStruct((t, DIFF_QK_COLS), BF16), jax.ShapeDtypeStruct((vw, t), BF16),
                   jax.ShapeDtypeStruct((t, GATE_COLS), BF16)],
        compiler_params=_params(),
        name="in_proj",
    )(x, g, w_all, w_vt)


def _split3(a):
    hi = a.astype(BF16)
    r1 = a - hi.astype(F32)
    mid = r1.astype(BF16)
    lo = (r1 - mid.astype(F32)).astype(BF16)
    return hi, mid, lo


def _softplus(z):
    return jnp.maximum(z, 0.0) + jnp.log(1.0 + jnp.exp(-jnp.abs(z)))


def _rwkv_kernel(p_ref, mu_ref, w0_ref, w2_ref, a0_ref, a2_ref, g2_ref, kk_ref, ka_ref, rk_ref, lnw_ref, lnb_ref,
                 o_ref, carry_ref, state_ref, *, tiles_per_seq):
    c = CHUNK
    n = RWKV_N
    ts = RWKV_TILE
    nc = ts // c

    @pl.when(pl.program_id(0) % tiles_per_seq == 0)
    def _():
        carry_ref[...] = jnp.zeros_like(carry_ref)
        state_ref[...] = jnp.zeros_like(state_ref)

    p = p_ref[...]
    row = lax.broadcasted_iota(jnp.int32, (ts, 1), 0)
    shifted = jnp.where(row == 0, carry_ref[...], pltpu.roll(p, 1, 0))
    carry_ref[...] = p[ts - 1:ts, :]
    pm = p + (shifted - p) * mu_ref[...]

    r = pm[:, 0:512]
    k = pm[:, 512:1024]
    v = pm[:, 1024:1536]
    pw = pm[:, 1536:1600]
    pa = pm[:, 1600:1664]
    pg = pm[:, 1664:1792]

    w_log = -_softplus(-(w0_ref[...] + _dot(jnp.tanh(pw).astype(BF16), w2_ref[...]))) - 0.5
    logd = -jnp.exp(w_log)
    a = jax.nn.sigmoid(a0_ref[...] + _dot(pa.astype(BF16), a2_ref[...]))
    g = _dot(jax.nn.sigmoid(pg).astype(BF16), g2_ref[...])

    tr = lax.broadcasted_iota(jnp.int32, (ts, ts), 0)
    tc = lax.broadcasted_iota(jnp.int32, (ts, ts), 1)
    tri = ((tr >= tc) & (tr // c == tc // c)).astype(BF16)
    hi, mid, lo = _split3(logd)
    cs = _dot(tri, hi) + _dot(tri, mid) + _dot(tri, lo)
    total = jnp.concatenate([jnp.broadcast_to(cs[(m + 1) * c - 1:(m + 1) * c, :], (c, RWKV_DIM)) for m in range(nc)], axis=0)
    e_in = jnp.exp(cs)
    e_ex = jnp.exp(cs - logd)
    e_inv = jnp.exp(-cs)
    e_end = jnp.exp(total - cs)
    g_end = jnp.exp(total)

    n_pairs = RWKV_HEADS // 2
    lo_t = lax.broadcasted_iota(jnp.int32, (ts, LANES), 1) < n
    lo_c = lax.broadcasted_iota(jnp.int32, (c, LANES), 1) < n

    def head_sum(x):
        blocks = []
        for q in range(n_pairs):
            xb = x[:, q * LANES:(q + 1) * LANES]
            s_lo = jnp.sum(jnp.where(lo_t, xb, 0.0), axis=-1, keepdims=True)
            s_hi = jnp.sum(jnp.where(lo_t, 0.0, xb), axis=-1, keepdims=True)
            blocks.append(jnp.where(lo_t, s_lo, s_hi))
        return jnp.concatenate(blocks, axis=-1)

    k2 = k * (1.0 + (a - 1.0) * ka_ref[...])
    kku = k * kk_ref[...]
    kkn = kku / jnp.maximum(jnp.sqrt(head_sum(kku * kku)), 1e-12)
    b = kkn * a
    full = {"at": -kkn * e_ex, "rt": r * e_in, "bt": b * e_inv, "kt": k2 * e_inv, "v": v,
            "be": b * e_end, "ke": k2 * e_end}

    ti = lax.broadcasted_iota(jnp.int32, (2 * c, 4 * c), 0) % c
    si = lax.broadcasted_iota(jnp.int32, (2 * c, 4 * c), 1) % c
    strict = ti > si
    incl = ti >= si
    eye = (lax.broadcasted_iota(jnp.int32, (2 * c, 2 * c), 0)
           == lax.broadcasted_iota(jnp.int32, (2 * c, 2 * c), 1)).astype(F32)

    items = [(m, q) for m in range(nc) for q in range(n_pairs)]

    def stacked(name, m, q):
        xb = full[name][m * c:(m + 1) * c, q * LANES:(q + 1) * LANES]
        return jnp.concatenate([jnp.where(lo_c, xb, 0.0), jnp.where(lo_c, 0.0, xb)], axis=0).astype(BF16)

    st_ops = {name: [stacked(name, m, q) for m, q in items] for name in full}
    big_l = [_dot_nt(jnp.concatenate([at, rtb], axis=0), jnp.concatenate([bt, kt], axis=0))
             for at, rtb, bt, kt in zip(st_ops["at"], st_ops["rt"], st_ops["bt"], st_ops["kt"])]
    top_l = [jnp.where(strict, big[:2 * c, :], 0.0) for big in big_l]
    bot_l = [jnp.where(incl, big[2 * c:, :], 0.0).astype(BF16) for big in big_l]
    lakv_l = [_dot(top[:, 2 * c:].astype(BF16), vb) for top, vb in zip(top_l, st_ops["v"])]

    x_l = [top[:, :2 * c] for top in top_l]
    tinv_l = [eye + x for x in x_l]
    xb_l = [x.astype(BF16) for x in x_l]
    for _ in range(int(math.log2(c)) - 1):
        x_l = [_dot(xb, xb) for xb in xb_l]
        xb_l = [x.astype(BF16) for x in x_l]
        tinv_l = [tinv + _dot(tinv.astype(BF16), xb) for tinv, xb in zip(tinv_l, xb_l)]
    tinvb_l = [tinv.astype(BF16) for tinv in tinv_l]
    abar_l = [_dot(tb, at).astype(BF16) for tb, at in zip(tinvb_l, st_ops["at"])]
    vbar_l = [_dot(tb, lv.astype(BF16)).astype(BF16) for tb, lv in zip(tinvb_l, lakv_l)]
    rhat_l = [(rtb.astype(F32) + _dot(bot[:, :2 * c], ab)).astype(BF16)
              for rtb, bot, ab in zip(st_ops["rt"], bot_l, abar_l)]
    uv_l = [jnp.concatenate([vbar, vb], axis=0) for vbar, vb in zip(vbar_l, st_ops["v"])]
    y0_l = [_dot(bot, uv) for bot, uv in zip(bot_l, uv_l)]
    p_l = [_dot_tn(ab, be).astype(BF16) for ab, be in zip(abar_l, st_ops["be"])]
    q_l = [_dot_tn(uv, jnp.concatenate([be, ke], axis=0)) for uv, be, ke in zip(uv_l, st_ops["be"], st_ops["ke"])]

    st_l = [state_ref[q] for q in range(n_pairs)]
    y_rows = []
    for m in range(nc):
        y_blocks = []
        for q in range(n_pairs):
            idx = m * n_pairs + q
            st = st_l[q]
            st_b = st.astype(BF16)
            y_s = _dot_nt(rhat_l[idx], st_b) + y0_l[idx]
            y_blocks.append(y_s[:c] + y_s[c:])
            decay = g_end[(m + 1) * c - 1:(m + 1) * c, q * LANES:(q + 1) * LANES]
            st_l[q] = st * decay + _dot(st_b, p_l[idx]) + q_l[idx]
        y_rows.append(jnp.concatenate(y_blocks, axis=-1))
    for q in range(n_pairs):
        state_ref[q] = st_l[q]
    y = jnp.concatenate(y_rows, axis=0)

    inv_n = 1.0 / n
    yc = y - head_sum(y) * inv_n
    var = head_sum(yc * yc) * inv_n
    yn = yc * lax.rsqrt(var + RWKV_GN_EPS) * lnw_ref[...] + lnb_ref[...]
    bonus = head_sum(r * k2 * rk_ref[...]) * v
    o_ref[...] = ((yn + bonus) * g).astype(BF16)


def _rwkv(p_rwkv, seq, mu, w0, w2, a0, a2, g2, k_k, k_a, r_k, ln_w, ln_b):
    t = p_rwkv.shape[0]
    ts = RWKV_TILE
    kern = functools.partial(_rwkv_kernel, tiles_per_seq=seq // ts)
    vec = lambda: _const_spec((1, RWKV_DIM))
    return pl.pallas_call(
        kern,
        grid=(t // ts,),
        in_specs=[_row_spec(ts, RWKV_COLS), _const_spec((1, RWKV_COLS)), vec(), _const_spec((64, RWKV_DIM)), vec(),
                  _const_spec((64, RWKV_DIM)), _const_spec((128, RWKV_DIM)), vec(), vec(), vec(), vec(), vec()],
        out_specs=_row_spec(ts, RWKV_DIM),
        out_shape=jax.ShapeDtypeStruct((t, RWKV_DIM), BF16),
        scratch_shapes=[pltpu.VMEM((1, RWKV_COLS), F32), pltpu.VMEM((RWKV_HEADS // 2, LANES, LANES), F32)],
        compiler_params=_params(),
        name="rwkv7",
    )(p_rwkv, mu, w0, w2, a0, a2, g2, k_k, k_a, r_k, ln_w, ln_b)


def _rope_kernel(pos_ref, freq_ref, cos_out, sin_out):
    ang = pos_ref[...].astype(F32) * freq_ref[...]
    cos_out[...] = jnp.cos(ang)
    sin_out[...] = jnp.sin(ang)


def _rope_tables(pos_col, freq):
    t = pos_col.shape[0]
    return pl.pallas_call(
        _rope_kernel,
        grid=(t // TM,),
        in_specs=[_row_spec(TM, 1), _const_spec((1, LANES))],
        out_specs=[_row_spec(TM, LANES), _row_spec(TM, LANES)],
        out_shape=[jax.ShapeDtypeStruct((t, LANES), F32), jax.ShapeDtypeStruct((t, LANES), F32)],
        compiler_params=_params(),
        name="rope_tables",
    )(pos_col, freq)


def _mla_prep_kernel(p_ref, cos_ref, sin_ref, qn_ref, kvn_ref, wq_ref, wqr_ref, wk_ref, wv_ref, q_out, k_out, v_out):
    p = p_ref[...]
    hq = _rms(p[:, 0:MLA_Q_LORA], qn_ref[...], NORM_EPS).astype(BF16)
    hkv = _rms(p[:, MLA_Q_LORA:MLA_Q_LORA + MLA_KV_LORA], kvn_ref[...], NORM_EPS).astype(BF16)
    blk = p[:, 384:512]
    cos = cos_ref[...]
    sin = sin_ref[...]
    scale = (MLA_NOPE + MLA_ROPE) ** -0.5 * LOG2E
    qa = _dot(hq, wq_ref[...])
    qr = _dot(hq, wqr_ref[...])
    lane = lax.broadcasted_iota(jnp.int32, blk.shape, 1)
    kr = pltpu.roll(blk, 64, 1)
    rot = jnp.where(lane < 80, -pltpu.roll(blk, 48, 1), pltpu.roll(blk, 80, 1))
    rot = jnp.where((lane >= 64) & (lane < 96), rot, 0.0)
    krope = kr * cos + rot * sin
    kn = _dot(hkv, wk_ref[...])
    for h in range(MLA_HEADS):
        hs = slice(h * LANES, (h + 1) * LANES)
        q_out[:, hs] = ((qa[:, hs] * cos + qr[:, hs] * sin) * scale).astype(BF16)
        k_out[:, hs] = (kn[:, hs] + krope).astype(BF16)
    vt = _dot_nt(wv_ref[...], hkv)
    rowi = lax.broadcasted_iota(jnp.int32, vt.shape, 0)
    v_out[...] = (vt + (rowi % LANES == MLA_V).astype(F32)).astype(BF16)


def _mla_prep(p_mla, cos, sin, q_norm, kv_norm, wq, wqr, wk, wv):
    t = p_mla.shape[0]
    hw = MLA_HEADS * LANES
    return pl.pallas_call(
        _mla_prep_kernel,
        grid=(t // TM,),
        in_specs=[_row_spec(TM, MLA_PAD), _row_spec(TM, LANES), _row_spec(TM, LANES), _const_spec((1, MLA_Q_LORA)),
                  _const_spec((1, MLA_KV_LORA)), _const_spec((MLA_Q_LORA, hw)), _const_spec((MLA_Q_LORA, hw)),
                  _const_spec((MLA_KV_LORA, hw)), _const_spec((hw, MLA_KV_LORA))],
        out_specs=[_row_spec(TM, hw), _row_spec(TM, hw), pl.BlockSpec((hw, TM), lambda i: (0, i))],
        out_shape=[jax.ShapeDtypeStruct((t, hw), BF16), jax.ShapeDtypeStruct((t, hw), BF16),
                   jax.ShapeDtypeStruct((hw, t), BF16)],
        compiler_params=_params(),
        name="mla_prep",
    )(p_mla, cos, sin, q_norm, kv_norm, wq, wqr, wk, wv)


def _softmax_stage_t(s_l, pv_fn, m_scr, acc_scr, idx):
    n = len(s_l)
    m_old = [m_scr[c][0:1, :] for c in idx]
    m_new = [jnp.maximum(m, jnp.max(s, axis=0, keepdims=True)) for m, s in zip(m_old, s_l)]
    p_l = [jnp.exp2(s - m).astype(BF16) for s, m in zip(s_l, m_new)]
    alpha = [jnp.exp2(mo - mn) for mo, mn in zip(m_old, m_new)]
    pv_l = [pv_fn(c, p_l[c]) for c in range(n)]
    for c in range(n):
        m_scr[idx[c]] = jnp.broadcast_to(m_new[c], m_scr.shape[1:])
        acc_scr[idx[c]] = alpha[c] * acc_scr[idx[c]] + pv_l[c]


def _mla_attn_kernel(q_ref, k_ref, vt_ref, o_ref, m_scr, acc_scr):
    tq = TQ
    i = pl.program_id(1)
    causal = lax.broadcasted_iota(jnp.int32, (tq, tq), 0) <= lax.broadcasted_iota(jnp.int32, (tq, tq), 1)
    lane = lax.broadcasted_iota(jnp.int32, (tq, LANES), 1)
    m_scr[...] = jnp.full(m_scr.shape, NEG, F32)
    acc_scr[...] = jnp.zeros(acc_scr.shape, F32)
    hsl = [slice(h * LANES, (h + 1) * LANES) for h in range(MLA_HEADS)]
    heads = list(range(MLA_HEADS))

    def tile(off, tk, mask):
        s_l = [_dot_nt(k_ref[pl.ds(off, tk), hs], q_ref[:, hs]) for hs in hsl]
        if mask:
            s_l = [jnp.where(causal, s, NEG) for s in s_l]
        _softmax_stage_t(s_l, lambda c, p: _dot(vt_ref[hsl[c], pl.ds(off, tk)], p), m_scr, acc_scr, heads)

    def pair_body(j, carry):
        tile(pl.multiple_of(j * 2 * tq, 2 * tq), 2 * tq, False)
        return carry

    def single_body(j, carry):
        tile(pl.multiple_of(j * tq, tq), tq, False)
        return carry

    n_pairs = i // 2
    lax.fori_loop(0, n_pairs, pair_body, 0)
    lax.fori_loop(2 * n_pairs, i, single_body, 0)
    tile(pl.multiple_of(i * tq, tq), tq, True)
    for pr in range(MLA_HEADS // 2):
        outs = []
        for hh in range(2):
            acc = acc_scr[2 * pr + hh]
            outs.append(jnp.transpose(acc / acc[MLA_V:MLA_V + 1, :]))
        o_ref[:, pr * LANES:(pr + 1) * LANES] = jnp.where(lane < MLA_V, outs[0], pltpu.roll(outs[1], MLA_V, 1)).astype(BF16)


def _mla_attn(q, k, vt, batch, seq):
    t = q.shape[0]
    nq = seq // TQ
    hw = MLA_HEADS * LANES
    vw = MLA_HEADS * MLA_V
    return pl.pallas_call(
        _mla_attn_kernel,
        grid=(batch, nq),
        in_specs=[pl.BlockSpec((TQ, hw), lambda b, i: (b * nq + i, 0)),
                  pl.BlockSpec((seq, hw), lambda b, i: (b, 0), pipeline_mode=pl.Buffered(1)),
                  pl.BlockSpec((hw, seq), lambda b, i: (0, b), pipeline_mode=pl.Buffered(1))],
        out_specs=pl.BlockSpec((TQ, vw), lambda b, i: (b * nq + i, 0)),
        out_shape=jax.ShapeDtypeStruct((t, vw), BF16),
        scratch_shapes=[pltpu.VMEM((MLA_HEADS, 8, TQ), F32), pltpu.VMEM((MLA_HEADS, LANES, TQ), F32)],
        compiler_params=_params2(),
        name="mla_attn",
    )(q, k, vt)


def _t5_bucket_table():
    n = np.arange(0, REL_MAX_DISTANCE + 1)
    max_exact = REL_BUCKETS // 2
    nf = np.maximum(n, 1).astype(np.float32)
    ratio = np.log(nf / np.float32(max_exact)) / np.float32(math.log(REL_MAX_DISTANCE / max_exact))
    large = max_exact + (ratio * np.float32(REL_BUCKETS - max_exact)).astype(np.int32)
    large = np.minimum(large, REL_BUCKETS - 1)
    return np.where(n < max_exact, n, large)


_BUCKETS = _t5_bucket_table()
_FAR_BUCKET = int(_BUCKETS[REL_MAX_DISTANCE])
assert _FAR_BUCKET == REL_BUCKETS - 1 and np.all(np.diff(_BUCKETS) >= 0)
_BUCKET_STARTS = [int(np.argmax(_BUCKETS >= b)) for b in range(REL_BUCKETS // 2 + 1, REL_BUCKETS)]


def _diff_attn_kernel(rb_ref, lam_ref, subln_ref, q_ref, k_ref, vt_ref, o_ref, bias_scr, m_scr, acc_scr, *, lambda_init):
    tq = TQ
    i = pl.program_id(1)
    r_i = lax.broadcasted_iota(jnp.int32, (tq, tq), 0)
    c_i = lax.broadcasted_iota(jnp.int32, (tq, tq), 1)
    causal = r_i <= c_i

    @pl.when((pl.program_id(0) == 0) & (i == 0))
    def _():
        for t_idx, delta in enumerate((0, tq)):
            d = jnp.maximum(c_i - r_i + delta, 0)
            log_b = REL_BUCKETS // 2
            for start in _BUCKET_STARTS:
                log_b = log_b + (d >= start).astype(jnp.int32)
            bucket = jnp.where(d < REL_BUCKETS // 2, d, log_b)
            for h in range(DIFF_HEADS):
                bias = jnp.zeros((tq, tq), F32)
                for b in range(REL_BUCKETS):
                    bias = jnp.where(bucket == b, rb_ref[b, h], bias)
                bias_scr[t_idx, h] = (bias - rb_ref[_FAR_BUCKET, h]) * LOG2E

    m_scr[...] = jnp.full(m_scr.shape, NEG, F32)
    acc_scr[...] = jnp.zeros(acc_scr.shape, F32)
    lane = lax.broadcasted_iota(jnp.int32, (tq, LANES), 1)
    ones_rows = {tk: (lax.broadcasted_iota(jnp.int32, (DIFF_VT_ROWS - DIFF_V, tk), 0) == 0).astype(BF16)
                 for tk in (tq, 2 * tq)}
    hsl = [slice(h * LANES, (h + 1) * LANES) for h in range(DIFF_HEADS)]
    heads = list(range(DIFF_HEADS))
    q_l = []
    for hs in hsl:
        qp = q_ref[:, hs]
        q_l.append(jnp.concatenate([jnp.where(lane < DIFF_QK, qp, jnp.zeros_like(qp)),
                                    jnp.where(lane >= DIFF_QK, qp, jnp.zeros_like(qp))], axis=0))

    def tile(off, tk, bias_idx, mask):
        s_l = [_dot_nt(k_ref[pl.ds(off, tk), hsl[h]], q_l[h]) for h in heads]
        if bias_idx is not None:
            s_l = [s + jnp.concatenate([bias_scr[bias_idx, h]] * 2, axis=1) for h, s in enumerate(s_l)]
        if mask:
            mask2 = jnp.concatenate([causal, causal], axis=1)
            s_l = [jnp.where(mask2, s, NEG) for s in s_l]

        def pv(c, p):
            vt = jnp.concatenate([vt_ref[hsl[c], pl.ds(off, tk)], ones_rows[tk]], axis=0)
            return jnp.concatenate([_dot(vt, p[:, :tq]), _dot(vt, p[:, tq:])], axis=1)

        _softmax_stage_t(s_l, pv, m_scr, acc_scr, heads)

    def far_pair_body(j, carry):
        tile(pl.multiple_of(j * 2 * tq, 2 * tq), 2 * tq, None, False)
        return carry

    def far_body(j, carry):
        tile(pl.multiple_of(j * tq, tq), tq, None, False)
        return carry

    def near_body(j, carry):
        tile(pl.multiple_of(j * tq, tq), tq, 1, False)
        return carry

    n_far = jnp.maximum(i - 1, 0)
    n_pairs = n_far // 2
    lax.fori_loop(0, n_pairs, far_pair_body, 0)
    lax.fori_loop(2 * n_pairs, n_far, far_body, 0)
    lax.fori_loop(n_far, i, near_body, 0)
    tile(pl.multiple_of(i * tq, tq), tq, 0, True)

    lam = lam_ref[...]
    lam_full = (jnp.exp(jnp.sum(lam[0:1] * lam[1:2], axis=-1, keepdims=True))
                - jnp.exp(jnp.sum(lam[2:3] * lam[3:4], axis=-1, keepdims=True)) + lambda_init)
    for h in heads:
        acc = acc_scr[h]
        on = acc[:DIFF_V, :] / acc[DIFF_V:DIFF_V + 1, :]
        ot = on[:, :tq] - lam_full * on[:, tq:]
        ot = ot * lax.rsqrt(jnp.mean(ot * ot, axis=0, keepdims=True) + SUBLN_EPS)
        o_ref[:, hsl[h]] = (jnp.transpose(ot) * subln_ref[...] * (1.0 - lambda_init)).astype(BF16)


def _diff_attn(p_qk, p_vt, rel_bias, lam, subln, batch, seq, layer_idx):
    t = p_qk.shape[0]
    nq = seq // TQ
    w = DIFF_HEADS * LANES
    lambda_init = 0.8 - 0.6 * math.exp(-0.3 * layer_idx)
    kern = functools.partial(_diff_attn_kernel, lambda_init=lambda_init)
    const2 = lambda shape: pl.BlockSpec(shape, lambda b, i: (0, 0))
    return pl.pallas_call(
        kern,
        grid=(batch, nq),
        in_specs=[pl.BlockSpec(memory_space=pltpu.SMEM),
                  const2((4, DIFF_QK)), const2((1, DIFF_V)),
                  pl.BlockSpec((TQ, w), lambda b, i: (b * nq + i, 0)),
                  pl.BlockSpec((seq, w), lambda b, i: (b, 1), pipeline_mode=pl.Buffered(1)),
                  pl.BlockSpec((w, seq), lambda b, i: (0, b), pipeline_mode=pl.Buffered(1))],
        out_specs=pl.BlockSpec((TQ, w), lambda b, i: (b * nq + i, 0)),
        out_shape=jax.ShapeDtypeStruct((t, w), BF16),
        scratch_shapes=[pltpu.VMEM((2, DIFF_HEADS, TQ, TQ), F32), pltpu.VMEM((DIFF_HEADS, 8, 2 * TQ), F32),
                        pltpu.VMEM((DIFF_HEADS, DIFF_VT_ROWS, 2 * TQ), F32)],
        compiler_params=_params2(),
        name="diff_attn",
    )(rel_bias, lam, subln, p_qk, p_qk, p_vt)


def _merge_kernel(x_ref, pg_ref, bg_ref, or_ref, om_ref, od_ref, wr_ref, wm_ref, wd_ref, wo_ref, gf_ref, x_out, h_out):
    def gate(idx):
        cs = slice(idx * D_MODEL, (idx + 1) * D_MODEL)
        return jax.nn.sigmoid(pg_ref[:, cs].astype(F32) + bg_ref[:, cs])

    merged = gate(0) * _dot(or_ref[...], wr_ref[...])
    merged = merged + gate(1) * _dot(om_ref[...], wm_ref[...])
    merged = merged + gate(2) * _dot(od_ref[...], wd_ref[...])
    x1 = x_ref[...] + _dot(merged.astype(BF16), wo_ref[...])
    x_out[...] = x1
    h_out[...] = _rms(x1, gf_ref[...], NORM_EPS).astype(BF16)


def _merge(x, p_gate, b_gate, o_r, o_m, o_d, w_r, w_m, w_d, w_o, g_ffn):
    t = x.shape[0]
    return pl.pallas_call(
        _merge_kernel,
        grid=(t // TM,),
        in_specs=[_row_spec(TM, D_MODEL), _row_spec(TM, GATE_COLS), _const_spec((1, GATE_COLS)),
                  _row_spec(TM, 512), _row_spec(TM, 512), _row_spec(TM, 512),
                  _const_spec((512, D_MODEL)), _const_spec((512, D_MODEL)), _const_spec((512, D_MODEL)),
                  _const_spec((D_MODEL, D_MODEL)), _const_spec((1, D_MODEL))],
        out_specs=[_row_spec(TM, D_MODEL), _row_spec(TM, D_MODEL)],
        out_shape=[jax.ShapeDtypeStruct((t, D_MODEL), F32), jax.ShapeDtypeStruct((t, D_MODEL), BF16)],
        compiler_params=_params(),
        name="merge",
    )(x, p_gate, b_gate, o_r, o_m, o_d, w_r, w_m, w_d, w_o, g_ffn)


def _ffn_kernel(x_ref, h_ref, wup_ref, cw_ref, cb_ref, wdn_ref, gfin_ref, o_ref, carry_ref, *, tiles_per_seq, final_norm):
    tm = TM_FFN

    @pl.when(pl.program_id(0) % tiles_per_seq == 0)
    def _():
        carry_ref[0:8, :] = jnp.zeros((8, carry_ref.shape[1]), F32)

    h = h_ref[...]

    def conv(u, cols):
        carry_ref[8:, cols] = u
        u1 = carry_ref[7:7 + tm, cols]
        u2 = carry_ref[6:6 + tm, cols]
        carry_ref[0:8, cols] = u[tm - 8:tm, :]
        return cw_ref[0:1, cols] * u2 + cw_ref[1:2, cols] * u1 + cw_ref[2:3, cols] * u + cb_ref[:, cols]

    def up(ck):
        gc = slice(ck * FF_CHUNK, (ck + 1) * FF_CHUNK)
        vc = slice(D_FF + ck * FF_CHUNK, D_FF + (ck + 1) * FF_CHUNK)
        return _dot(h, wup_ref[:, gc]), _dot(h, wup_ref[:, vc])

    n_chunks = D_FF // FF_CHUNK
    acc = x_ref[...]
    nxt = up(0)
    acts = []
    for ck in range(n_chunks):
        ug, uv = nxt
        if ck + 1 < n_chunks:
            nxt = up(ck + 1)
        gc = slice(ck * FF_CHUNK, (ck + 1) * FF_CHUNK)
        vc = slice(D_FF + ck * FF_CHUNK, D_FF + (ck + 1) * FF_CHUNK)
        gate = conv(ug, gc)
        val = conv(uv, vc)
        acts.append((gate * jax.nn.sigmoid(gate) * val).astype(BF16))
        if len(acts) == FF_DOWN_GROUP or ck + 1 == n_chunks:
            rows = slice((ck + 1 - len(acts)) * FF_CHUNK, (ck + 1) * FF_CHUNK)
            acc = acc + _dot(jnp.concatenate(acts, axis=-1), wdn_ref[rows, :])
            acts = []
    if final_norm:
        acc = _rms(acc, gfin_ref[...], NORM_EPS)
    o_ref[...] = acc


def _ffn(x1, h2, w_up, conv_w, conv_b, w_down, g_final, seq, final_norm):
    t = x1.shape[0]
    tm = TM_FFN
    kern = functools.partial(_ffn_kernel, tiles_per_seq=seq // tm, final_norm=final_norm)
    return pl.pallas_call(
        kern,
        grid=(t // tm,),
        in_specs=[_row_spec(tm, D_MODEL), _row_spec(tm, D_MODEL), _const_spec((D_MODEL, 2 * D_FF)),
                  _const_spec((3, 2 * D_FF)), _const_spec((1, 2 * D_FF)), _const_spec((D_FF, D_MODEL)),
                  _const_spec((1, D_MODEL))],
        out_specs=_row_spec(tm, D_MODEL),
        out_shape=jax.ShapeDtypeStruct((t, D_MODEL), F32),
        scratch_shapes=[pltpu.VMEM((8 + tm, 2 * D_FF), F32)],
        compiler_params=_params(),
        name="conv_ffn",
    )(x1, h2, w_up, conv_w, conv_b, w_down, g_final)


def _mla_weights(w_uq, w_ukv):
    qd = MLA_NOPE + MLA_ROPE
    half = MLA_ROPE // 2
    wq = w_uq.reshape(MLA_Q_LORA, MLA_HEADS, qd)
    zq = jnp.zeros((MLA_Q_LORA, MLA_HEADS, LANES - qd), F32)
    wq_main = jnp.concatenate([wq, zq], axis=-1)
    x1 = wq[:, :, MLA_NOPE:MLA_NOPE + half]
    x2 = wq[:, :, MLA_NOPE + half:]
    wq_rot = jnp.concatenate([jnp.zeros((MLA_Q_LORA, MLA_HEADS, MLA_NOPE), F32), -x2, x1, zq], axis=-1)
    wkv = w_ukv.reshape(MLA_KV_LORA, MLA_HEADS, MLA_NOPE + MLA_V)
    zkv = jnp.zeros((MLA_KV_LORA, MLA_HEADS, LANES - MLA_NOPE), F32)
    wk = jnp.concatenate([wkv[:, :, :MLA_NOPE], zkv], axis=-1)
    wv = jnp.concatenate([wkv[:, :, MLA_NOPE:], zkv], axis=-1)
    flat = lambda w: w.reshape(w.shape[0], -1).astype(BF16)
    return flat(wq_main), flat(wq_rot), flat(wk), flat(wv).T


def kernel(x, positions, rel_bias, norm_mix, w_in, b_gate, rwkv_mu, rwkv_w0, rwkv_w2, rwkv_a0, rwkv_a2, rwkv_g2, rwkv_k_k, rwkv_k_a, rwkv_r_k, rwkv_ln_w, rwkv_ln_b, mla_q_norm, mla_w_uq, mla_kv_norm, mla_w_ukv, diff_lambda, diff_subln, w_branch_rwkv, w_branch_mla, w_branch_diff, w_o, norm_ffn, ffn_w_up, ffn_conv_w, ffn_conv_b, ffn_w_down, norm_final):
    batch, seq, _ = x.shape
    depth = w_in.shape[0]
    t = batch * seq
    assert seq % TQ == 0 and seq % TM == 0 and seq % TM_FFN == 0 and seq % RWKV_TILE == 0 and RWKV_TILE % CHUNK == 0
    xf = x.reshape(t, D_MODEL)
    pos_col = positions.reshape(t, 1)
    inv_freq = ROPE_BASE ** (-jnp.arange(0, MLA_ROPE, 2, dtype=F32) / MLA_ROPE)
    freq = jnp.concatenate([jnp.zeros((MLA_NOPE,), F32), inv_freq, inv_freq,
                            jnp.zeros((LANES - MLA_NOPE - MLA_ROPE,), F32)]).reshape(1, LANES)
    cos, sin = _rope_tables(pos_col, freq)
    row = lambda v: v.reshape(1, -1)
    diff_scale = jnp.concatenate([jnp.full((512,), DIFF_QK ** -0.5 * LOG2E, F32), jnp.ones((512,), F32)])

    for l in range(depth):
        s0, s1, s2 = RWKV_COLS, RWKV_COLS + MLA_COLS, RWKV_COLS + MLA_COLS + DIFF_COLS
        sv = s1 + DIFF_QK_COLS
        w = w_in[l]
        w_all = jnp.concatenate([w[:, :s0], w[:, s0:s1], jnp.zeros((D_MODEL, MLA_PAD - MLA_COLS), F32),
                                 w[:, s1:sv] * diff_scale, w[:, s2:]], axis=1).astype(BF16)
        w_vt = w[:, sv:s2].T.astype(BF16)
        p_rwkv, p_mla, p_qk, p_vt, p_gate = _in_proj(xf, row(norm_mix[l]), w_all, w_vt)

        o_r = _rwkv(p_rwkv, seq, row(rwkv_mu[l]), row(rwkv_w0[l]), rwkv_w2[l].astype(BF16), row(rwkv_a0[l]),
                    rwkv_a2[l].astype(BF16), rwkv_g2[l].astype(BF16), row(rwkv_k_k[l]), row(rwkv_k_a[l]),
                    row(rwkv_r_k[l]), row(rwkv_ln_w[l]), row(rwkv_ln_b[l]))

        wq, wqr, wk, wv = _mla_weights(mla_w_uq[l], mla_w_ukv[l])
        q_m, k_m, v_m = _mla_prep(p_mla, cos, sin, row(mla_q_norm[l]), row(mla_kv_norm[l]), wq, wqr, wk, wv)
        o_m = _mla_attn(q_m, k_m, v_m, batch, seq)

        o_d = _diff_attn(p_qk, p_vt, rel_bias, diff_lambda[l], row(diff_subln[l]), batch, seq, l)

        x1, h2 = _merge(xf, p_gate, row(b_gate[l]), o_r, o_m, o_d, w_branch_rwkv[l].astype(BF16),
                        w_branch_mla[l].astype(BF16), w_branch_diff[l].astype(BF16), w_o[l].astype(BF16),
                        row(norm_ffn[l]))
        xf = _ffn(x1, h2, ffn_w_up[l].astype(BF16), ffn_conv_w[l], row(ffn_conv_b[l]), ffn_w_down[l].astype(BF16),
                  row(norm_final), seq, final_norm=(l == depth - 1))
    return xf.reshape(batch, seq, D_MODEL)
```

```python
import functools
import math

import numpy as np
import jax
import jax.numpy as jnp
from jax import lax
from jax.experimental import pallas as pl
from jax.experimental.pallas import tpu as pltpu

F32 = jnp.float32
BF16 = jnp.bfloat16

D_MODEL = 1024
RWKV_HEADS = 8
RWKV_N = 64
RWKV_DIM = 512
RWKV_COLS = 1792
RWKV_GN_EPS = 64e-5
MLA_HEADS = 8
MLA_Q_LORA = 256
MLA_KV_LORA = 128
MLA_NOPE = 64
MLA_ROPE = 32
MLA_V = 64
MLA_COLS = 416
MLA_PAD = 512
ROPE_BASE = 10000.0
DIFF_HEADS = 4
DIFF_QK = 64
DIFF_V = 128
DIFF_COLS = 1536
DIFF_QK_COLS = 1024
DIFF_VT_ROWS = DIFF_V + 16
REL_BUCKETS = 32
REL_MAX_DISTANCE = 128
D_FF = 2816
GATE_COLS = 3072
NORM_EPS = 1e-6
SUBLN_EPS = 1e-5

LANES = 128
VMEM_LIMIT = 58 * 1024 * 1024
TM = 256
TM_FFN = 256
TM_MERGE = 512
CHUNK = 64
RWKV_TILE = 256
TQ = 512
FF_CHUNK = 256
FF_DOWN_GROUP = 11
NEG = -1e30
LOG2E = 1.4426950408889634

_NT = (((1,), (1,)), ((), ()))
_TN = (((0,), (0,)), ((), ()))


def _dot(a, b):
    return jnp.dot(a, b, preferred_element_type=F32)


def _dot_nt(a, b):
    return lax.dot_general(a, b, _NT, preferred_element_type=F32)


def _dot_tn(a, b):
    return lax.dot_general(a, b, _TN, preferred_element_type=F32)


def _rms(x, g, eps):
    return x * lax.rsqrt(jnp.mean(x * x, axis=-1, keepdims=True) + eps) * g


def _params():
    return pltpu.CompilerParams(dimension_semantics=("arbitrary",), vmem_limit_bytes=VMEM_LIMIT)


def _params2():
    return pltpu.CompilerParams(dimension_semantics=("arbitrary", "arbitrary"), vmem_limit_bytes=VMEM_LIMIT)


def _const_spec(shape):
    return pl.BlockSpec(shape, lambda *_: (0,) * len(shape), pipeline_mode=pl.Buffered(1))


def _row_spec(tm, cols, col_block=0):
    return pl.BlockSpec((tm, cols), lambda i: (i, col_block))


def _in_proj_kernel(x_ref, g_ref, w_ref, wvt_ref, pr_ref, pm_ref, pqk_ref, pvt_ref, pg_ref):
    h = _rms(x_ref[...], g_ref[...], NORM_EPS).astype(BF16)
    c0 = 0
    pr_ref[...] = _dot(h, w_ref[:, c0:c0 + RWKV_COLS])
    c0 += RWKV_COLS
    pm_ref[...] = _dot(h, w_ref[:, c0:c0 + MLA_PAD])
    c0 += MLA_PAD
    pqk_ref[...] = _dot(h, w_ref[:, c0:c0 + DIFF_QK_COLS]).astype(BF16)
    c0 += DIFF_QK_COLS
    pg_ref[...] = _dot(h, w_ref[:, c0:c0 + GATE_COLS]).astype(BF16)
    pvt_ref[...] = _dot_nt(wvt_ref[...], h).astype(BF16)


def _in_proj(x, g, w_all, w_vt):
    t = x.shape[0]
    n_all = w_all.shape[1]
    vw = DIFF_HEADS * DIFF_V
    return pl.pallas_call(
        _in_proj_kernel,
        grid=(t // TM,),
        in_specs=[_row_spec(TM, D_MODEL), _const_spec((1, D_MODEL)), _const_spec((D_MODEL, n_all)),
                  _const_spec((vw, D_MODEL))],
        out_specs=[_row_spec(TM, RWKV_COLS), _row_spec(TM, MLA_PAD), _row_spec(TM, DIFF_QK_COLS),
                   pl.BlockSpec((vw, TM), lambda i: (0, i)), _row_spec(TM, GATE_COLS)],
        out_shape=[jax.ShapeDtypeStruct((t, RWKV_COLS), F32), jax.ShapeDtypeStruct((t, MLA_PAD), F32),
                   jax.ShapeDtypeStruct((t, DIFF_QK_COLS), BF16), jax.ShapeDtypeStruct((vw, t), BF16),
                   jax.ShapeDtypeStruct((t, GATE_COLS), BF16)],
        compiler_params=_params(),
        name="in_proj",
    )(x, g, w_all, w_vt)


def _split3(a):
    hi = a.astype(BF16)
    r1 = a - hi.astype(F32)
    mid = r1.astype(BF16)
    lo = (r1 - mid.astype(F32)).astype(BF16)
    return hi, mid, lo


def _softplus(z):
    return jnp.maximum(z, 0.0) + jnp.log(1.0 + jnp.exp(-jnp.abs(z)))


def _rwkv_kernel(p_ref, mu_ref, w0_ref, w2_ref, a0_ref, a2_ref, g2_ref, kk_ref, ka_ref, rk_ref, lnw_ref, lnb_ref,
                 o_ref, carry_ref, state_ref, *, tiles_per_seq):
    c = CHUNK
    n = RWKV_N
    ts = RWKV_TILE
    nc = ts // c

    @pl.when(pl.program_id(0) % tiles_per_seq == 0)
    def _():
        carry_ref[...] = jnp.zeros_like(carry_ref)
        state_ref[...] = jnp.zeros_like(state_ref)

    p = p_ref[...]
    row = lax.broadcasted_iota(jnp.int32, (ts, 1), 0)
    shifted = jnp.where(row == 0, carry_ref[...], pltpu.roll(p, 1, 0))
    carry_ref[...] = p[ts - 1:ts, :]
    pm = p + (shifted - p) * mu_ref[...]

    r = pm[:, 0:512]
    k = pm[:, 512:1024]
    v = pm[:, 1024:1536]
    pw = pm[:, 1536:1600]
    pa = pm[:, 1600:1664]
    pg = pm[:, 1664:1792]

    w_log = -_softplus(-(w0_ref[...] + _dot(jnp.tanh(pw).astype(BF16), w2_ref[...]))) - 0.5
    logd = -jnp.exp(w_log)
    a = jax.nn.sigmoid(a0_ref[...] + _dot(pa.astype(BF16), a2_ref[...]))
    g = _dot(jax.nn.sigmoid(pg).astype(BF16), g2_ref[...])

    tr = lax.broadcasted_iota(jnp.int32, (ts, ts), 0)
    tc = lax.broadcasted_iota(jnp.int32, (ts, ts), 1)
    tri = ((tr >= tc) & (tr // c == tc // c)).astype(BF16)
    hi, mid, lo = _split3(logd)
    cs = _dot(tri, hi) + _dot(tri, mid) + _dot(tri, lo)
    total = jnp.concatenate([jnp.broadcast_to(cs[(m + 1) * c - 1:(m + 1) * c, :], (c, RWKV_DIM)) for m in range(nc)], axis=0)
    e_in = jnp.exp(cs)
    e_ex = jnp.exp(cs - logd)
    e_inv = jnp.exp(-cs)
    e_end = jnp.exp(total - cs)
    g_end = jnp.exp(total)

    n_pairs = RWKV_HEADS // 2
    lo_t = lax.broadcasted_iota(jnp.int32, (ts, LANES), 1) < n
    lo_c = lax.broadcasted_iota(jnp.int32, (c, LANES), 1) < n

    def head_sum(x):
        blocks = []
        for q in range(n_pairs):
            xb = x[:, q * LANES:(q + 1) * LANES]
            s_lo = jnp.sum(jnp.where(lo_t, xb, 0.0), axis=-1, keepdims=True)
            s_hi = jnp.sum(jnp.where(lo_t, 0.0, xb), axis=-1, keepdims=True)
            blocks.append(jnp.where(lo_t, s_lo, s_hi))
        return jnp.concatenate(blocks, axis=-1)

    k2 = k * (1.0 + (a - 1.0) * ka_ref[...])
    kku = k * kk_ref[...]
    kkn = kku / jnp.maximum(jnp.sqrt(head_sum(kku * kku)), 1e-12)
    b = kkn * a
    full = {"at": -kkn * e_ex, "rt": r * e_in, "bt": b * e_inv, "kt": k2 * e_inv, "v": v,
            "be": b * e_end, "ke": k2 * e_end}

    ti = lax.broadcasted_iota(jnp.int32, (2 * c, 4 * c), 0) % c
    si = lax.broadcasted_iota(jnp.int32, (2 * c, 4 * c), 1) % c
    strict = ti > si
    incl = ti >= si
    eye = (lax.broadcasted_iota(jnp.int32, (2 * c, 2 * c), 0)
           == lax.broadcasted_iota(jnp.int32, (2 * c, 2 * c), 1)).astype(F32)

    items = [(m, q) for m in range(nc) for q in range(n_pairs)]

    def stacked(name, m, q):
        xb = full[name][m * c:(m + 1) * c, q * LANES:(q + 1) * LANES]
        return jnp.concatenate([jnp.where(lo_c, xb, 0.0), jnp.where(lo_c, 0.0, xb)], axis=0).astype(BF16)

    st_ops = {name: [stacked(name, m, q) for m, q in items] for name in full}
    big_l = [_dot_nt(jnp.concatenate([at, rtb], axis=0), jnp.concatenate([bt, kt], axis=0))
             for at, rtb, bt, kt in zip(st_ops["at"], st_ops["rt"], st_ops["bt"], st_ops["kt"])]
    top_l = [jnp.where(strict, big[:2 * c, :], 0.0) for big in big_l]
    bot_l = [jnp.where(incl, big[2 * c:, :], 0.0).astype(BF16) for big in big_l]
    lakv_l = [_dot(top[:, 2 * c:].astype(BF16), vb) for top, vb in zip(top_l, st_ops["v"])]

    x_l = [top[:, :2 * c] for top in top_l]
    tinv_l = [eye + x for x in x_l]
    xb_l = [x.astype(BF16) for x in x_l]
    for _ in range(int(math.log2(c)) - 1):
        x_l = [_dot(xb, xb) for xb in xb_l]
        xb_l = [x.astype(BF16) for x in x_l]
        tinv_l = [tinv + _dot(tinv.astype(BF16), xb) for tinv, xb in zip(tinv_l, xb_l)]
    tinvb_l = [tinv.astype(BF16) for tinv in tinv_l]
    abar_l = [_dot(tb, at).astype(BF16) for tb, at in zip(tinvb_l, st_ops["at"])]
    vbar_l = [_dot(tb, lv.astype(BF16)).astype(BF16) for tb, lv in zip(tinvb_l, lakv_l)]
    rhat_l = [(rtb.astype(F32) + _dot(bot[:, :2 * c], ab)).astype(BF16)
              for rtb, bot, ab in zip(st_ops["rt"], bot_l, abar_l)]
    uv_l = [jnp.concatenate([vbar, vb], axis=0) for vbar, vb in zip(vbar_l, st_ops["v"])]
    y0_l = [_dot(bot, uv) for bot, uv in zip(bot_l, uv_l)]
    p_l = [_dot_tn(ab, be).astype(BF16) for ab, be in zip(abar_l, st_ops["be"])]
    q_l = [_dot_tn(uv, jnp.concatenate([be, ke], axis=0)) for uv, be, ke in zip(uv_l, st_ops["be"], st_ops["ke"])]

    st_l = [state_ref[q] for q in range(n_pairs)]
    y_rows = []
    for m in range(nc):
        y_blocks = []
        for q in range(n_pairs):
            idx = m * n_pairs + q
            st = st_l[q]
            st_b = st.astype(BF16)
            y_s = _dot_nt(rhat_l[idx], st_b) + y0_l[idx]
            y_blocks.append(y_s[:c] + y_s[c:])
            decay = g_end[(m + 1) * c - 1:(m + 1) * c, q * LANES:(q + 1) * LANES]
            st_l[q] = st * decay + _dot(st_b, p_l[idx]) + q_l[idx]
        y_rows.append(jnp.concatenate(y_blocks, axis=-1))
    for q in range(n_pairs):
        state_ref[q] = st_l[q]
    y = jnp.concatenate(y_rows, axis=0)

    inv_n = 1.0 / n
    yc = y - head_sum(y) * inv_n
    var = head_sum(yc * yc) * inv_n
    yn = yc * lax.rsqrt(var + RWKV_GN_EPS) * lnw_ref[...] + lnb_ref[...]
    bonus = head_sum(r * k2 * rk_ref[...]) * v
    o_ref[...] = ((yn + bonus) * g).astype(BF16)


def _rwkv(p_rwkv, seq, mu, w0, w2, a0, a2, g2, k_k, k_a, r_k, ln_w, ln_b):
    t = p_rwkv.shape[0]
    ts = RWKV_TILE
    kern = functools.partial(_rwkv_kernel, tiles_per_seq=seq // ts)
    vec = lambda: _const_spec((1, RWKV_DIM))
    return pl.pallas_call(
        kern,
        grid=(t // ts,),
        in_specs=[_row_spec(ts, RWKV_COLS), _const_spec((1, RWKV_COLS)), vec(), _const_spec((64, RWKV_DIM)), vec(),
                  _const_spec((64, RWKV_DIM)), _const_spec((128, RWKV_DIM)), vec(), vec(), vec(), vec(), vec()],
        out_specs=_row_spec(ts, RWKV_DIM),
        out_shape=jax.ShapeDtypeStruct((t, RWKV_DIM), BF16),
        scratch_shapes=[pltpu.VMEM((1, RWKV_COLS), F32), pltpu.VMEM((RWKV_HEADS // 2, LANES, LANES), F32)],
        compiler_params=_params(),
        name="rwkv7",
    )(p_rwkv, mu, w0, w2, a0, a2, g2, k_k, k_a, r_k, ln_w, ln_b)


def _mla_prep_kernel(p_ref, pos_ref, freq_ref, qn_ref, kvn_ref, wq_ref, wqr_ref, wk_ref, wv_ref, q_out, k_out, v_out):
    p = p_ref[...]
    hq = _rms(p[:, 0:MLA_Q_LORA], qn_ref[...], NORM_EPS).astype(BF16)
    hkv = _rms(p[:, MLA_Q_LORA:MLA_Q_LORA + MLA_KV_LORA], kvn_ref[...], NORM_EPS).astype(BF16)
    blk = p[:, 384:512]
    ang = pos_ref[...].astype(F32) * freq_ref[...]
    cos = jnp.cos(ang)
    sin = jnp.sin(ang)
    scale = (MLA_NOPE + MLA_ROPE) ** -0.5 * LOG2E
    qa = _dot(hq, wq_ref[...])
    qr = _dot(hq, wqr_ref[...])
    lane = lax.broadcasted_iota(jnp.int32, blk.shape, 1)
    kr = pltpu.roll(blk, 64, 1)
    rot = jnp.where(lane < 80, -pltpu.roll(blk, 48, 1), pltpu.roll(blk, 80, 1))
    rot = jnp.where((lane >= 64) & (lane < 96), rot, 0.0)
    krope = kr * cos + rot * sin
    kn = _dot(hkv, wk_ref[...])
    for h in range(MLA_HEADS):
        hs = slice(h * LANES, (h + 1) * LANES)
        q_out[:, hs] = ((qa[:, hs] * cos + qr[:, hs] * sin) * scale).astype(BF16)
        k_out[:, hs] = (kn[:, hs] + krope).astype(BF16)
    vt = _dot_nt(wv_ref[...], hkv)
    rowi = lax.broadcasted_iota(jnp.int32, vt.shape, 0)
    v_out[...] = (vt + (rowi % LANES == MLA_V).astype(F32)).astype(BF16)


def _mla_prep(p_mla, pos_col, freq, q_norm, kv_norm, wq, wqr, wk, wv):
    t = p_mla.shape[0]
    hw = MLA_HEADS * LANES
    return pl.pallas_call(
        _mla_prep_kernel,
        grid=(t // TM,),
        in_specs=[_row_spec(TM, MLA_PAD), _row_spec(TM, 1), _const_spec((1, LANES)), _const_spec((1, MLA_Q_LORA)),
                  _const_spec((1, MLA_KV_LORA)), _const_spec((MLA_Q_LORA, hw)), _const_spec((MLA_Q_LORA, hw)),
                  _const_spec((MLA_KV_LORA, hw)), _const_spec((hw, MLA_KV_LORA))],
        out_specs=[_row_spec(TM, hw), _row_spec(TM, hw), pl.BlockSpec((hw, TM), lambda i: (0, i))],
        out_shape=[jax.ShapeDtypeStruct((t, hw), BF16), jax.ShapeDtypeStruct((t, hw), BF16),
                   jax.ShapeDtypeStruct((hw, t), BF16)],
        compiler_params=_params(),
        name="mla_prep",
    )(p_mla, pos_col, freq, q_norm, kv_norm, wq, wqr, wk, wv)


def _softmax_stage_t(s_l, pv_fn, m_scr, acc_scr, idx):
    n = len(s_l)
    m_old = [m_scr[c][0:1, :] for c in idx]
    m_new = [jnp.maximum(m, jnp.max(s, axis=0, keepdims=True)) for m, s in zip(m_old, s_l)]
    p_l = [jnp.exp2(s - m).astype(BF16) for s, m in zip(s_l, m_new)]
    alpha = [jnp.exp2(mo - mn) for mo, mn in zip(m_old, m_new)]
    pv_l = [pv_fn(c, p_l[c]) for c in range(n)]
    for c in range(n):
        m_scr[idx[c]] = jnp.broadcast_to(m_new[c], m_scr.shape[1:])
        acc_scr[idx[c]] = alpha[c] * acc_scr[idx[c]] + pv_l[c]


def _mla_attn_kernel(q_ref, k_ref, vt_ref, o_ref, m_scr, acc_scr):
    tq = TQ
    i = pl.program_id(1)
    causal = lax.broadcasted_iota(jnp.int32, (tq, tq), 0) <= lax.broadcasted_iota(jnp.int32, (tq, tq), 1)
    lane = lax.broadcasted_iota(jnp.int32, (tq, LANES), 1)
    m_scr[...] = jnp.full(m_scr.shape, NEG, F32)
    acc_scr[...] = jnp.zeros(acc_scr.shape, F32)
    hsl = [slice(h * LANES, (h + 1) * LANES) for h in range(MLA_HEADS)]
    heads = list(range(MLA_HEADS))

    def tile(off, tk, mask):
        s_l = [_dot_nt(k_ref[pl.ds(off, tk), hs], q_ref[:, hs]) for hs in hsl]
        if mask:
            s_l = [jnp.where(causal, s, NEG) for s in s_l]
        _softmax_stage_t(s_l, lambda c, p: _dot(vt_ref[hsl[c], pl.ds(off, tk)], p), m_scr, acc_scr, heads)

    def pair_body(j, carry):
        tile(pl.multiple_of(j * 2 * tq, 2 * tq), 2 * tq, False)
        return carry

    def single_body(j, carry):
        tile(pl.multiple_of(j * tq, tq), tq, False)
        return carry

    n_pairs = i // 2
    lax.fori_loop(0, n_pairs, pair_body, 0)
    lax.fori_loop(2 * n_pairs, i, single_body, 0)
    tile(pl.multiple_of(i * tq, tq), tq, True)
    for pr in range(MLA_HEADS // 2):
        outs = []
        for hh in range(2):
            acc = acc_scr[2 * pr + hh]
            outs.append(jnp.transpose(acc / acc[MLA_V:MLA_V + 1, :]))
        o_ref[:, pr * LANES:(pr + 1) * LANES] = jnp.where(lane < MLA_V, outs[0], pltpu.roll(outs[1], MLA_V, 1)).astype(BF16)


def _mla_attn(q, k, vt, batch, seq):
    t = q.shape[0]
    nq = seq // TQ
    hw = MLA_HEADS * LANES
    vw = MLA_HEADS * MLA_V
    return pl.pallas_call(
        _mla_attn_kernel,
        grid=(batch, nq),
        in_specs=[pl.BlockSpec((TQ, hw), lambda b, i: (b * nq + i, 0)),
                  pl.BlockSpec((seq, hw), lambda b, i: (b, 0)),
                  pl.BlockSpec((hw, seq), lambda b, i: (0, b))],
        out_specs=pl.BlockSpec((TQ, vw), lambda b, i: (b * nq + i, 0)),
        out_shape=jax.ShapeDtypeStruct((t, vw), BF16),
        scratch_shapes=[pltpu.VMEM((MLA_HEADS, 8, TQ), F32), pltpu.VMEM((MLA_HEADS, LANES, TQ), F32)],
        compiler_params=_params2(),
        name="mla_attn",
    )(q, k, vt)


def _t5_bucket_table():
    n = np.arange(0, REL_MAX_DISTANCE + 1)
    max_exact = REL_BUCKETS // 2
    nf = np.maximum(n, 1).astype(np.float32)
    ratio = np.log(nf / np.float32(max_exact)) / np.float32(math.log(REL_MAX_DISTANCE / max_exact))
    large = max_exact + (ratio * np.float32(REL_BUCKETS - max_exact)).astype(np.int32)
    large = np.minimum(large, REL_BUCKETS - 1)
    return np.where(n < max_exact, n, large)


_BUCKETS = _t5_bucket_table()
_FAR_BUCKET = int(_BUCKETS[REL_MAX_DISTANCE])
assert _FAR_BUCKET == REL_BUCKETS - 1 and np.all(np.diff(_BUCKETS) >= 0)
_BUCKET_STARTS = [int(np.argmax(_BUCKETS >= b)) for b in range(REL_BUCKETS // 2 + 1, REL_BUCKETS)]


def _diff_attn_kernel(rb_ref, lam_ref, subln_ref, q_ref, k_ref, vt_ref, o_ref, bias_scr, m_scr, acc_scr, *, lambda_init):
    tq = TQ
    i = pl.program_id(1)
    r_i = lax.broadcasted_iota(jnp.int32, (tq, tq), 0)
    c_i = lax.broadcasted_iota(jnp.int32, (tq, tq), 1)
    causal = r_i <= c_i

    @pl.when((pl.program_id(0) == 0) & (i == 0))
    def _():
        for t_idx, delta in enumerate((0, tq)):
            d = jnp.maximum(c_i - r_i + delta, 0)
            log_b = REL_BUCKETS // 2
            for start in _BUCKET_STARTS:
                log_b = log_b + (d >= start).astype(jnp.int32)
            bucket = jnp.where(d < REL_BUCKETS // 2, d, log_b)
            for h in range(DIFF_HEADS):
                bias = jnp.zeros((tq, tq), F32)
                for b in range(REL_BUCKETS):
                    bias = jnp.where(bucket == b, rb_ref[b, h], bias)
                bias_scr[t_idx, h] = (bias - rb_ref[_FAR_BUCKET, h]) * LOG2E

    m_scr[...] = jnp.full(m_scr.shape, NEG, F32)
    acc_scr[...] = jnp.zeros(acc_scr.shape, F32)
    lane = lax.broadcasted_iota(jnp.int32, (tq, LANES), 1)
    ones_rows = {tk: (lax.broadcasted_iota(jnp.int32, (DIFF_VT_ROWS - DIFF_V, tk), 0) == 0).astype(BF16)
                 for tk in (tq, 2 * tq)}
    hsl = [slice(h * LANES, (h + 1) * LANES) for h in range(DIFF_HEADS)]
    heads = list(range(DIFF_HEADS))
    q_l = []
    for hs in hsl:
        qp = q_ref[:, hs]
        q_l.append(jnp.concatenate([jnp.where(lane < DIFF_QK, qp, jnp.zeros_like(qp)),
                                    jnp.where(lane >= DIFF_QK, qp, jnp.zeros_like(qp))], axis=0))

    def tile(off, tk, bias_idx, mask):
        s_l = [_dot_nt(k_ref[pl.ds(off, tk), hsl[h]], q_l[h]) for h in heads]
        if bias_idx is not None:
            s_l = [s + jnp.concatenate([bias_scr[bias_idx, h]] * 2, axis=1) for h, s in enumerate(s_l)]
        if mask:
            mask2 = jnp.concatenate([causal, causal], axis=1)
            s_l = [jnp.where(mask2, s, NEG) for s in s_l]

        def pv(c, p):
            vt = jnp.concatenate([vt_ref[hsl[c], pl.ds(off, tk)], ones_rows[tk]], axis=0)
            return jnp.concatenate([_dot(vt, p[:, :tq]), _dot(vt, p[:, tq:])], axis=1)

        _softmax_stage_t(s_l, pv, m_scr, acc_scr, heads)

    def far_pair_body(j, carry):
        tile(pl.multiple_of(j * 2 * tq, 2 * tq), 2 * tq, None, False)
        return carry

    def far_body(j, carry):
        tile(pl.multiple_of(j * tq, tq), tq, None, False)
        return carry

    def near_body(j, carry):
        tile(pl.multiple_of(j * tq, tq), tq, 1, False)
        return carry

    n_far = jnp.maximum(i - 1, 0)
    n_pairs = n_far // 2
    lax.fori_loop(0, n_pairs, far_pair_body, 0)
    lax.fori_loop(2 * n_pairs, n_far, far_body, 0)
    lax.fori_loop(n_far, i, near_body, 0)
    tile(pl.multiple_of(i * tq, tq), tq, 0, True)

    lam = lam_ref[...]
    lam_full = (jnp.exp(jnp.sum(lam[0:1] * lam[1:2], axis=-1, keepdims=True))
                - jnp.exp(jnp.sum(lam[2:3] * lam[3:4], axis=-1, keepdims=True)) + lambda_init)
    for h in heads:
        acc = acc_scr[h]
        on = acc[:DIFF_V, :] / acc[DIFF_V:DIFF_V + 1, :]
        ot = on[:, :tq] - lam_full * on[:, tq:]
        ot = ot * lax.rsqrt(jnp.mean(ot * ot, axis=0, keepdims=True) + SUBLN_EPS)
        o_ref[:, hsl[h]] = (jnp.transpose(ot) * subln_ref[...] * (1.0 - lambda_init)).astype(BF16)


def _diff_attn(p_qk, p_vt, rel_bias, lam, subln, batch, seq, layer_idx):
    t = p_qk.shape[0]
    nq = seq // TQ
    w = DIFF_HEADS * LANES
    lambda_init = 0.8 - 0.6 * math.exp(-0.3 * layer_idx)
    kern = functools.partial(_diff_attn_kernel, lambda_init=lambda_init)
    const2 = lambda shape: pl.BlockSpec(shape, lambda b, i: (0, 0))
    return pl.pallas_call(
        kern,
        grid=(batch, nq),
        in_specs=[pl.BlockSpec(memory_space=pltpu.SMEM),
                  const2((4, DIFF_QK)), const2((1, DIFF_V)),
                  pl.BlockSpec((TQ, w), lambda b, i: (b * nq + i, 0)),
                  pl.BlockSpec((seq, w), lambda b, i: (b, 1)),
                  pl.BlockSpec((w, seq), lambda b, i: (0, b))],
        out_specs=pl.BlockSpec((TQ, w), lambda b, i: (b * nq + i, 0)),
        out_shape=jax.ShapeDtypeStruct((t, w), BF16),
        scratch_shapes=[pltpu.VMEM((2, DIFF_HEADS, TQ, TQ), F32), pltpu.VMEM((DIFF_HEADS, 8, 2 * TQ), F32),
                        pltpu.VMEM((DIFF_HEADS, DIFF_VT_ROWS, 2 * TQ), F32)],
        compiler_params=_params2(),
        name="diff_attn",
    )(rel_bias, lam, subln, p_qk, p_qk, p_vt)


def _merge_kernel(x_ref, pg_ref, bg_ref, or_ref, om_ref, od_ref, wr_ref, wm_ref, wd_ref, wo_ref, gf_ref, x_out, h_out):
    halves = [slice(s * (TM_MERGE // 2), (s + 1) * (TM_MERGE // 2)) for s in range(2)]
    branch = [[_dot(o_ref[rs, :], w_ref[...]) for o_ref, w_ref in ((or_ref, wr_ref), (om_ref, wm_ref), (od_ref, wd_ref))]
              for rs in halves]
    for rs, d in zip(halves, branch):
        merged = None
        for idx in range(3):
            cs = slice(idx * D_MODEL, (idx + 1) * D_MODEL)
            term = jax.nn.sigmoid(pg_ref[rs, cs].astype(F32) + bg_ref[:, cs]) * d[idx]
            merged = term if merged is None else merged + term
        x1 = x_ref[rs, :] + _dot(merged.astype(BF16), wo_ref[...])
        x_out[rs, :] = x1
        h_out[rs, :] = _rms(x1, gf_ref[...], NORM_EPS).astype(BF16)


def _merge(x, p_gate, b_gate, o_r, o_m, o_d, w_r, w_m, w_d, w_o, g_ffn):
    t = x.shape[0]
    tm = TM_MERGE
    return pl.pallas_call(
        _merge_kernel,
        grid=(t // tm,),
        in_specs=[_row_spec(tm, D_MODEL), _row_spec(tm, GATE_COLS), _const_spec((1, GATE_COLS)),
                  _row_spec(tm, 512), _row_spec(tm, 512), _row_spec(tm, 512),
                  _const_spec((512, D_MODEL)), _const_spec((512, D_MODEL)), _const_spec((512, D_MODEL)),
                  _const_spec((D_MODEL, D_MODEL)), _const_spec((1, D_MODEL))],
        out_specs=[_row_spec(tm, D_MODEL), _row_spec(tm, D_MODEL)],
        out_shape=[jax.ShapeDtypeStruct((t, D_MODEL), F32), jax.ShapeDtypeStruct((t, D_MODEL), BF16)],
        compiler_params=_params(),
        name="merge",
    )(x, p_gate, b_gate, o_r, o_m, o_d, w_r, w_m, w_d, w_o, g_ffn)


def _ffn_kernel(x_ref, h_ref, wup_ref, cw_ref, cb_ref, wdn_ref, gfin_ref, o_ref, carry_ref, *, tiles_per_seq, final_norm):
    tm = TM_FFN

    @pl.when(pl.program_id(0) % tiles_per_seq == 0)
    def _():
        carry_ref[0:8, :] = jnp.zeros((8, carry_ref.shape[1]), F32)

    h = h_ref[...]

    def conv(u, cols):
        carry_ref[8:, cols] = u
        u1 = carry_ref[7:7 + tm, cols]
        u2 = carry_ref[6:6 + tm, cols]
        carry_ref[0:8, cols] = u[tm - 8:tm, :]
        return cw_ref[0:1, cols] * u2 + cw_ref[1:2, cols] * u1 + cw_ref[2:3, cols] * u + cb_ref[:, cols]

    def up(ck):
        gc = slice(ck * FF_CHUNK, (ck + 1) * FF_CHUNK)
        vc = slice(D_FF + ck * FF_CHUNK, D_FF + (ck + 1) * FF_CHUNK)
        return _dot(h, wup_ref[:, gc]), _dot(h, wup_ref[:, vc])

    n_chunks = D_FF // FF_CHUNK
    acc = x_ref[...]
    nxt = up(0)
    acts = []
    for ck in range(n_chunks):
        ug, uv = nxt
        if ck + 1 < n_chunks:
            nxt = up(ck + 1)
        gc = slice(ck * FF_CHUNK, (ck + 1) * FF_CHUNK)
        vc = slice(D_FF + ck * FF_CHUNK, D_FF + (ck + 1) * FF_CHUNK)
        gate = conv(ug, gc)
        val = conv(uv, vc)
        acts.append((gate * jax.nn.sigmoid(gate) * val).astype(BF16))
        if len(acts) == FF_DOWN_GROUP or ck + 1 == n_chunks:
            rows = slice((ck + 1 - len(acts)) * FF_CHUNK, (ck + 1) * FF_CHUNK)
            acc = acc + _dot(jnp.concatenate(acts, axis=-1), wdn_ref[rows, :])
            acts = []
    if final_norm:
        acc = _rms(acc, gfin_ref[...], NORM_EPS)
    o_ref[...] = acc


def _ffn(x1, h2, w_up, conv_w, conv_b, w_down, g_final, seq, final_norm):
    t = x1.shape[0]
    tm = TM_FFN
    kern = functools.partial(_ffn_kernel, tiles_per_seq=seq // tm, final_norm=final_norm)
    return pl.pallas_call(
        kern,
        grid=(t // tm,),
        in_specs=[_row_spec(tm, D_MODEL), _row_spec(tm, D_MODEL), _const_spec((D_MODEL, 2 * D_FF)),
                  _const_spec((3, 2 * D_FF)), _const_spec((1, 2 * D_FF)), _const_spec((D_FF, D_MODEL)),
                  _const_spec((1, D_MODEL))],
        out_specs=_row_spec(tm, D_MODEL),
        out_shape=jax.ShapeDtypeStruct((t, D_MODEL), F32),
        scratch_shapes=[pltpu.VMEM((8 + tm, 2 * D_FF), F32)],
        compiler_params=_params(),
        name="conv_ffn",
    )(x1, h2, w_up, conv_w, conv_b, w_down, g_final)


def _mla_weights(w_uq, w_ukv):
    qd = MLA_NOPE + MLA_ROPE
    half = MLA_ROPE // 2
    wq = w_uq.reshape(MLA_Q_LORA, MLA_HEADS, qd)
    zq = jnp.zeros((MLA_Q_LORA, MLA_HEADS, LANES - qd), F32)
    wq_main = jnp.concatenate([wq, zq], axis=-1)
    x1 = wq[:, :, MLA_NOPE:MLA_NOPE + half]
    x2 = wq[:, :, MLA_NOPE + half:]
    wq_rot = jnp.concatenate([jnp.zeros((MLA_Q_LORA, MLA_HEADS, MLA_NOPE), F32), -x2, x1, zq], axis=-1)
    wkv = w_ukv.reshape(MLA_KV_LORA, MLA_HEADS, MLA_NOPE + MLA_V)
    zkv = jnp.zeros((MLA_KV_LORA, MLA_HEADS, LANES - MLA_NOPE), F32)
    wk = jnp.concatenate([wkv[:, :, :MLA_NOPE], zkv], axis=-1)
    wv = jnp.concatenate([wkv[:, :, MLA_NOPE:], zkv], axis=-1)
    flat = lambda w: w.reshape(w.shape[0], -1).astype(BF16)
    return flat(wq_main), flat(wq_rot), flat(wk), flat(wv).T


def kernel(x, positions, rel_bias, norm_mix, w_in, b_gate, rwkv_mu, rwkv_w0, rwkv_w2, rwkv_a0, rwkv_a2, rwkv_g2, rwkv_k_k, rwkv_k_a, rwkv_r_k, rwkv_ln_w, rwkv_ln_b, mla_q_norm, mla_w_uq, mla_kv_norm, mla_w_ukv, diff_lambda, diff_subln, w_branch_rwkv, w_branch_mla, w_branch_diff, w_o, norm_ffn, ffn_w_up, ffn_conv_w, ffn_conv_b, ffn_w_down, norm_final):
    batch, seq, _ = x.shape
    depth = w_in.shape[0]
    t = batch * seq
    assert seq % TQ == 0 and seq % TM == 0 and seq % TM_FFN == 0 and seq % RWKV_TILE == 0 and RWKV_TILE % CHUNK == 0
    assert (batch * seq) % TM_MERGE == 0
    xf = x.reshape(t, D_MODEL)
    pos_col = positions.reshape(t, 1)
    inv_freq = ROPE_BASE ** (-jnp.arange(0, MLA_ROPE, 2, dtype=F32) / MLA_ROPE)
    freq = jnp.concatenate([jnp.zeros((MLA_NOPE,), F32), inv_freq, inv_freq,
                            jnp.zeros((LANES - MLA_NOPE - MLA_ROPE,), F32)]).reshape(1, LANES)
    row = lambda v: v.reshape(1, -1)
    diff_scale = jnp.concatenate([jnp.full((512,), DIFF_QK ** -0.5 * LOG2E, F32), jnp.ones((512,), F32)])

    for l in range(depth):
        s0, s1, s2 = RWKV_COLS, RWKV_COLS + MLA_COLS, RWKV_COLS + MLA_COLS + DIFF_COLS
        sv = s1 + DIFF_QK_COLS
        w = w_in[l]
        w_all = jnp.concatenate([w[:, :s0], w[:, s0:s1], jnp.zeros((D_MODEL, MLA_PAD - MLA_COLS), F32),
                                 w[:, s1:sv] * diff_scale, w[:, s2:]], axis=1).astype(BF16)
        w_vt = w[:, sv:s2].T.astype(BF16)
        p_rwkv, p_mla, p_qk, p_vt, p_gate = _in_proj(xf, row(norm_mix[l]), w_all, w_vt)

        o_r = _rwkv(p_rwkv, seq, row(rwkv_mu[l]), row(rwkv_w0[l]), rwkv_w2[l].astype(BF16), row(rwkv_a0[l]),
                    rwkv_a2[l].astype(BF16), rwkv_g2[l].astype(BF16), row(rwkv_k_k[l]), row(rwkv_k_a[l]),
                    row(rwkv_r_k[l]), row(rwkv_ln_w[l]), row(rwkv_ln_b[l]))

        wq, wqr, wk, wv = _mla_weights(mla_w_uq[l], mla_w_ukv[l])
        q_m, k_m, v_m = _mla_prep(p_mla, pos_col, freq, row(mla_q_norm[l]), row(mla_kv_norm[l]), wq, wqr, wk, wv)
        o_m = _mla_attn(q_m, k_m, v_m, batch, seq)

        o_d = _diff_attn(p_qk, p_vt, rel_bias, diff_lambda[l], row(diff_subln[l]), batch, seq, l)

        x1, h2 = _merge(xf, p_gate, row(b_gate[l]), o_r, o_m, o_d, w_branch_rwkv[l].astype(BF16),
                        w_branch_mla[l].astype(BF16), w_branch_diff[l].astype(BF16), w_o[l].astype(BF16),
                        row(norm_ffn[l]))
        xf = _ffn(x1, h2, ffn_w_up[l].astype(BF16), ffn_conv_w[l], row(ffn_conv_b[l]), ffn_w_down[l].astype(BF16),
                  row(norm_final), seq, final_norm=(l == depth - 1))
    return xf.reshape(batch, seq, D_MODEL)
```

```python
import functools
import math

import numpy as np
import jax
import jax.numpy as jnp
from jax import lax
from jax.experimental import pallas as pl
from jax.experimental.pallas import tpu as pltpu

F32 = jnp.float32
BF16 = jnp.bfloat16

D_MODEL = 1024
RWKV_HEADS = 8
RWKV_N = 64
RWKV_DIM = 512
RWKV_COLS = 1792
RWKV_GN_EPS = 64e-5
MLA_HEADS = 8
MLA_Q_LORA = 256
MLA_KV_LORA = 128
MLA_NOPE = 64
MLA_ROPE = 32
MLA_V = 64
MLA_COLS = 416
MLA_PAD = 512
ROPE_BASE = 10000.0
DIFF_HEADS = 4
DIFF_QK = 64
DIFF_V = 128
DIFF_COLS = 1536
DIFF_QK_COLS = 1024
DIFF_VT_ROWS = DIFF_V + 16
REL_BUCKETS = 32
REL_MAX_DISTANCE = 128
D_FF = 2816
GATE_COLS = 3072
NORM_EPS = 1e-6
SUBLN_EPS = 1e-5

LANES = 128
VMEM_LIMIT = 58 * 1024 * 1024
TM = 256
TM_FFN = 256
TM_MERGE = 512
CHUNK = 64
RWKV_TILE = 256
TQ = 512
KEY_TILES_PER_STEP = 1
FF_CHUNK = 256
FF_DOWN_GROUP = 11
NEG = -1e30
LOG2E = 1.4426950408889634

_NT = (((1,), (1,)), ((), ()))
_TN = (((0,), (0,)), ((), ()))


def _dot(a, b):
    return jnp.dot(a, b, preferred_element_type=F32)


def _dot_nt(a, b):
    return lax.dot_general(a, b, _NT, preferred_element_type=F32)


def _dot_tn(a, b):
    return lax.dot_general(a, b, _TN, preferred_element_type=F32)


def _rms(x, g, eps):
    return x * lax.rsqrt(jnp.mean(x * x, axis=-1, keepdims=True) + eps) * g


def _params():
    return pltpu.CompilerParams(dimension_semantics=("arbitrary",), vmem_limit_bytes=VMEM_LIMIT)


def _params2():
    return pltpu.CompilerParams(dimension_semantics=("arbitrary", "arbitrary"), vmem_limit_bytes=VMEM_LIMIT)


def _const_spec(shape):
    return pl.BlockSpec(shape, lambda *_: (0,) * len(shape), pipeline_mode=pl.Buffered(1))


def _row_spec(tm, cols, col_block=0):
    return pl.BlockSpec((tm, cols), lambda i: (i, col_block))


def _in_proj_kernel(x_ref, g_ref, w_ref, wvt_ref, pr_ref, pm_ref, pqk_ref, pvt_ref, pg_ref):
    h = _rms(x_ref[...], g_ref[...], NORM_EPS).astype(BF16)
    c0 = 0
    pr_ref[...] = _dot(h, w_ref[:, c0:c0 + RWKV_COLS])
    c0 += RWKV_COLS
    pm_ref[...] = _dot(h, w_ref[:, c0:c0 + MLA_PAD])
    c0 += MLA_PAD
    pqk_ref[...] = _dot(h, w_ref[:, c0:c0 + DIFF_QK_COLS]).astype(BF16)
    c0 += DIFF_QK_COLS
    pg_ref[...] = _dot(h, w_ref[:, c0:c0 + GATE_COLS]).astype(BF16)
    pvt_ref[...] = _dot_nt(wvt_ref[...], h).astype(BF16)


def _in_proj(x, g, w_all, w_vt):
    t = x.shape[0]
    n_all = w_all.shape[1]
    vw = DIFF_HEADS * DIFF_V
    return pl.pallas_call(
        _in_proj_kernel,
        grid=(t // TM,),
        in_specs=[_row_spec(TM, D_MODEL), _const_spec((1, D_MODEL)), _const_spec((D_MODEL, n_all)),
                  _const_spec((vw, D_MODEL))],
        out_specs=[_row_spec(TM, RWKV_COLS), _row_spec(TM, MLA_PAD), _row_spec(TM, DIFF_QK_COLS),
                   pl.BlockSpec((vw, TM), lambda i: (0, i)), _row_spec(TM, GATE_COLS)],
        out_shape=[jax.ShapeDtypeStruct((t, RWKV_COLS), F32), jax.ShapeDtypeStruct((t, MLA_PAD), F32),
                   jax.ShapeDtypeStruct((t, DIFF_QK_COLS), BF16), jax.ShapeDtypeStruct((vw, t), BF16),
                   jax.ShapeDtypeStruct((t, GATE_COLS), BF16)],
        compiler_params=_params(),
        name="in_proj",
    )(x, g, w_all, w_vt)


def _split3(a):
    hi = a.astype(BF16)
    r1 = a - hi.astype(F32)
    mid = r1.astype(BF16)
    lo = (r1 - mid.astype(F32)).astype(BF16)
    return hi, mid, lo


def _softplus(z):
    return jnp.maximum(z, 0.0) + jnp.log(1.0 + jnp.exp(-jnp.abs(z)))


def _rwkv_kernel(p_ref, mu_ref, w0_ref, w2_ref, a0_ref, a2_ref, g2_ref, kk_ref, ka_ref, rk_ref, lnw_ref, lnb_ref,
                 o_ref, carry_ref, state_ref, *, tiles_per_seq):
    c = CHUNK
    n = RWKV_N
    ts = RWKV_TILE
    nc = ts // c

    @pl.when(pl.program_id(0) % tiles_per_seq == 0)
    def _():
        carry_ref[...] = jnp.zeros_like(carry_ref)
        state_ref[...] = jnp.zeros_like(state_ref)

    p = p_ref[...]
    row = lax.broadcasted_iota(jnp.int32, (ts, 1), 0)
    shifted = jnp.where(row == 0, carry_ref[...], pltpu.roll(p, 1, 0))
    carry_ref[...] = p[ts - 1:ts, :]
    pm = p + (shifted - p) * mu_ref[...]

    r = pm[:, 0:512]
    k = pm[:, 512:1024]
    v = pm[:, 1024:1536]
    pw = pm[:, 1536:1600]
    pa = pm[:, 1600:1664]
    pg = pm[:, 1664:1792]

    w_log = -_softplus(-(w0_ref[...] + _dot(jnp.tanh(pw).astype(BF16), w2_ref[...]))) - 0.5
    logd = -jnp.exp(w_log)
    a = jax.nn.sigmoid(a0_ref[...] + _dot(pa.astype(BF16), a2_ref[...]))
    g = _dot(jax.nn.sigmoid(pg).astype(BF16), g2_ref[...])

    tr = lax.broadcasted_iota(jnp.int32, (ts, ts), 0)
    tc = lax.broadcasted_iota(jnp.int32, (ts, ts), 1)
    tri = ((tr >= tc) & (tr // c == tc // c)).astype(BF16)
    hi, mid, lo = _split3(logd)
    cs = _dot(tri, hi) + _dot(tri, mid) + _dot(tri, lo)
    total = jnp.concatenate([jnp.broadcast_to(cs[(m + 1) * c - 1:(m + 1) * c, :], (c, RWKV_DIM)) for m in range(nc)], axis=0)
    e_in = jnp.exp(cs)
    e_ex = jnp.exp(cs - logd)
    e_inv = jnp.exp(-cs)
    e_end = jnp.exp(total - cs)
    g_end = jnp.exp(total)

    n_pairs = RWKV_HEADS // 2
    lo_t = lax.broadcasted_iota(jnp.int32, (ts, LANES), 1) < n
    lo_c = lax.broadcasted_iota(jnp.int32, (c, LANES), 1) < n

    def head_sum(x):
        blocks = []
        for q in range(n_pairs):
            xb = x[:, q * LANES:(q + 1) * LANES]
            s_lo = jnp.sum(jnp.where(lo_t, xb, 0.0), axis=-1, keepdims=True)
            s_hi = jnp.sum(jnp.where(lo_t, 0.0, xb), axis=-1, keepdims=True)
            blocks.append(jnp.where(lo_t, s_lo, s_hi))
        return jnp.concatenate(blocks, axis=-1)

    k2 = k * (1.0 + (a - 1.0) * ka_ref[...])
    kku = k * kk_ref[...]
    kkn = kku / jnp.maximum(jnp.sqrt(head_sum(kku * kku)), 1e-12)
    b = kkn * a
    full = {"at": -kkn * e_ex, "rt": r * e_in, "bt": b * e_inv, "kt": k2 * e_inv, "v": v,
            "be": b * e_end, "ke": k2 * e_end}

    ti = lax.broadcasted_iota(jnp.int32, (2 * c, 4 * c), 0) % c
    si = lax.broadcasted_iota(jnp.int32, (2 * c, 4 * c), 1) % c
    strict = ti > si
    incl = ti >= si
    eye = (lax.broadcasted_iota(jnp.int32, (2 * c, 2 * c), 0)
           == lax.broadcasted_iota(jnp.int32, (2 * c, 2 * c), 1)).astype(F32)

    items = [(m, q) for m in range(nc) for q in range(n_pairs)]

    def stacked(name, m, q):
        xb = full[name][m * c:(m + 1) * c, q * LANES:(q + 1) * LANES]
        return jnp.concatenate([jnp.where(lo_c, xb, 0.0), jnp.where(lo_c, 0.0, xb)], axis=0).astype(BF16)

    st_ops = {name: [stacked(name, m, q) for m, q in items] for name in full}
    big_l = [_dot_nt(jnp.concatenate([at, rtb], axis=0), jnp.concatenate([bt, kt], axis=0))
             for at, rtb, bt, kt in zip(st_ops["at"], st_ops["rt"], st_ops["bt"], st_ops["kt"])]
    top_l = [jnp.where(strict, big[:2 * c, :], 0.0) for big in big_l]
    bot_l = [jnp.where(incl, big[2 * c:, :], 0.0).astype(BF16) for big in big_l]
    lakv_l = [_dot(top[:, 2 * c:].astype(BF16), vb) for top, vb in zip(top_l, st_ops["v"])]

    x_l = [top[:, :2 * c] for top in top_l]
    tinv_l = [eye + x for x in x_l]
    xb_l = [x.astype(BF16) for x in x_l]
    for _ in range(int(math.log2(c)) - 1):
        x_l = [_dot(xb, xb) for xb in xb_l]
        xb_l = [x.astype(BF16) for x in x_l]
        tinv_l = [tinv + _dot(tinv.astype(BF16), xb) for tinv, xb in zip(tinv_l, xb_l)]
    tinvb_l = [tinv.astype(BF16) for tinv in tinv_l]
    abar_l = [_dot(tb, at).astype(BF16) for tb, at in zip(tinvb_l, st_ops["at"])]
    vbar_l = [_dot(tb, lv.astype(BF16)).astype(BF16) for tb, lv in zip(tinvb_l, lakv_l)]
    rhat_l = [(rtb.astype(F32) + _dot(bot[:, :2 * c], ab)).astype(BF16)
              for rtb, bot, ab in zip(st_ops["rt"], bot_l, abar_l)]
    uv_l = [jnp.concatenate([vbar, vb], axis=0) for vbar, vb in zip(vbar_l, st_ops["v"])]
    y0_l = [_dot(bot, uv) for bot, uv in zip(bot_l, uv_l)]
    p_l = [_dot_tn(ab, be).astype(BF16) for ab, be in zip(abar_l, st_ops["be"])]
    q_l = [_dot_tn(uv, jnp.concatenate([be, ke], axis=0)) for uv, be, ke in zip(uv_l, st_ops["be"], st_ops["ke"])]

    st_l = [state_ref[q] for q in range(n_pairs)]
    y_rows = []
    for m in range(nc):
        y_blocks = []
        for q in range(n_pairs):
            idx = m * n_pairs + q
            st = st_l[q]
            st_b = st.astype(BF16)
            y_s = _dot_nt(rhat_l[idx], st_b) + y0_l[idx]
            y_blocks.append(y_s[:c] + y_s[c:])
            decay = g_end[(m + 1) * c - 1:(m + 1) * c, q * LANES:(q + 1) * LANES]
            st_l[q] = st * decay + _dot(st_b, p_l[idx]) + q_l[idx]
        y_rows.append(jnp.concatenate(y_blocks, axis=-1))
    for q in range(n_pairs):
        state_ref[q] = st_l[q]
    y = jnp.concatenate(y_rows, axis=0)

    inv_n = 1.0 / n
    yc = y - head_sum(y) * inv_n
    var = head_sum(yc * yc) * inv_n
    yn = yc * lax.rsqrt(var + RWKV_GN_EPS) * lnw_ref[...] + lnb_ref[...]
    bonus = head_sum(r * k2 * rk_ref[...]) * v
    o_ref[...] = ((yn + bonus) * g).astype(BF16)


def _rwkv(p_rwkv, seq, mu, w0, w2, a0, a2, g2, k_k, k_a, r_k, ln_w, ln_b):
    t = p_rwkv.shape[0]
    ts = RWKV_TILE
    kern = functools.partial(_rwkv_kernel, tiles_per_seq=seq // ts)
    vec = lambda: _const_spec((1, RWKV_DIM))
    return pl.pallas_call(
        kern,
        grid=(t // ts,),
        in_specs=[_row_spec(ts, RWKV_COLS), _const_spec((1, RWKV_COLS)), vec(), _const_spec((64, RWKV_DIM)), vec(),
                  _const_spec((64, RWKV_DIM)), _const_spec((128, RWKV_DIM)), vec(), vec(), vec(), vec(), vec()],
        out_specs=_row_spec(ts, RWKV_DIM),
        out_shape=jax.ShapeDtypeStruct((t, RWKV_DIM), BF16),
        scratch_shapes=[pltpu.VMEM((1, RWKV_COLS), F32), pltpu.VMEM((RWKV_HEADS // 2, LANES, LANES), F32)],
        compiler_params=_params(),
        name="rwkv7",
    )(p_rwkv, mu, w0, w2, a0, a2, g2, k_k, k_a, r_k, ln_w, ln_b)


def _mla_prep_kernel(p_ref, pos_ref, freq_ref, qn_ref, kvn_ref, wq_ref, wqr_ref, wk_ref, wv_ref, q_out, k_out, v_out):
    p = p_ref[...]
    hq = _rms(p[:, 0:MLA_Q_LORA], qn_ref[...], NORM_EPS).astype(BF16)
    hkv = _rms(p[:, MLA_Q_LORA:MLA_Q_LORA + MLA_KV_LORA], kvn_ref[...], NORM_EPS).astype(BF16)
    blk = p[:, 384:512]
    ang = pos_ref[...].astype(F32) * freq_ref[...]
    cos = jnp.cos(ang)
    sin = jnp.sin(ang)
    scale = (MLA_NOPE + MLA_ROPE) ** -0.5 * LOG2E
    qa = _dot(hq, wq_ref[...])
    qr = _dot(hq, wqr_ref[...])
    lane = lax.broadcasted_iota(jnp.int32, blk.shape, 1)
    kr = pltpu.roll(blk, 64, 1)
    rot = jnp.where(lane < 80, -pltpu.roll(blk, 48, 1), pltpu.roll(blk, 80, 1))
    rot = jnp.where((lane >= 64) & (lane < 96), rot, 0.0)
    krope = kr * cos + rot * sin
    kn = _dot(hkv, wk_ref[...])
    for h in range(MLA_HEADS):
        hs = slice(h * LANES, (h + 1) * LANES)
        q_out[:, hs] = ((qa[:, hs] * cos + qr[:, hs] * sin) * scale).astype(BF16)
        k_out[:, hs] = (kn[:, hs] + krope).astype(BF16)
    vt = _dot_nt(wv_ref[...], hkv)
    rowi = lax.broadcasted_iota(jnp.int32, vt.shape, 0)
    v_out[...] = (vt + (rowi % LANES == MLA_V).astype(F32)).astype(BF16)


def _mla_prep(p_mla, pos_col, freq, q_norm, kv_norm, wq, wqr, wk, wv):
    t = p_mla.shape[0]
    hw = MLA_HEADS * LANES
    return pl.pallas_call(
        _mla_prep_kernel,
        grid=(t // TM,),
        in_specs=[_row_spec(TM, MLA_PAD), _row_spec(TM, 1), _const_spec((1, LANES)), _const_spec((1, MLA_Q_LORA)),
                  _const_spec((1, MLA_KV_LORA)), _const_spec((MLA_Q_LORA, hw)), _const_spec((MLA_Q_LORA, hw)),
                  _const_spec((MLA_KV_LORA, hw)), _const_spec((hw, MLA_KV_LORA))],
        out_specs=[_row_spec(TM, hw), _row_spec(TM, hw), pl.BlockSpec((hw, TM), lambda i: (0, i))],
        out_shape=[jax.ShapeDtypeStruct((t, hw), BF16), jax.ShapeDtypeStruct((t, hw), BF16),
                   jax.ShapeDtypeStruct((hw, t), BF16)],
        compiler_params=_params(),
        name="mla_prep",
    )(p_mla, pos_col, freq, q_norm, kv_norm, wq, wqr, wk, wv)


def _softmax_stage_t(s_l, pv_fn, m_scr, acc_scr, idx, s_scr=None):
    n = len(s_l)
    m_old = [m_scr[c][0:1, :] for c in idx]
    m_new = [jnp.maximum(m, jnp.max(s, axis=0, keepdims=True)) for m, s in zip(m_old, s_l)]
    alpha = [jnp.exp2(mo - mn) for mo, mn in zip(m_old, m_new)]
    if s_scr is not None:
        for c in range(n):
            s_scr[c] = s_l[c]
        s_l = [s_scr[c] for c in range(n)]
    pv_l = [pv_fn(c, jnp.exp2(s_l[c] - m_new[c]).astype(BF16)) for c in range(n)]
    for c in range(n):
        m_scr[idx[c]] = jnp.broadcast_to(m_new[c], m_scr.shape[1:])
        acc_scr[idx[c]] = alpha[c] * acc_scr[idx[c]] + pv_l[c]


def _mla_attn_kernel(q_ref, k_ref, vt_ref, o_ref, m_scr, acc_scr, s_scr):
    tq = TQ
    i = pl.program_id(1)
    causal = lax.broadcasted_iota(jnp.int32, (tq, tq), 0) <= lax.broadcasted_iota(jnp.int32, (tq, tq), 1)
    lane = lax.broadcasted_iota(jnp.int32, (tq, LANES), 1)
    m_scr[...] = jnp.full(m_scr.shape, NEG, F32)
    acc_scr[...] = jnp.zeros(acc_scr.shape, F32)
    hsl = [slice(h * LANES, (h + 1) * LANES) for h in range(MLA_HEADS)]
    heads = list(range(MLA_HEADS))

    def tile(off, tk, mask):
        s_l = [_dot_nt(k_ref[pl.ds(off, tk), hs], q_ref[:, hs]) for hs in hsl]
        if mask:
            s_l = [jnp.where(causal, s, NEG) for s in s_l]
        _softmax_stage_t(s_l, lambda c, p: _dot(vt_ref[hsl[c], pl.ds(off, tk)], p), m_scr, acc_scr, heads,
                         s_scr.at[:, 0:tk, :])

    def pair_body(j, carry):
        tile(pl.multiple_of(j * 2 * tq, 2 * tq), 2 * tq, False)
        return carry

    def single_body(j, carry):
        tile(pl.multiple_of(j * tq, tq), tq, False)
        return carry

    n_pairs = i // 2 if KEY_TILES_PER_STEP == 2 else 0
    if KEY_TILES_PER_STEP == 2:
        lax.fori_loop(0, n_pairs, pair_body, 0)
    lax.fori_loop(2 * n_pairs, i, single_body, 0)
    tile(pl.multiple_of(i * tq, tq), tq, True)
    for pr in range(MLA_HEADS // 2):
        outs = []
        for hh in range(2):
            acc = acc_scr[2 * pr + hh]
            outs.append(jnp.transpose(acc / acc[MLA_V:MLA_V + 1, :]))
        o_ref[:, pr * LANES:(pr + 1) * LANES] = jnp.where(lane < MLA_V, outs[0], pltpu.roll(outs[1], MLA_V, 1)).astype(BF16)


def _mla_attn(q, k, vt, batch, seq):
    t = q.shape[0]
    nq = seq // TQ
    hw = MLA_HEADS * LANES
    vw = MLA_HEADS * MLA_V
    return pl.pallas_call(
        _mla_attn_kernel,
        grid=(batch, nq),
        in_specs=[pl.BlockSpec((TQ, hw), lambda b, i: (b * nq + i, 0)),
                  pl.BlockSpec((seq, hw), lambda b, i: (b, 0)),
                  pl.BlockSpec((hw, seq), lambda b, i: (0, b))],
        out_specs=pl.BlockSpec((TQ, vw), lambda b, i: (b * nq + i, 0)),
        out_shape=jax.ShapeDtypeStruct((t, vw), BF16),
        scratch_shapes=[pltpu.VMEM((MLA_HEADS, 8, TQ), F32), pltpu.VMEM((MLA_HEADS, LANES, TQ), F32),
                        pltpu.VMEM((MLA_HEADS, KEY_TILES_PER_STEP * TQ, TQ), F32)],
        compiler_params=_params2(),
        name="mla_attn",
    )(q, k, vt)


def _t5_bucket_table():
    n = np.arange(0, REL_MAX_DISTANCE + 1)
    max_exact = REL_BUCKETS // 2
    nf = np.maximum(n, 1).astype(np.float32)
    ratio = np.log(nf / np.float32(max_exact)) / np.float32(math.log(REL_MAX_DISTANCE / max_exact))
    large = max_exact + (ratio * np.float32(REL_BUCKETS - max_exact)).astype(np.int32)
    large = np.minimum(large, REL_BUCKETS - 1)
    return np.where(n < max_exact, n, large)


_BUCKETS = _t5_bucket_table()
_FAR_BUCKET = int(_BUCKETS[REL_MAX_DISTANCE])
assert _FAR_BUCKET == REL_BUCKETS - 1 and np.all(np.diff(_BUCKETS) >= 0)
_BUCKET_STARTS = [int(np.argmax(_BUCKETS >= b)) for b in range(REL_BUCKETS // 2 + 1, REL_BUCKETS)]


def _diff_attn_kernel(rb_ref, lam_ref, subln_ref, q_ref, k_ref, vt_ref, o_ref, bias_scr, m_scr, acc_scr, s_scr, *,
                      lambda_init):
    tq = TQ
    i = pl.program_id(1)
    r_i = lax.broadcasted_iota(jnp.int32, (tq, tq), 0)
    c_i = lax.broadcasted_iota(jnp.int32, (tq, tq), 1)
    causal = r_i <= c_i

    @pl.when((pl.program_id(0) == 0) & (i == 0))
    def _():
        for t_idx, delta in enumerate((0, tq)):
            d = jnp.maximum(c_i - r_i + delta, 0)
            log_b = REL_BUCKETS // 2
            for start in _BUCKET_STARTS:
                log_b = log_b + (d >= start).astype(jnp.int32)
            bucket = jnp.where(d < REL_BUCKETS // 2, d, log_b)
            for h in range(DIFF_HEADS):
                bias = jnp.zeros((tq, tq), F32)
                for b in range(REL_BUCKETS):
                    bias = jnp.where(bucket == b, rb_ref[b, h], bias)
                bias_scr[t_idx, h] = (bias - rb_ref[_FAR_BUCKET, h]) * LOG2E

    m_scr[...] = jnp.full(m_scr.shape, NEG, F32)
    acc_scr[...] = jnp.zeros(acc_scr.shape, F32)
    lane = lax.broadcasted_iota(jnp.int32, (tq, LANES), 1)
    ones_rows = {tk: (lax.broadcasted_iota(jnp.int32, (DIFF_VT_ROWS - DIFF_V, tk), 0) == 0).astype(BF16)
                 for tk in (tq, 2 * tq)}
    hsl = [slice(h * LANES, (h + 1) * LANES) for h in range(DIFF_HEADS)]
    heads = list(range(DIFF_HEADS))
    q_l = []
    for hs in hsl:
        qp = q_ref[:, hs]
        q_l.append(jnp.concatenate([jnp.where(lane < DIFF_QK, qp, jnp.zeros_like(qp)),
                                    jnp.where(lane >= DIFF_QK, qp, jnp.zeros_like(qp))], axis=0))

    def tile(off, tk, bias_idx, mask):
        s_l = [_dot_nt(k_ref[pl.ds(off, tk), hsl[h]], q_l[h]) for h in heads]
        if bias_idx is not None:
            s_l = [s + jnp.concatenate([bias_scr[bias_idx, h]] * 2, axis=1) for h, s in enumerate(s_l)]
        if mask:
            mask2 = jnp.concatenate([causal, causal], axis=1)
            s_l = [jnp.where(mask2, s, NEG) for s in s_l]

        def pv(c, p):
            vt = jnp.concatenate([vt_ref[hsl[c], pl.ds(off, tk)], ones_rows[tk]], axis=0)
            return jnp.concatenate([_dot(vt, p[:, :tq]), _dot(vt, p[:, tq:])], axis=1)

        _softmax_stage_t(s_l, pv, m_scr, acc_scr, heads, s_scr.at[:, 0:tk, :])

    def far_pair_body(j, carry):
        tile(pl.multiple_of(j * 2 * tq, 2 * tq), 2 * tq, None, False)
        return carry

    def far_body(j, carry):
        tile(pl.multiple_of(j * tq, tq), tq, None, False)
        return carry

    def near_body(j, carry):
        tile(pl.multiple_of(j * tq, tq), tq, 1, False)
        return carry

    n_far = jnp.maximum(i - 1, 0)
    n_pairs = n_far // 2 if KEY_TILES_PER_STEP == 2 else 0
    if KEY_TILES_PER_STEP == 2:
        lax.fori_loop(0, n_pairs, far_pair_body, 0)
    lax.fori_loop(2 * n_pairs, n_far, far_body, 0)
    lax.fori_loop(n_far, i, near_body, 0)
    tile(pl.multiple_of(i * tq, tq), tq, 0, True)

    lam = lam_ref[...]
    lam_full = (jnp.exp(jnp.sum(lam[0:1] * lam[1:2], axis=-1, keepdims=True))
                - jnp.exp(jnp.sum(lam[2:3] * lam[3:4], axis=-1, keepdims=True)) + lambda_init)
    for h in heads:
        acc = acc_scr[h]
        on = acc[:DIFF_V, :] / acc[DIFF_V:DIFF_V + 1, :]
        ot = on[:, :tq] - lam_full * on[:, tq:]
        ot = ot * lax.rsqrt(jnp.mean(ot * ot, axis=0, keepdims=True) + SUBLN_EPS)
        o_ref[:, hsl[h]] = (jnp.transpose(ot) * subln_ref[...] * (1.0 - lambda_init)).astype(BF16)


def _diff_attn(p_qk, p_vt, rel_bias, lam, subln, batch, seq, layer_idx):
    t = p_qk.shape[0]
    nq = seq // TQ
    w = DIFF_HEADS * LANES
    lambda_init = 0.8 - 0.6 * math.exp(-0.3 * layer_idx)
    kern = functools.partial(_diff_attn_kernel, lambda_init=lambda_init)
    const2 = lambda shape: pl.BlockSpec(shape, lambda b, i: (0, 0))
    return pl.pallas_call(
        kern,
        grid=(batch, nq),
        in_specs=[pl.BlockSpec(memory_space=pltpu.SMEM),
                  const2((4, DIFF_QK)), const2((1, DIFF_V)),
                  pl.BlockSpec((TQ, w), lambda b, i: (b * nq + i, 0)),
                  pl.BlockSpec((seq, w), lambda b, i: (b, 1)),
                  pl.BlockSpec((w, seq), lambda b, i: (0, b))],
        out_specs=pl.BlockSpec((TQ, w), lambda b, i: (b * nq + i, 0)),
        out_shape=jax.ShapeDtypeStruct((t, w), BF16),
        scratch_shapes=[pltpu.VMEM((2, DIFF_HEADS, TQ, TQ), F32), pltpu.VMEM((DIFF_HEADS, 8, 2 * TQ), F32),
                        pltpu.VMEM((DIFF_HEADS, DIFF_VT_ROWS, 2 * TQ), F32),
                        pltpu.VMEM((DIFF_HEADS, KEY_TILES_PER_STEP * TQ, 2 * TQ), F32)],
        compiler_params=_params2(),
        name="diff_attn",
    )(rel_bias, lam, subln, p_qk, p_qk, p_vt)


def _merge_kernel(x_ref, pg_ref, bg_ref, or_ref, om_ref, od_ref, wr_ref, wm_ref, wd_ref, wo_ref, gf_ref, x_out, h_out):
    halves = [slice(s * (TM_MERGE // 2), (s + 1) * (TM_MERGE // 2)) for s in range(2)]
    branch = [[_dot(o_ref[rs, :], w_ref[...]) for o_ref, w_ref in ((or_ref, wr_ref), (om_ref, wm_ref), (od_ref, wd_ref))]
              for rs in halves]
    for rs, d in zip(halves, branch):
        merged = None
        for idx in range(3):
            cs = slice(idx * D_MODEL, (idx + 1) * D_MODEL)
            term = jax.nn.sigmoid(pg_ref[rs, cs].astype(F32) + bg_ref[:, cs]) * d[idx]
            merged = term if merged is None else merged + term
        x1 = x_ref[rs, :] + _dot(merged.astype(BF16), wo_ref[...])
        x_out[rs, :] = x1
        h_out[rs, :] = _rms(x1, gf_ref[...], NORM_EPS).astype(BF16)


def _merge(x, p_gate, b_gate, o_r, o_m, o_d, w_r, w_m, w_d, w_o, g_ffn):
    t = x.shape[0]
    tm = TM_MERGE
    return pl.pallas_call(
        _merge_kernel,
        grid=(t // tm,),
        in_specs=[_row_spec(tm, D_MODEL), _row_spec(tm, GATE_COLS), _const_spec((1, GATE_COLS)),
                  _row_spec(tm, 512), _row_spec(tm, 512), _row_spec(tm, 512),
                  _const_spec((512, D_MODEL)), _const_spec((512, D_MODEL)), _const_spec((512, D_MODEL)),
                  _const_spec((D_MODEL, D_MODEL)), _const_spec((1, D_MODEL))],
        out_specs=[_row_spec(tm, D_MODEL), _row_spec(tm, D_MODEL)],
        out_shape=[jax.ShapeDtypeStruct((t, D_MODEL), F32), jax.ShapeDtypeStruct((t, D_MODEL), BF16)],
        compiler_params=_params(),
        name="merge",
    )(x, p_gate, b_gate, o_r, o_m, o_d, w_r, w_m, w_d, w_o, g_ffn)


def _ffn_kernel(x_ref, h_ref, wup_ref, cw_ref, cb_ref, wdn_ref, gfin_ref, o_ref, carry_ref, *, tiles_per_seq, final_norm):
    tm = TM_FFN

    @pl.when(pl.program_id(0) % tiles_per_seq == 0)
    def _():
        carry_ref[0:8, :] = jnp.zeros((8, carry_ref.shape[1]), F32)

    h = h_ref[...]

    def conv(u, cols):
        carry_ref[8:, cols] = u
        u1 = carry_ref[7:7 + tm, cols]
        u2 = carry_ref[6:6 + tm, cols]
        carry_ref[0:8, cols] = u[tm - 8:tm, :]
        return cw_ref[0:1, cols] * u2 + cw_ref[1:2, cols] * u1 + cw_ref[2:3, cols] * u + cb_ref[:, cols]

    def up(ck):
        gc = slice(ck * FF_CHUNK, (ck + 1) * FF_CHUNK)
        vc = slice(D_FF + ck * FF_CHUNK, D_FF + (ck + 1) * FF_CHUNK)
        return _dot(h, wup_ref[:, gc]), _dot(h, wup_ref[:, vc])

    n_chunks = D_FF // FF_CHUNK
    acc = x_ref[...]
    nxt = up(0)
    acts = []
    for ck in range(n_chunks):
        ug, uv = nxt
        if ck + 1 < n_chunks:
            nxt = up(ck + 1)
        gc = slice(ck * FF_CHUNK, (ck + 1) * FF_CHUNK)
        vc = slice(D_FF + ck * FF_CHUNK, D_FF + (ck + 1) * FF_CHUNK)
        gate = conv(ug, gc)
        val = conv(uv, vc)
        acts.append((gate * jax.nn.sigmoid(gate) * val).astype(BF16))
        if len(acts) == FF_DOWN_GROUP or ck + 1 == n_chunks:
            rows = slice((ck + 1 - len(acts)) * FF_CHUNK, (ck + 1) * FF_CHUNK)
            acc = acc + _dot(jnp.concatenate(acts, axis=-1), wdn_ref[rows, :])
            acts = []
    if final_norm:
        acc = _rms(acc, gfin_ref[...], NORM_EPS)
    o_ref[...] = acc


def _ffn(x1, h2, w_up, conv_w, conv_b, w_down, g_final, seq, final_norm):
    t = x1.shape[0]
    tm = TM_FFN
    kern = functools.partial(_ffn_kernel, tiles_per_seq=seq // tm, final_norm=final_norm)
    return pl.pallas_call(
        kern,
        grid=(t // tm,),
        in_specs=[_row_spec(tm, D_MODEL), _row_spec(tm, D_MODEL), _const_spec((D_MODEL, 2 * D_FF)),
                  _const_spec((3, 2 * D_FF)), _const_spec((1, 2 * D_FF)), _const_spec((D_FF, D_MODEL)),
                  _const_spec((1, D_MODEL))],
        out_specs=_row_spec(tm, D_MODEL),
        out_shape=jax.ShapeDtypeStruct((t, D_MODEL), F32),
        scratch_shapes=[pltpu.VMEM((8 + tm, 2 * D_FF), F32)],
        compiler_params=_params(),
        name="conv_ffn",
    )(x1, h2, w_up, conv_w, conv_b, w_down, g_final)


def _mla_weights(w_uq, w_ukv):
    qd = MLA_NOPE + MLA_ROPE
    half = MLA_ROPE // 2
    wq = w_uq.reshape(MLA_Q_LORA, MLA_HEADS, qd)
    zq = jnp.zeros((MLA_Q_LORA, MLA_HEADS, LANES - qd), F32)
    wq_main = jnp.concatenate([wq, zq], axis=-1)
    x1 = wq[:, :, MLA_NOPE:MLA_NOPE + half]
    x2 = wq[:, :, MLA_NOPE + half:]
    wq_rot = jnp.concatenate([jnp.zeros((MLA_Q_LORA, MLA_HEADS, MLA_NOPE), F32), -x2, x1, zq], axis=-1)
    wkv = w_ukv.reshape(MLA_KV_LORA, MLA_HEADS, MLA_NOPE + MLA_V)
    zkv = jnp.zeros((MLA_KV_LORA, MLA_HEADS, LANES - MLA_NOPE), F32)
    wk = jnp.concatenate([wkv[:, :, :MLA_NOPE], zkv], axis=-1)
    wv = jnp.concatenate([wkv[:, :, MLA_NOPE:], zkv], axis=-1)
    flat = lambda w: w.reshape(w.shape[0], -1).astype(BF16)
    return flat(wq_main), flat(wq_rot), flat(wk), flat(wv).T


def kernel(x, positions, rel_bias, norm_mix, w_in, b_gate, rwkv_mu, rwkv_w0, rwkv_w2, rwkv_a0, rwkv_a2, rwkv_g2, rwkv_k_k, rwkv_k_a, rwkv_r_k, rwkv_ln_w, rwkv_ln_b, mla_q_norm, mla_w_uq, mla_kv_norm, mla_w_ukv, diff_lambda, diff_subln, w_branch_rwkv, w_branch_mla, w_branch_diff, w_o, norm_ffn, ffn_w_up, ffn_conv_w, ffn_conv_b, ffn_w_down, norm_final):
    batch, seq, _ = x.shape
    depth = w_in.shape[0]
    t = batch * seq
    assert seq % TQ == 0 and seq % TM == 0 and seq % TM_FFN == 0 and seq % RWKV_TILE == 0 and RWKV_TILE % CHUNK == 0
    assert (batch * seq) % TM_MERGE == 0
    xf = x.reshape(t, D_MODEL)
    pos_col = positions.reshape(t, 1)
    inv_freq = ROPE_BASE ** (-jnp.arange(0, MLA_ROPE, 2, dtype=F32) / MLA_ROPE)
    freq = jnp.concatenate([jnp.zeros((MLA_NOPE,), F32), inv_freq, inv_freq,
                            jnp.zeros((LANES - MLA_NOPE - MLA_ROPE,), F32)]).reshape(1, LANES)
    row = lambda v: v.reshape(1, -1)
    diff_scale = jnp.concatenate([jnp.full((512,), DIFF_QK ** -0.5 * LOG2E, F32), jnp.ones((512,), F32)])

    for l in range(depth):
        s0, s1, s2 = RWKV_COLS, RWKV_COLS + MLA_COLS, RWKV_COLS + MLA_COLS + DIFF_COLS
        sv = s1 + DIFF_QK_COLS
        w = w_in[l]
        w_all = jnp.concatenate([w[:, :s0], w[:, s0:s1], jnp.zeros((D_MODEL, MLA_PAD - MLA_COLS), F32),
                                 w[:, s1:sv] * diff_scale, w[:, s2:]], axis=1).astype(BF16)
        w_vt = w[:, sv:s2].T.astype(BF16)
        p_rwkv, p_mla, p_qk, p_vt, p_gate = _in_proj(xf, row(norm_mix[l]), w_all, w_vt)

        o_r = _rwkv(p_rwkv, seq, row(rwkv_mu[l]), row(rwkv_w0[l]), rwkv_w2[l].astype(BF16), row(rwkv_a0[l]),
                    rwkv_a2[l].astype(BF16), rwkv_g2[l].astype(BF16), row(rwkv_k_k[l]), row(rwkv_k_a[l]),
                    row(rwkv_r_k[l]), row(rwkv_ln_w[l]), row(rwkv_ln_b[l]))

        wq, wqr, wk, wv = _mla_weights(mla_w_uq[l], mla_w_ukv[l])
        q_m, k_m, v_m = _mla_prep(p_mla, pos_col, freq, row(mla_q_norm[l]), row(mla_kv_norm[l]), wq, wqr, wk, wv)
        o_m = _mla_attn(q_m, k_m, v_m, batch, seq)

        o_d = _diff_attn(p_qk, p_vt, rel_bias, diff_lambda[l], row(diff_subln[l]), batch, seq, l)

        x1, h2 = _merge(xf, p_gate, row(b_gate[l]), o_r, o_m, o_d, w_branch_rwkv[l].astype(BF16),
                        w_branch_mla[l].astype(BF16), w_branch_diff[l].astype(BF16), w_o[l].astype(BF16),
                        row(norm_ffn[l]))
        xf = _ffn(x1, h2, ffn_w_up[l].astype(BF16), ffn_conv_w[l], row(ffn_conv_b[l]), ffn_w_down[l].astype(BF16),
                  row(norm_final), seq, final_norm=(l == depth - 1))
    return xf.reshape(batch, seq, D_MODEL)
```

```python
import functools
import math

import numpy as np
import jax
import jax.numpy as jnp
from jax import lax
from jax.experimental import pallas as pl
from jax.experimental.pallas import tpu as pltpu

F32 = jnp.float32
BF16 = jnp.bfloat16

D_MODEL = 1024
RWKV_HEADS = 8
RWKV_N = 64
RWKV_DIM = 512
RWKV_COLS = 1792
RWKV_GN_EPS = 64e-5
MLA_HEADS = 8
MLA_Q_LORA = 256
MLA_KV_LORA = 128
MLA_NOPE = 64
MLA_ROPE = 32
MLA_V = 64
MLA_COLS = 416
MLA_PAD = 512
ROPE_BASE = 10000.0
DIFF_HEADS = 4
DIFF_QK = 64
DIFF_V = 128
DIFF_COLS = 1536
DIFF_QK_COLS = 1024
DIFF_VT_ROWS = DIFF_V + 16
REL_BUCKETS = 32
REL_MAX_DISTANCE = 128
D_FF = 2816
GATE_COLS = 3072
NORM_EPS = 1e-6
SUBLN_EPS = 1e-5

LANES = 128
VMEM_LIMIT = 58 * 1024 * 1024
TM = 256
TM_FFN = 256
TM_MERGE = 512
CHUNK = 64
RWKV_TILE = 256
TQ = 512
KEY_TILES_PER_STEP = 1
FF_CHUNK = 256
FF_DOWN_GROUP = 11
NEG = -1e30
LOG2E = 1.4426950408889634

_NT = (((1,), (1,)), ((), ()))
_TN = (((0,), (0,)), ((), ()))


def _dot(a, b):
    return jnp.dot(a, b, preferred_element_type=F32)


def _dot_nt(a, b):
    return lax.dot_general(a, b, _NT, preferred_element_type=F32)


def _dot_tn(a, b):
    return lax.dot_general(a, b, _TN, preferred_element_type=F32)


def _rms(x, g, eps):
    return x * lax.rsqrt(jnp.mean(x * x, axis=-1, keepdims=True) + eps) * g


def _params():
    return pltpu.CompilerParams(dimension_semantics=("arbitrary",), vmem_limit_bytes=VMEM_LIMIT)


def _params2():
    return pltpu.CompilerParams(dimension_semantics=("arbitrary", "arbitrary"), vmem_limit_bytes=VMEM_LIMIT)


def _const_spec(shape):
    return pl.BlockSpec(shape, lambda *_: (0,) * len(shape), pipeline_mode=pl.Buffered(1))


def _row_spec(tm, cols, col_block=0):
    return pl.BlockSpec((tm, cols), lambda i: (i, col_block))


def _in_proj_kernel(x_ref, g_ref, w_ref, wvt_ref, pos_ref, freq_ref, qn_ref, kvn_ref, wq_ref, wqr_ref, wk_ref, wv_ref,
                    pr_ref, mq_ref, mk_ref, mvt_ref, pqk_ref, pvt_ref, pg_ref):
    h = _rms(x_ref[...], g_ref[...], NORM_EPS).astype(BF16)
    c_m = RWKV_COLS
    c_qk = c_m + MLA_PAD
    c_g = c_qk + DIFF_QK_COLS
    p_mla = _dot(h, w_ref[:, c_m:c_qk])
    pqk_ref[...] = _dot(h, w_ref[:, c_qk:c_g]).astype(BF16)
    pg_ref[...] = _dot(h, w_ref[:, c_g:c_g + GATE_COLS]).astype(BF16)
    _mla_prep_body(p_mla, pos_ref, freq_ref, qn_ref, kvn_ref, wq_ref, wqr_ref, wk_ref, wv_ref, mq_ref, mk_ref, mvt_ref)
    pr_ref[...] = _dot(h, w_ref[:, 0:RWKV_COLS])
    pvt_ref[...] = _dot_nt(wvt_ref[...], h).astype(BF16)


def _in_proj(x, g, w_all, w_vt, pos_col, freq, q_norm, kv_norm, wq, wqr, wk, wv):
    t = x.shape[0]
    n_all = w_all.shape[1]
    vw = DIFF_HEADS * DIFF_V
    hw = MLA_HEADS * LANES
    return pl.pallas_call(
        _in_proj_kernel,
        grid=(t // TM,),
        in_specs=[_row_spec(TM, D_MODEL), _const_spec((1, D_MODEL)), _const_spec((D_MODEL, n_all)),
                  _const_spec((vw, D_MODEL)), _row_spec(TM, 1), _const_spec((1, LANES)), _const_spec((1, MLA_Q_LORA)),
                  _const_spec((1, MLA_KV_LORA)), _const_spec((MLA_Q_LORA, hw)), _const_spec((MLA_Q_LORA, hw)),
                  _const_spec((MLA_KV_LORA, hw)), _const_spec((hw, MLA_KV_LORA))],
        out_specs=[_row_spec(TM, RWKV_COLS), _row_spec(TM, hw), _row_spec(TM, hw),
                   pl.BlockSpec((hw, TM), lambda i: (0, i)), _row_spec(TM, DIFF_QK_COLS),
                   pl.BlockSpec((vw, TM), lambda i: (0, i)), _row_spec(TM, GATE_COLS)],
        out_shape=[jax.ShapeDtypeStruct((t, RWKV_COLS), F32), jax.ShapeDtypeStruct((t, hw), BF16),
                   jax.ShapeDtypeStruct((t, hw), BF16), jax.ShapeDtypeStruct((hw, t), BF16),
                   jax.ShapeDtypeStruct((t, DIFF_QK_COLS), BF16), jax.ShapeDtypeStruct((vw, t), BF16),
                   jax.ShapeDtypeStruct((t, GATE_COLS), BF16)],
        compiler_params=_params(),
        name="in_proj",
    )(x, g, w_all, w_vt, pos_col, freq, q_norm, kv_norm, wq, wqr, wk, wv)


def _split3(a):
    hi = a.astype(BF16)
    r1 = a - hi.astype(F32)
    mid = r1.astype(BF16)
    lo = (r1 - mid.astype(F32)).astype(BF16)
    return hi, mid, lo


def _softplus(z):
    return jnp.maximum(z, 0.0) + jnp.log(1.0 + jnp.exp(-jnp.abs(z)))


def _rwkv_kernel(p_ref, mu_ref, w0_ref, w2_ref, a0_ref, a2_ref, g2_ref, kk_ref, ka_ref, rk_ref, lnw_ref, lnb_ref,
                 o_ref, carry_ref, state_ref, *, tiles_per_seq):
    c = CHUNK
    n = RWKV_N
    ts = RWKV_TILE
    nc = ts // c

    @pl.when(pl.program_id(0) % tiles_per_seq == 0)
    def _():
        carry_ref[...] = jnp.zeros_like(carry_ref)
        state_ref[...] = jnp.zeros_like(state_ref)

    p = p_ref[...]
    row = lax.broadcasted_iota(jnp.int32, (ts, 1), 0)
    shifted = jnp.where(row == 0, carry_ref[...], pltpu.roll(p, 1, 0))
    carry_ref[...] = p[ts - 1:ts, :]
    pm = p + (shifted - p) * mu_ref[...]

    r = pm[:, 0:512]
    k = pm[:, 512:1024]
    v = pm[:, 1024:1536]
    pw = pm[:, 1536:1600]
    pa = pm[:, 1600:1664]
    pg = pm[:, 1664:1792]

    w_log = -_softplus(-(w0_ref[...] + _dot(jnp.tanh(pw).astype(BF16), w2_ref[...]))) - 0.5
    logd = -jnp.exp(w_log)
    a = jax.nn.sigmoid(a0_ref[...] + _dot(pa.astype(BF16), a2_ref[...]))
    g = _dot(jax.nn.sigmoid(pg).astype(BF16), g2_ref[...])

    tr = lax.broadcasted_iota(jnp.int32, (ts, ts), 0)
    tc = lax.broadcasted_iota(jnp.int32, (ts, ts), 1)
    tri = ((tr >= tc) & (tr // c == tc // c)).astype(BF16)
    hi, mid, lo = _split3(logd)
    cs = _dot(tri, hi) + _dot(tri, mid) + _dot(tri, lo)
    total = jnp.concatenate([jnp.broadcast_to(cs[(m + 1) * c - 1:(m + 1) * c, :], (c, RWKV_DIM)) for m in range(nc)], axis=0)
    e_in = jnp.exp(cs)
    e_ex = jnp.exp(cs - logd)
    e_inv = jnp.exp(-cs)
    e_end = jnp.exp(total - cs)
    g_end = jnp.exp(total)

    n_pairs = RWKV_HEADS // 2
    lo_t = lax.broadcasted_iota(jnp.int32, (ts, LANES), 1) < n
    lo_c = lax.broadcasted_iota(jnp.int32, (c, LANES), 1) < n

    def head_sum(x):
        blocks = []
        for q in range(n_pairs):
            xb = x[:, q * LANES:(q + 1) * LANES]
            s_lo = jnp.sum(jnp.where(lo_t, xb, 0.0), axis=-1, keepdims=True)
            s_hi = jnp.sum(jnp.where(lo_t, 0.0, xb), axis=-1, keepdims=True)
            blocks.append(jnp.where(lo_t, s_lo, s_hi))
        return jnp.concatenate(blocks, axis=-1)

    k2 = k * (1.0 + (a - 1.0) * ka_ref[...])
    kku = k * kk_ref[...]
    kkn = kku / jnp.maximum(jnp.sqrt(head_sum(kku * kku)), 1e-12)
    b = kkn * a
    full = {"at": -kkn * e_ex, "rt": r * e_in, "bt": b * e_inv, "kt": k2 * e_inv, "v": v,
            "be": b * e_end, "ke": k2 * e_end}

    ti = lax.broadcasted_iota(jnp.int32, (2 * c, 4 * c), 0) % c
    si = lax.broadcasted_iota(jnp.int32, (2 * c, 4 * c), 1) % c
    strict = ti > si
    incl = ti >= si
    eye = (lax.broadcasted_iota(jnp.int32, (2 * c, 2 * c), 0)
           == lax.broadcasted_iota(jnp.int32, (2 * c, 2 * c), 1)).astype(F32)

    items = [(m, q) for m in range(nc) for q in range(n_pairs)]

    def stacked(name, m, q):
        xb = full[name][m * c:(m + 1) * c, q * LANES:(q + 1) * LANES]
        return jnp.concatenate([jnp.where(lo_c, xb, 0.0), jnp.where(lo_c, 0.0, xb)], axis=0).astype(BF16)

    st_ops = {name: [stacked(name, m, q) for m, q in items] for name in full}
    big_l = [_dot_nt(jnp.concatenate([at, rtb], axis=0), jnp.concatenate([bt, kt], axis=0))
             for at, rtb, bt, kt in zip(st_ops["at"], st_ops["rt"], st_ops["bt"], st_ops["kt"])]
    top_l = [jnp.where(strict, big[:2 * c, :], 0.0) for big in big_l]
    bot_l = [jnp.where(incl, big[2 * c:, :], 0.0).astype(BF16) for big in big_l]
    lakv_l = [_dot(top[:, 2 * c:].astype(BF16), vb) for top, vb in zip(top_l, st_ops["v"])]

    x_l = [top[:, :2 * c] for top in top_l]
    tinv_l = [eye + x for x in x_l]
    xb_l = [x.astype(BF16) for x in x_l]
    for _ in range(int(math.log2(c)) - 1):
        x_l = [_dot(xb, xb) for xb in xb_l]
        xb_l = [x.astype(BF16) for x in x_l]
        tinv_l = [tinv + _dot(tinv.astype(BF16), xb) for tinv, xb in zip(tinv_l, xb_l)]
    tinvb_l = [tinv.astype(BF16) for tinv in tinv_l]
    abar_l = [_dot(tb, at).astype(BF16) for tb, at in zip(tinvb_l, st_ops["at"])]
    vbar_l = [_dot(tb, lv.astype(BF16)).astype(BF16) for tb, lv in zip(tinvb_l, lakv_l)]
    rhat_l = [(rtb.astype(F32) + _dot(bot[:, :2 * c], ab)).astype(BF16)
              for rtb, bot, ab in zip(st_ops["rt"], bot_l, abar_l)]
    uv_l = [jnp.concatenate([vbar, vb], axis=0) for vbar, vb in zip(vbar_l, st_ops["v"])]
    y0_l = [_dot(bot, uv) for bot, uv in zip(bot_l, uv_l)]
    p_l = [_dot_tn(ab, be).astype(BF16) for ab, be in zip(abar_l, st_ops["be"])]
    q_l = [_dot_tn(uv, jnp.concatenate([be, ke], axis=0)) for uv, be, ke in zip(uv_l, st_ops["be"], st_ops["ke"])]

    st_l = [state_ref[q] for q in range(n_pairs)]
    y_rows = []
    for m in range(nc):
        y_blocks = []
        for q in range(n_pairs):
            idx = m * n_pairs + q
            st = st_l[q]
            st_b = st.astype(BF16)
            y_s = _dot_nt(rhat_l[idx], st_b) + y0_l[idx]
            y_blocks.append(y_s[:c] + y_s[c:])
            decay = g_end[(m + 1) * c - 1:(m + 1) * c, q * LANES:(q + 1) * LANES]
            st_l[q] = st * decay + _dot(st_b, p_l[idx]) + q_l[idx]
        y_rows.append(jnp.concatenate(y_blocks, axis=-1))
    for q in range(n_pairs):
        state_ref[q] = st_l[q]
    y = jnp.concatenate(y_rows, axis=0)

    inv_n = 1.0 / n
    yc = y - head_sum(y) * inv_n
    var = head_sum(yc * yc) * inv_n
    yn = yc * lax.rsqrt(var + RWKV_GN_EPS) * lnw_ref[...] + lnb_ref[...]
    bonus = head_sum(r * k2 * rk_ref[...]) * v
    o_ref[...] = ((yn + bonus) * g).astype(BF16)


def _rwkv(p_rwkv, seq, mu, w0, w2, a0, a2, g2, k_k, k_a, r_k, ln_w, ln_b):
    t = p_rwkv.shape[0]
    ts = RWKV_TILE
    kern = functools.partial(_rwkv_kernel, tiles_per_seq=seq // ts)
    vec = lambda: _const_spec((1, RWKV_DIM))
    return pl.pallas_call(
        kern,
        grid=(t // ts,),
        in_specs=[_row_spec(ts, RWKV_COLS), _const_spec((1, RWKV_COLS)), vec(), _const_spec((64, RWKV_DIM)), vec(),
                  _const_spec((64, RWKV_DIM)), _const_spec((128, RWKV_DIM)), vec(), vec(), vec(), vec(), vec()],
        out_specs=_row_spec(ts, RWKV_DIM),
        out_shape=jax.ShapeDtypeStruct((t, RWKV_DIM), BF16),
        scratch_shapes=[pltpu.VMEM((1, RWKV_COLS), F32), pltpu.VMEM((RWKV_HEADS // 2, LANES, LANES), F32)],
        compiler_params=_params(),
        name="rwkv7",
    )(p_rwkv, mu, w0, w2, a0, a2, g2, k_k, k_a, r_k, ln_w, ln_b)


def _mla_prep_body(p, pos_ref, freq_ref, qn_ref, kvn_ref, wq_ref, wqr_ref, wk_ref, wv_ref, q_out, k_out, v_out):
    hq = _rms(p[:, 0:MLA_Q_LORA], qn_ref[...], NORM_EPS).astype(BF16)
    hkv = _rms(p[:, MLA_Q_LORA:MLA_Q_LORA + MLA_KV_LORA], kvn_ref[...], NORM_EPS).astype(BF16)
    blk = p[:, 384:512]
    ang = pos_ref[...].astype(F32) * freq_ref[...]
    cos = jnp.cos(ang)
    sin = jnp.sin(ang)
    scale = (MLA_NOPE + MLA_ROPE) ** -0.5 * LOG2E
    qa = _dot(hq, wq_ref[...])
    qr = _dot(hq, wqr_ref[...])
    lane = lax.broadcasted_iota(jnp.int32, blk.shape, 1)
    kr = pltpu.roll(blk, 64, 1)
    rot = jnp.where(lane < 80, -pltpu.roll(blk, 48, 1), pltpu.roll(blk, 80, 1))
    rot = jnp.where((lane >= 64) & (lane < 96), rot, 0.0)
    krope = kr * cos + rot * sin
    kn = _dot(hkv, wk_ref[...])
    for h in range(MLA_HEADS):
        hs = slice(h * LANES, (h + 1) * LANES)
        q_out[:, hs] = ((qa[:, hs] * cos + qr[:, hs] * sin) * scale).astype(BF16)
        k_out[:, hs] = (kn[:, hs] + krope).astype(BF16)
    vt = _dot_nt(wv_ref[...], hkv)
    rowi = lax.broadcasted_iota(jnp.int32, vt.shape, 0)
    v_out[...] = (vt + (rowi % LANES == MLA_V).astype(F32)).astype(BF16)


def _softmax_stage_t(s_l, pv_fn, m_scr, acc_scr, idx, s_scr=None):
    n = len(s_l)
    m_old = [m_scr[c][0:1, :] for c in idx]
    m_new = [jnp.maximum(m, jnp.max(s, axis=0, keepdims=True)) for m, s in zip(m_old, s_l)]
    alpha = [jnp.exp2(mo - mn) for mo, mn in zip(m_old, m_new)]
    if s_scr is not None:
        for c in range(n):
            s_scr[c] = s_l[c]
        s_l = [s_scr[c] for c in range(n)]
    pv_l = [pv_fn(c, jnp.exp2(s_l[c] - m_new[c]).astype(BF16)) for c in range(n)]
    for c in range(n):
        m_scr[idx[c]] = jnp.broadcast_to(m_new[c], m_scr.shape[1:])
        acc_scr[idx[c]] = alpha[c] * acc_scr[idx[c]] + pv_l[c]


def _mla_attn_kernel(q_ref, k_ref, vt_ref, o_ref, m_scr, acc_scr, s_scr):
    tq = TQ
    i = pl.program_id(1)
    causal = lax.broadcasted_iota(jnp.int32, (tq, tq), 0) <= lax.broadcasted_iota(jnp.int32, (tq, tq), 1)
    lane = lax.broadcasted_iota(jnp.int32, (tq, LANES), 1)
    m_scr[...] = jnp.full(m_scr.shape, NEG, F32)
    acc_scr[...] = jnp.zeros(acc_scr.shape, F32)
    hsl = [slice(h * LANES, (h + 1) * LANES) for h in range(MLA_HEADS)]
    heads = list(range(MLA_HEADS))

    def tile(off, tk, mask):
        s_l = [_dot_nt(k_ref[pl.ds(off, tk), hs], q_ref[:, hs]) for hs in hsl]
        if mask:
            s_l = [jnp.where(causal, s, NEG) for s in s_l]
        _softmax_stage_t(s_l, lambda c, p: _dot(vt_ref[hsl[c], pl.ds(off, tk)], p), m_scr, acc_scr, heads,
                         s_scr.at[:, 0:tk, :])

    def pair_body(j, carry):
        tile(pl.multiple_of(j * 2 * tq, 2 * tq), 2 * tq, False)
        return carry

    def single_body(j, carry):
        tile(pl.multiple_of(j * tq, tq), tq, False)
        return carry

    n_pairs = i // 2 if KEY_TILES_PER_STEP == 2 else 0
    if KEY_TILES_PER_STEP == 2:
        lax.fori_loop(0, n_pairs, pair_body, 0)
    lax.fori_loop(2 * n_pairs, i, single_body, 0)
    tile(pl.multiple_of(i * tq, tq), tq, True)
    for pr in range(MLA_HEADS // 2):
        outs = []
        for hh in range(2):
            acc = acc_scr[2 * pr + hh]
            outs.append(jnp.transpose(acc / acc[MLA_V:MLA_V + 1, :]))
        o_ref[:, pr * LANES:(pr + 1) * LANES] = jnp.where(lane < MLA_V, outs[0], pltpu.roll(outs[1], MLA_V, 1)).astype(BF16)


def _mla_attn(q, k, vt, batch, seq):
    t = q.shape[0]
    nq = seq // TQ
    hw = MLA_HEADS * LANES
    vw = MLA_HEADS * MLA_V
    return pl.pallas_call(
        _mla_attn_kernel,
        grid=(batch, nq),
        in_specs=[pl.BlockSpec((TQ, hw), lambda b, i: (b * nq + i, 0)),
                  pl.BlockSpec((seq, hw), lambda b, i: (b, 0)),
                  pl.BlockSpec((hw, seq), lambda b, i: (0, b))],
        out_specs=pl.BlockSpec((TQ, vw), lambda b, i: (b * nq + i, 0)),
        out_shape=jax.ShapeDtypeStruct((t, vw), BF16),
        scratch_shapes=[pltpu.VMEM((MLA_HEADS, 8, TQ), F32), pltpu.VMEM((MLA_HEADS, LANES, TQ), F32),
                        pltpu.VMEM((MLA_HEADS, KEY_TILES_PER_STEP * TQ, TQ), F32)],
        compiler_params=_params2(),
        name="mla_attn",
    )(q, k, vt)


def _t5_bucket_table():
    n = np.arange(0, REL_MAX_DISTANCE + 1)
    max_exact = REL_BUCKETS // 2
    nf = np.maximum(n, 1).astype(np.float32)
    ratio = np.log(nf / np.float32(max_exact)) / np.float32(math.log(REL_MAX_DISTANCE / max_exact))
    large = max_exact + (ratio * np.float32(REL_BUCKETS - max_exact)).astype(np.int32)
    large = np.minimum(large, REL_BUCKETS - 1)
    return np.where(n < max_exact, n, large)


_BUCKETS = _t5_bucket_table()
_FAR_BUCKET = int(_BUCKETS[REL_MAX_DISTANCE])
assert _FAR_BUCKET == REL_BUCKETS - 1 and np.all(np.diff(_BUCKETS) >= 0)
_BUCKET_STARTS = [int(np.argmax(_BUCKETS >= b)) for b in range(REL_BUCKETS // 2 + 1, REL_BUCKETS)]


def _diff_attn_kernel(rb_ref, lam_ref, subln_ref, q_ref, k_ref, vt_ref, o_ref, bias_scr, m_scr, acc_scr, s_scr, *,
                      lambda_init):
    tq = TQ
    i = pl.program_id(1)
    r_i = lax.broadcasted_iota(jnp.int32, (tq, tq), 0)
    c_i = lax.broadcasted_iota(jnp.int32, (tq, tq), 1)
    causal = r_i <= c_i

    @pl.when((pl.program_id(0) == 0) & (i == 0))
    def _():
        for t_idx, delta in enumerate((0, tq)):
            d = jnp.maximum(c_i - r_i + delta, 0)
            log_b = REL_BUCKETS // 2
            for start in _BUCKET_STARTS:
                log_b = log_b + (d >= start).astype(jnp.int32)
            bucket = jnp.where(d < REL_BUCKETS // 2, d, log_b)
            for h in range(DIFF_HEADS):
                bias = jnp.zeros((tq, tq), F32)
                for b in range(REL_BUCKETS):
                    bias = jnp.where(bucket == b, rb_ref[b, h], bias)
                bias_scr[t_idx, h] = (bias - rb_ref[_FAR_BUCKET, h]) * LOG2E

    m_scr[...] = jnp.full(m_scr.shape, NEG, F32)
    acc_scr[...] = jnp.zeros(acc_scr.shape, F32)
    lane = lax.broadcasted_iota(jnp.int32, (tq, LANES), 1)
    ones_rows = {tk: (lax.broadcasted_iota(jnp.int32, (DIFF_VT_ROWS - DIFF_V, tk), 0) == 0).astype(BF16)
                 for tk in (tq, 2 * tq)}
    hsl = [slice(h * LANES, (h + 1) * LANES) for h in range(DIFF_HEADS)]
    heads = list(range(DIFF_HEADS))
    q_l = []
    for hs in hsl:
        qp = q_ref[:, hs]
        q_l.append(jnp.concatenate([jnp.where(lane < DIFF_QK, qp, jnp.zeros_like(qp)),
                                    jnp.where(lane >= DIFF_QK, qp, jnp.zeros_like(qp))], axis=0))

    def tile(off, tk, bias_idx, mask):
        s_l = [_dot_nt(k_ref[pl.ds(off, tk), hsl[h]], q_l[h]) for h in heads]
        if bias_idx is not None:
            s_l = [s + jnp.concatenate([bias_scr[bias_idx, h]] * 2, axis=1) for h, s in enumerate(s_l)]
        if mask:
            mask2 = jnp.concatenate([causal, causal], axis=1)
            s_l = [jnp.where(mask2, s, NEG) for s in s_l]

        def pv(c, p):
            vt = jnp.concatenate([vt_ref[hsl[c], pl.ds(off, tk)], ones_rows[tk]], axis=0)
            return jnp.concatenate([_dot(vt, p[:, :tq]), _dot(vt, p[:, tq:])], axis=1)

        _softmax_stage_t(s_l, pv, m_scr, acc_scr, heads, s_scr.at[:, 0:tk, :])

    def far_pair_body(j, carry):
        tile(pl.multiple_of(j * 2 * tq, 2 * tq), 2 * tq, None, False)
        return carry

    def far_body(j, carry):
        tile(pl.multiple_of(j * tq, tq), tq, None, False)
        return carry

    def near_body(j, carry):
        tile(pl.multiple_of(j * tq, tq), tq, 1, False)
        return carry

    n_far = jnp.maximum(i - 1, 0)
    n_pairs = n_far // 2 if KEY_TILES_PER_STEP == 2 else 0
    if KEY_TILES_PER_STEP == 2:
        lax.fori_loop(0, n_pairs, far_pair_body, 0)
    lax.fori_loop(2 * n_pairs, n_far, far_body, 0)
    lax.fori_loop(n_far, i, near_body, 0)
    tile(pl.multiple_of(i * tq, tq), tq, 0, True)

    lam = lam_ref[...]
    lam_full = (jnp.exp(jnp.sum(lam[0:1] * lam[1:2], axis=-1, keepdims=True))
                - jnp.exp(jnp.sum(lam[2:3] * lam[3:4], axis=-1, keepdims=True)) + lambda_init)
    for h in heads:
        acc = acc_scr[h]
        on = acc[:DIFF_V, :] / acc[DIFF_V:DIFF_V + 1, :]
        ot = on[:, :tq] - lam_full * on[:, tq:]
        ot = ot * lax.rsqrt(jnp.mean(ot * ot, axis=0, keepdims=True) + SUBLN_EPS)
        o_ref[:, hsl[h]] = (jnp.transpose(ot) * subln_ref[...] * (1.0 - lambda_init)).astype(BF16)


def _diff_attn(p_qk, p_vt, rel_bias, lam, subln, batch, seq, layer_idx):
    t = p_qk.shape[0]
    nq = seq // TQ
    w = DIFF_HEADS * LANES
    lambda_init = 0.8 - 0.6 * math.exp(-0.3 * layer_idx)
    kern = functools.partial(_diff_attn_kernel, lambda_init=lambda_init)
    const2 = lambda shape: pl.BlockSpec(shape, lambda b, i: (0, 0))
    return pl.pallas_call(
        kern,
        grid=(batch, nq),
        in_specs=[pl.BlockSpec(memory_space=pltpu.SMEM),
                  const2((4, DIFF_QK)), const2((1, DIFF_V)),
                  pl.BlockSpec((TQ, w), lambda b, i: (b * nq + i, 0)),
                  pl.BlockSpec((seq, w), lambda b, i: (b, 1)),
                  pl.BlockSpec((w, seq), lambda b, i: (0, b))],
        out_specs=pl.BlockSpec((TQ, w), lambda b, i: (b * nq + i, 0)),
        out_shape=jax.ShapeDtypeStruct((t, w), BF16),
        scratch_shapes=[pltpu.VMEM((2, DIFF_HEADS, TQ, TQ), F32), pltpu.VMEM((DIFF_HEADS, 8, 2 * TQ), F32),
                        pltpu.VMEM((DIFF_HEADS, DIFF_VT_ROWS, 2 * TQ), F32),
                        pltpu.VMEM((DIFF_HEADS, KEY_TILES_PER_STEP * TQ, 2 * TQ), F32)],
        compiler_params=_params2(),
        name="diff_attn",
    )(rel_bias, lam, subln, p_qk, p_qk, p_vt)


def _merge_kernel(x_ref, pg_ref, bg_ref, or_ref, om_ref, od_ref, wr_ref, wm_ref, wd_ref, wo_ref, gf_ref, x_out, h_out):
    halves = [slice(s * (TM_MERGE // 2), (s + 1) * (TM_MERGE // 2)) for s in range(2)]
    branch = [[_dot(o_ref[rs, :], w_ref[...]) for o_ref, w_ref in ((or_ref, wr_ref), (om_ref, wm_ref), (od_ref, wd_ref))]
              for rs in halves]
    for rs, d in zip(halves, branch):
        merged = None
        for idx in range(3):
            cs = slice(idx * D_MODEL, (idx + 1) * D_MODEL)
            term = jax.nn.sigmoid(pg_ref[rs, cs].astype(F32) + bg_ref[:, cs]) * d[idx]
            merged = term if merged is None else merged + term
        x1 = x_ref[rs, :] + _dot(merged.astype(BF16), wo_ref[...])
        x_out[rs, :] = x1
        h_out[rs, :] = _rms(x1, gf_ref[...], NORM_EPS).astype(BF16)


def _merge(x, p_gate, b_gate, o_r, o_m, o_d, w_r, w_m, w_d, w_o, g_ffn):
    t = x.shape[0]
    tm = TM_MERGE
    return pl.pallas_call(
        _merge_kernel,
        grid=(t // tm,),
        in_specs=[_row_spec(tm, D_MODEL), _row_spec(tm, GATE_COLS), _const_spec((1, GATE_COLS)),
                  _row_spec(tm, 512), _row_spec(tm, 512), _row_spec(tm, 512),
                  _const_spec((512, D_MODEL)), _const_spec((512, D_MODEL)), _const_spec((512, D_MODEL)),
                  _const_spec((D_MODEL, D_MODEL)), _const_spec((1, D_MODEL))],
        out_specs=[_row_spec(tm, D_MODEL), _row_spec(tm, D_MODEL)],
        out_shape=[jax.ShapeDtypeStruct((t, D_MODEL), F32), jax.ShapeDtypeStruct((t, D_MODEL), BF16)],
        compiler_params=_params(),
        name="merge",
    )(x, p_gate, b_gate, o_r, o_m, o_d, w_r, w_m, w_d, w_o, g_ffn)


def _ffn_kernel(x_ref, h_ref, wup_ref, cw_ref, cb_ref, wdn_ref, gfin_ref, o_ref, carry_ref, *, tiles_per_seq, final_norm):
    tm = TM_FFN

    @pl.when(pl.program_id(0) % tiles_per_seq == 0)
    def _():
        carry_ref[0:8, :] = jnp.zeros((8, carry_ref.shape[1]), F32)

    h = h_ref[...]

    def conv(u, cols):
        carry_ref[8:, cols] = u
        u1 = carry_ref[7:7 + tm, cols]
        u2 = carry_ref[6:6 + tm, cols]
        carry_ref[0:8, cols] = u[tm - 8:tm, :]
        return cw_ref[0:1, cols] * u2 + cw_ref[1:2, cols] * u1 + cw_ref[2:3, cols] * u + cb_ref[:, cols]

    def up(ck):
        gc = slice(ck * FF_CHUNK, (ck + 1) * FF_CHUNK)
        vc = slice(D_FF + ck * FF_CHUNK, D_FF + (ck + 1) * FF_CHUNK)
        return _dot(h, wup_ref[:, gc]), _dot(h, wup_ref[:, vc])

    n_chunks = D_FF // FF_CHUNK
    acc = x_ref[...]
    nxt = up(0)
    acts = []
    for ck in range(n_chunks):
        ug, uv = nxt
        if ck + 1 < n_chunks:
            nxt = up(ck + 1)
        gc = slice(ck * FF_CHUNK, (ck + 1) * FF_CHUNK)
        vc = slice(D_FF + ck * FF_CHUNK, D_FF + (ck + 1) * FF_CHUNK)
        gate = conv(ug, gc)
        val = conv(uv, vc)
        acts.append((gate * jax.nn.sigmoid(gate) * val).astype(BF16))
        if len(acts) == FF_DOWN_GROUP or ck + 1 == n_chunks:
            rows = slice((ck + 1 - len(acts)) * FF_CHUNK, (ck + 1) * FF_CHUNK)
            acc = acc + _dot(jnp.concatenate(acts, axis=-1), wdn_ref[rows, :])
            acts = []
    if final_norm:
        acc = _rms(acc, gfin_ref[...], NORM_EPS)
    o_ref[...] = acc


def _ffn(x1, h2, w_up, conv_w, conv_b, w_down, g_final, seq, final_norm):
    t = x1.shape[0]
    tm = TM_FFN
    kern = functools.partial(_ffn_kernel, tiles_per_seq=seq // tm, final_norm=final_norm)
    return pl.pallas_call(
        kern,
        grid=(t // tm,),
        in_specs=[_row_spec(tm, D_MODEL), _row_spec(tm, D_MODEL), _const_spec((D_MODEL, 2 * D_FF)),
                  _const_spec((3, 2 * D_FF)), _const_spec((1, 2 * D_FF)), _const_spec((D_FF, D_MODEL)),
                  _const_spec((1, D_MODEL))],
        out_specs=_row_spec(tm, D_MODEL),
        out_shape=jax.ShapeDtypeStruct((t, D_MODEL), F32),
        scratch_shapes=[pltpu.VMEM((8 + tm, 2 * D_FF), F32)],
        compiler_params=_params(),
        name="conv_ffn",
    )(x1, h2, w_up, conv_w, conv_b, w_down, g_final)


def _mla_weights(w_uq, w_ukv):
    qd = MLA_NOPE + MLA_ROPE
    half = MLA_ROPE // 2
    wq = w_uq.reshape(MLA_Q_LORA, MLA_HEADS, qd)
    zq = jnp.zeros((MLA_Q_LORA, MLA_HEADS, LANES - qd), F32)
    wq_main = jnp.concatenate([wq, zq], axis=-1)
    x1 = wq[:, :, MLA_NOPE:MLA_NOPE + half]
    x2 = wq[:, :, MLA_NOPE + half:]
    wq_rot = jnp.concatenate([jnp.zeros((MLA_Q_LORA, MLA_HEADS, MLA_NOPE), F32), -x2, x1, zq], axis=-1)
    wkv = w_ukv.reshape(MLA_KV_LORA, MLA_HEADS, MLA_NOPE + MLA_V)
    zkv = jnp.zeros((MLA_KV_LORA, MLA_HEADS, LANES - MLA_NOPE), F32)
    wk = jnp.concatenate([wkv[:, :, :MLA_NOPE], zkv], axis=-1)
    wv = jnp.concatenate([wkv[:, :, MLA_NOPE:], zkv], axis=-1)
    flat = lambda w: w.reshape(w.shape[0], -1).astype(BF16)
    return flat(wq_main), flat(wq_rot), flat(wk), flat(wv).T


def kernel(x, positions, rel_bias, norm_mix, w_in, b_gate, rwkv_mu, rwkv_w0, rwkv_w2, rwkv_a0, rwkv_a2, rwkv_g2, rwkv_k_k, rwkv_k_a, rwkv_r_k, rwkv_ln_w, rwkv_ln_b, mla_q_norm, mla_w_uq, mla_kv_norm, mla_w_ukv, diff_lambda, diff_subln, w_branch_rwkv, w_branch_mla, w_branch_diff, w_o, norm_ffn, ffn_w_up, ffn_conv_w, ffn_conv_b, ffn_w_down, norm_final):
    batch, seq, _ = x.shape
    depth = w_in.shape[0]
    t = batch * seq
    assert seq % TQ == 0 and seq % TM == 0 and seq % TM_FFN == 0 and seq % RWKV_TILE == 0 and RWKV_TILE % CHUNK == 0
    assert (batch * seq) % TM_MERGE == 0
    xf = x.reshape(t, D_MODEL)
    pos_col = positions.reshape(t, 1)
    inv_freq = ROPE_BASE ** (-jnp.arange(0, MLA_ROPE, 2, dtype=F32) / MLA_ROPE)
    freq = jnp.concatenate([jnp.zeros((MLA_NOPE,), F32), inv_freq, inv_freq,
                            jnp.zeros((LANES - MLA_NOPE - MLA_ROPE,), F32)]).reshape(1, LANES)
    row = lambda v: v.reshape(1, -1)
    diff_scale = jnp.concatenate([jnp.full((512,), DIFF_QK ** -0.5 * LOG2E, F32), jnp.ones((512,), F32)])

    for l in range(depth):
        s0, s1, s2 = RWKV_COLS, RWKV_COLS + MLA_COLS, RWKV_COLS + MLA_COLS + DIFF_COLS
        sv = s1 + DIFF_QK_COLS
        w = w_in[l]
        w_all = jnp.concatenate([w[:, :s0], w[:, s0:s1], jnp.zeros((D_MODEL, MLA_PAD - MLA_COLS), F32),
                                 w[:, s1:sv] * diff_scale, w[:, s2:]], axis=1).astype(BF16)
        w_vt = w[:, sv:s2].T.astype(BF16)
        wq, wqr, wk, wv = _mla_weights(mla_w_uq[l], mla_w_ukv[l])
        p_rwkv, q_m, k_m, v_m, p_qk, p_vt, p_gate = _in_proj(
            xf, row(norm_mix[l]), w_all, w_vt, pos_col, freq, row(mla_q_norm[l]), row(mla_kv_norm[l]), wq, wqr, wk, wv)

        o_r = _rwkv(p_rwkv, seq, row(rwkv_mu[l]), row(rwkv_w0[l]), rwkv_w2[l].astype(BF16), row(rwkv_a0[l]),
                    rwkv_a2[l].astype(BF16), rwkv_g2[l].astype(BF16), row(rwkv_k_k[l]), row(rwkv_k_a[l]),
                    row(rwkv_r_k[l]), row(rwkv_ln_w[l]), row(rwkv_ln_b[l]))

        o_m = _mla_attn(q_m, k_m, v_m, batch, seq)

        o_d = _diff_attn(p_qk, p_vt, rel_bias, diff_lambda[l], row(diff_subln[l]), batch, seq, l)

        x1, h2 = _merge(xf, p_gate, row(b_gate[l]), o_r, o_m, o_d, w_branch_rwkv[l].astype(BF16),
                        w_branch_mla[l].astype(BF16), w_branch_diff[l].astype(BF16), w_o[l].astype(BF16),
                        row(norm_ffn[l]))
        xf = _ffn(x1, h2, ffn_w_up[l].astype(BF16), ffn_conv_w[l], row(ffn_conv_b[l]), ffn_w_down[l].astype(BF16),
                  row(norm_final), seq, final_norm=(l == depth - 1))
    return xf.reshape(batch, seq, D_MODEL)
```

```python
import functools
import math

import numpy as np
import jax
import jax.numpy as jnp
from jax import lax
from jax.experimental import pallas as pl
from jax.experimental.pallas import tpu as pltpu

F32 = jnp.float32
BF16 = jnp.bfloat16

D_MODEL = 1024
RWKV_HEADS = 8
RWKV_N = 64
RWKV_DIM = 512
RWKV_COLS = 1792
RWKV_GN_EPS = 64e-5
MLA_HEADS = 8
MLA_Q_LORA = 256
MLA_KV_LORA = 128
MLA_NOPE = 64
MLA_ROPE = 32
MLA_V = 64
MLA_COLS = 416
MLA_PAD = 512
ROPE_BASE = 10000.0
DIFF_HEADS = 4
DIFF_QK = 64
DIFF_V = 128
DIFF_COLS = 1536
DIFF_QK_COLS = 1024
DIFF_VT_ROWS = DIFF_V + 16
REL_BUCKETS = 32
REL_MAX_DISTANCE = 128
D_FF = 2816
GATE_COLS = 3072
NORM_EPS = 1e-6
SUBLN_EPS = 1e-5

LANES = 128
VMEM_LIMIT = 58 * 1024 * 1024
TM = 512
TM_FFN = 256
TM_MERGE = 512
CHUNK = 64
RWKV_TILE = 256
TQ = 512
KEY_TILES_PER_STEP = 1
FF_CHUNK = 256
FF_DOWN_GROUP = 11
NEG = -1e30
LOG2E = 1.4426950408889634

_NT = (((1,), (1,)), ((), ()))
_TN = (((0,), (0,)), ((), ()))


def _dot(a, b):
    return jnp.dot(a, b, preferred_element_type=F32)


def _dot_nt(a, b):
    return lax.dot_general(a, b, _NT, preferred_element_type=F32)


def _dot_tn(a, b):
    return lax.dot_general(a, b, _TN, preferred_element_type=F32)


def _rms(x, g, eps):
    return x * lax.rsqrt(jnp.mean(x * x, axis=-1, keepdims=True) + eps) * g


def _params():
    return pltpu.CompilerParams(dimension_semantics=("arbitrary",), vmem_limit_bytes=VMEM_LIMIT)


def _params2():
    return pltpu.CompilerParams(dimension_semantics=("arbitrary", "arbitrary"), vmem_limit_bytes=VMEM_LIMIT)


def _const_spec(shape):
    return pl.BlockSpec(shape, lambda *_: (0,) * len(shape), pipeline_mode=pl.Buffered(1))


def _row_spec(tm, cols, col_block=0):
    return pl.BlockSpec((tm, cols), lambda i: (i, col_block))


def _in_proj_kernel(x_ref, g_ref, wr_ref, wm_ref, wqk_ref, wg_ref, wvt_ref, pos_ref, freq_ref, qn_ref, kvn_ref, wq_ref,
                    wqr_ref, wk_ref, wv_ref, pr_ref, mq_ref, mk_ref, mvt_ref, pqk_ref, pvt_ref, pg_ref):
    h = _rms(x_ref[...], g_ref[...], NORM_EPS).astype(BF16)
    p_mla = _dot(h, wm_ref[...])
    pqk_ref[...] = _dot(h, wqk_ref[...]).astype(BF16)
    pg_ref[...] = _dot(h, wg_ref[...]).astype(BF16)
    _mla_prep_body(p_mla, pos_ref, freq_ref, qn_ref, kvn_ref, wq_ref, wqr_ref, wk_ref, wv_ref, mq_ref, mk_ref, mvt_ref)
    pr_ref[...] = _dot(h, wr_ref[...])
    pvt_ref[...] = _dot_nt(wvt_ref[...], h).astype(BF16)


def _in_proj(x, g, w_r, w_m, w_qk, w_g, w_vt, pos_col, freq, q_norm, kv_norm, wq, wqr, wk, wv):
    t = x.shape[0]
    vw = DIFF_HEADS * DIFF_V
    hw = MLA_HEADS * LANES
    return pl.pallas_call(
        _in_proj_kernel,
        grid=(t // TM,),
        in_specs=[_row_spec(TM, D_MODEL), _const_spec((1, D_MODEL)), _const_spec((D_MODEL, RWKV_COLS)),
                  _const_spec((D_MODEL, MLA_PAD)), _const_spec((D_MODEL, DIFF_QK_COLS)), _const_spec((D_MODEL, GATE_COLS)),
                  _const_spec((vw, D_MODEL)), _row_spec(TM, 1), _const_spec((1, LANES)), _const_spec((1, MLA_Q_LORA)),
                  _const_spec((1, MLA_KV_LORA)), _const_spec((MLA_Q_LORA, hw)), _const_spec((MLA_Q_LORA, hw)),
                  _const_spec((MLA_KV_LORA, hw)), _const_spec((hw, MLA_KV_LORA))],
        out_specs=[_row_spec(TM, RWKV_COLS), _row_spec(TM, hw), _row_spec(TM, hw),
                   pl.BlockSpec((hw, TM), lambda i: (0, i)), _row_spec(TM, DIFF_QK_COLS),
                   pl.BlockSpec((vw, TM), lambda i: (0, i)), _row_spec(TM, GATE_COLS)],
        out_shape=[jax.ShapeDtypeStruct((t, RWKV_COLS), F32), jax.ShapeDtypeStruct((t, hw), BF16),
                   jax.ShapeDtypeStruct((t, hw), BF16), jax.ShapeDtypeStruct((hw, t), BF16),
                   jax.ShapeDtypeStruct((t, DIFF_QK_COLS), BF16), jax.ShapeDtypeStruct((vw, t), BF16),
                   jax.ShapeDtypeStruct((t, GATE_COLS), BF16)],
        compiler_params=_params(),
        name="in_proj",
    )(x, g, w_r, w_m, w_qk, w_g, w_vt, pos_col, freq, q_norm, kv_norm, wq, wqr, wk, wv)


def _split3(a):
    hi = a.astype(BF16)
    r1 = a - hi.astype(F32)
    mid = r1.astype(BF16)
    lo = (r1 - mid.astype(F32)).astype(BF16)
    return hi, mid, lo


def _softplus(z):
    return jnp.maximum(z, 0.0) + jnp.log(1.0 + jnp.exp(-jnp.abs(z)))


def _rwkv_kernel(p_ref, mu_ref, w0_ref, w2_ref, a0_ref, a2_ref, g2_ref, kk_ref, ka_ref, rk_ref, lnw_ref, lnb_ref,
                 o_ref, carry_ref, state_ref, *, tiles_per_seq):
    c = CHUNK
    n = RWKV_N
    ts = RWKV_TILE
    nc = ts // c

    @pl.when(pl.program_id(0) % tiles_per_seq == 0)
    def _():
        carry_ref[...] = jnp.zeros_like(carry_ref)
        state_ref[...] = jnp.zeros_like(state_ref)

    p = p_ref[...]
    row = lax.broadcasted_iota(jnp.int32, (ts, 1), 0)
    shifted = jnp.where(row == 0, carry_ref[...], pltpu.roll(p, 1, 0))
    carry_ref[...] = p[ts - 1:ts, :]
    pm = p + (shifted - p) * mu_ref[...]

    r = pm[:, 0:512]
    k = pm[:, 512:1024]
    v = pm[:, 1024:1536]
    pw = pm[:, 1536:1600]
    pa = pm[:, 1600:1664]
    pg = pm[:, 1664:1792]

    w_log = -_softplus(-(w0_ref[...] + _dot(jnp.tanh(pw).astype(BF16), w2_ref[...]))) - 0.5
    logd = -jnp.exp(w_log)
    a = jax.nn.sigmoid(a0_ref[...] + _dot(pa.astype(BF16), a2_ref[...]))
    g = _dot(jax.nn.sigmoid(pg).astype(BF16), g2_ref[...])

    tr = lax.broadcasted_iota(jnp.int32, (ts, ts), 0)
    tc = lax.broadcasted_iota(jnp.int32, (ts, ts), 1)
    tri = ((tr >= tc) & (tr // c == tc // c)).astype(BF16)
    hi, mid, lo = _split3(logd)
    cs = _dot(tri, hi) + _dot(tri, mid) + _dot(tri, lo)
    total = jnp.concatenate([jnp.broadcast_to(cs[(m + 1) * c - 1:(m + 1) * c, :], (c, RWKV_DIM)) for m in range(nc)], axis=0)
    e_in = jnp.exp(cs)
    e_ex = jnp.exp(cs - logd)
    e_inv = jnp.exp(-cs)
    e_end = jnp.exp(total - cs)
    g_end = jnp.exp(total)

    n_pairs = RWKV_HEADS // 2
    lo_t = lax.broadcasted_iota(jnp.int32, (ts, LANES), 1) < n
    lo_c = lax.broadcasted_iota(jnp.int32, (c, LANES), 1) < n

    def head_sum(x):
        blocks = []
        for q in range(n_pairs):
            xb = x[:, q * LANES:(q + 1) * LANES]
            s_lo = jnp.sum(jnp.where(lo_t, xb, 0.0), axis=-1, keepdims=True)
            s_hi = jnp.sum(jnp.where(lo_t, 0.0, xb), axis=-1, keepdims=True)
            blocks.append(jnp.where(lo_t, s_lo, s_hi))
        return jnp.concatenate(blocks, axis=-1)

    k2 = k * (1.0 + (a - 1.0) * ka_ref[...])
    kku = k * kk_ref[...]
    kkn = kku / jnp.maximum(jnp.sqrt(head_sum(kku * kku)), 1e-12)
    b = kkn * a
    full = {"at": -kkn * e_ex, "rt": r * e_in, "bt": b * e_inv, "kt": k2 * e_inv, "v": v,
            "be": b * e_end, "ke": k2 * e_end}

    ti = lax.broadcasted_iota(jnp.int32, (2 * c, 4 * c), 0) % c
    si = lax.broadcasted_iota(jnp.int32, (2 * c, 4 * c), 1) % c
    strict = ti > si
    incl = ti >= si
    eye = (lax.broadcasted_iota(jnp.int32, (2 * c, 2 * c), 0)
           == lax.broadcasted_iota(jnp.int32, (2 * c, 2 * c), 1)).astype(F32)

    items = [(m, q) for m in range(nc) for q in range(n_pairs)]

    def stacked(name, m, q):
        xb = full[name][m * c:(m + 1) * c, q * LANES:(q + 1) * LANES]
        return jnp.concatenate([jnp.where(lo_c, xb, 0.0), jnp.where(lo_c, 0.0, xb)], axis=0).astype(BF16)

    st_ops = {name: [stacked(name, m, q) for m, q in items] for name in full}
    big_l = [_dot_nt(jnp.concatenate([at, rtb], axis=0), jnp.concatenate([bt, kt], axis=0))
             for at, rtb, bt, kt in zip(st_ops["at"], st_ops["rt"], st_ops["bt"], st_ops["kt"])]
    top_l = [jnp.where(strict, big[:2 * c, :], 0.0) for big in big_l]
    bot_l = [jnp.where(incl, big[2 * c:, :], 0.0).astype(BF16) for big in big_l]
    lakv_l = [_dot(top[:, 2 * c:].astype(BF16), vb) for top, vb in zip(top_l, st_ops["v"])]

    x_l = [top[:, :2 * c] for top in top_l]
    tinv_l = [eye + x for x in x_l]
    xb_l = [x.astype(BF16) for x in x_l]
    for _ in range(int(math.log2(c)) - 1):
        x_l = [_dot(xb, xb) for xb in xb_l]
        xb_l = [x.astype(BF16) for x in x_l]
        tinv_l = [tinv + _dot(tinv.astype(BF16), xb) for tinv, xb in zip(tinv_l, xb_l)]
    tinvb_l = [tinv.astype(BF16) for tinv in tinv_l]
    abar_l = [_dot(tb, at).astype(BF16) for tb, at in zip(tinvb_l, st_ops["at"])]
    vbar_l = [_dot(tb, lv.astype(BF16)).astype(BF16) for tb, lv in zip(tinvb_l, lakv_l)]
    rhat_l = [(rtb.astype(F32) + _dot(bot[:, :2 * c], ab)).astype(BF16)
              for rtb, bot, ab in zip(st_ops["rt"], bot_l, abar_l)]
    uv_l = [jnp.concatenate([vbar, vb], axis=0) for vbar, vb in zip(vbar_l, st_ops["v"])]
    y0_l = [_dot(bot, uv) for bot, uv in zip(bot_l, uv_l)]
    p_l = [_dot_tn(ab, be).astype(BF16) for ab, be in zip(abar_l, st_ops["be"])]
    q_l = [_dot_tn(uv, jnp.concatenate([be, ke], axis=0)) for uv, be, ke in zip(uv_l, st_ops["be"], st_ops["ke"])]

    st_l = [state_ref[q] for q in range(n_pairs)]
    y_rows = []
    for m in range(nc):
        y_blocks = []
        for q in range(n_pairs):
            idx = m * n_pairs + q
            st = st_l[q]
            st_b = st.astype(BF16)
            y_s = _dot_nt(rhat_l[idx], st_b) + y0_l[idx]
            y_blocks.append(y_s[:c] + y_s[c:])
            decay = g_end[(m + 1) * c - 1:(m + 1) * c, q * LANES:(q + 1) * LANES]
            st_l[q] = st * decay + _dot(st_b, p_l[idx]) + q_l[idx]
        y_rows.append(jnp.concatenate(y_blocks, axis=-1))
    for q in range(n_pairs):
        state_ref[q] = st_l[q]
    y = jnp.concatenate(y_rows, axis=0)

    inv_n = 1.0 / n
    yc = y - head_sum(y) * inv_n
    var = head_sum(yc * yc) * inv_n
    yn = yc * lax.rsqrt(var + RWKV_GN_EPS) * lnw_ref[...] + lnb_ref[...]
    bonus = head_sum(r * k2 * rk_ref[...]) * v
    o_ref[...] = ((yn + bonus) * g).astype(BF16)


def _rwkv(p_rwkv, seq, mu, w0, w2, a0, a2, g2, k_k, k_a, r_k, ln_w, ln_b):
    t = p_rwkv.shape[0]
    ts = RWKV_TILE
    kern = functools.partial(_rwkv_kernel, tiles_per_seq=seq // ts)
    vec = lambda: _const_spec((1, RWKV_DIM))
    return pl.pallas_call(
        kern,
        grid=(t // ts,),
        in_specs=[_row_spec(ts, RWKV_COLS), _const_spec((1, RWKV_COLS)), vec(), _const_spec((64, RWKV_DIM)), vec(),
                  _const_spec((64, RWKV_DIM)), _const_spec((128, RWKV_DIM)), vec(), vec(), vec(), vec(), vec()],
        out_specs=_row_spec(ts, RWKV_DIM),
        out_shape=jax.ShapeDtypeStruct((t, RWKV_DIM), BF16),
        scratch_shapes=[pltpu.VMEM((1, RWKV_COLS), F32), pltpu.VMEM((RWKV_HEADS // 2, LANES, LANES), F32)],
        compiler_params=_params(),
        name="rwkv7",
    )(p_rwkv, mu, w0, w2, a0, a2, g2, k_k, k_a, r_k, ln_w, ln_b)


def _mla_prep_body(p, pos_ref, freq_ref, qn_ref, kvn_ref, wq_ref, wqr_ref, wk_ref, wv_ref, q_out, k_out, v_out):
    hq = _rms(p[:, 0:MLA_Q_LORA], qn_ref[...], NORM_EPS).astype(BF16)
    hkv = _rms(p[:, MLA_Q_LORA:MLA_Q_LORA + MLA_KV_LORA], kvn_ref[...], NORM_EPS).astype(BF16)
    blk = p[:, 384:512]
    ang = pos_ref[...].astype(F32) * freq_ref[...]
    cos = jnp.cos(ang)
    sin = jnp.sin(ang)
    scale = (MLA_NOPE + MLA_ROPE) ** -0.5 * LOG2E
    qa = _dot(hq, wq_ref[...])
    qr = _dot(hq, wqr_ref[...])
    lane = lax.broadcasted_iota(jnp.int32, blk.shape, 1)
    kr = pltpu.roll(blk, 64, 1)
    rot = jnp.where(lane < 80, -pltpu.roll(blk, 48, 1), pltpu.roll(blk, 80, 1))
    rot = jnp.where((lane >= 64) & (lane < 96), rot, 0.0)
    krope = kr * cos + rot * sin
    kn = _dot(hkv, wk_ref[...])
    for h in range(MLA_HEADS):
        hs = slice(h * LANES, (h + 1) * LANES)
        q_out[:, hs] = ((qa[:, hs] * cos + qr[:, hs] * sin) * scale).astype(BF16)
        k_out[:, hs] = (kn[:, hs] + krope).astype(BF16)
    vt = _dot_nt(wv_ref[...], hkv)
    rowi = lax.broadcasted_iota(jnp.int32, vt.shape, 0)
    v_out[...] = (vt + (rowi % LANES == MLA_V).astype(F32)).astype(BF16)


def _softmax_stage_t(s_l, pv_fn, m_scr, acc_scr, idx, s_scr=None):
    n = len(s_l)
    m_old = [m_scr[c][0:1, :] for c in idx]
    m_new = [jnp.maximum(m, jnp.max(s, axis=0, keepdims=True)) for m, s in zip(m_old, s_l)]
    alpha = [jnp.exp2(mo - mn) for mo, mn in zip(m_old, m_new)]
    if s_scr is not None:
        for c in range(n):
            s_scr[c] = s_l[c]
        s_l = [s_scr[c] for c in range(n)]
    pv_l = [pv_fn(c, jnp.exp2(s_l[c] - m_new[c]).astype(BF16)) for c in range(n)]
    for c in range(n):
        m_scr[idx[c]] = jnp.broadcast_to(m_new[c], m_scr.shape[1:])
        acc_scr[idx[c]] = alpha[c] * acc_scr[idx[c]] + pv_l[c]


def _mla_attn_kernel(q_ref, k_ref, vt_ref, o_ref, m_scr, acc_scr, s_scr):
    tq = TQ
    i = pl.program_id(1)
    causal = lax.broadcasted_iota(jnp.int32, (tq, tq), 0) <= lax.broadcasted_iota(jnp.int32, (tq, tq), 1)
    lane = lax.broadcasted_iota(jnp.int32, (tq, LANES), 1)
    m_scr[...] = jnp.full(m_scr.shape, NEG, F32)
    acc_scr[...] = jnp.zeros(acc_scr.shape, F32)
    hsl = [slice(h * LANES, (h + 1) * LANES) for h in range(MLA_HEADS)]
    heads = list(range(MLA_HEADS))

    def tile(off, tk, mask):
        s_l = [_dot_nt(k_ref[pl.ds(off, tk), hs], q_ref[:, hs]) for hs in hsl]
        if mask:
            s_l = [jnp.where(causal, s, NEG) for s in s_l]
        _softmax_stage_t(s_l, lambda c, p: _dot(vt_ref[hsl[c], pl.ds(off, tk)], p), m_scr, acc_scr, heads,
                         s_scr.at[:, 0:tk, :])

    def pair_body(j, carry):
        tile(pl.multiple_of(j * 2 * tq, 2 * tq), 2 * tq, False)
        return carry

    def single_body(j, carry):
        tile(pl.multiple_of(j * tq, tq), tq, False)
        return carry

    n_pairs = i // 2 if KEY_TILES_PER_STEP == 2 else 0
    if KEY_TILES_PER_STEP == 2:
        lax.fori_loop(0, n_pairs, pair_body, 0)
    lax.fori_loop(2 * n_pairs, i, single_body, 0)
    tile(pl.multiple_of(i * tq, tq), tq, True)
    for pr in range(MLA_HEADS // 2):
        outs = []
        for hh in range(2):
            acc = acc_scr[2 * pr + hh]
            outs.append(jnp.transpose(acc / acc[MLA_V:MLA_V + 1, :]))
        o_ref[:, pr * LANES:(pr + 1) * LANES] = jnp.where(lane < MLA_V, outs[0], pltpu.roll(outs[1], MLA_V, 1)).astype(BF16)


def _mla_attn(q, k, vt, batch, seq):
    t = q.shape[0]
    nq = seq // TQ
    hw = MLA_HEADS * LANES
    vw = MLA_HEADS * MLA_V
    return pl.pallas_call(
        _mla_attn_kernel,
        grid=(batch, nq),
        in_specs=[pl.BlockSpec((TQ, hw), lambda b, i: (b * nq + i, 0)),
                  pl.BlockSpec((seq, hw), lambda b, i: (b, 0)),
                  pl.BlockSpec((hw, seq), lambda b, i: (0, b))],
        out_specs=pl.BlockSpec((TQ, vw), lambda b, i: (b * nq + i, 0)),
        out_shape=jax.ShapeDtypeStruct((t, vw), BF16),
        scratch_shapes=[pltpu.VMEM((MLA_HEADS, 8, TQ), F32), pltpu.VMEM((MLA_HEADS, LANES, TQ), F32),
                        pltpu.VMEM((MLA_HEADS, KEY_TILES_PER_STEP * TQ, TQ), F32)],
        compiler_params=_params2(),
        name="mla_attn",
    )(q, k, vt)


def _t5_bucket_table():
    n = np.arange(0, REL_MAX_DISTANCE + 1)
    max_exact = REL_BUCKETS // 2
    nf = np.maximum(n, 1).astype(np.float32)
    ratio = np.log(nf / np.float32(max_exact)) / np.float32(math.log(REL_MAX_DISTANCE / max_exact))
    large = max_exact + (ratio * np.float32(REL_BUCKETS - max_exact)).astype(np.int32)
    large = np.minimum(large, REL_BUCKETS - 1)
    return np.where(n < max_exact, n, large)


_BUCKETS = _t5_bucket_table()
_FAR_BUCKET = int(_BUCKETS[REL_MAX_DISTANCE])
assert _FAR_BUCKET == REL_BUCKETS - 1 and np.all(np.diff(_BUCKETS) >= 0)
_BUCKET_STARTS = [int(np.argmax(_BUCKETS >= b)) for b in range(REL_BUCKETS // 2 + 1, REL_BUCKETS)]


def _diff_attn_kernel(rb_ref, lam_ref, subln_ref, q_ref, k_ref, vt_ref, o_ref, bias_scr, m_scr, acc_scr, s_scr, *,
                      lambda_init):
    tq = TQ
    i = pl.program_id(1)
    r_i = lax.broadcasted_iota(jnp.int32, (tq, tq), 0)
    c_i = lax.broadcasted_iota(jnp.int32, (tq, tq), 1)
    causal = r_i <= c_i

    @pl.when((pl.program_id(0) == 0) & (i == 0))
    def _():
        for t_idx, delta in enumerate((0, tq)):
            d = jnp.maximum(c_i - r_i + delta, 0)
            log_b = REL_BUCKETS // 2
            for start in _BUCKET_STARTS:
                log_b = log_b + (d >= start).astype(jnp.int32)
            bucket = jnp.where(d < REL_BUCKETS // 2, d, log_b)
            for h in range(DIFF_HEADS):
                bias = jnp.zeros((tq, tq), F32)
                for b in range(REL_BUCKETS):
                    bias = jnp.where(bucket == b, rb_ref[b, h], bias)
                bias_scr[t_idx, h] = (bias - rb_ref[_FAR_BUCKET, h]) * LOG2E

    m_scr[...] = jnp.full(m_scr.shape, NEG, F32)
    acc_scr[...] = jnp.zeros(acc_scr.shape, F32)
    lane = lax.broadcasted_iota(jnp.int32, (tq, LANES), 1)
    ones_rows = {tk: (lax.broadcasted_iota(jnp.int32, (DIFF_VT_ROWS - DIFF_V, tk), 0) == 0).astype(BF16)
                 for tk in (tq, 2 * tq)}
    hsl = [slice(h * LANES, (h + 1) * LANES) for h in range(DIFF_HEADS)]
    heads = list(range(DIFF_HEADS))
    q_l = []
    for hs in hsl:
        qp = q_ref[:, hs]
        q_l.append(jnp.concatenate([jnp.where(lane < DIFF_QK, qp, jnp.zeros_like(qp)),
                                    jnp.where(lane >= DIFF_QK, qp, jnp.zeros_like(qp))], axis=0))

    def tile(off, tk, bias_idx, mask):
        s_l = [_dot_nt(k_ref[pl.ds(off, tk), hsl[h]], q_l[h]) for h in heads]
        if bias_idx is not None:
            s_l = [s + jnp.concatenate([bias_scr[bias_idx, h]] * 2, axis=1) for h, s in enumerate(s_l)]
        if mask:
            mask2 = jnp.concatenate([causal, causal], axis=1)
            s_l = [jnp.where(mask2, s, NEG) for s in s_l]

        def pv(c, p):
            vt = jnp.concatenate([vt_ref[hsl[c], pl.ds(off, tk)], ones_rows[tk]], axis=0)
            return jnp.concatenate([_dot(vt, p[:, :tq]), _dot(vt, p[:, tq:])], axis=1)

        _softmax_stage_t(s_l, pv, m_scr, acc_scr, heads, s_scr.at[:, 0:tk, :])

    def far_pair_body(j, carry):
        tile(pl.multiple_of(j * 2 * tq, 2 * tq), 2 * tq, None, False)
        return carry

    def far_body(j, carry):
        tile(pl.multiple_of(j * tq, tq), tq, None, False)
        return carry

    def near_body(j, carry):
        tile(pl.multiple_of(j * tq, tq), tq, 1, False)
        return carry

    n_far = jnp.maximum(i - 1, 0)
    n_pairs = n_far // 2 if KEY_TILES_PER_STEP == 2 else 0
    if KEY_TILES_PER_STEP == 2:
        lax.fori_loop(0, n_pairs, far_pair_body, 0)
    lax.fori_loop(2 * n_pairs, n_far, far_body, 0)
    lax.fori_loop(n_far, i, near_body, 0)
    tile(pl.multiple_of(i * tq, tq), tq, 0, True)

    lam = lam_ref[...]
    lam_full = (jnp.exp(jnp.sum(lam[0:1] * lam[1:2], axis=-1, keepdims=True))
                - jnp.exp(jnp.sum(lam[2:3] * lam[3:4], axis=-1, keepdims=True)) + lambda_init)
    for h in heads:
        acc = acc_scr[h]
        on = acc[:DIFF_V, :] / acc[DIFF_V:DIFF_V + 1, :]
        ot = on[:, :tq] - lam_full * on[:, tq:]
        ot = ot * lax.rsqrt(jnp.mean(ot * ot, axis=0, keepdims=True) + SUBLN_EPS)
        o_ref[:, hsl[h]] = (jnp.transpose(ot) * subln_ref[...] * (1.0 - lambda_init)).astype(BF16)


def _diff_attn(p_qk, p_vt, rel_bias, lam, subln, batch, seq, layer_idx):
    t = p_qk.shape[0]
    nq = seq // TQ
    w = DIFF_HEADS * LANES
    lambda_init = 0.8 - 0.6 * math.exp(-0.3 * layer_idx)
    kern = functools.partial(_diff_attn_kernel, lambda_init=lambda_init)
    const2 = lambda shape: pl.BlockSpec(shape, lambda b, i: (0, 0))
    return pl.pallas_call(
        kern,
        grid=(batch, nq),
        in_specs=[pl.BlockSpec(memory_space=pltpu.SMEM),
                  const2((4, DIFF_QK)), const2((1, DIFF_V)),
                  pl.BlockSpec((TQ, w), lambda b, i: (b * nq + i, 0)),
                  pl.BlockSpec((seq, w), lambda b, i: (b, 1)),
                  pl.BlockSpec((w, seq), lambda b, i: (0, b))],
        out_specs=pl.BlockSpec((TQ, w), lambda b, i: (b * nq + i, 0)),
        out_shape=jax.ShapeDtypeStruct((t, w), BF16),
        scratch_shapes=[pltpu.VMEM((2, DIFF_HEADS, TQ, TQ), F32), pltpu.VMEM((DIFF_HEADS, 8, 2 * TQ), F32),
                        pltpu.VMEM((DIFF_HEADS, DIFF_VT_ROWS, 2 * TQ), F32),
                        pltpu.VMEM((DIFF_HEADS, KEY_TILES_PER_STEP * TQ, 2 * TQ), F32)],
        compiler_params=_params2(),
        name="diff_attn",
    )(rel_bias, lam, subln, p_qk, p_qk, p_vt)


def _merge_kernel(x_ref, pg_ref, bg_ref, or_ref, om_ref, od_ref, wr_ref, wm_ref, wd_ref, wo_ref, gf_ref, x_out, h_out):
    halves = [slice(s * (TM_MERGE // 2), (s + 1) * (TM_MERGE // 2)) for s in range(2)]
    branch = [[_dot(o_ref[rs, :], w_ref[...]) for o_ref, w_ref in ((or_ref, wr_ref), (om_ref, wm_ref), (od_ref, wd_ref))]
              for rs in halves]
    for rs, d in zip(halves, branch):
        merged = None
        for idx in range(3):
            cs = slice(idx * D_MODEL, (idx + 1) * D_MODEL)
            term = jax.nn.sigmoid(pg_ref[rs, cs].astype(F32) + bg_ref[:, cs]) * d[idx]
            merged = term if merged is None else merged + term
        x1 = x_ref[rs, :] + _dot(merged.astype(BF16), wo_ref[...])
        x_out[rs, :] = x1
        h_out[rs, :] = _rms(x1, gf_ref[...], NORM_EPS).astype(BF16)


def _merge(x, p_gate, b_gate, o_r, o_m, o_d, w_r, w_m, w_d, w_o, g_ffn):
    t = x.shape[0]
    tm = TM_MERGE
    return pl.pallas_call(
        _merge_kernel,
        grid=(t // tm,),
        in_specs=[_row_spec(tm, D_MODEL), _row_spec(tm, GATE_COLS), _const_spec((1, GATE_COLS)),
                  _row_spec(tm, 512), _row_spec(tm, 512), _row_spec(tm, 512),
                  _const_spec((512, D_MODEL)), _const_spec((512, D_MODEL)), _const_spec((512, D_MODEL)),
                  _const_spec((D_MODEL, D_MODEL)), _const_spec((1, D_MODEL))],
        out_specs=[_row_spec(tm, D_MODEL), _row_spec(tm, D_MODEL)],
        out_shape=[jax.ShapeDtypeStruct((t, D_MODEL), F32), jax.ShapeDtypeStruct((t, D_MODEL), BF16)],
        compiler_params=_params(),
        name="merge",
    )(x, p_gate, b_gate, o_r, o_m, o_d, w_r, w_m, w_d, w_o, g_ffn)


def _ffn_kernel(x_ref, h_ref, wup_ref, cw_ref, cb_ref, wdn_ref, gfin_ref, o_ref, carry_ref, *, tiles_per_seq, final_norm):
    tm = TM_FFN

    @pl.when(pl.program_id(0) % tiles_per_seq == 0)
    def _():
        carry_ref[0:8, :] = jnp.zeros((8, carry_ref.shape[1]), F32)

    h = h_ref[...]

    def conv(u, cols):
        carry_ref[8:, cols] = u
        u1 = carry_ref[7:7 + tm, cols]
        u2 = carry_ref[6:6 + tm, cols]
        carry_ref[0:8, cols] = u[tm - 8:tm, :]
        return cw_ref[0:1, cols] * u2 + cw_ref[1:2, cols] * u1 + cw_ref[2:3, cols] * u + cb_ref[:, cols]

    def up(ck):
        gc = slice(ck * FF_CHUNK, (ck + 1) * FF_CHUNK)
        vc = slice(D_FF + ck * FF_CHUNK, D_FF + (ck + 1) * FF_CHUNK)
        return _dot(h, wup_ref[:, gc]), _dot(h, wup_ref[:, vc])

    n_chunks = D_FF // FF_CHUNK
    acc = x_ref[...]
    nxt = up(0)
    acts = []
    for ck in range(n_chunks):
        ug, uv = nxt
        if ck + 1 < n_chunks:
            nxt = up(ck + 1)
        gc = slice(ck * FF_CHUNK, (ck + 1) * FF_CHUNK)
        vc = slice(D_FF + ck * FF_CHUNK, D_FF + (ck + 1) * FF_CHUNK)
        gate = conv(ug, gc)
        val = conv(uv, vc)
        acts.append((gate * jax.nn.sigmoid(gate) * val).astype(BF16))
        if len(acts) == FF_DOWN_GROUP or ck + 1 == n_chunks:
            rows = slice((ck + 1 - len(acts)) * FF_CHUNK, (ck + 1) * FF_CHUNK)
            acc = acc + _dot(jnp.concatenate(acts, axis=-1), wdn_ref[rows, :])
            acts = []
    if final_norm:
        acc = _rms(acc, gfin_ref[...], NORM_EPS)
    o_ref[...] = acc


def _ffn(x1, h2, w_up, conv_w, conv_b, w_down, g_final, seq, final_norm):
    t = x1.shape[0]
    tm = TM_FFN
    kern = functools.partial(_ffn_kernel, tiles_per_seq=seq // tm, final_norm=final_norm)
    return pl.pallas_call(
        kern,
        grid=(t // tm,),
        in_specs=[_row_spec(tm, D_MODEL), _row_spec(tm, D_MODEL), _const_spec((D_MODEL, 2 * D_FF)),
                  _const_spec((3, 2 * D_FF)), _const_spec((1, 2 * D_FF)), _const_spec((D_FF, D_MODEL)),
                  _const_spec((1, D_MODEL))],
        out_specs=_row_spec(tm, D_MODEL),
        out_shape=jax.ShapeDtypeStruct((t, D_MODEL), F32),
        scratch_shapes=[pltpu.VMEM((8 + tm, 2 * D_FF), F32)],
        compiler_params=_params(),
        name="conv_ffn",
    )(x1, h2, w_up, conv_w, conv_b, w_down, g_final)


def _mla_weights(w_uq, w_ukv):
    qd = MLA_NOPE + MLA_ROPE
    half = MLA_ROPE // 2
    wq = w_uq.reshape(MLA_Q_LORA, MLA_HEADS, qd)
    zq = jnp.zeros((MLA_Q_LORA, MLA_HEADS, LANES - qd), F32)
    wq_main = jnp.concatenate([wq, zq], axis=-1)
    x1 = wq[:, :, MLA_NOPE:MLA_NOPE + half]
    x2 = wq[:, :, MLA_NOPE + half:]
    wq_rot = jnp.concatenate([jnp.zeros((MLA_Q_LORA, MLA_HEADS, MLA_NOPE), F32), -x2, x1, zq], axis=-1)
    wkv = w_ukv.reshape(MLA_KV_LORA, MLA_HEADS, MLA_NOPE + MLA_V)
    zkv = jnp.zeros((MLA_KV_LORA, MLA_HEADS, LANES - MLA_NOPE), F32)
    wk = jnp.concatenate([wkv[:, :, :MLA_NOPE], zkv], axis=-1)
    wv = jnp.concatenate([wkv[:, :, MLA_NOPE:], zkv], axis=-1)
    flat = lambda w: w.reshape(w.shape[0], -1).astype(BF16)
    return flat(wq_main), flat(wq_rot), flat(wk), flat(wv).T


def kernel(x, positions, rel_bias, norm_mix, w_in, b_gate, rwkv_mu, rwkv_w0, rwkv_w2, rwkv_a0, rwkv_a2, rwkv_g2, rwkv_k_k, rwkv_k_a, rwkv_r_k, rwkv_ln_w, rwkv_ln_b, mla_q_norm, mla_w_uq, mla_kv_norm, mla_w_ukv, diff_lambda, diff_subln, w_branch_rwkv, w_branch_mla, w_branch_diff, w_o, norm_ffn, ffn_w_up, ffn_conv_w, ffn_conv_b, ffn_w_down, norm_final):
    batch, seq, _ = x.shape
    depth = w_in.shape[0]
    t = batch * seq
    assert seq % TQ == 0 and seq % TM == 0 and seq % TM_FFN == 0 and seq % RWKV_TILE == 0 and RWKV_TILE % CHUNK == 0
    assert (batch * seq) % TM_MERGE == 0
    xf = x.reshape(t, D_MODEL)
    pos_col = positions.reshape(t, 1)
    inv_freq = ROPE_BASE ** (-jnp.arange(0, MLA_ROPE, 2, dtype=F32) / MLA_ROPE)
    freq = jnp.concatenate([jnp.zeros((MLA_NOPE,), F32), inv_freq, inv_freq,
                            jnp.zeros((LANES - MLA_NOPE - MLA_ROPE,), F32)]).reshape(1, LANES)
    row = lambda v: v.reshape(1, -1)
    diff_scale = jnp.concatenate([jnp.full((512,), DIFF_QK ** -0.5 * LOG2E, F32), jnp.ones((512,), F32)])

    for l in range(depth):
        s0, s1, s2 = RWKV_COLS, RWKV_COLS + MLA_COLS, RWKV_COLS + MLA_COLS + DIFF_COLS
        sv = s1 + DIFF_QK_COLS
        w = w_in[l]
        w_r = w[:, :s0].astype(BF16)
        w_m = jnp.pad(w[:, s0:s1], ((0, 0), (0, MLA_PAD - MLA_COLS))).astype(BF16)
        w_qk = (w[:, s1:sv] * diff_scale).astype(BF16)
        w_vt = w[:, sv:s2].T.astype(BF16)
        w_g = w[:, s2:].astype(BF16)
        wq, wqr, wk, wv = _mla_weights(mla_w_uq[l], mla_w_ukv[l])
        p_rwkv, q_m, k_m, v_m, p_qk, p_vt, p_gate = _in_proj(
            xf, row(norm_mix[l]), w_r, w_m, w_qk, w_g, w_vt, pos_col, freq, row(mla_q_norm[l]), row(mla_kv_norm[l]),
            wq, wqr, wk, wv)

        o_r = _rwkv(p_rwkv, seq, row(rwkv_mu[l]), row(rwkv_w0[l]), rwkv_w2[l].astype(BF16), row(rwkv_a0[l]),
                    rwkv_a2[l].astype(BF16), rwkv_g2[l].astype(BF16), row(rwkv_k_k[l]), row(rwkv_k_a[l]),
                    row(rwkv_r_k[l]), row(rwkv_ln_w[l]), row(rwkv_ln_b[l]))

        o_m = _mla_attn(q_m, k_m, v_m, batch, seq)

        o_d = _diff_attn(p_qk, p_vt, rel_bias, diff_lambda[l], row(diff_subln[l]), batch, seq, l)

        x1, h2 = _merge(xf, p_gate, row(b_gate[l]), o_r, o_m, o_d, w_branch_rwkv[l].astype(BF16),
                        w_branch_mla[l].astype(BF16), w_branch_diff[l].astype(BF16), w_o[l].astype(BF16),
                        row(norm_ffn[l]))
        xf = _ffn(x1, h2, ffn_w_up[l].astype(BF16), ffn_conv_w[l], row(ffn_conv_b[l]), ffn_w_down[l].astype(BF16),
                  row(norm_final), seq, final_norm=(l == depth - 1))
    return xf.reshape(batch, seq, D_MODEL)
```

```python
import functools
import math

import numpy as np
import jax
import jax.numpy as jnp
from jax import lax
from jax.experimental import pallas as pl
from jax.experimental.pallas import tpu as pltpu

F32 = jnp.float32
BF16 = jnp.bfloat16

D_MODEL = 1024
RWKV_HEADS = 8
RWKV_N = 64
RWKV_DIM = 512
RWKV_COLS = 1792
RWKV_GN_EPS = 64e-5
MLA_HEADS = 8
MLA_Q_LORA = 256
MLA_KV_LORA = 128
MLA_NOPE = 64
MLA_ROPE = 32
MLA_V = 64
MLA_VT_ROWS = MLA_V + 16
MLA_COLS = 416
MLA_PAD = 512
ROPE_BASE = 10000.0
DIFF_HEADS = 4
DIFF_QK = 64
DIFF_V = 128
DIFF_COLS = 1536
DIFF_QK_COLS = 1024
DIFF_VT_ROWS = DIFF_V + 16
REL_BUCKETS = 32
REL_MAX_DISTANCE = 128
D_FF = 2816
GATE_COLS = 3072
NORM_EPS = 1e-6
SUBLN_EPS = 1e-5

LANES = 128
VMEM_LIMIT = 58 * 1024 * 1024
TM = 512
TM_FFN = 256
TM_MERGE = 512
CHUNK = 64
RWKV_TILE = 256
TQ = 512
KEY_TILES_PER_STEP = 1
FF_CHUNK = 256
FF_DOWN_GROUP = 11
NEG = -1e30
LOG2E = 1.4426950408889634

_NT = (((1,), (1,)), ((), ()))
_TN = (((0,), (0,)), ((), ()))


def _dot(a, b):
    return jnp.dot(a, b, preferred_element_type=F32)


def _dot_nt(a, b):
    return lax.dot_general(a, b, _NT, preferred_element_type=F32)


def _dot_tn(a, b):
    return lax.dot_general(a, b, _TN, preferred_element_type=F32)


def _rms(x, g, eps):
    return x * lax.rsqrt(jnp.mean(x * x, axis=-1, keepdims=True) + eps) * g


def _params():
    return pltpu.CompilerParams(dimension_semantics=("arbitrary",), vmem_limit_bytes=VMEM_LIMIT)


def _params2():
    return pltpu.CompilerParams(dimension_semantics=("arbitrary", "arbitrary"), vmem_limit_bytes=VMEM_LIMIT)


def _const_spec(shape):
    return pl.BlockSpec(shape, lambda *_: (0,) * len(shape), pipeline_mode=pl.Buffered(1))


def _row_spec(tm, cols, col_block=0):
    return pl.BlockSpec((tm, cols), lambda i: (i, col_block))


def _in_proj_kernel(x_ref, g_ref, wr_ref, wm_ref, wqk_ref, wg_ref, wvt_ref, pos_ref, freq_ref, qn_ref, kvn_ref, wq_ref,
                    wqr_ref, wk_ref, wv_ref, pr_ref, mq_ref, mk_ref, mvt_ref, pqk_ref, pvt_ref, pg_ref):
    h = _rms(x_ref[...], g_ref[...], NORM_EPS).astype(BF16)
    p_mla = _dot(h, wm_ref[...])
    pqk_ref[...] = _dot(h, wqk_ref[...]).astype(BF16)
    pg_ref[...] = _dot(h, wg_ref[...]).astype(BF16)
    _mla_prep_body(p_mla, pos_ref, freq_ref, qn_ref, kvn_ref, wq_ref, wqr_ref, wk_ref, wv_ref, mq_ref, mk_ref, mvt_ref)
    pr_ref[...] = _dot(h, wr_ref[...])
    pvt_ref[...] = _dot_nt(wvt_ref[...], h).astype(BF16)


def _in_proj(x, g, w_r, w_m, w_qk, w_g, w_vt, pos_col, freq, q_norm, kv_norm, wq, wqr, wk, wv):
    t = x.shape[0]
    vw = DIFF_HEADS * DIFF_V
    hw = MLA_HEADS * LANES
    mvw = MLA_HEADS * MLA_VT_ROWS
    return pl.pallas_call(
        _in_proj_kernel,
        grid=(t // TM,),
        in_specs=[_row_spec(TM, D_MODEL), _const_spec((1, D_MODEL)), _const_spec((D_MODEL, RWKV_COLS)),
                  _const_spec((D_MODEL, MLA_PAD)), _const_spec((D_MODEL, DIFF_QK_COLS)), _const_spec((D_MODEL, GATE_COLS)),
                  _const_spec((vw, D_MODEL)), _row_spec(TM, 1), _const_spec((1, LANES)), _const_spec((1, MLA_Q_LORA)),
                  _const_spec((1, MLA_KV_LORA)), _const_spec((MLA_Q_LORA, hw)), _const_spec((MLA_Q_LORA, hw)),
                  _const_spec((MLA_KV_LORA, hw)), _const_spec((mvw, MLA_KV_LORA))],
        out_specs=[_row_spec(TM, RWKV_COLS), _row_spec(TM, hw), _row_spec(TM, hw),
                   pl.BlockSpec((mvw, TM), lambda i: (0, i)), _row_spec(TM, DIFF_QK_COLS),
                   pl.BlockSpec((vw, TM), lambda i: (0, i)), _row_spec(TM, GATE_COLS)],
        out_shape=[jax.ShapeDtypeStruct((t, RWKV_COLS), F32), jax.ShapeDtypeStruct((t, hw), BF16),
                   jax.ShapeDtypeStruct((t, hw), BF16), jax.ShapeDtypeStruct((mvw, t), BF16),
                   jax.ShapeDtypeStruct((t, DIFF_QK_COLS), BF16), jax.ShapeDtypeStruct((vw, t), BF16),
                   jax.ShapeDtypeStruct((t, GATE_COLS), BF16)],
        compiler_params=_params(),
        name="in_proj",
    )(x, g, w_r, w_m, w_qk, w_g, w_vt, pos_col, freq, q_norm, kv_norm, wq, wqr, wk, wv)


def _split3(a):
    hi = a.astype(BF16)
    r1 = a - hi.astype(F32)
    mid = r1.astype(BF16)
    lo = (r1 - mid.astype(F32)).astype(BF16)
    return hi, mid, lo


def _softplus(z):
    return jnp.maximum(z, 0.0) + jnp.log(1.0 + jnp.exp(-jnp.abs(z)))


def _rwkv_kernel(p_ref, mu_ref, w0_ref, w2_ref, a0_ref, a2_ref, g2_ref, kk_ref, ka_ref, rk_ref, lnw_ref, lnb_ref,
                 o_ref, carry_ref, state_ref, *, tiles_per_seq):
    c = CHUNK
    n = RWKV_N
    ts = RWKV_TILE
    nc = ts // c

    @pl.when(pl.program_id(0) % tiles_per_seq == 0)
    def _():
        carry_ref[...] = jnp.zeros_like(carry_ref)
        state_ref[...] = jnp.zeros_like(state_ref)

    p = p_ref[...]
    row = lax.broadcasted_iota(jnp.int32, (ts, 1), 0)
    shifted = jnp.where(row == 0, carry_ref[...], pltpu.roll(p, 1, 0))
    carry_ref[...] = p[ts - 1:ts, :]
    pm = p + (shifted - p) * mu_ref[...]

    r = pm[:, 0:512]
    k = pm[:, 512:1024]
    v = pm[:, 1024:1536]
    pw = pm[:, 1536:1600]
    pa = pm[:, 1600:1664]
    pg = pm[:, 1664:1792]

    w_log = -_softplus(-(w0_ref[...] + _dot(jnp.tanh(pw).astype(BF16), w2_ref[...]))) - 0.5
    logd = -jnp.exp(w_log)
    a = jax.nn.sigmoid(a0_ref[...] + _dot(pa.astype(BF16), a2_ref[...]))
    g = _dot(jax.nn.sigmoid(pg).astype(BF16), g2_ref[...])

    tr = lax.broadcasted_iota(jnp.int32, (ts, ts), 0)
    tc = lax.broadcasted_iota(jnp.int32, (ts, ts), 1)
    tri = ((tr >= tc) & (tr // c == tc // c)).astype(BF16)
    hi, mid, lo = _split3(logd)
    cs = _dot(tri, hi) + _dot(tri, mid) + _dot(tri, lo)
    total = jnp.concatenate([jnp.broadcast_to(cs[(m + 1) * c - 1:(m + 1) * c, :], (c, RWKV_DIM)) for m in range(nc)], axis=0)
    e_in = jnp.exp(cs)
    e_ex = jnp.exp(cs - logd)
    e_inv = jnp.exp(-cs)
    e_end = jnp.exp(total - cs)
    g_end = jnp.exp(total)

    n_pairs = RWKV_HEADS // 2
    lo_t = lax.broadcasted_iota(jnp.int32, (ts, LANES), 1) < n
    lo_c = lax.broadcasted_iota(jnp.int32, (c, LANES), 1) < n

    def head_sum(x):
        blocks = []
        for q in range(n_pairs):
            xb = x[:, q * LANES:(q + 1) * LANES]
            s_lo = jnp.sum(jnp.where(lo_t, xb, 0.0), axis=-1, keepdims=True)
            s_hi = jnp.sum(jnp.where(lo_t, 0.0, xb), axis=-1, keepdims=True)
            blocks.append(jnp.where(lo_t, s_lo, s_hi))
        return jnp.concatenate(blocks, axis=-1)

    k2 = k * (1.0 + (a - 1.0) * ka_ref[...])
    kku = k * kk_ref[...]
    kkn = kku / jnp.maximum(jnp.sqrt(head_sum(kku * kku)), 1e-12)
    b = kkn * a
    full = {"at": -kkn * e_ex, "rt": r * e_in, "bt": b * e_inv, "kt": k2 * e_inv, "v": v,
            "be": b * e_end, "ke": k2 * e_end}

    ti = lax.broadcasted_iota(jnp.int32, (2 * c, 4 * c), 0) % c
    si = lax.broadcasted_iota(jnp.int32, (2 * c, 4 * c), 1) % c
    strict = ti > si
    incl = ti >= si
    eye = (lax.broadcasted_iota(jnp.int32, (2 * c, 2 * c), 0)
           == lax.broadcasted_iota(jnp.int32, (2 * c, 2 * c), 1)).astype(F32)

    items = [(m, q) for m in range(nc) for q in range(n_pairs)]

    def stacked(name, m, q):
        xb = full[name][m * c:(m + 1) * c, q * LANES:(q + 1) * LANES]
        return jnp.concatenate([jnp.where(lo_c, xb, 0.0), jnp.where(lo_c, 0.0, xb)], axis=0).astype(BF16)

    st_ops = {name: [stacked(name, m, q) for m, q in items] for name in full}
    big_l = [_dot_nt(jnp.concatenate([at, rtb], axis=0), jnp.concatenate([bt, kt], axis=0))
             for at, rtb, bt, kt in zip(st_ops["at"], st_ops["rt"], st_ops["bt"], st_ops["kt"])]
    top_l = [jnp.where(strict, big[:2 * c, :], 0.0) for big in big_l]
    bot_l = [jnp.where(incl, big[2 * c:, :], 0.0).astype(BF16) for big in big_l]
    lakv_l = [_dot(top[:, 2 * c:].astype(BF16), vb) for top, vb in zip(top_l, st_ops["v"])]

    x_l = [top[:, :2 * c] for top in top_l]
    tinv_l = [eye + x for x in x_l]
    xb_l = [x.astype(BF16) for x in x_l]
    for _ in range(int(math.log2(c)) - 1):
        x_l = [_dot(xb, xb) for xb in xb_l]
        xb_l = [x.astype(BF16) for x in x_l]
        tinv_l = [tinv + _dot(tinv.astype(BF16), xb) for tinv, xb in zip(tinv_l, xb_l)]
    tinvb_l = [tinv.astype(BF16) for tinv in tinv_l]
    abar_l = [_dot(tb, at).astype(BF16) for tb, at in zip(tinvb_l, st_ops["at"])]
    vbar_l = [_dot(tb, lv.astype(BF16)).astype(BF16) for tb, lv in zip(tinvb_l, lakv_l)]
    rhat_l = [(rtb.astype(F32) + _dot(bot[:, :2 * c], ab)).astype(BF16)
              for rtb, bot, ab in zip(st_ops["rt"], bot_l, abar_l)]
    uv_l = [jnp.concatenate([vbar, vb], axis=0) for vbar, vb in zip(vbar_l, st_ops["v"])]
    y0_l = [_dot(bot, uv) for bot, uv in zip(bot_l, uv_l)]
    p_l = [_dot_tn(ab, be).astype(BF16) for ab, be in zip(abar_l, st_ops["be"])]
    q_l = [_dot_tn(uv, jnp.concatenate([be, ke], axis=0)) for uv, be, ke in zip(uv_l, st_ops["be"], st_ops["ke"])]

    st_l = [state_ref[q] for q in range(n_pairs)]
    y_rows = []
    for m in range(nc):
        y_blocks = []
        for q in range(n_pairs):
            idx = m * n_pairs + q
            st = st_l[q]
            st_b = st.astype(BF16)
            y_s = _dot_nt(rhat_l[idx], st_b) + y0_l[idx]
            y_blocks.append(y_s[:c] + y_s[c:])
            decay = g_end[(m + 1) * c - 1:(m + 1) * c, q * LANES:(q + 1) * LANES]
            st_l[q] = st * decay + _dot(st_b, p_l[idx]) + q_l[idx]
        y_rows.append(jnp.concatenate(y_blocks, axis=-1))
    for q in range(n_pairs):
        state_ref[q] = st_l[q]
    y = jnp.concatenate(y_rows, axis=0)

    inv_n = 1.0 / n
    yc = y - head_sum(y) * inv_n
    var = head_sum(yc * yc) * inv_n
    yn = yc * lax.rsqrt(var + RWKV_GN_EPS) * lnw_ref[...] + lnb_ref[...]
    bonus = head_sum(r * k2 * rk_ref[...]) * v
    o_ref[...] = ((yn + bonus) * g).astype(BF16)


def _rwkv(p_rwkv, seq, mu, w0, w2, a0, a2, g2, k_k, k_a, r_k, ln_w, ln_b):
    t = p_rwkv.shape[0]
    ts = RWKV_TILE
    kern = functools.partial(_rwkv_kernel, tiles_per_seq=seq // ts)
    vec = lambda: _const_spec((1, RWKV_DIM))
    return pl.pallas_call(
        kern,
        grid=(t // ts,),
        in_specs=[_row_spec(ts, RWKV_COLS), _const_spec((1, RWKV_COLS)), vec(), _const_spec((64, RWKV_DIM)), vec(),
                  _const_spec((64, RWKV_DIM)), _const_spec((128, RWKV_DIM)), vec(), vec(), vec(), vec(), vec()],
        out_specs=_row_spec(ts, RWKV_DIM),
        out_shape=jax.ShapeDtypeStruct((t, RWKV_DIM), BF16),
        scratch_shapes=[pltpu.VMEM((1, RWKV_COLS), F32), pltpu.VMEM((RWKV_HEADS // 2, LANES, LANES), F32)],
        compiler_params=_params(),
        name="rwkv7",
    )(p_rwkv, mu, w0, w2, a0, a2, g2, k_k, k_a, r_k, ln_w, ln_b)


def _mla_prep_body(p, pos_ref, freq_ref, qn_ref, kvn_ref, wq_ref, wqr_ref, wk_ref, wv_ref, q_out, k_out, v_out):
    hq = _rms(p[:, 0:MLA_Q_LORA], qn_ref[...], NORM_EPS).astype(BF16)
    hkv = _rms(p[:, MLA_Q_LORA:MLA_Q_LORA + MLA_KV_LORA], kvn_ref[...], NORM_EPS).astype(BF16)
    blk = p[:, 384:512]
    ang = pos_ref[...].astype(F32) * freq_ref[...]
    cos = jnp.cos(ang)
    sin = jnp.sin(ang)
    scale = (MLA_NOPE + MLA_ROPE) ** -0.5 * LOG2E
    qa = _dot(hq, wq_ref[...])
    qr = _dot(hq, wqr_ref[...])
    lane = lax.broadcasted_iota(jnp.int32, blk.shape, 1)
    kr = pltpu.roll(blk, 64, 1)
    rot = jnp.where(lane < 80, -pltpu.roll(blk, 48, 1), pltpu.roll(blk, 80, 1))
    rot = jnp.where((lane >= 64) & (lane < 96), rot, 0.0)
    krope = kr * cos + rot * sin
    kn = _dot(hkv, wk_ref[...])
    for h in range(MLA_HEADS):
        hs = slice(h * LANES, (h + 1) * LANES)
        q_out[:, hs] = ((qa[:, hs] * cos + qr[:, hs] * sin) * scale).astype(BF16)
        k_out[:, hs] = (kn[:, hs] + krope).astype(BF16)
    vt = _dot_nt(wv_ref[...], hkv)
    rowi = lax.broadcasted_iota(jnp.int32, vt.shape, 0)
    v_out[...] = (vt + (rowi % MLA_VT_ROWS == MLA_V).astype(F32)).astype(BF16)


def _softmax_stage_t(s_l, pv_fn, m_scr, acc_scr, idx, s_scr=None, l_scr=None):
    n = len(s_l)
    m_old = [m_scr[c][0:1, :] for c in idx]
    m_new = [jnp.maximum(m, jnp.max(s, axis=0, keepdims=True)) for m, s in zip(m_old, s_l)]
    alpha = [jnp.exp2(mo - mn) for mo, mn in zip(m_old, m_new)]
    if s_scr is not None:
        for c in range(n):
            s_scr[c] = s_l[c]
        s_l = [s_scr[c] for c in range(n)]
    pv_l = []
    for c in range(n):
        p = jnp.exp2(s_l[c] - m_new[c])
        if l_scr is not None:
            l_new = alpha[c] * l_scr[idx[c]][0:1, :] + jnp.sum(p, axis=0, keepdims=True)
            l_scr[idx[c]] = jnp.broadcast_to(l_new, l_scr.shape[1:])
        pv_l.append(pv_fn(c, p.astype(BF16)))
    for c in range(n):
        m_scr[idx[c]] = jnp.broadcast_to(m_new[c], m_scr.shape[1:])
        acc_scr[idx[c]] = alpha[c] * acc_scr[idx[c]] + pv_l[c]


def _mla_attn_kernel(q_ref, k_ref, vt_ref, o_ref, m_scr, acc_scr, s_scr):
    tq = TQ
    i = pl.program_id(1)
    causal = lax.broadcasted_iota(jnp.int32, (tq, tq), 0) <= lax.broadcasted_iota(jnp.int32, (tq, tq), 1)
    m_scr[...] = jnp.full(m_scr.shape, NEG, F32)
    acc_scr[...] = jnp.zeros(acc_scr.shape, F32)
    hsl = [slice(h * LANES, (h + 1) * LANES) for h in range(MLA_HEADS)]
    vsl =[slice(h * MLA_VT_ROWS, (h + 1) * MLA_VT_ROWS) for h in range(MLA_HEADS)]
    heads = list(range(MLA_HEADS))

    def tile(off, tk, mask):
        s_l = [_dot_nt(k_ref[pl.ds(off, tk), hs], q_ref[:, hs]) for hs in hsl]
        if mask:
            s_l = [jnp.where(causal, s, NEG) for s in s_l]
        _softmax_stage_t(s_l, lambda c, p: _dot(vt_ref[vsl[c], pl.ds(off, tk)], p), m_scr, acc_scr, heads,
                         s_scr.at[:, 0:tk, :])

    def pair_body(j, carry):
        tile(pl.multiple_of(j * 2 * tq, 2 * tq), 2 * tq, False)
        return carry

    def single_body(j, carry):
        tile(pl.multiple_of(j * tq, tq), tq, False)
        return carry

    n_pairs = i // 2 if KEY_TILES_PER_STEP == 2 else 0
    if KEY_TILES_PER_STEP == 2:
        lax.fori_loop(0, n_pairs, pair_body, 0)
    lax.fori_loop(2 * n_pairs, i, single_body, 0)
    tile(pl.multiple_of(i * tq, tq), tq, True)
    for pr in range(MLA_HEADS // 2):
        accs = [acc_scr[2 * pr + hh] for hh in range(2)]
        ot = jnp.concatenate([acc[:MLA_V, :] / acc[MLA_V:MLA_V + 1, :] for acc in accs], axis=0)
        o_ref[:, pr * LANES:(pr + 1) * LANES] = jnp.transpose(ot).astype(BF16)


def _mla_attn(q, k, vt, batch, seq):
    t = q.shape[0]
    nq = seq // TQ
    hw = MLA_HEADS * LANES
    vw = MLA_HEADS * MLA_V
    return pl.pallas_call(
        _mla_attn_kernel,
        grid=(batch, nq),
        in_specs=[pl.BlockSpec((TQ, hw), lambda b, i: (b * nq + i, 0)),
                  pl.BlockSpec((seq, hw), lambda b, i: (b, 0)),
                  pl.BlockSpec((MLA_HEADS * MLA_VT_ROWS, seq), lambda b, i: (0, b))],
        out_specs=pl.BlockSpec((TQ, vw), lambda b, i: (b * nq + i, 0)),
        out_shape=jax.ShapeDtypeStruct((t, vw), BF16),
        scratch_shapes=[pltpu.VMEM((MLA_HEADS, 8, TQ), F32), pltpu.VMEM((MLA_HEADS, MLA_VT_ROWS, TQ), F32),
                        pltpu.VMEM((MLA_HEADS, KEY_TILES_PER_STEP * TQ, TQ), F32)],
        compiler_params=_params2(),
        name="mla_attn",
    )(q, k, vt)


def _t5_bucket_table():
    n = np.arange(0, REL_MAX_DISTANCE + 1)
    max_exact = REL_BUCKETS // 2
    nf = np.maximum(n, 1).astype(np.float32)
    ratio = np.log(nf / np.float32(max_exact)) / np.float32(math.log(REL_MAX_DISTANCE / max_exact))
    large = max_exact + (ratio * np.float32(REL_BUCKETS - max_exact)).astype(np.int32)
    large = np.minimum(large, REL_BUCKETS - 1)
    return np.where(n < max_exact, n, large)


_BUCKETS = _t5_bucket_table()
_FAR_BUCKET = int(_BUCKETS[REL_MAX_DISTANCE])
assert _FAR_BUCKET == REL_BUCKETS - 1 and np.all(np.diff(_BUCKETS) >= 0)
_BUCKET_STARTS = [int(np.argmax(_BUCKETS >= b)) for b in range(REL_BUCKETS // 2 + 1, REL_BUCKETS)]


def _diff_attn_kernel(rb_ref, lam_ref, subln_ref, q_ref, k_ref, vt_ref, o_ref, bias_scr, m_scr, acc_scr, s_scr, *,
                      lambda_init):
    tq = TQ
    i = pl.program_id(1)
    r_i = lax.broadcasted_iota(jnp.int32, (tq, tq), 0)
    c_i = lax.broadcasted_iota(jnp.int32, (tq, tq), 1)
    causal = r_i <= c_i

    @pl.when((pl.program_id(0) == 0) & (i == 0))
    def _():
        for t_idx, delta in enumerate((0, tq)):
            d = jnp.maximum(c_i - r_i + delta, 0)
            log_b = REL_BUCKETS // 2
            for start in _BUCKET_STARTS:
                log_b = log_b + (d >= start).astype(jnp.int32)
            bucket = jnp.where(d < REL_BUCKETS // 2, d, log_b)
            for h in range(DIFF_HEADS):
                bias = jnp.zeros((tq, tq), F32)
                for b in range(REL_BUCKETS):
                    bias = jnp.where(bucket == b, rb_ref[b, h], bias)
                bias_scr[t_idx, h] = (bias - rb_ref[_FAR_BUCKET, h]) * LOG2E

    m_scr[...] = jnp.full(m_scr.shape, NEG, F32)
    acc_scr[...] = jnp.zeros(acc_scr.shape, F32)
    lane = lax.broadcasted_iota(jnp.int32, (tq, LANES), 1)
    ones_rows = {tk: (lax.broadcasted_iota(jnp.int32, (DIFF_VT_ROWS - DIFF_V, tk), 0) == 0).astype(BF16)
                 for tk in (tq, 2 * tq)}
    hsl = [slice(h * LANES, (h + 1) * LANES) for h in range(DIFF_HEADS)]
    heads = list(range(DIFF_HEADS))
    q_l = []
    for hs in hsl:
        qp = q_ref[:, hs]
        q_l.append(jnp.concatenate([jnp.where(lane < DIFF_QK, qp, jnp.zeros_like(qp)),
                                    jnp.where(lane >= DIFF_QK, qp, jnp.zeros_like(qp))], axis=0))

    def tile(off, tk, bias_idx, mask):
        s_l = [_dot_nt(k_ref[pl.ds(off, tk), hsl[h]], q_l[h]) for h in heads]
        if bias_idx is not None:
            s_l = [s + jnp.concatenate([bias_scr[bias_idx, h]] * 2, axis=1) for h, s in enumerate(s_l)]
        if mask:
            mask2 = jnp.concatenate([causal, causal], axis=1)
            s_l = [jnp.where(mask2, s, NEG) for s in s_l]

        def pv(c, p):
            vt = jnp.concatenate([vt_ref[hsl[c], pl.ds(off, tk)], ones_rows[tk]], axis=0)
            return jnp.concatenate([_dot(vt, p[:, :tq]), _dot(vt, p[:, tq:])], axis=1)

        _softmax_stage_t(s_l, pv, m_scr, acc_scr, heads, s_scr.at[:, 0:tk, :])

    def far_pair_body(j, carry):
        tile(pl.multiple_of(j * 2 * tq, 2 * tq), 2 * tq, None, False)
        return carry

    def far_body(j, carry):
        tile(pl.multiple_of(j * tq, tq), tq, None, False)
        return carry

    def near_body(j, carry):
        tile(pl.multiple_of(j * tq, tq), tq, 1, False)
        return carry

    n_far = jnp.maximum(i - 1, 0)
    n_pairs = n_far // 2 if KEY_TILES_PER_STEP == 2 else 0
    if KEY_TILES_PER_STEP == 2:
        lax.fori_loop(0, n_pairs, far_pair_body, 0)
    lax.fori_loop(2 * n_pairs, n_far, far_body, 0)
    lax.fori_loop(n_far, i, near_body, 0)
    tile(pl.multiple_of(i * tq, tq), tq, 0, True)

    lam = lam_ref[...]
    lam_full = (jnp.exp(jnp.sum(lam[0:1] * lam[1:2], axis=-1, keepdims=True))
                - jnp.exp(jnp.sum(lam[2:3] * lam[3:4], axis=-1, keepdims=True)) + lambda_init)
    for h in heads:
        acc = acc_scr[h]
        on = acc[:DIFF_V, :] / acc[DIFF_V:DIFF_V + 1, :]
        ot = on[:, :tq] - lam_full * on[:, tq:]
        ot = ot * lax.rsqrt(jnp.mean(ot * ot, axis=0, keepdims=True) + SUBLN_EPS)
        o_ref[:, hsl[h]] = (jnp.transpose(ot) * subln_ref[...] * (1.0 - lambda_init)).astype(BF16)


def _diff_attn(p_qk, p_vt, rel_bias, lam, subln, batch, seq, layer_idx):
    t = p_qk.shape[0]
    nq = seq // TQ
    w = DIFF_HEADS * LANES
    lambda_init = 0.8 - 0.6 * math.exp(-0.3 * layer_idx)
    kern = functools.partial(_diff_attn_kernel, lambda_init=lambda_init)
    const2 = lambda shape: pl.BlockSpec(shape, lambda b, i: (0, 0))
    return pl.pallas_call(
        kern,
        grid=(batch, nq),
        in_specs=[pl.BlockSpec(memory_space=pltpu.SMEM),
                  const2((4, DIFF_QK)), const2((1, DIFF_V)),
                  pl.BlockSpec((TQ, w), lambda b, i: (b * nq + i, 0)),
                  pl.BlockSpec((seq, w), lambda b, i: (b, 1)),
                  pl.BlockSpec((w, seq), lambda b, i: (0, b))],
        out_specs=pl.BlockSpec((TQ, w), lambda b, i: (b * nq + i, 0)),
        out_shape=jax.ShapeDtypeStruct((t, w), BF16),
        scratch_shapes=[pltpu.VMEM((2, DIFF_HEADS, TQ, TQ), F32), pltpu.VMEM((DIFF_HEADS, 8, 2 * TQ), F32),
                        pltpu.VMEM((DIFF_HEADS, DIFF_VT_ROWS, 2 * TQ), F32),
                        pltpu.VMEM((DIFF_HEADS, KEY_TILES_PER_STEP * TQ, 2 * TQ), F32)],
        compiler_params=_params2(),
        name="diff_attn",
    )(rel_bias, lam, subln, p_qk, p_qk, p_vt)


def _merge_kernel(x_ref, pg_ref, bg_ref, or_ref, om_ref, od_ref, wr_ref, wm_ref, wd_ref, wo_ref, gf_ref, x_out, h_out):
    halves = [slice(s * (TM_MERGE // 2), (s + 1) * (TM_MERGE // 2)) for s in range(2)]
    branch = [[_dot(o_ref[rs, :], w_ref[...]) for o_ref, w_ref in ((or_ref, wr_ref), (om_ref, wm_ref), (od_ref, wd_ref))]
              for rs in halves]
    for rs, d in zip(halves, branch):
        merged = None
        for idx in range(3):
            cs = slice(idx * D_MODEL, (idx + 1) * D_MODEL)
            term = jax.nn.sigmoid(pg_ref[rs, cs].astype(F32) + bg_ref[:, cs]) * d[idx]
            merged = term if merged is None else merged + term
        x1 = x_ref[rs, :] + _dot(merged.astype(BF16), wo_ref[...])
        x_out[rs, :] = x1
        h_out[rs, :] = _rms(x1, gf_ref[...], NORM_EPS).astype(BF16)


def _merge(x, p_gate, b_gate, o_r, o_m, o_d, w_r, w_m, w_d, w_o, g_ffn):
    t = x.shape[0]
    tm = TM_MERGE
    return pl.pallas_call(
        _merge_kernel,
        grid=(t // tm,),
        in_specs=[_row_spec(tm, D_MODEL), _row_spec(tm, GATE_COLS), _const_spec((1, GATE_COLS)),
                  _row_spec(tm, 512), _row_spec(tm, 512), _row_spec(tm, 512),
                  _const_spec((512, D_MODEL)), _const_spec((512, D_MODEL)), _const_spec((512, D_MODEL)),
                  _const_spec((D_MODEL, D_MODEL)), _const_spec((1, D_MODEL))],
        out_specs=[_row_spec(tm, D_MODEL), _row_spec(tm, D_MODEL)],
        out_shape=[jax.ShapeDtypeStruct((t, D_MODEL), F32), jax.ShapeDtypeStruct((t, D_MODEL), BF16)],
        compiler_params=_params(),
        name="merge",
    )(x, p_gate, b_gate, o_r, o_m, o_d, w_r, w_m, w_d, w_o, g_ffn)


def _ffn_kernel(x_ref, h_ref, wup_ref, cw_ref, cb_ref, wdn_ref, gfin_ref, o_ref, carry_ref, *, tiles_per_seq, final_norm):
    tm = TM_FFN

    @pl.when(pl.program_id(0) % tiles_per_seq == 0)
    def _():
        carry_ref[0:8, :] = jnp.zeros((8, carry_ref.shape[1]), F32)

    h = h_ref[...]

    def conv(u, cols):
        carry_ref[8:, cols] = u
        u1 = carry_ref[7:7 + tm, cols]
        u2 = carry_ref[6:6 + tm, cols]
        carry_ref[0:8, cols] = u[tm - 8:tm, :]
        return cw_ref[0:1, cols] * u2 + cw_ref[1:2, cols] * u1 + cw_ref[2:3, cols] * u + cb_ref[:, cols]

    def up(ck):
        gc = slice(ck * FF_CHUNK, (ck + 1) * FF_CHUNK)
        vc = slice(D_FF + ck * FF_CHUNK, D_FF + (ck + 1) * FF_CHUNK)
        return _dot(h, wup_ref[:, gc]), _dot(h, wup_ref[:, vc])

    n_chunks = D_FF // FF_CHUNK
    acc = x_ref[...]
    nxt = up(0)
    acts = []
    for ck in range(n_chunks):
        ug, uv = nxt
        if ck + 1 < n_chunks:
            nxt = up(ck + 1)
        gc = slice(ck * FF_CHUNK, (ck + 1) * FF_CHUNK)
        vc = slice(D_FF + ck * FF_CHUNK, D_FF + (ck + 1) * FF_CHUNK)
        gate = conv(ug, gc)
        val = conv(uv, vc)
        acts.append((gate * jax.nn.sigmoid(gate) * val).astype(BF16))
        if len(acts) == FF_DOWN_GROUP or ck + 1 == n_chunks:
            rows = slice((ck + 1 - len(acts)) * FF_CHUNK, (ck + 1) * FF_CHUNK)
            acc = acc + _dot(jnp.concatenate(acts, axis=-1), wdn_ref[rows, :])
            acts = []
    if final_norm:
        acc = _rms(acc, gfin_ref[...], NORM_EPS)
    o_ref[...] = acc


def _ffn(x1, h2, w_up, conv_w, conv_b, w_down, g_final, seq, final_norm):
    t = x1.shape[0]
    tm = TM_FFN
    kern = functools.partial(_ffn_kernel, tiles_per_seq=seq // tm, final_norm=final_norm)
    return pl.pallas_call(
        kern,
        grid=(t // tm,),
        in_specs=[_row_spec(tm, D_MODEL), _row_spec(tm, D_MODEL), _const_spec((D_MODEL, 2 * D_FF)),
                  _const_spec((3, 2 * D_FF)), _const_spec((1, 2 * D_FF)), _const_spec((D_FF, D_MODEL)),
                  _const_spec((1, D_MODEL))],
        out_specs=_row_spec(tm, D_MODEL),
        out_shape=jax.ShapeDtypeStruct((t, D_MODEL), F32),
        scratch_shapes=[pltpu.VMEM((8 + tm, 2 * D_FF), F32)],
        compiler_params=_params(),
        name="conv_ffn",
    )(x1, h2, w_up, conv_w, conv_b, w_down, g_final)


def _mla_weights(w_uq, w_ukv):
    qd = MLA_NOPE + MLA_ROPE
    half = MLA_ROPE // 2
    wq = w_uq.reshape(MLA_Q_LORA, MLA_HEADS, qd)
    zq = jnp.zeros((MLA_Q_LORA, MLA_HEADS, LANES - qd), F32)
    wq_main = jnp.concatenate([wq, zq], axis=-1)
    x1 = wq[:, :, MLA_NOPE:MLA_NOPE + half]
    x2 = wq[:, :, MLA_NOPE + half:]
    wq_rot = jnp.concatenate([jnp.zeros((MLA_Q_LORA, MLA_HEADS, MLA_NOPE), F32), -x2, x1, zq], axis=-1)
    wkv = w_ukv.reshape(MLA_KV_LORA, MLA_HEADS, MLA_NOPE + MLA_V)
    zkv = jnp.zeros((MLA_KV_LORA, MLA_HEADS, LANES - MLA_NOPE), F32)
    wk = jnp.concatenate([wkv[:, :, :MLA_NOPE], zkv], axis=-1)
    wv = jnp.concatenate([wkv[:, :, MLA_NOPE:], jnp.zeros((MLA_KV_LORA, MLA_HEADS, MLA_VT_ROWS - MLA_V), F32)], axis=-1)
    flat = lambda w: w.reshape(w.shape[0], -1).astype(BF16)
    return flat(wq_main), flat(wq_rot), flat(wk), flat(wv).T


def kernel(x, positions, rel_bias, norm_mix, w_in, b_gate, rwkv_mu, rwkv_w0, rwkv_w2, rwkv_a0, rwkv_a2, rwkv_g2, rwkv_k_k, rwkv_k_a, rwkv_r_k, rwkv_ln_w, rwkv_ln_b, mla_q_norm, mla_w_uq, mla_kv_norm, mla_w_ukv, diff_lambda, diff_subln, w_branch_rwkv, w_branch_mla, w_branch_diff, w_o, norm_ffn, ffn_w_up, ffn_conv_w, ffn_conv_b, ffn_w_down, norm_final):
    batch, seq, _ = x.shape
    depth = w_in.shape[0]
    t = batch * seq
    assert seq % TQ == 0 and seq % TM == 0 and seq % TM_FFN == 0 and seq % RWKV_TILE == 0 and RWKV_TILE % CHUNK == 0
    assert (batch * seq) % TM_MERGE == 0
    xf = x.reshape(t, D_MODEL)
    pos_col = positions.reshape(t, 1)
    inv_freq = ROPE_BASE ** (-jnp.arange(0, MLA_ROPE, 2, dtype=F32) / MLA_ROPE)
    freq = jnp.concatenate([jnp.zeros((MLA_NOPE,), F32), inv_freq, inv_freq,
                            jnp.zeros((LANES - MLA_NOPE - MLA_ROPE,), F32)]).reshape(1, LANES)
    row = lambda v: v.reshape(1, -1)
    diff_scale = jnp.concatenate([jnp.full((512,), DIFF_QK ** -0.5 * LOG2E, F32), jnp.ones((512,), F32)])

    for l in range(depth):
        s0, s1, s2 = RWKV_COLS, RWKV_COLS + MLA_COLS, RWKV_COLS + MLA_COLS + DIFF_COLS
        sv = s1 + DIFF_QK_COLS
        w = w_in[l]
        w_r = w[:, :s0].astype(BF16)
        w_m = jnp.pad(w[:, s0:s1], ((0, 0), (0, MLA_PAD - MLA_COLS))).astype(BF16)
        w_qk = (w[:, s1:sv] * diff_scale).astype(BF16)
        w_vt = w[:, sv:s2].T.astype(BF16)
        w_g = w[:, s2:].astype(BF16)
        wq, wqr, wk, wv = _mla_weights(mla_w_uq[l], mla_w_ukv[l])
        p_rwkv, q_m, k_m, v_m, p_qk, p_vt, p_gate = _in_proj(
            xf, row(norm_mix[l]), w_r, w_m, w_qk, w_g, w_vt, pos_col, freq, row(mla_q_norm[l]), row(mla_kv_norm[l]),
            wq, wqr, wk, wv)

        o_r = _rwkv(p_rwkv, seq, row(rwkv_mu[l]), row(rwkv_w0[l]), rwkv_w2[l].astype(BF16), row(rwkv_a0[l]),
                    rwkv_a2[l].astype(BF16), rwkv_g2[l].astype(BF16), row(rwkv_k_k[l]), row(rwkv_k_a[l]),
                    row(rwkv_r_k[l]), row(rwkv_ln_w[l]), row(rwkv_ln_b[l]))

        o_m = _mla_attn(q_m, k_m, v_m, batch, seq)

        o_d = _diff_attn(p_qk, p_vt, rel_bias, diff_lambda[l], row(diff_subln[l]), batch, seq, l)

        x1, h2 = _merge(xf, p_gate, row(b_gate[l]), o_r, o_m, o_d, w_branch_rwkv[l].astype(BF16),
                        w_branch_mla[l].astype(BF16), w_branch_diff[l].astype(BF16), w_o[l].astype(BF16),
                        row(norm_ffn[l]))
        xf = _ffn(x1, h2, ffn_w_up[l].astype(BF16), ffn_conv_w[l], row(ffn_conv_b[l]), ffn_w_down[l].astype(BF16),
                  row(norm_final), seq, final_norm=(l == depth - 1))
    return xf.reshape(batch, seq, D_MODEL)
```

```python
import functools
import math

import numpy as np
import jax
import jax.numpy as jnp
from jax import lax
from jax.experimental import pallas as pl
from jax.experimental.pallas import tpu as pltpu

F32 = jnp.float32
BF16 = jnp.bfloat16

D_MODEL = 1024
RWKV_HEADS = 8
RWKV_N = 64
RWKV_DIM = 512
RWKV_COLS = 1792
RWKV_GN_EPS = 64e-5
MLA_HEADS = 8
MLA_Q_LORA = 256
MLA_KV_LORA = 128
MLA_NOPE = 64
MLA_ROPE = 32
MLA_V = 64
MLA_VT_ROWS = MLA_V + 16
MLA_COLS = 416
MLA_PAD = 512
ROPE_BASE = 10000.0
DIFF_HEADS = 4
DIFF_QK = 64
DIFF_V = 128
DIFF_COLS = 1536
DIFF_QK_COLS = 1024
DIFF_VT_ROWS = DIFF_V + 16
REL_BUCKETS = 32
REL_MAX_DISTANCE = 128
D_FF = 2816
GATE_COLS = 3072
NORM_EPS = 1e-6
SUBLN_EPS = 1e-5

LANES = 128
VMEM_LIMIT = 58 * 1024 * 1024
TM = 512
TM_FFN = 256
TM_MERGE = 512
CHUNK = 64
RWKV_TILE = 256
TQ = 512
KEY_TILES_PER_STEP = 2
FF_CHUNK = 256
FF_DOWN_GROUP = 11
NEG = -1e30
LOG2E = 1.4426950408889634

_NT = (((1,), (1,)), ((), ()))
_TN = (((0,), (0,)), ((), ()))


def _dot(a, b):
    return jnp.dot(a, b, preferred_element_type=F32)


def _dot_nt(a, b):
    return lax.dot_general(a, b, _NT, preferred_element_type=F32)


def _dot_tn(a, b):
    return lax.dot_general(a, b, _TN, preferred_element_type=F32)


def _rms(x, g, eps):
    return x * lax.rsqrt(jnp.mean(x * x, axis=-1, keepdims=True) + eps) * g


def _params():
    return pltpu.CompilerParams(dimension_semantics=("arbitrary",), vmem_limit_bytes=VMEM_LIMIT)


def _params2():
    return pltpu.CompilerParams(dimension_semantics=("arbitrary", "arbitrary"), vmem_limit_bytes=VMEM_LIMIT)


def _const_spec(shape):
    return pl.BlockSpec(shape, lambda *_: (0,) * len(shape), pipeline_mode=pl.Buffered(1))


def _row_spec(tm, cols, col_block=0):
    return pl.BlockSpec((tm, cols), lambda i: (i, col_block))


def _in_proj_kernel(x_ref, g_ref, wr_ref, wm_ref, wqk_ref, wg_ref, wvt_ref, pos_ref, freq_ref, qn_ref, kvn_ref, wq_ref,
                    wqr_ref, wk_ref, wv_ref, pr_ref, mq_ref, mk_ref, mvt_ref, pqk_ref, pvt_ref, pg_ref):
    h = _rms(x_ref[...], g_ref[...], NORM_EPS).astype(BF16)
    p_mla = _dot(h, wm_ref[...])
    pqk_ref[...] = _dot(h, wqk_ref[...]).astype(BF16)
    pg_ref[...] = _dot(h, wg_ref[...]).astype(BF16)
    _mla_prep_body(p_mla, pos_ref, freq_ref, qn_ref, kvn_ref, wq_ref, wqr_ref, wk_ref, wv_ref, mq_ref, mk_ref, mvt_ref)
    pr_ref[...] = _dot(h, wr_ref[...])
    pvt_ref[...] = _dot_nt(wvt_ref[...], h).astype(BF16)


def _in_proj(x, g, w_r, w_m, w_qk, w_g, w_vt, pos_col, freq, q_norm, kv_norm, wq, wqr, wk, wv):
    t = x.shape[0]
    vw = DIFF_HEADS * DIFF_V
    hw = MLA_HEADS * LANES
    mvw = MLA_HEADS * MLA_VT_ROWS
    return pl.pallas_call(
        _in_proj_kernel,
        grid=(t // TM,),
        in_specs=[_row_spec(TM, D_MODEL), _const_spec((1, D_MODEL)), _const_spec((D_MODEL, RWKV_COLS)),
                  _const_spec((D_MODEL, MLA_PAD)), _const_spec((D_MODEL, DIFF_QK_COLS)), _const_spec((D_MODEL, GATE_COLS)),
                  _const_spec((vw, D_MODEL)), _row_spec(TM, 1), _const_spec((1, LANES)), _const_spec((1, MLA_Q_LORA)),
                  _const_spec((1, MLA_KV_LORA)), _const_spec((MLA_Q_LORA, hw)), _const_spec((MLA_Q_LORA, hw)),
                  _const_spec((MLA_KV_LORA, hw)), _const_spec((mvw, MLA_KV_LORA))],
        out_specs=[_row_spec(TM, RWKV_COLS), _row_spec(TM, hw), _row_spec(TM, hw),
                   pl.BlockSpec((mvw, TM), lambda i: (0, i)), _row_spec(TM, DIFF_QK_COLS),
                   pl.BlockSpec((vw, TM), lambda i: (0, i)), _row_spec(TM, GATE_COLS)],
        out_shape=[jax.ShapeDtypeStruct((t, RWKV_COLS), F32), jax.ShapeDtypeStruct((t, hw), BF16),
                   jax.ShapeDtypeStruct((t, hw), BF16), jax.ShapeDtypeStruct((mvw, t), BF16),
                   jax.ShapeDtypeStruct((t, DIFF_QK_COLS), BF16), jax.ShapeDtypeStruct((vw, t), BF16),
                   jax.ShapeDtypeStruct((t, GATE_COLS), BF16)],
        compiler_params=_params(),
        name="in_proj",
    )(x, g, w_r, w_m, w_qk, w_g, w_vt, pos_col, freq, q_norm, kv_norm, wq, wqr, wk, wv)


def _split3(a):
    hi = a.astype(BF16)
    r1 = a - hi.astype(F32)
    mid = r1.astype(BF16)
    lo = (r1 - mid.astype(F32)).astype(BF16)
    return hi, mid, lo


def _softplus(z):
    return jnp.maximum(z, 0.0) + jnp.log(1.0 + jnp.exp(-jnp.abs(z)))


def _rwkv_kernel(p_ref, mu_ref, w0_ref, w2_ref, a0_ref, a2_ref, g2_ref, kk_ref, ka_ref, rk_ref, lnw_ref, lnb_ref,
                 o_ref, carry_ref, state_ref, *, tiles_per_seq):
    c = CHUNK
    n = RWKV_N
    ts = RWKV_TILE
    nc = ts // c

    @pl.when(pl.program_id(0) % tiles_per_seq == 0)
    def _():
        carry_ref[...] = jnp.zeros_like(carry_ref)
        state_ref[...] = jnp.zeros_like(state_ref)

    p = p_ref[...]
    row = lax.broadcasted_iota(jnp.int32, (ts, 1), 0)
    shifted = jnp.where(row == 0, carry_ref[...], pltpu.roll(p, 1, 0))
    carry_ref[...] = p[ts - 1:ts, :]
    pm = p + (shifted - p) * mu_ref[...]

    r = pm[:, 0:512]
    k = pm[:, 512:1024]
    v = pm[:, 1024:1536]
    pw = pm[:, 1536:1600]
    pa = pm[:, 1600:1664]
    pg = pm[:, 1664:1792]

    w_log = -_softplus(-(w0_ref[...] + _dot(jnp.tanh(pw).astype(BF16), w2_ref[...]))) - 0.5
    logd = -jnp.exp(w_log)
    a = jax.nn.sigmoid(a0_ref[...] + _dot(pa.astype(BF16), a2_ref[...]))
    g = _dot(jax.nn.sigmoid(pg).astype(BF16), g2_ref[...])

    tr = lax.broadcasted_iota(jnp.int32, (ts, ts), 0)
    tc = lax.broadcasted_iota(jnp.int32, (ts, ts), 1)
    tri = ((tr >= tc) & (tr // c == tc // c)).astype(BF16)
    hi, mid, lo = _split3(logd)
    cs = _dot(tri, hi) + _dot(tri, mid) + _dot(tri, lo)
    total = jnp.concatenate([jnp.broadcast_to(cs[(m + 1) * c - 1:(m + 1) * c, :], (c, RWKV_DIM)) for m in range(nc)], axis=0)
    e_in = jnp.exp(cs)
    e_ex = jnp.exp(cs - logd)
    e_inv = jnp.exp(-cs)
    e_end = jnp.exp(total - cs)
    g_end = jnp.exp(total)

    n_pairs = RWKV_HEADS // 2
    lo_t = lax.broadcasted_iota(jnp.int32, (ts, LANES), 1) < n
    lo_c = lax.broadcasted_iota(jnp.int32, (c, LANES), 1) < n

    def head_sum(x):
        blocks = []
        for q in range(n_pairs):
            xb = x[:, q * LANES:(q + 1) * LANES]
            s_lo = jnp.sum(jnp.where(lo_t, xb, 0.0), axis=-1, keepdims=True)
            s_hi = jnp.sum(jnp.where(lo_t, 0.0, xb), axis=-1, keepdims=True)
            blocks.append(jnp.where(lo_t, s_lo, s_hi))
        return jnp.concatenate(blocks, axis=-1)

    k2 = k * (1.0 + (a - 1.0) * ka_ref[...])
    kku = k * kk_ref[...]
    kkn = kku / jnp.maximum(jnp.sqrt(head_sum(kku * kku)), 1e-12)
    b = kkn * a
    full = {"at": -kkn * e_ex, "rt": r * e_in, "bt": b * e_inv, "kt": k2 * e_inv, "v": v,
            "be": b * e_end, "ke": k2 * e_end}

    ti = lax.broadcasted_iota(jnp.int32, (2 * c, 4 * c), 0) % c
    si = lax.broadcasted_iota(jnp.int32, (2 * c, 4 * c), 1) % c
    strict = ti > si
    incl = ti >= si
    eye = (lax.broadcasted_iota(jnp.int32, (2 * c, 2 * c), 0)
           == lax.broadcasted_iota(jnp.int32, (2 * c, 2 * c), 1)).astype(F32)

    items = [(m, q) for m in range(nc) for q in range(n_pairs)]

    def stacked(name, m, q):
        xb = full[name][m * c:(m + 1) * c, q * LANES:(q + 1) * LANES]
        return jnp.concatenate([jnp.where(lo_c, xb, 0.0), jnp.where(lo_c, 0.0, xb)], axis=0).astype(BF16)

    st_ops = {name: [stacked(name, m, q) for m, q in items] for name in full}
    big_l = [_dot_nt(jnp.concatenate([at, rtb], axis=0), jnp.concatenate([bt, kt], axis=0))
             for at, rtb, bt, kt in zip(st_ops["at"], st_ops["rt"], st_ops["bt"], st_ops["kt"])]
    top_l = [jnp.where(strict, big[:2 * c, :], 0.0) for big in big_l]
    bot_l = [jnp.where(incl, big[2 * c:, :], 0.0).astype(BF16) for big in big_l]
    lakv_l = [_dot(top[:, 2 * c:].astype(BF16), vb) for top, vb in zip(top_l, st_ops["v"])]

    x_l = [top[:, :2 * c] for top in top_l]
    tinv_l = [eye + x for x in x_l]
    xb_l = [x.astype(BF16) for x in x_l]
    for _ in range(int(math.log2(c)) - 1):
        x_l = [_dot(xb, xb) for xb in xb_l]
        xb_l = [x.astype(BF16) for x in x_l]
        tinv_l = [tinv + _dot(tinv.astype(BF16), xb) for tinv, xb in zip(tinv_l, xb_l)]
    tinvb_l = [tinv.astype(BF16) for tinv in tinv_l]
    abar_l = [_dot(tb, at).astype(BF16) for tb, at in zip(tinvb_l, st_ops["at"])]
    vbar_l = [_dot(tb, lv.astype(BF16)).astype(BF16) for tb, lv in zip(tinvb_l, lakv_l)]
    rhat_l = [(rtb.astype(F32) + _dot(bot[:, :2 * c], ab)).astype(BF16)
              for rtb, bot, ab in zip(st_ops["rt"], bot_l, abar_l)]
    uv_l = [jnp.concatenate([vbar, vb], axis=0) for vbar, vb in zip(vbar_l, st_ops["v"])]
    y0_l = [_dot(bot, uv) for bot, uv in zip(bot_l, uv_l)]
    p_l = [_dot_tn(ab, be).astype(BF16) for ab, be in zip(abar_l, st_ops["be"])]
    q_l = [_dot_tn(uv, jnp.concatenate([be, ke], axis=0)) for uv, be, ke in zip(uv_l, st_ops["be"], st_ops["ke"])]

    st_l = [state_ref[q] for q in range(n_pairs)]
    y_rows = []
    for m in range(nc):
        y_blocks = []
        for q in range(n_pairs):
            idx = m * n_pairs + q
            st = st_l[q]
            st_b = st.astype(BF16)
            y_s = _dot_nt(rhat_l[idx], st_b) + y0_l[idx]
            y_blocks.append(y_s[:c] + y_s[c:])
            decay = g_end[(m + 1) * c - 1:(m + 1) * c, q * LANES:(q + 1) * LANES]
            st_l[q] = st * decay + _dot(st_b, p_l[idx]) + q_l[idx]
        y_rows.append(jnp.concatenate(y_blocks, axis=-1))
    for q in range(n_pairs):
        state_ref[q] = st_l[q]
    y = jnp.concatenate(y_rows, axis=0)

    inv_n = 1.0 / n
    yc = y - head_sum(y) * inv_n
    var = head_sum(yc * yc) * inv_n
    yn = yc * lax.rsqrt(var + RWKV_GN_EPS) * lnw_ref[...] + lnb_ref[...]
    bonus = head_sum(r * k2 * rk_ref[...]) * v
    o_ref[...] = ((yn + bonus) * g).astype(BF16)


def _rwkv(p_rwkv, seq, mu, w0, w2, a0, a2, g2, k_k, k_a, r_k, ln_w, ln_b):
    t = p_rwkv.shape[0]
    ts = RWKV_TILE
    kern = functools.partial(_rwkv_kernel, tiles_per_seq=seq // ts)
    vec = lambda: _const_spec((1, RWKV_DIM))
    return pl.pallas_call(
        kern,
        grid=(t // ts,),
        in_specs=[_row_spec(ts, RWKV_COLS), _const_spec((1, RWKV_COLS)), vec(), _const_spec((64, RWKV_DIM)), vec(),
                  _const_spec((64, RWKV_DIM)), _const_spec((128, RWKV_DIM)), vec(), vec(), vec(), vec(), vec()],
        out_specs=_row_spec(ts, RWKV_DIM),
        out_shape=jax.ShapeDtypeStruct((t, RWKV_DIM), BF16),
        scratch_shapes=[pltpu.VMEM((1, RWKV_COLS), F32), pltpu.VMEM((RWKV_HEADS // 2, LANES, LANES), F32)],
        compiler_params=_params(),
        name="rwkv7",
    )(p_rwkv, mu, w0, w2, a0, a2, g2, k_k, k_a, r_k, ln_w, ln_b)


def _mla_prep_body(p, pos_ref, freq_ref, qn_ref, kvn_ref, wq_ref, wqr_ref, wk_ref, wv_ref, q_out, k_out, v_out):
    hq = _rms(p[:, 0:MLA_Q_LORA], qn_ref[...], NORM_EPS).astype(BF16)
    hkv = _rms(p[:, MLA_Q_LORA:MLA_Q_LORA + MLA_KV_LORA], kvn_ref[...], NORM_EPS).astype(BF16)
    blk = p[:, 384:512]
    ang = pos_ref[...].astype(F32) * freq_ref[...]
    cos = jnp.cos(ang)
    sin = jnp.sin(ang)
    scale = (MLA_NOPE + MLA_ROPE) ** -0.5 * LOG2E
    qa = _dot(hq, wq_ref[...])
    qr = _dot(hq, wqr_ref[...])
    lane = lax.broadcasted_iota(jnp.int32, blk.shape, 1)
    kr = pltpu.roll(blk, 64, 1)
    rot = jnp.where(lane < 80, -pltpu.roll(blk, 48, 1), pltpu.roll(blk, 80, 1))
    rot = jnp.where((lane >= 64) & (lane < 96), rot, 0.0)
    krope = kr * cos + rot * sin
    kn = _dot(hkv, wk_ref[...])
    for h in range(MLA_HEADS):
        hs = slice(h * LANES, (h + 1) * LANES)
        q_out[:, hs] = ((qa[:, hs] * cos + qr[:, hs] * sin) * scale).astype(BF16)
        k_out[:, hs] = (kn[:, hs] + krope).astype(BF16)
    vt = _dot_nt(wv_ref[...], hkv)
    rowi = lax.broadcasted_iota(jnp.int32, vt.shape, 0)
    v_out[...] = (vt + (rowi % MLA_VT_ROWS == MLA_V).astype(F32)).astype(BF16)


def _softmax_stage_t(s_l, pv_fn, m_scr, acc_scr, idx, s_scr=None, l_scr=None):
    n = len(s_l)
    m_old = [m_scr[c][0:1, :] for c in idx]
    m_new = [jnp.maximum(m, jnp.max(s, axis=0, keepdims=True)) for m, s in zip(m_old, s_l)]
    alpha = [jnp.exp2(mo - mn) for mo, mn in zip(m_old, m_new)]
    if s_scr is not None:
        for c in range(n):
            s_scr[c] = s_l[c]
        s_l = [s_scr[c] for c in range(n)]
    pv_l = []
    for c in range(n):
        p = jnp.exp2(s_l[c] - m_new[c])
        if l_scr is not None:
            l_new = alpha[c] * l_scr[idx[c]][0:1, :] + jnp.sum(p, axis=0, keepdims=True)
            l_scr[idx[c]] = jnp.broadcast_to(l_new, l_scr.shape[1:])
        pv_l.append(pv_fn(c, p.astype(BF16)))
    for c in range(n):
        m_scr[idx[c]] = jnp.broadcast_to(m_new[c], m_scr.shape[1:])
        acc_scr[idx[c]] = alpha[c] * acc_scr[idx[c]] + pv_l[c]


def _mla_attn_kernel(q_ref, k_ref, vt_ref, o_ref, m_scr, acc_scr, s_scr):
    tq = TQ
    i = pl.program_id(1)
    causal = lax.broadcasted_iota(jnp.int32, (tq, tq), 0) <= lax.broadcasted_iota(jnp.int32, (tq, tq), 1)
    m_scr[...] = jnp.full(m_scr.shape, NEG, F32)
    acc_scr[...] = jnp.zeros(acc_scr.shape, F32)
    hsl = [slice(h * LANES, (h + 1) * LANES) for h in range(MLA_HEADS)]
    vsl =[slice(h * MLA_VT_ROWS, (h + 1) * MLA_VT_ROWS) for h in range(MLA_HEADS)]
    heads = list(range(MLA_HEADS))

    def tile(off, tk, mask):
        s_l = [_dot_nt(k_ref[pl.ds(off, tk), hs], q_ref[:, hs]) for hs in hsl]
        if mask:
            s_l = [jnp.where(causal, s, NEG) for s in s_l]
        _softmax_stage_t(s_l, lambda c, p: _dot(vt_ref[vsl[c], pl.ds(off, tk)], p), m_scr, acc_scr, heads,
                         s_scr.at[:, 0:tk, :])

    def pair_body(j, carry):
        tile(pl.multiple_of(j * 2 * tq, 2 * tq), 2 * tq, False)
        return carry

    def single_body(j, carry):
        tile(pl.multiple_of(j * tq, tq), tq, False)
        return carry

    n_pairs = i // 2 if KEY_TILES_PER_STEP == 2 else 0
    if KEY_TILES_PER_STEP == 2:
        lax.fori_loop(0, n_pairs, pair_body, 0)
    lax.fori_loop(2 * n_pairs, i, single_body, 0)
    tile(pl.multiple_of(i * tq, tq), tq, True)
    for pr in range(MLA_HEADS // 2):
        accs = [acc_scr[2 * pr + hh] for hh in range(2)]
        ot = jnp.concatenate([acc[:MLA_V, :] / acc[MLA_V:MLA_V + 1, :] for acc in accs], axis=0)
        o_ref[:, pr * LANES:(pr + 1) * LANES] = jnp.transpose(ot).astype(BF16)


def _mla_attn(q, k, vt, batch, seq):
    t = q.shape[0]
    nq = seq // TQ
    hw = MLA_HEADS * LANES
    vw = MLA_HEADS * MLA_V
    return pl.pallas_call(
        _mla_attn_kernel,
        grid=(batch, nq),
        in_specs=[pl.BlockSpec((TQ, hw), lambda b, i: (b * nq + i, 0)),
                  pl.BlockSpec((seq, hw), lambda b, i: (b, 0)),
                  pl.BlockSpec((MLA_HEADS * MLA_VT_ROWS, seq), lambda b, i: (0, b))],
        out_specs=pl.BlockSpec((TQ, vw), lambda b, i: (b * nq + i, 0)),
        out_shape=jax.ShapeDtypeStruct((t, vw), BF16),
        scratch_shapes=[pltpu.VMEM((MLA_HEADS, 8, TQ), F32), pltpu.VMEM((MLA_HEADS, MLA_VT_ROWS, TQ), F32),
                        pltpu.VMEM((MLA_HEADS, KEY_TILES_PER_STEP * TQ, TQ), F32)],
        compiler_params=_params2(),
        name="mla_attn",
    )(q, k, vt)


def _t5_bucket_table():
    n = np.arange(0, REL_MAX_DISTANCE + 1)
    max_exact = REL_BUCKETS // 2
    nf = np.maximum(n, 1).astype(np.float32)
    ratio = np.log(nf / np.float32(max_exact)) / np.float32(math.log(REL_MAX_DISTANCE / max_exact))
    large = max_exact + (ratio * np.float32(REL_BUCKETS - max_exact)).astype(np.int32)
    large = np.minimum(large, REL_BUCKETS - 1)
    return np.where(n < max_exact, n, large)


_BUCKETS = _t5_bucket_table()
_FAR_BUCKET = int(_BUCKETS[REL_MAX_DISTANCE])
assert _FAR_BUCKET == REL_BUCKETS - 1 and np.all(np.diff(_BUCKETS) >= 0)
_BUCKET_STARTS = [int(np.argmax(_BUCKETS >= b)) for b in range(REL_BUCKETS // 2 + 1, REL_BUCKETS)]


def _diff_attn_kernel(rb_ref, lam_ref, subln_ref, q_ref, k_ref, vt_ref, o_ref, bias_scr, m_scr, acc_scr, s_scr, *,
                      lambda_init):
    tq = TQ
    i = pl.program_id(1)
    r_i = lax.broadcasted_iota(jnp.int32, (tq, tq), 0)
    c_i = lax.broadcasted_iota(jnp.int32, (tq, tq), 1)
    causal = r_i <= c_i

    @pl.when((pl.program_id(0) == 0) & (i == 0))
    def _():
        for t_idx, delta in enumerate((0, tq)):
            d = jnp.maximum(c_i - r_i + delta, 0)
            log_b = REL_BUCKETS // 2
            for start in _BUCKET_STARTS:
                log_b = log_b + (d >= start).astype(jnp.int32)
            bucket = jnp.where(d < REL_BUCKETS // 2, d, log_b)
            for h in range(DIFF_HEADS):
                bias = jnp.zeros((tq, tq), F32)
                for b in range(REL_BUCKETS):
                    bias = jnp.where(bucket == b, rb_ref[b, h], bias)
                bias_scr[t_idx, h] = (bias - rb_ref[_FAR_BUCKET, h]) * LOG2E

    m_scr[...] = jnp.full(m_scr.shape, NEG, F32)
    acc_scr[...] = jnp.zeros(acc_scr.shape, F32)
    lane = lax.broadcasted_iota(jnp.int32, (tq, LANES), 1)
    ones_rows = {tk: (lax.broadcasted_iota(jnp.int32, (DIFF_VT_ROWS - DIFF_V, tk), 0) == 0).astype(BF16)
                 for tk in (tq, 2 * tq)}
    hsl = [slice(h * LANES, (h + 1) * LANES) for h in range(DIFF_HEADS)]
    heads = list(range(DIFF_HEADS))
    q_l = []
    for hs in hsl:
        qp = q_ref[:, hs]
        q_l.append(jnp.concatenate([jnp.where(lane < DIFF_QK, qp, jnp.zeros_like(qp)),
                                    jnp.where(lane >= DIFF_QK, qp, jnp.zeros_like(qp))], axis=0))

    def tile(off, tk, bias_idx, mask):
        s_l = [_dot_nt(k_ref[pl.ds(off, tk), hsl[h]], q_l[h]) for h in heads]
        if bias_idx is not None:
            s_l = [s + jnp.concatenate([bias_scr[bias_idx, h]] * 2, axis=1) for h, s in enumerate(s_l)]
        if mask:
            mask2 = jnp.concatenate([causal, causal], axis=1)
            s_l = [jnp.where(mask2, s, NEG) for s in s_l]

        def pv(c, p):
            vt = jnp.concatenate([vt_ref[hsl[c], pl.ds(off, tk)], ones_rows[tk]], axis=0)
            return jnp.concatenate([_dot(vt, p[:, :tq]), _dot(vt, p[:, tq:])], axis=1)

        _softmax_stage_t(s_l, pv, m_scr, acc_scr, heads, s_scr.at[:, 0:tk, :])

    def far_pair_body(j, carry):
        tile(pl.multiple_of(j * 2 * tq, 2 * tq), 2 * tq, None, False)
        return carry

    def far_body(j, carry):
        tile(pl.multiple_of(j * tq, tq), tq, None, False)
        return carry

    def near_body(j, carry):
        tile(pl.multiple_of(j * tq, tq), tq, 1, False)
        return carry

    n_far = jnp.maximum(i - 1, 0)
    n_pairs = n_far // 2 if KEY_TILES_PER_STEP == 2 else 0
    if KEY_TILES_PER_STEP == 2:
        lax.fori_loop(0, n_pairs, far_pair_body, 0)
    lax.fori_loop(2 * n_pairs, n_far, far_body, 0)
    lax.fori_loop(n_far, i, near_body, 0)
    tile(pl.multiple_of(i * tq, tq), tq, 0, True)

    lam = lam_ref[...]
    lam_full = (jnp.exp(jnp.sum(lam[0:1] * lam[1:2], axis=-1, keepdims=True))
                - jnp.exp(jnp.sum(lam[2:3] * lam[3:4], axis=-1, keepdims=True)) + lambda_init)
    for h in heads:
        acc = acc_scr[h]
        on = acc[:DIFF_V, :] / acc[DIFF_V:DIFF_V + 1, :]
        ot = on[:, :tq] - lam_full * on[:, tq:]
        ot = ot * lax.rsqrt(jnp.mean(ot * ot, axis=0, keepdims=True) + SUBLN_EPS)
        o_ref[:, hsl[h]] = (jnp.transpose(ot) * subln_ref[...] * (1.0 - lambda_init)).astype(BF16)


def _diff_attn(p_qk, p_vt, rel_bias, lam, subln, batch, seq, layer_idx):
    t = p_qk.shape[0]
    nq = seq // TQ
    w = DIFF_HEADS * LANES
    lambda_init = 0.8 - 0.6 * math.exp(-0.3 * layer_idx)
    kern = functools.partial(_diff_attn_kernel, lambda_init=lambda_init)
    const2 = lambda shape: pl.BlockSpec(shape, lambda b, i: (0, 0))
    return pl.pallas_call(
        kern,
        grid=(batch, nq),
        in_specs=[pl.BlockSpec(memory_space=pltpu.SMEM),
                  const2((4, DIFF_QK)), const2((1, DIFF_V)),
                  pl.BlockSpec((TQ, w), lambda b, i: (b * nq + i, 0)),
                  pl.BlockSpec((seq, w), lambda b, i: (b, 1)),
                  pl.BlockSpec((w, seq), lambda b, i: (0, b))],
        out_specs=pl.BlockSpec((TQ, w), lambda b, i: (b * nq + i, 0)),
        out_shape=jax.ShapeDtypeStruct((t, w), BF16),
        scratch_shapes=[pltpu.VMEM((2, DIFF_HEADS, TQ, TQ), F32), pltpu.VMEM((DIFF_HEADS, 8, 2 * TQ), F32),
                        pltpu.VMEM((DIFF_HEADS, DIFF_VT_ROWS, 2 * TQ), F32),
                        pltpu.VMEM((DIFF_HEADS, KEY_TILES_PER_STEP * TQ, 2 * TQ), F32)],
        compiler_params=_params2(),
        name="diff_attn",
    )(rel_bias, lam, subln, p_qk, p_qk, p_vt)


def _merge_kernel(x_ref, pg_ref, bg_ref, or_ref, om_ref, od_ref, wr_ref, wm_ref, wd_ref, wo_ref, gf_ref, x_out, h_out):
    halves = [slice(s * (TM_MERGE // 2), (s + 1) * (TM_MERGE // 2)) for s in range(2)]
    branch = [[_dot(o_ref[rs, :], w_ref[...]) for o_ref, w_ref in ((or_ref, wr_ref), (om_ref, wm_ref), (od_ref, wd_ref))]
              for rs in halves]
    for rs, d in zip(halves, branch):
        merged = None
        for idx in range(3):
            cs = slice(idx * D_MODEL, (idx + 1) * D_MODEL)
            term = jax.nn.sigmoid(pg_ref[rs, cs].astype(F32) + bg_ref[:, cs]) * d[idx]
            merged = term if merged is None else merged + term
        x1 = x_ref[rs, :] + _dot(merged.astype(BF16), wo_ref[...])
        x_out[rs, :] = x1
        h_out[rs, :] = _rms(x1, gf_ref[...], NORM_EPS).astype(BF16)


def _merge(x, p_gate, b_gate, o_r, o_m, o_d, w_r, w_m, w_d, w_o, g_ffn):
    t = x.shape[0]
    tm = TM_MERGE
    return pl.pallas_call(
        _merge_kernel,
        grid=(t // tm,),
        in_specs=[_row_spec(tm, D_MODEL), _row_spec(tm, GATE_COLS), _const_spec((1, GATE_COLS)),
                  _row_spec(tm, 512), _row_spec(tm, 512), _row_spec(tm, 512),
                  _const_spec((512, D_MODEL)), _const_spec((512, D_MODEL)), _const_spec((512, D_MODEL)),
                  _const_spec((D_MODEL, D_MODEL)), _const_spec((1, D_MODEL))],
        out_specs=[_row_spec(tm, D_MODEL), _row_spec(tm, D_MODEL)],
        out_shape=[jax.ShapeDtypeStruct((t, D_MODEL), F32), jax.ShapeDtypeStruct((t, D_MODEL), BF16)],
        compiler_params=_params(),
        name="merge",
    )(x, p_gate, b_gate, o_r, o_m, o_d, w_r, w_m, w_d, w_o, g_ffn)


def _ffn_kernel(x_ref, h_ref, wup_ref, cw_ref, cb_ref, wdn_ref, gfin_ref, o_ref, carry_ref, *, tiles_per_seq, final_norm):
    tm = TM_FFN

    @pl.when(pl.program_id(0) % tiles_per_seq == 0)
    def _():
        carry_ref[0:8, :] = jnp.zeros((8, carry_ref.shape[1]), F32)

    h = h_ref[...]

    def conv(u, cols):
        carry_ref[8:, cols] = u
        u1 = carry_ref[7:7 + tm, cols]
        u2 = carry_ref[6:6 + tm, cols]
        carry_ref[0:8, cols] = u[tm - 8:tm, :]
        return cw_ref[0:1, cols] * u2 + cw_ref[1:2, cols] * u1 + cw_ref[2:3, cols] * u + cb_ref[:, cols]

    def up(ck):
        gc = slice(ck * FF_CHUNK, (ck + 1) * FF_CHUNK)
        vc = slice(D_FF + ck * FF_CHUNK, D_FF + (ck + 1) * FF_CHUNK)
        return _dot(h, wup_ref[:, gc]), _dot(h, wup_ref[:, vc])

    n_chunks = D_FF // FF_CHUNK
    acc = x_ref[...]
    nxt = up(0)
    acts = []
    for ck in range(n_chunks):
        ug, uv = nxt
        if ck + 1 < n_chunks:
            nxt = up(ck + 1)
        gc = slice(ck * FF_CHUNK, (ck + 1) * FF_CHUNK)
        vc = slice(D_FF + ck * FF_CHUNK, D_FF + (ck + 1) * FF_CHUNK)
        gate = conv(ug, gc)
        val = conv(uv, vc)
        acts.append((gate * jax.nn.sigmoid(gate) * val).astype(BF16))
        if len(acts) == FF_DOWN_GROUP or ck + 1 == n_chunks:
            rows = slice((ck + 1 - len(acts)) * FF_CHUNK, (ck + 1) * FF_CHUNK)
            acc = acc + _dot(jnp.concatenate(acts, axis=-1), wdn_ref[rows, :])
            acts = []
    if final_norm:
        acc = _rms(acc, gfin_ref[...], NORM_EPS)
    o_ref[...] = acc


def _ffn(x1, h2, w_up, conv_w, conv_b, w_down, g_final, seq, final_norm):
    t = x1.shape[0]
    tm = TM_FFN
    kern = functools.partial(_ffn_kernel, tiles_per_seq=seq // tm, final_norm=final_norm)
    return pl.pallas_call(
        kern,
        grid=(t // tm,),
        in_specs=[_row_spec(tm, D_MODEL), _row_spec(tm, D_MODEL), _const_spec((D_MODEL, 2 * D_FF)),
                  _const_spec((3, 2 * D_FF)), _const_spec((1, 2 * D_FF)), _const_spec((D_FF, D_MODEL)),
                  _const_spec((1, D_MODEL))],
        out_specs=_row_spec(tm, D_MODEL),
        out_shape=jax.ShapeDtypeStruct((t, D_MODEL), F32),
        scratch_shapes=[pltpu.VMEM((8 + tm, 2 * D_FF), F32)],
        compiler_params=_params(),
        name="conv_ffn",
    )(x1, h2, w_up, conv_w, conv_b, w_down, g_final)


def _mla_weights(w_uq, w_ukv):
    qd = MLA_NOPE + MLA_ROPE
    half = MLA_ROPE // 2
    wq = w_uq.reshape(MLA_Q_LORA, MLA_HEADS, qd)
    zq = jnp.zeros((MLA_Q_LORA, MLA_HEADS, LANES - qd), F32)
    wq_main = jnp.concatenate([wq, zq], axis=-1)
    x1 = wq[:, :, MLA_NOPE:MLA_NOPE + half]
    x2 = wq[:, :, MLA_NOPE + half:]
    wq_rot = jnp.concatenate([jnp.zeros((MLA_Q_LORA, MLA_HEADS, MLA_NOPE), F32), -x2, x1, zq], axis=-1)
    wkv = w_ukv.reshape(MLA_KV_LORA, MLA_HEADS, MLA_NOPE + MLA_V)
    zkv = jnp.zeros((MLA_KV_LORA, MLA_HEADS, LANES - MLA_NOPE), F32)
    wk = jnp.concatenate([wkv[:, :, :MLA_NOPE], zkv], axis=-1)
    wv = jnp.concatenate([wkv[:, :, MLA_NOPE:], jnp.zeros((MLA_KV_LORA, MLA_HEADS, MLA_VT_ROWS - MLA_V), F32)], axis=-1)
    flat = lambda w: w.reshape(w.shape[0], -1).astype(BF16)
    return flat(wq_main), flat(wq_rot), flat(wk), flat(wv).T


def kernel(x, positions, rel_bias, norm_mix, w_in, b_gate, rwkv_mu, rwkv_w0, rwkv_w2, rwkv_a0, rwkv_a2, rwkv_g2, rwkv_k_k, rwkv_k_a, rwkv_r_k, rwkv_ln_w, rwkv_ln_b, mla_q_norm, mla_w_uq, mla_kv_norm, mla_w_ukv, diff_lambda, diff_subln, w_branch_rwkv, w_branch_mla, w_branch_diff, w_o, norm_ffn, ffn_w_up, ffn_conv_w, ffn_conv_b, ffn_w_down, norm_final):
    batch, seq, _ = x.shape
    depth = w_in.shape[0]
    t = batch * seq
    assert seq % TQ == 0 and seq % TM == 0 and seq % TM_FFN == 0 and seq % RWKV_TILE == 0 and RWKV_TILE % CHUNK == 0
    assert (batch * seq) % TM_MERGE == 0
    xf = x.reshape(t, D_MODEL)
    pos_col = positions.reshape(t, 1)
    inv_freq = ROPE_BASE ** (-jnp.arange(0, MLA_ROPE, 2, dtype=F32) / MLA_ROPE)
    freq = jnp.concatenate([jnp.zeros((MLA_NOPE,), F32), inv_freq, inv_freq,
                            jnp.zeros((LANES - MLA_NOPE - MLA_ROPE,), F32)]).reshape(1, LANES)
    row = lambda v: v.reshape(1, -1)
    diff_scale = jnp.concatenate([jnp.full((512,), DIFF_QK ** -0.5 * LOG2E, F32), jnp.ones((512,), F32)])

    for l in range(depth):
        s0, s1, s2 = RWKV_COLS, RWKV_COLS + MLA_COLS, RWKV_COLS + MLA_COLS + DIFF_COLS
        sv = s1 + DIFF_QK_COLS
        w = w_in[l]
        w_r = w[:, :s0].astype(BF16)
        w_m = jnp.pad(w[:, s0:s1], ((0, 0), (0, MLA_PAD - MLA_COLS))).astype(BF16)
        w_qk = (w[:, s1:sv] * diff_scale).astype(BF16)
        w_vt = w[:, sv:s2].T.astype(BF16)
        w_g = w[:, s2:].astype(BF16)
        wq, wqr, wk, wv = _mla_weights(mla_w_uq[l], mla_w_ukv[l])
        p_rwkv, q_m, k_m, v_m, p_qk, p_vt, p_gate = _in_proj(
            xf, row(norm_mix[l]), w_r, w_m, w_qk, w_g, w_vt, pos_col, freq, row(mla_q_norm[l]), row(mla_kv_norm[l]),
            wq, wqr, wk, wv)

        o_r = _rwkv(p_rwkv, seq, row(rwkv_mu[l]), row(rwkv_w0[l]), rwkv_w2[l].astype(BF16), row(rwkv_a0[l]),
                    rwkv_a2[l].astype(BF16), rwkv_g2[l].astype(BF16), row(rwkv_k_k[l]), row(rwkv_k_a[l]),
                    row(rwkv_r_k[l]), row(rwkv_ln_w[l]), row(rwkv_ln_b[l]))

        o_m = _mla_attn(q_m, k_m, v_m, batch, seq)

        o_d = _diff_attn(p_qk, p_vt, rel_bias, diff_lambda[l], row(diff_subln[l]), batch, seq, l)

        x1, h2 = _merge(xf, p_gate, row(b_gate[l]), o_r, o_m, o_d, w_branch_rwkv[l].astype(BF16),
                        w_branch_mla[l].astype(BF16), w_branch_diff[l].astype(BF16), w_o[l].astype(BF16),
                        row(norm_ffn[l]))
        xf = _ffn(x1, h2, ffn_w_up[l].astype(BF16), ffn_conv_w[l], row(ffn_conv_b[l]), ffn_w_down[l].astype(BF16),
                  row(norm_final), seq, final_norm=(l == depth - 1))
    return xf.reshape(batch, seq, D_MODEL)
```

```python
import functools
import math

import numpy as np
import jax
import jax.numpy as jnp
from jax import lax
from jax.experimental import pallas as pl
from jax.experimental.pallas import tpu as pltpu

F32 = jnp.float32
BF16 = jnp.bfloat16

D_MODEL = 1024
RWKV_HEADS = 8
RWKV_N = 64
RWKV_DIM = 512
RWKV_COLS = 1792
RWKV_GN_EPS = 64e-5
MLA_HEADS = 8
MLA_Q_LORA = 256
MLA_KV_LORA = 128
MLA_NOPE = 64
MLA_ROPE = 32
MLA_V = 64
MLA_VT_ROWS = MLA_V + 16
MLA_COLS = 416
MLA_PAD = 512
ROPE_BASE = 10000.0
DIFF_HEADS = 4
DIFF_QK = 64
DIFF_V = 128
DIFF_COLS = 1536
DIFF_QK_COLS = 1024
DIFF_VT_ROWS = DIFF_V + 16
REL_BUCKETS = 32
REL_MAX_DISTANCE = 128
D_FF = 2816
GATE_COLS = 3072
NORM_EPS = 1e-6
SUBLN_EPS = 1e-5

LANES = 128
VMEM_LIMIT = 58 * 1024 * 1024
TM = 512
TM_FFN = 256
CHUNK = 64
RWKV_TILE = 256
TQ = 512
KEY_TILES_PER_STEP = 2
FF_CHUNK = 256
NEG = -1e30
LOG2E = 1.4426950408889634

_NT = (((1,), (1,)), ((), ()))
_TN = (((0,), (0,)), ((), ()))


def _dot(a, b):
    return jnp.dot(a, b, preferred_element_type=F32)


def _dot_nt(a, b):
    return lax.dot_general(a, b, _NT, preferred_element_type=F32)


def _dot_tn(a, b):
    return lax.dot_general(a, b, _TN, preferred_element_type=F32)


def _rms(x, g, eps):
    return x * lax.rsqrt(jnp.mean(x * x, axis=-1, keepdims=True) + eps) * g


def _params():
    return pltpu.CompilerParams(dimension_semantics=("arbitrary",), vmem_limit_bytes=VMEM_LIMIT)


def _params2():
    return pltpu.CompilerParams(dimension_semantics=("arbitrary", "arbitrary"), vmem_limit_bytes=VMEM_LIMIT)


def _const_spec(shape):
    return pl.BlockSpec(shape, lambda *_: (0,) * len(shape), pipeline_mode=pl.Buffered(1))


def _row_spec(tm, cols, col_block=0):
    return pl.BlockSpec((tm, cols), lambda i: (i, col_block))


def _in_proj_kernel(x_ref, g_ref, wr_ref, wm_ref, wqk_ref, wg_ref, wvt_ref, pos_ref, freq_ref, qn_ref, kvn_ref, wq_ref,
                    wqr_ref, wk_ref, wv_ref, pr_ref, mq_ref, mk_ref, mvt_ref, pqk_ref, pvt_ref, pg_ref):
    h = _rms(x_ref[...], g_ref[...], NORM_EPS).astype(BF16)
    p_mla = _dot(h, wm_ref[...])
    pqk_ref[...] = _dot(h, wqk_ref[...]).astype(BF16)
    pg_ref[...] = _dot(h, wg_ref[...]).astype(BF16)
    _mla_prep_body(p_mla, pos_ref, freq_ref, qn_ref, kvn_ref, wq_ref, wqr_ref, wk_ref, wv_ref, mq_ref, mk_ref, mvt_ref)
    pr_ref[...] = _dot(h, wr_ref[...])
    pvt_ref[...] = _dot_nt(wvt_ref[...], h).astype(BF16)


def _in_proj(x, g, w_r, w_m, w_qk, w_g, w_vt, pos_col, freq, q_norm, kv_norm, wq, wqr, wk, wv):
    t = x.shape[0]
    vw = DIFF_HEADS * DIFF_V
    hw = MLA_HEADS * LANES
    mvw = MLA_HEADS * MLA_VT_ROWS
    return pl.pallas_call(
        _in_proj_kernel,
        grid=(t // TM,),
        in_specs=[_row_spec(TM, D_MODEL), _const_spec((1, D_MODEL)), _const_spec((D_MODEL, RWKV_COLS)),
                  _const_spec((D_MODEL, MLA_PAD)), _const_spec((D_MODEL, DIFF_QK_COLS)), _const_spec((D_MODEL, GATE_COLS)),
                  _const_spec((vw, D_MODEL)), _row_spec(TM, 1), _const_spec((1, LANES)), _const_spec((1, MLA_Q_LORA)),
                  _const_spec((1, MLA_KV_LORA)), _const_spec((MLA_Q_LORA, hw)), _const_spec((MLA_Q_LORA, hw)),
                  _const_spec((MLA_KV_LORA, hw)), _const_spec((mvw, MLA_KV_LORA))],
        out_specs=[_row_spec(TM, RWKV_COLS), _row_spec(TM, hw), _row_spec(TM, hw),
                   pl.BlockSpec((mvw, TM), lambda i: (0, i)), _row_spec(TM, DIFF_QK_COLS),
                   pl.BlockSpec((vw, TM), lambda i: (0, i)), _row_spec(TM, GATE_COLS)],
        out_shape=[jax.ShapeDtypeStruct((t, RWKV_COLS), F32), jax.ShapeDtypeStruct((t, hw), BF16),
                   jax.ShapeDtypeStruct((t, hw), BF16), jax.ShapeDtypeStruct((mvw, t), BF16),
                   jax.ShapeDtypeStruct((t, DIFF_QK_COLS), BF16), jax.ShapeDtypeStruct((vw, t), BF16),
                   jax.ShapeDtypeStruct((t, GATE_COLS), BF16)],
        compiler_params=_params(),
        name="in_proj",
    )(x, g, w_r, w_m, w_qk, w_g, w_vt, pos_col, freq, q_norm, kv_norm, wq, wqr, wk, wv)


def _split3(a):
    hi = a.astype(BF16)
    r1 = a - hi.astype(F32)
    mid = r1.astype(BF16)
    lo = (r1 - mid.astype(F32)).astype(BF16)
    return hi, mid, lo


def _softplus(z):
    return jnp.maximum(z, 0.0) + jnp.log(1.0 + jnp.exp(-jnp.abs(z)))


def _rwkv_kernel(p_ref, mu_ref, w0_ref, w2_ref, a0_ref, a2_ref, g2_ref, kk_ref, ka_ref, rk_ref, lnw_ref, lnb_ref,
                 o_ref, carry_ref, state_ref, *, tiles_per_seq):
    c = CHUNK
    n = RWKV_N
    ts = RWKV_TILE
    nc = ts // c

    @pl.when(pl.program_id(0) % tiles_per_seq == 0)
    def _():
        carry_ref[...] = jnp.zeros_like(carry_ref)
        state_ref[...] = jnp.zeros_like(state_ref)

    p = p_ref[...]
    row = lax.broadcasted_iota(jnp.int32, (ts, 1), 0)
    shifted = jnp.where(row == 0, carry_ref[...], pltpu.roll(p, 1, 0))
    carry_ref[...] = p[ts - 1:ts, :]
    pm = p + (shifted - p) * mu_ref[...]

    r = pm[:, 0:512]
    k = pm[:, 512:1024]
    v = pm[:, 1024:1536]
    pw = pm[:, 1536:1600]
    pa = pm[:, 1600:1664]
    pg = pm[:, 1664:1792]

    w_log = -_softplus(-(w0_ref[...] + _dot(jnp.tanh(pw).astype(BF16), w2_ref[...]))) - 0.5
    logd = -jnp.exp(w_log)
    a = jax.nn.sigmoid(a0_ref[...] + _dot(pa.astype(BF16), a2_ref[...]))
    g = _dot(jax.nn.sigmoid(pg).astype(BF16), g2_ref[...])

    tr = lax.broadcasted_iota(jnp.int32, (ts, ts), 0)
    tc = lax.broadcasted_iota(jnp.int32, (ts, ts), 1)
    tri = ((tr >= tc) & (tr // c == tc // c)).astype(BF16)
    hi, mid, lo = _split3(logd)
    cs = _dot(tri, hi) + _dot(tri, mid) + _dot(tri, lo)
    total = jnp.concatenate([jnp.broadcast_to(cs[(m + 1) * c - 1:(m + 1) * c, :], (c, RWKV_DIM)) for m in range(nc)], axis=0)
    e_in = jnp.exp(cs)
    e_ex = jnp.exp(cs - logd)
    e_inv = jnp.exp(-cs)
    e_end = jnp.exp(total - cs)
    g_end = jnp.exp(total)

    n_pairs = RWKV_HEADS // 2
    lo_t = lax.broadcasted_iota(jnp.int32, (ts, LANES), 1) < n
    lo_c = lax.broadcasted_iota(jnp.int32, (c, LANES), 1) < n

    def head_sum(x):
        blocks = []
        for q in range(n_pairs):
            xb = x[:, q * LANES:(q + 1) * LANES]
            s_lo = jnp.sum(jnp.where(lo_t, xb, 0.0), axis=-1, keepdims=True)
            s_hi = jnp.sum(jnp.where(lo_t, 0.0, xb), axis=-1, keepdims=True)
            blocks.append(jnp.where(lo_t, s_lo, s_hi))
        return jnp.concatenate(blocks, axis=-1)

    k2 = k * (1.0 + (a - 1.0) * ka_ref[...])
    kku = k * kk_ref[...]
    kkn = kku / jnp.maximum(jnp.sqrt(head_sum(kku * kku)), 1e-12)
    b = kkn * a
    full = {"at": -kkn * e_ex, "rt": r * e_in, "bt": b * e_inv, "kt": k2 * e_inv, "v": v,
            "be": b * e_end, "ke": k2 * e_end}

    ti = lax.broadcasted_iota(jnp.int32, (2 * c, 4 * c), 0) % c
    si = lax.broadcasted_iota(jnp.int32, (2 * c, 4 * c), 1) % c
    strict = ti > si
    incl = ti >= si
    eye = (lax.broadcasted_iota(jnp.int32, (2 * c, 2 * c), 0)
           == lax.broadcasted_iota(jnp.int32, (2 * c, 2 * c), 1)).astype(F32)

    items = [(m, q) for m in range(nc) for q in range(n_pairs)]

    def stacked(name, m, q):
        xb = full[name][m * c:(m + 1) * c, q * LANES:(q + 1) * LANES]
        return jnp.concatenate([jnp.where(lo_c, xb, 0.0), jnp.where(lo_c, 0.0, xb)], axis=0).astype(BF16)

    st_ops = {name: [stacked(name, m, q) for m, q in items] for name in full}
    big_l = [_dot_nt(jnp.concatenate([at, rtb], axis=0), jnp.concatenate([bt, kt], axis=0))
             for at, rtb, bt, kt in zip(st_ops["at"], st_ops["rt"], st_ops["bt"], st_ops["kt"])]
    top_l = [jnp.where(strict, big[:2 * c, :], 0.0) for big in big_l]
    bot_l = [jnp.where(incl, big[2 * c:, :], 0.0).astype(BF16) for big in big_l]
    lakv_l = [_dot(top[:, 2 * c:].astype(BF16), vb) for top, vb in zip(top_l, st_ops["v"])]

    x_l = [top[:, :2 * c] for top in top_l]
    tinv_l = [eye + x for x in x_l]
    xb_l = [x.astype(BF16) for x in x_l]
    for _ in range(int(math.log2(c)) - 1):
        x_l = [_dot(xb, xb) for xb in xb_l]
        xb_l = [x.astype(BF16) for x in x_l]
        tinv_l = [tinv + _dot(tinv.astype(BF16), xb) for tinv, xb in zip(tinv_l, xb_l)]
    tinvb_l = [tinv.astype(BF16) for tinv in tinv_l]
    abar_l = [_dot(tb, at).astype(BF16) for tb, at in zip(tinvb_l, st_ops["at"])]
    vbar_l = [_dot(tb, lv.astype(BF16)).astype(BF16) for tb, lv in zip(tinvb_l, lakv_l)]
    rhat_l = [(rtb.astype(F32) + _dot(bot[:, :2 * c], ab)).astype(BF16)
              for rtb, bot, ab in zip(st_ops["rt"], bot_l, abar_l)]
    uv_l = [jnp.concatenate([vbar, vb], axis=0) for vbar, vb in zip(vbar_l, st_ops["v"])]
    y0_l = [_dot(bot, uv) for bot, uv in zip(bot_l, uv_l)]
    p_l = [_dot_tn(ab, be).astype(BF16) for ab, be in zip(abar_l, st_ops["be"])]
    q_l = [_dot_tn(uv, jnp.concatenate([be, ke], axis=0)) for uv, be, ke in zip(uv_l, st_ops["be"], st_ops["ke"])]

    st_l = [state_ref[q] for q in range(n_pairs)]
    y_rows = []
    for m in range(nc):
        y_blocks = []
        for q in range(n_pairs):
            idx = m * n_pairs + q
            st = st_l[q]
            st_b = st.astype(BF16)
            y_s = _dot_nt(rhat_l[idx], st_b) + y0_l[idx]
            y_blocks.append(y_s[:c] + y_s[c:])
            decay = g_end[(m + 1) * c - 1:(m + 1) * c, q * LANES:(q + 1) * LANES]
            st_l[q] = st * decay + _dot(st_b, p_l[idx]) + q_l[idx]
        y_rows.append(jnp.concatenate(y_blocks, axis=-1))
    for q in range(n_pairs):
        state_ref[q] = st_l[q]
    y = jnp.concatenate(y_rows, axis=0)

    inv_n = 1.0 / n
    yc = y - head_sum(y) * inv_n
    var = head_sum(yc * yc) * inv_n
    yn = yc * lax.rsqrt(var + RWKV_GN_EPS) * lnw_ref[...] + lnb_ref[...]
    bonus = head_sum(r * k2 * rk_ref[...]) * v
    o_ref[...] = ((yn + bonus) * g).astype(BF16)


def _rwkv(p_rwkv, seq, mu, w0, w2, a0, a2, g2, k_k, k_a, r_k, ln_w, ln_b):
    t = p_rwkv.shape[0]
    ts = RWKV_TILE
    kern = functools.partial(_rwkv_kernel, tiles_per_seq=seq // ts)
    vec = lambda: _const_spec((1, RWKV_DIM))
    return pl.pallas_call(
        kern,
        grid=(t // ts,),
        in_specs=[_row_spec(ts, RWKV_COLS), _const_spec((1, RWKV_COLS)), vec(), _const_spec((64, RWKV_DIM)), vec(),
                  _const_spec((64, RWKV_DIM)), _const_spec((128, RWKV_DIM)), vec(), vec(), vec(), vec(), vec()],
        out_specs=_row_spec(ts, RWKV_DIM),
        out_shape=jax.ShapeDtypeStruct((t, RWKV_DIM), BF16),
        scratch_shapes=[pltpu.VMEM((1, RWKV_COLS), F32), pltpu.VMEM((RWKV_HEADS // 2, LANES, LANES), F32)],
        compiler_params=_params(),
        name="rwkv7",
    )(p_rwkv, mu, w0, w2, a0, a2, g2, k_k, k_a, r_k, ln_w, ln_b)


def _mla_prep_body(p, pos_ref, freq_ref, qn_ref, kvn_ref, wq_ref, wqr_ref, wk_ref, wv_ref, q_out, k_out, v_out):
    hq = _rms(p[:, 0:MLA_Q_LORA], qn_ref[...], NORM_EPS).astype(BF16)
    hkv = _rms(p[:, MLA_Q_LORA:MLA_Q_LORA + MLA_KV_LORA], kvn_ref[...], NORM_EPS).astype(BF16)
    blk = p[:, 384:512]
    ang = pos_ref[...].astype(F32) * freq_ref[...]
    cos = jnp.cos(ang)
    sin = jnp.sin(ang)
    scale = (MLA_NOPE + MLA_ROPE) ** -0.5 * LOG2E
    qa = _dot(hq, wq_ref[...])
    qr = _dot(hq, wqr_ref[...])
    lane = lax.broadcasted_iota(jnp.int32, blk.shape, 1)
    kr = pltpu.roll(blk, 64, 1)
    rot = jnp.where(lane < 80, -pltpu.roll(blk, 48, 1), pltpu.roll(blk, 80, 1))
    rot = jnp.where((lane >= 64) & (lane < 96), rot, 0.0)
    krope = kr * cos + rot * sin
    kn = _dot(hkv, wk_ref[...])
    for h in range(MLA_HEADS):
        hs = slice(h * LANES, (h + 1) * LANES)
        q_out[:, hs] = ((qa[:, hs] * cos + qr[:, hs] * sin) * scale).astype(BF16)
        k_out[:, hs] = (kn[:, hs] + krope).astype(BF16)
    vt = _dot_nt(wv_ref[...], hkv)
    rowi = lax.broadcasted_iota(jnp.int32, vt.shape, 0)
    v_out[...] = (vt + (rowi % MLA_VT_ROWS == MLA_V).astype(F32)).astype(BF16)


def _softmax_stage_t(s_l, pv_fn, m_scr, acc_scr, idx, s_scr=None, l_scr=None):
    n = len(s_l)
    m_old = [m_scr[c][0:1, :] for c in idx]
    m_new = [jnp.maximum(m, jnp.max(s, axis=0, keepdims=True)) for m, s in zip(m_old, s_l)]
    alpha = [jnp.exp2(mo - mn) for mo, mn in zip(m_old, m_new)]
    if s_scr is not None:
        for c in range(n):
            s_scr[c] = s_l[c]
        s_l = [s_scr[c] for c in range(n)]
    pv_l = []
    for c in range(n):
        p = jnp.exp2(s_l[c] - m_new[c])
        if l_scr is not None:
            l_new = alpha[c] * l_scr[idx[c]][0:1, :] + jnp.sum(p, axis=0, keepdims=True)
            l_scr[idx[c]] = jnp.broadcast_to(l_new, l_scr.shape[1:])
        pv_l.append(pv_fn(c, p.astype(BF16)))
    for c in range(n):
        m_scr[idx[c]] = jnp.broadcast_to(m_new[c], m_scr.shape[1:])
        acc_scr[idx[c]] = alpha[c] * acc_scr[idx[c]] + pv_l[c]


def _mla_attn_kernel(q_ref, k_ref, vt_ref, o_ref, m_scr, acc_scr, s_scr):
    tq = TQ
    i = pl.program_id(1)
    causal = lax.broadcasted_iota(jnp.int32, (tq, tq), 0) <= lax.broadcasted_iota(jnp.int32, (tq, tq), 1)
    m_scr[...] = jnp.full(m_scr.shape, NEG, F32)
    acc_scr[...] = jnp.zeros(acc_scr.shape, F32)
    hsl = [slice(h * LANES, (h + 1) * LANES) for h in range(MLA_HEADS)]
    vsl =[slice(h * MLA_VT_ROWS, (h + 1) * MLA_VT_ROWS) for h in range(MLA_HEADS)]
    heads = list(range(MLA_HEADS))

    def tile(off, tk, mask):
        s_l = [_dot_nt(k_ref[pl.ds(off, tk), hs], q_ref[:, hs]) for hs in hsl]
        if mask:
            s_l = [jnp.where(causal, s, NEG) for s in s_l]
        _softmax_stage_t(s_l, lambda c, p: _dot(vt_ref[vsl[c], pl.ds(off, tk)], p), m_scr, acc_scr, heads,
                         s_scr.at[:, 0:tk, :])

    def pair_body(j, carry):
        tile(pl.multiple_of(j * 2 * tq, 2 * tq), 2 * tq, False)
        return carry

    def single_body(j, carry):
        tile(pl.multiple_of(j * tq, tq), tq, False)
        return carry

    n_pairs = i // 2 if KEY_TILES_PER_STEP == 2 else 0
    if KEY_TILES_PER_STEP == 2:
        lax.fori_loop(0, n_pairs, pair_body, 0)
    lax.fori_loop(2 * n_pairs, i, single_body, 0)
    tile(pl.multiple_of(i * tq, tq), tq, True)
    for pr in range(MLA_HEADS // 2):
        accs = [acc_scr[2 * pr + hh] for hh in range(2)]
        ot = jnp.concatenate([acc[:MLA_V, :] / acc[MLA_V:MLA_V + 1, :] for acc in accs], axis=0)
        o_ref[:, pr * LANES:(pr + 1) * LANES] = jnp.transpose(ot).astype(BF16)


def _mla_attn(q, k, vt, batch, seq):
    t = q.shape[0]
    nq = seq // TQ
    hw = MLA_HEADS * LANES
    vw = MLA_HEADS * MLA_V
    return pl.pallas_call(
        _mla_attn_kernel,
        grid=(batch, nq),
        in_specs=[pl.BlockSpec((TQ, hw), lambda b, i: (b * nq + i, 0)),
                  pl.BlockSpec((seq, hw), lambda b, i: (b, 0)),
                  pl.BlockSpec((MLA_HEADS * MLA_VT_ROWS, seq), lambda b, i: (0, b))],
        out_specs=pl.BlockSpec((TQ, vw), lambda b, i: (b * nq + i, 0)),
        out_shape=jax.ShapeDtypeStruct((t, vw), BF16),
        scratch_shapes=[pltpu.VMEM((MLA_HEADS, 8, TQ), F32), pltpu.VMEM((MLA_HEADS, MLA_VT_ROWS, TQ), F32),
                        pltpu.VMEM((MLA_HEADS, KEY_TILES_PER_STEP * TQ, TQ), F32)],
        compiler_params=_params2(),
        name="mla_attn",
    )(q, k, vt)


def _t5_bucket_table():
    n = np.arange(0, REL_MAX_DISTANCE + 1)
    max_exact = REL_BUCKETS // 2
    nf = np.maximum(n, 1).astype(np.float32)
    ratio = np.log(nf / np.float32(max_exact)) / np.float32(math.log(REL_MAX_DISTANCE / max_exact))
    large = max_exact + (ratio * np.float32(REL_BUCKETS - max_exact)).astype(np.int32)
    large = np.minimum(large, REL_BUCKETS - 1)
    return np.where(n < max_exact, n, large)


_BUCKETS = _t5_bucket_table()
_FAR_BUCKET = int(_BUCKETS[REL_MAX_DISTANCE])
assert _FAR_BUCKET == REL_BUCKETS - 1 and np.all(np.diff(_BUCKETS) >= 0)
_BUCKET_STARTS = [int(np.argmax(_BUCKETS >= b)) for b in range(REL_BUCKETS // 2 + 1, REL_BUCKETS)]


def _diff_attn_kernel(rb_ref, lam_ref, subln_ref, q_ref, k_ref, vt_ref, o_ref, bias_scr, m_scr, acc_scr, s_scr, *,
                      lambda_init):
    tq = TQ
    i = pl.program_id(1)
    r_i = lax.broadcasted_iota(jnp.int32, (tq, tq), 0)
    c_i = lax.broadcasted_iota(jnp.int32, (tq, tq), 1)
    causal = r_i <= c_i

    @pl.when((pl.program_id(0) == 0) & (i == 0))
    def _():
        for t_idx, delta in enumerate((0, tq)):
            d = jnp.maximum(c_i - r_i + delta, 0)
            log_b = REL_BUCKETS // 2
            for start in _BUCKET_STARTS:
                log_b = log_b + (d >= start).astype(jnp.int32)
            bucket = jnp.where(d < REL_BUCKETS // 2, d, log_b)
            for h in range(DIFF_HEADS):
                bias = jnp.zeros((tq, tq), F32)
                for b in range(REL_BUCKETS):
                    bias = jnp.where(bucket == b, rb_ref[b, h], bias)
                bias_scr[t_idx, h] = (bias - rb_ref[_FAR_BUCKET, h]) * LOG2E

    m_scr[...] = jnp.full(m_scr.shape, NEG, F32)
    acc_scr[...] = jnp.zeros(acc_scr.shape, F32)
    lane = lax.broadcasted_iota(jnp.int32, (tq, LANES), 1)
    ones_rows = {tk: (lax.broadcasted_iota(jnp.int32, (DIFF_VT_ROWS - DIFF_V, tk), 0) == 0).astype(BF16)
                 for tk in (tq, 2 * tq)}
    hsl = [slice(h * LANES, (h + 1) * LANES) for h in range(DIFF_HEADS)]
    heads = list(range(DIFF_HEADS))
    q_l = []
    for hs in hsl:
        qp = q_ref[:, hs]
        q_l.append(jnp.concatenate([jnp.where(lane < DIFF_QK, qp, jnp.zeros_like(qp)),
                                    jnp.where(lane >= DIFF_QK, qp, jnp.zeros_like(qp))], axis=0))

    def tile(off, tk, bias_idx, mask):
        s_l = [_dot_nt(k_ref[pl.ds(off, tk), hsl[h]], q_l[h]) for h in heads]
        if bias_idx is not None:
            s_l = [s + jnp.concatenate([bias_scr[bias_idx, h]] * 2, axis=1) for h, s in enumerate(s_l)]
        if mask:
            mask2 = jnp.concatenate([causal, causal], axis=1)
            s_l = [jnp.where(mask2, s, NEG) for s in s_l]

        def pv(c, p):
            vt = jnp.concatenate([vt_ref[hsl[c], pl.ds(off, tk)], ones_rows[tk]], axis=0)
            return jnp.concatenate([_dot(vt, p[:, :tq]), _dot(vt, p[:, tq:])], axis=1)

        _softmax_stage_t(s_l, pv, m_scr, acc_scr, heads, s_scr.at[:, 0:tk, :])

    def far_pair_body(j, carry):
        tile(pl.multiple_of(j * 2 * tq, 2 * tq), 2 * tq, None, False)
        return carry

    def far_body(j, carry):
        tile(pl.multiple_of(j * tq, tq), tq, None, False)
        return carry

    def near_body(j, carry):
        tile(pl.multiple_of(j * tq, tq), tq, 1, False)
        return carry

    n_far = jnp.maximum(i - 1, 0)
    n_pairs = n_far // 2 if KEY_TILES_PER_STEP == 2 else 0
    if KEY_TILES_PER_STEP == 2:
        lax.fori_loop(0, n_pairs, far_pair_body, 0)
    lax.fori_loop(2 * n_pairs, n_far, far_body, 0)
    lax.fori_loop(n_far, i, near_body, 0)
    tile(pl.multiple_of(i * tq, tq), tq, 0, True)

    lam = lam_ref[...]
    lam_full = (jnp.exp(jnp.sum(lam[0:1] * lam[1:2], axis=-1, keepdims=True))
                - jnp.exp(jnp.sum(lam[2:3] * lam[3:4], axis=-1, keepdims=True)) + lambda_init)
    for h in heads:
        acc = acc_scr[h]
        on = acc[:DIFF_V, :] / acc[DIFF_V:DIFF_V + 1, :]
        ot = on[:, :tq] - lam_full * on[:, tq:]
        ot = ot * lax.rsqrt(jnp.mean(ot * ot, axis=0, keepdims=True) + SUBLN_EPS)
        o_ref[:, hsl[h]] = (jnp.transpose(ot) * subln_ref[...] * (1.0 - lambda_init)).astype(BF16)


def _diff_attn(p_qk, p_vt, rel_bias, lam, subln, batch, seq, layer_idx):
    t = p_qk.shape[0]
    nq = seq // TQ
    w = DIFF_HEADS * LANES
    lambda_init = 0.8 - 0.6 * math.exp(-0.3 * layer_idx)
    kern = functools.partial(_diff_attn_kernel, lambda_init=lambda_init)
    const2 = lambda shape: pl.BlockSpec(shape, lambda b, i: (0, 0))
    return pl.pallas_call(
        kern,
        grid=(batch, nq),
        in_specs=[pl.BlockSpec(memory_space=pltpu.SMEM),
                  const2((4, DIFF_QK)), const2((1, DIFF_V)),
                  pl.BlockSpec((TQ, w), lambda b, i: (b * nq + i, 0)),
                  pl.BlockSpec((seq, w), lambda b, i: (b, 1)),
                  pl.BlockSpec((w, seq), lambda b, i: (0, b))],
        out_specs=pl.BlockSpec((TQ, w), lambda b, i: (b * nq + i, 0)),
        out_shape=jax.ShapeDtypeStruct((t, w), BF16),
        scratch_shapes=[pltpu.VMEM((2, DIFF_HEADS, TQ, TQ), F32), pltpu.VMEM((DIFF_HEADS, 8, 2 * TQ), F32),
                        pltpu.VMEM((DIFF_HEADS, DIFF_VT_ROWS, 2 * TQ), F32),
                        pltpu.VMEM((DIFF_HEADS, KEY_TILES_PER_STEP * TQ, 2 * TQ), F32)],
        compiler_params=_params2(),
        name="diff_attn",
    )(rel_bias, lam, subln, p_qk, p_qk, p_vt)


def _merge_ffn_kernel(x_ref, pg_ref, bg_ref, or_ref, om_ref, od_ref, wr_ref, wm_ref, wd_ref, wo_ref, gf_ref,
                      wup_ref, cw_ref, cb_ref, wdn_ref, gfin_ref, o_ref, carry_ref, *, tiles_per_seq, final_norm):
    tm = TM_FFN

    @pl.when(pl.program_id(0) % tiles_per_seq == 0)
    def _():
        carry_ref[0:8, :] = jnp.zeros((8, carry_ref.shape[1]), F32)

    branch = [_dot(o_ref[...], w_ref[...]) for o_ref, w_ref in ((or_ref, wr_ref), (om_ref, wm_ref), (od_ref, wd_ref))]
    merged = None
    for idx in range(3):
        cs = slice(idx * D_MODEL, (idx + 1) * D_MODEL)
        term = jax.nn.sigmoid(pg_ref[:, cs].astype(F32) + bg_ref[:, cs]) * branch[idx]
        merged = term if merged is None else merged + term
    x1 = x_ref[...] + _dot(merged.astype(BF16), wo_ref[...])
    h = _rms(x1, gf_ref[...], NORM_EPS).astype(BF16)

    def conv(u, cols):
        carry_ref[8:, cols] = u
        u1 = carry_ref[7:7 + tm, cols]
        u2 = carry_ref[6:6 + tm, cols]
        carry_ref[0:8, cols] = u[tm - 8:tm, :]
        return cw_ref[0:1, cols] * u2 + cw_ref[1:2, cols] * u1 + cw_ref[2:3, cols] * u + cb_ref[:, cols]

    def up(ck):
        gc = slice(ck * FF_CHUNK, (ck + 1) * FF_CHUNK)
        vc = slice(D_FF + ck * FF_CHUNK, D_FF + (ck + 1) * FF_CHUNK)
        return _dot(h, wup_ref[:, gc]), _dot(h, wup_ref[:, vc])

    n_chunks = D_FF // FF_CHUNK
    nxt = up(0)
    acts = []
    for ck in range(n_chunks):
        ug, uv = nxt
        if ck + 1 < n_chunks:
            nxt = up(ck + 1)
        gc = slice(ck * FF_CHUNK, (ck + 1) * FF_CHUNK)
        vc = slice(D_FF + ck * FF_CHUNK, D_FF + (ck + 1) * FF_CHUNK)
        gate = conv(ug, gc)
        val = conv(uv, vc)
        acts.append((gate * jax.nn.sigmoid(gate) * val).astype(BF16))
    out = x1 + _dot(jnp.concatenate(acts, axis=-1), wdn_ref[...])
    if final_norm:
        out = _rms(out, gfin_ref[...], NORM_EPS)
    o_ref[...] = out


def _merge_ffn(x, p_gate, b_gate, o_r, o_m, o_d, w_r, w_m, w_d, w_o, g_ffn, w_up, conv_w, conv_b, w_down, g_final,
               seq, final_norm):
    t = x.shape[0]
    tm = TM_FFN
    kern = functools.partial(_merge_ffn_kernel, tiles_per_seq=seq // tm, final_norm=final_norm)
    return pl.pallas_call(
        kern,
        grid=(t // tm,),
        in_specs=[_row_spec(tm, D_MODEL), _row_spec(tm, GATE_COLS), _const_spec((1, GATE_COLS)),
                  _row_spec(tm, 512), _row_spec(tm, 512), _row_spec(tm, 512),
                  _const_spec((512, D_MODEL)), _const_spec((512, D_MODEL)), _const_spec((512, D_MODEL)),
                  _const_spec((D_MODEL, D_MODEL)), _const_spec((1, D_MODEL)),
                  _const_spec((D_MODEL, 2 * D_FF)), _const_spec((3, 2 * D_FF)), _const_spec((1, 2 * D_FF)),
                  _const_spec((D_FF, D_MODEL)), _const_spec((1, D_MODEL))],
        out_specs=_row_spec(tm, D_MODEL),
        out_shape=jax.ShapeDtypeStruct((t, D_MODEL), F32),
        scratch_shapes=[pltpu.VMEM((8 + tm, 2 * D_FF), F32)],
        compiler_params=_params(),
        name="merge_ffn",
    )(x, p_gate, b_gate, o_r, o_m, o_d, w_r, w_m, w_d, w_o, g_ffn, w_up, conv_w, conv_b, w_down, g_final)


def _mla_weights(w_uq, w_ukv):
    qd = MLA_NOPE + MLA_ROPE
    half = MLA_ROPE // 2
    wq = w_uq.reshape(MLA_Q_LORA, MLA_HEADS, qd)
    zq = jnp.zeros((MLA_Q_LORA, MLA_HEADS, LANES - qd), F32)
    wq_main = jnp.concatenate([wq, zq], axis=-1)
    x1 = wq[:, :, MLA_NOPE:MLA_NOPE + half]
    x2 = wq[:, :, MLA_NOPE + half:]
    wq_rot = jnp.concatenate([jnp.zeros((MLA_Q_LORA, MLA_HEADS, MLA_NOPE), F32), -x2, x1, zq], axis=-1)
    wkv = w_ukv.reshape(MLA_KV_LORA, MLA_HEADS, MLA_NOPE + MLA_V)
    zkv = jnp.zeros((MLA_KV_LORA, MLA_HEADS, LANES - MLA_NOPE), F32)
    wk = jnp.concatenate([wkv[:, :, :MLA_NOPE], zkv], axis=-1)
    wv = jnp.concatenate([wkv[:, :, MLA_NOPE:], jnp.zeros((MLA_KV_LORA, MLA_HEADS, MLA_VT_ROWS - MLA_V), F32)], axis=-1)
    flat = lambda w: w.reshape(w.shape[0], -1).astype(BF16)
    return flat(wq_main), flat(wq_rot), flat(wk), flat(wv).T


def kernel(x, positions, rel_bias, norm_mix, w_in, b_gate, rwkv_mu, rwkv_w0, rwkv_w2, rwkv_a0, rwkv_a2, rwkv_g2, rwkv_k_k, rwkv_k_a, rwkv_r_k, rwkv_ln_w, rwkv_ln_b, mla_q_norm, mla_w_uq, mla_kv_norm, mla_w_ukv, diff_lambda, diff_subln, w_branch_rwkv, w_branch_mla, w_branch_diff, w_o, norm_ffn, ffn_w_up, ffn_conv_w, ffn_conv_b, ffn_w_down, norm_final):
    batch, seq, _ = x.shape
    depth = w_in.shape[0]
    t = batch * seq
    assert seq % TQ == 0 and seq % TM == 0 and seq % TM_FFN == 0 and seq % RWKV_TILE == 0 and RWKV_TILE % CHUNK == 0
    xf = x.reshape(t, D_MODEL)
    pos_col = positions.reshape(t, 1)
    inv_freq = ROPE_BASE ** (-jnp.arange(0, MLA_ROPE, 2, dtype=F32) / MLA_ROPE)
    freq = jnp.concatenate([jnp.zeros((MLA_NOPE,), F32), inv_freq, inv_freq,
                            jnp.zeros((LANES - MLA_NOPE - MLA_ROPE,), F32)]).reshape(1, LANES)
    row = lambda v: v.reshape(1, -1)
    diff_scale = jnp.concatenate([jnp.full((512,), DIFF_QK ** -0.5 * LOG2E, F32), jnp.ones((512,), F32)])

    for l in range(depth):
        s0, s1, s2 = RWKV_COLS, RWKV_COLS + MLA_COLS, RWKV_COLS + MLA_COLS + DIFF_COLS
        sv = s1 + DIFF_QK_COLS
        w = w_in[l]
        w_r = w[:, :s0].astype(BF16)
        w_m = jnp.pad(w[:, s0:s1], ((0, 0), (0, MLA_PAD - MLA_COLS))).astype(BF16)
        w_qk = (w[:, s1:sv] * diff_scale).astype(BF16)
        w_vt = w[:, sv:s2].T.astype(BF16)
        w_g = w[:, s2:].astype(BF16)
        wq, wqr, wk, wv = _mla_weights(mla_w_uq[l], mla_w_ukv[l])
        p_rwkv, q_m, k_m, v_m, p_qk, p_vt, p_gate = _in_proj(
            xf, row(norm_mix[l]), w_r, w_m, w_qk, w_g, w_vt, pos_col, freq, row(mla_q_norm[l]), row(mla_kv_norm[l]),
            wq, wqr, wk, wv)

        o_r = _rwkv(p_rwkv, seq, row(rwkv_mu[l]), row(rwkv_w0[l]), rwkv_w2[l].astype(BF16), row(rwkv_a0[l]),
                    rwkv_a2[l].astype(BF16), rwkv_g2[l].astype(BF16), row(rwkv_k_k[l]), row(rwkv_k_a[l]),
                    row(rwkv_r_k[l]), row(rwkv_ln_w[l]), row(rwkv_ln_b[l]))

        o_m = _mla_attn(q_m, k_m, v_m, batch, seq)

        o_d = _diff_attn(p_qk, p_vt, rel_bias, diff_lambda[l], row(diff_subln[l]), batch, seq, l)

        xf = _merge_ffn(xf, p_gate, row(b_gate[l]), o_r, o_m, o_d, w_branch_rwkv[l].astype(BF16),
                        w_branch_mla[l].astype(BF16), w_branch_diff[l].astype(BF16), w_o[l].astype(BF16),
                        row(norm_ffn[l]), ffn_w_up[l].astype(BF16), ffn_conv_w[l], row(ffn_conv_b[l]),
                        ffn_w_down[l].astype(BF16), row(norm_final), seq, final_norm=(l == depth - 1))
    return xf.reshape(batch, seq, D_MODEL)
```

```python
import functools
import math

import numpy as np
import jax
import jax.numpy as jnp
from jax import lax
from jax.experimental import pallas as pl
from jax.experimental.pallas import tpu as pltpu

F32 = jnp.float32
BF16 = jnp.bfloat16

D_MODEL = 1024
RWKV_HEADS = 8
RWKV_N = 64
RWKV_DIM = 512
RWKV_COLS = 1792
RWKV_GN_EPS = 64e-5
MLA_HEADS = 8
MLA_Q_LORA = 256
MLA_KV_LORA = 128
MLA_NOPE = 64
MLA_ROPE = 32
MLA_V = 64
MLA_VT_ROWS = MLA_V + 16
MLA_COLS = 416
MLA_PAD = 512
ROPE_BASE = 10000.0
DIFF_HEADS = 4
DIFF_QK = 64
DIFF_V = 128
DIFF_COLS = 1536
DIFF_QK_COLS = 1024
DIFF_VT_ROWS = DIFF_V + 16
REL_BUCKETS = 32
REL_MAX_DISTANCE = 128
D_FF = 2816
GATE_COLS = 3072
NORM_EPS = 1e-6
SUBLN_EPS = 1e-5

LANES = 128
VMEM_LIMIT = 58 * 1024 * 1024
TM = 512
TM_FFN = 256
TM_MERGE = 512
CHUNK = 64
RWKV_TILE = 256
TQ = 512
KEY_TILES_PER_STEP = 2
FF_CHUNK = 256
NEG = -1e30
LOG2E = 1.4426950408889634

_NT = (((1,), (1,)), ((), ()))
_TN = (((0,), (0,)), ((), ()))


def _dot(a, b):
    return jnp.dot(a, b, preferred_element_type=F32)


def _dot_nt(a, b):
    return lax.dot_general(a, b, _NT, preferred_element_type=F32)


def _dot_tn(a, b):
    return lax.dot_general(a, b, _TN, preferred_element_type=F32)


def _rms(x, g, eps):
    return x * lax.rsqrt(jnp.mean(x * x, axis=-1, keepdims=True) + eps) * g


def _params():
    return pltpu.CompilerParams(dimension_semantics=("arbitrary",), vmem_limit_bytes=VMEM_LIMIT)


def _params2():
    return pltpu.CompilerParams(dimension_semantics=("arbitrary", "arbitrary"), vmem_limit_bytes=VMEM_LIMIT)


def _const_spec(shape):
    return pl.BlockSpec(shape, lambda *_: (0,) * len(shape), pipeline_mode=pl.Buffered(1))


class _Layer:
    def __init__(self, stacked, layer):
        self.stacked, self.layer = stacked, layer


def _const_operands(*params):
    specs, arrays = [], []
    for p in params:
        if isinstance(p, _Layer):
            shape = p.stacked.shape[1:]
            specs.append(pl.BlockSpec((None,) + shape, lambda *_, l=p.layer, n=len(shape): (l,) + (0,) * n,
                                      pipeline_mode=pl.Buffered(1)))
            arrays.append(p.stacked)
        else:
            specs.append(_const_spec(p.shape))
            arrays.append(p)
    return specs, arrays


def _row_spec(tm, cols, col_block=0):
    return pl.BlockSpec((tm, cols), lambda i: (i, col_block))


def _in_proj_kernel(x_ref, g_ref, wr_ref, wm_ref, wqk_ref, wg_ref, wvt_ref, pos_ref, freq_ref, qn_ref, kvn_ref, wq_ref,
                    wqr_ref, wk_ref, wv_ref, pr_ref, mq_ref, mk_ref, mvt_ref, pqk_ref, pvt_ref, pg_ref):
    h = _rms(x_ref[...], g_ref[...], NORM_EPS).astype(BF16)
    p_mla = _dot(h, wm_ref[...])
    pqk_ref[...] = _dot(h, wqk_ref[...]).astype(BF16)
    pg_ref[...] = _dot(h, wg_ref[...]).astype(BF16)
    _mla_prep_body(p_mla, pos_ref, freq_ref, qn_ref, kvn_ref, wq_ref, wqr_ref, wk_ref, wv_ref, mq_ref, mk_ref, mvt_ref)
    pr_ref[...] = _dot(h, wr_ref[...])
    pvt_ref[...] = _dot_nt(wvt_ref[...], h).astype(BF16)


def _in_proj(x, g, w_r, w_m, w_qk, w_g, w_vt, pos_col, freq, q_norm, kv_norm, wq, wqr, wk, wv):
    t = x.shape[0]
    vw = DIFF_HEADS * DIFF_V
    hw = MLA_HEADS * LANES
    mvw = MLA_HEADS * MLA_VT_ROWS
    c1, a1 = _const_operands(g, w_r, w_m, w_qk, w_g, w_vt)
    c2, a2 = _const_operands(freq, q_norm, kv_norm, wq, wqr, wk, wv)
    return pl.pallas_call(
        _in_proj_kernel,
        grid=(t // TM,),
        in_specs=[_row_spec(TM, D_MODEL)] + c1 + [_row_spec(TM, 1)] + c2,
        out_specs=[_row_spec(TM, RWKV_COLS), _row_spec(TM, hw), _row_spec(TM, hw),
                   pl.BlockSpec((mvw, TM), lambda i: (0, i)), _row_spec(TM, DIFF_QK_COLS),
                   pl.BlockSpec((vw, TM), lambda i: (0, i)), _row_spec(TM, GATE_COLS)],
        out_shape=[jax.ShapeDtypeStruct((t, RWKV_COLS), F32), jax.ShapeDtypeStruct((t, hw), BF16),
                   jax.ShapeDtypeStruct((t, hw), BF16), jax.ShapeDtypeStruct((mvw, t), BF16),
                   jax.ShapeDtypeStruct((t, DIFF_QK_COLS), BF16), jax.ShapeDtypeStruct((vw, t), BF16),
                   jax.ShapeDtypeStruct((t, GATE_COLS), BF16)],
        compiler_params=_params(),
        name="in_proj",
    )(x, *a1, pos_col, *a2)


def _split3(a):
    hi = a.astype(BF16)
    r1 = a - hi.astype(F32)
    mid = r1.astype(BF16)
    lo = (r1 - mid.astype(F32)).astype(BF16)
    return hi, mid, lo


def _softplus(z):
    return jnp.maximum(z, 0.0) + jnp.log(1.0 + jnp.exp(-jnp.abs(z)))


def _rwkv_kernel(p_ref, mu_ref, w0_ref, w2_ref, a0_ref, a2_ref, g2_ref, kk_ref, ka_ref, rk_ref, lnw_ref, lnb_ref,
                 o_ref, carry_ref, state_ref, *, tiles_per_seq):
    c = CHUNK
    n = RWKV_N
    ts = RWKV_TILE
    nc = ts // c

    @pl.when(pl.program_id(0) % tiles_per_seq == 0)
    def _():
        carry_ref[...] = jnp.zeros_like(carry_ref)
        state_ref[...] = jnp.zeros_like(state_ref)

    p = p_ref[...]
    row = lax.broadcasted_iota(jnp.int32, (ts, 1), 0)
    shifted = jnp.where(row == 0, carry_ref[...], pltpu.roll(p, 1, 0))
    carry_ref[...] = p[ts - 1:ts, :]
    pm = p + (shifted - p) * mu_ref[...]

    r = pm[:, 0:512]
    k = pm[:, 512:1024]
    v = pm[:, 1024:1536]
    pw = pm[:, 1536:1600]
    pa = pm[:, 1600:1664]
    pg = pm[:, 1664:1792]

    w_log = -_softplus(-(w0_ref[...] + _dot(jnp.tanh(pw).astype(BF16), w2_ref[...]))) - 0.5
    logd = -jnp.exp(w_log)
    a = jax.nn.sigmoid(a0_ref[...] + _dot(pa.astype(BF16), a2_ref[...]))
    g = _dot(jax.nn.sigmoid(pg).astype(BF16), g2_ref[...])

    tr = lax.broadcasted_iota(jnp.int32, (ts, ts), 0)
    tc = lax.broadcasted_iota(jnp.int32, (ts, ts), 1)
    tri = ((tr >= tc) & (tr // c == tc // c)).astype(BF16)
    hi, mid, lo = _split3(logd)
    cs = _dot(tri, hi) + _dot(tri, mid) + _dot(tri, lo)
    total = jnp.concatenate([jnp.broadcast_to(cs[(m + 1) * c - 1:(m + 1) * c, :], (c, RWKV_DIM)) for m in range(nc)], axis=0)
    e_in = jnp.exp(cs)
    e_ex = jnp.exp(cs - logd)
    e_inv = jnp.exp(-cs)
    e_end = jnp.exp(total - cs)
    g_end = jnp.exp(total)

    n_pairs = RWKV_HEADS // 2
    lo_t = lax.broadcasted_iota(jnp.int32, (ts, LANES), 1) < n
    lo_c = lax.broadcasted_iota(jnp.int32, (c, LANES), 1) < n

    def head_sum(x):
        blocks = []
        for q in range(n_pairs):
            xb = x[:, q * LANES:(q + 1) * LANES]
            s_lo = jnp.sum(jnp.where(lo_t, xb, 0.0), axis=-1, keepdims=True)
            s_hi = jnp.sum(jnp.where(lo_t, 0.0, xb), axis=-1, keepdims=True)
            blocks.append(jnp.where(lo_t, s_lo, s_hi))
        return jnp.concatenate(blocks, axis=-1)

    k2 = k * (1.0 + (a - 1.0) * ka_ref[...])
    kku = k * kk_ref[...]
    kkn = kku / jnp.maximum(jnp.sqrt(head_sum(kku * kku)), 1e-12)
    b = kkn * a
    full = {"at": -kkn * e_ex, "rt": r * e_in, "bt": b * e_inv, "kt": k2 * e_inv, "v": v,
            "be": b * e_end, "ke": k2 * e_end}

    ti = lax.broadcasted_iota(jnp.int32, (2 * c, 4 * c), 0) % c
    si = lax.broadcasted_iota(jnp.int32, (2 * c, 4 * c), 1) % c
    strict = ti > si
    incl = ti >= si
    eye = (lax.broadcasted_iota(jnp.int32, (2 * c, 2 * c), 0)
           == lax.broadcasted_iota(jnp.int32, (2 * c, 2 * c), 1)).astype(F32)

    items = [(m, q) for m in range(nc) for q in range(n_pairs)]

    def stacked(name, m, q):
        xb = full[name][m * c:(m + 1) * c, q * LANES:(q + 1) * LANES]
        return jnp.concatenate([jnp.where(lo_c, xb, 0.0), jnp.where(lo_c, 0.0, xb)], axis=0).astype(BF16)

    st_ops = {name: [stacked(name, m, q) for m, q in items] for name in full}
    big_l = [_dot_nt(jnp.concatenate([at, rtb], axis=0), jnp.concatenate([bt, kt], axis=0))
             for at, rtb, bt, kt in zip(st_ops["at"], st_ops["rt"], st_ops["bt"], st_ops["kt"])]
    top_l = [jnp.where(strict, big[:2 * c, :], 0.0) for big in big_l]
    bot_l = [jnp.where(incl, big[2 * c:, :], 0.0).astype(BF16) for big in big_l]
    lakv_l = [_dot(top[:, 2 * c:].astype(BF16), vb) for top, vb in zip(top_l, st_ops["v"])]

    x_l = [top[:, :2 * c] for top in top_l]
    tinv_l = [eye + x for x in x_l]
    xb_l = [x.astype(BF16) for x in x_l]
    for _ in range(int(math.log2(c)) - 1):
        x_l = [_dot(xb, xb) for xb in xb_l]
        xb_l = [x.astype(BF16) for x in x_l]
        tinv_l = [tinv + _dot(tinv.astype(BF16), xb) for tinv, xb in zip(tinv_l, xb_l)]
    tinvb_l = [tinv.astype(BF16) for tinv in tinv_l]
    abar_l = [_dot(tb, at).astype(BF16) for tb, at in zip(tinvb_l, st_ops["at"])]
    vbar_l = [_dot(tb, lv.astype(BF16)).astype(BF16) for tb, lv in zip(tinvb_l, lakv_l)]
    rhat_l = [(rtb.astype(F32) + _dot(bot[:, :2 * c], ab)).astype(BF16)
              for rtb, bot, ab in zip(st_ops["rt"], bot_l, abar_l)]
    uv_l = [jnp.concatenate([vbar, vb], axis=0) for vbar, vb in zip(vbar_l, st_ops["v"])]
    y0_l = [_dot(bot, uv) for bot, uv in zip(bot_l, uv_l)]
    p_l = [_dot_tn(ab, be).astype(BF16) for ab, be in zip(abar_l, st_ops["be"])]
    q_l = [_dot_tn(uv, jnp.concatenate([be, ke], axis=0)) for uv, be, ke in zip(uv_l, st_ops["be"], st_ops["ke"])]

    st_l = [state_ref[q] for q in range(n_pairs)]
    y_rows = []
    for m in range(nc):
        y_blocks = []
        for q in range(n_pairs):
            idx = m * n_pairs + q
            st = st_l[q]
            st_b = st.astype(BF16)
            y_s = _dot_nt(rhat_l[idx], st_b) + y0_l[idx]
            y_blocks.append(y_s[:c] + y_s[c:])
            decay = g_end[(m + 1) * c - 1:(m + 1) * c, q * LANES:(q + 1) * LANES]
            st_l[q] = st * decay + _dot(st_b, p_l[idx]) + q_l[idx]
        y_rows.append(jnp.concatenate(y_blocks, axis=-1))
    for q in range(n_pairs):
        state_ref[q] = st_l[q]
    y = jnp.concatenate(y_rows, axis=0)

    inv_n = 1.0 / n
    yc = y - head_sum(y) * inv_n
    var = head_sum(yc * yc) * inv_n
    yn = yc * lax.rsqrt(var + RWKV_GN_EPS) * lnw_ref[...] + lnb_ref[...]
    bonus = head_sum(r * k2 * rk_ref[...]) * v
    o_ref[...] = ((yn + bonus) * g).astype(BF16)


def _rwkv(p_rwkv, seq, mu, w0, w2, a0, a2, g2, k_k, k_a, r_k, ln_w, ln_b):
    t = p_rwkv.shape[0]
    ts = RWKV_TILE
    kern = functools.partial(_rwkv_kernel, tiles_per_seq=seq // ts)
    specs, params = _const_operands(mu, w0, w2, a0, a2, g2, k_k, k_a, r_k, ln_w, ln_b)
    return pl.pallas_call(
        kern,
        grid=(t // ts,),
        in_specs=[_row_spec(ts, RWKV_COLS)] + specs,
        out_specs=_row_spec(ts, RWKV_DIM),
        out_shape=jax.ShapeDtypeStruct((t, RWKV_DIM), BF16),
        scratch_shapes=[pltpu.VMEM((1, RWKV_COLS), F32), pltpu.VMEM((RWKV_HEADS // 2, LANES, LANES), F32)],
        compiler_params=_params(),
        name="rwkv7",
    )(p_rwkv, *params)


def _mla_prep_body(p, pos_ref, freq_ref, qn_ref, kvn_ref, wq_ref, wqr_ref, wk_ref, wv_ref, q_out, k_out, v_out):
    hq = _rms(p[:, 0:MLA_Q_LORA], qn_ref[...], NORM_EPS).astype(BF16)
    hkv = _rms(p[:, MLA_Q_LORA:MLA_Q_LORA + MLA_KV_LORA], kvn_ref[...], NORM_EPS).astype(BF16)
    blk = p[:, 384:512]
    ang = pos_ref[...].astype(F32) * freq_ref[...]
    cos = jnp.cos(ang)
    sin = jnp.sin(ang)
    scale = (MLA_NOPE + MLA_ROPE) ** -0.5 * LOG2E
    qa = _dot(hq, wq_ref[...])
    qr = _dot(hq, wqr_ref[...])
    lane = lax.broadcasted_iota(jnp.int32, blk.shape, 1)
    kr = pltpu.roll(blk, 64, 1)
    rot = jnp.where(lane < 80, -pltpu.roll(blk, 48, 1), pltpu.roll(blk, 80, 1))
    rot = jnp.where((lane >= 64) & (lane < 96), rot, 0.0)
    krope = kr * cos + rot * sin
    kn = _dot(hkv, wk_ref[...])
    for h in range(MLA_HEADS):
        hs = slice(h * LANES, (h + 1) * LANES)
        q_out[:, hs] = ((qa[:, hs] * cos + qr[:, hs] * sin) * scale).astype(BF16)
        k_out[:, hs] = (kn[:, hs] + krope).astype(BF16)
    vt = _dot_nt(wv_ref[...], hkv)
    rowi = lax.broadcasted_iota(jnp.int32, vt.shape, 0)
    v_out[...] = (vt + (rowi % MLA_VT_ROWS == MLA_V).astype(F32)).astype(BF16)


def _softmax_stage_t(s_l, pv_fn, m_scr, acc_scr, idx, s_scr=None, l_scr=None):
    n = len(s_l)
    m_old = [m_scr[c][0:1, :] for c in idx]
    m_new = [jnp.maximum(m, jnp.max(s, axis=0, keepdims=True)) for m, s in zip(m_old, s_l)]
    alpha = [jnp.exp2(mo - mn) for mo, mn in zip(m_old, m_new)]
    if s_scr is not None:
        for c in range(n):
            s_scr[c] = s_l[c]
        s_l = [s_scr[c] for c in range(n)]
    pv_l = []
    for c in range(n):
        p = jnp.exp2(s_l[c] - m_new[c])
        if l_scr is not None:
            l_new = alpha[c] * l_scr[idx[c]][0:1, :] + jnp.sum(p, axis=0, keepdims=True)
            l_scr[idx[c]] = jnp.broadcast_to(l_new, l_scr.shape[1:])
        pv_l.append(pv_fn(c, p.astype(BF16)))
    for c in range(n):
        m_scr[idx[c]] = jnp.broadcast_to(m_new[c], m_scr.shape[1:])
        acc_scr[idx[c]] = alpha[c] * acc_scr[idx[c]] + pv_l[c]


def _mla_attn_kernel(q_ref, k_ref, vt_ref, o_ref, m_scr, acc_scr, s_scr):
    tq = TQ
    i = pl.program_id(1)
    causal = lax.broadcasted_iota(jnp.int32, (tq, tq), 0) <= lax.broadcasted_iota(jnp.int32, (tq, tq), 1)
    m_scr[...] = jnp.full(m_scr.shape, NEG, F32)
    acc_scr[...] = jnp.zeros(acc_scr.shape, F32)
    hsl = [slice(h * LANES, (h + 1) * LANES) for h in range(MLA_HEADS)]
    vsl =[slice(h * MLA_VT_ROWS, (h + 1) * MLA_VT_ROWS) for h in range(MLA_HEADS)]
    heads = list(range(MLA_HEADS))

    def tile(off, tk, mask):
        s_l = [_dot_nt(k_ref[pl.ds(off, tk), hs], q_ref[:, hs]) for hs in hsl]
        if mask:
            s_l = [jnp.where(causal, s, NEG) for s in s_l]
        _softmax_stage_t(s_l, lambda c, p: _dot(vt_ref[vsl[c], pl.ds(off, tk)], p), m_scr, acc_scr, heads,
                         s_scr.at[:, 0:tk, :])

    def pair_body(j, carry):
        tile(pl.multiple_of(j * 2 * tq, 2 * tq), 2 * tq, False)
        return carry

    def single_body(j, carry):
        tile(pl.multiple_of(j * tq, tq), tq, False)
        return carry

    n_pairs = i // 2 if KEY_TILES_PER_STEP == 2 else 0
    if KEY_TILES_PER_STEP == 2:
        lax.fori_loop(0, n_pairs, pair_body, 0)
    lax.fori_loop(2 * n_pairs, i, single_body, 0)
    tile(pl.multiple_of(i * tq, tq), tq, True)
    for pr in range(MLA_HEADS // 2):
        accs = [acc_scr[2 * pr + hh] for hh in range(2)]
        ot = jnp.concatenate([acc[:MLA_V, :] / acc[MLA_V:MLA_V + 1, :] for acc in accs], axis=0)
        o_ref[:, pr * LANES:(pr + 1) * LANES] = jnp.transpose(ot).astype(BF16)


def _mla_attn(q, k, vt, batch, seq):
    t = q.shape[0]
    nq = seq // TQ
    hw = MLA_HEADS * LANES
    vw = MLA_HEADS * MLA_V
    return pl.pallas_call(
        _mla_attn_kernel,
        grid=(batch, nq),
        in_specs=[pl.BlockSpec((TQ, hw), lambda b, i: (b * nq + i, 0)),
                  pl.BlockSpec((seq, hw), lambda b, i: (b, 0)),
                  pl.BlockSpec((MLA_HEADS * MLA_VT_ROWS, seq), lambda b, i: (0, b))],
        out_specs=pl.BlockSpec((TQ, vw), lambda b, i: (b * nq + i, 0)),
        out_shape=jax.ShapeDtypeStruct((t, vw), BF16),
        scratch_shapes=[pltpu.VMEM((MLA_HEADS, 8, TQ), F32), pltpu.VMEM((MLA_HEADS, MLA_VT_ROWS, TQ), F32),
                        pltpu.VMEM((MLA_HEADS, KEY_TILES_PER_STEP * TQ, TQ), F32)],
        compiler_params=_params2(),
        name="mla_attn",
    )(q, k, vt)


def _t5_bucket_table():
    n = np.arange(0, REL_MAX_DISTANCE + 1)
    max_exact = REL_BUCKETS // 2
    nf = np.maximum(n, 1).astype(np.float32)
    ratio = np.log(nf / np.float32(max_exact)) / np.float32(math.log(REL_MAX_DISTANCE / max_exact))
    large = max_exact + (ratio * np.float32(REL_BUCKETS - max_exact)).astype(np.int32)
    large = np.minimum(large, REL_BUCKETS - 1)
    return np.where(n < max_exact, n, large)


_BUCKETS = _t5_bucket_table()
_FAR_BUCKET = int(_BUCKETS[REL_MAX_DISTANCE])
assert _FAR_BUCKET == REL_BUCKETS - 1 and np.all(np.diff(_BUCKETS) >= 0)
_BUCKET_STARTS = [int(np.argmax(_BUCKETS >= b)) for b in range(REL_BUCKETS // 2 + 1, REL_BUCKETS)]


def _diff_attn_kernel(rb_ref, lam_ref, subln_ref, q_ref, k_ref, vt_ref, o_ref, bias_scr, m_scr, acc_scr, s_scr, *,
                      lambda_init):
    tq = TQ
    i = pl.program_id(1)
    r_i = lax.broadcasted_iota(jnp.int32, (tq, tq), 0)
    c_i = lax.broadcasted_iota(jnp.int32, (tq, tq), 1)
    causal = r_i <= c_i

    @pl.when((pl.program_id(0) == 0) & (i == 0))
    def _():
        for t_idx, delta in enumerate((0, tq)):
            d = jnp.maximum(c_i - r_i + delta, 0)
            log_b = REL_BUCKETS // 2
            for start in _BUCKET_STARTS:
                log_b = log_b + (d >= start).astype(jnp.int32)
            bucket = jnp.where(d < REL_BUCKETS // 2, d, log_b)
            for h in range(DIFF_HEADS):
                bias = jnp.zeros((tq, tq), F32)
                for b in range(REL_BUCKETS):
                    bias = jnp.where(bucket == b, rb_ref[b, h], bias)
                bias_scr[t_idx, h] = (bias - rb_ref[_FAR_BUCKET, h]) * LOG2E

    m_scr[...] = jnp.full(m_scr.shape, NEG, F32)
    acc_scr[...] = jnp.zeros(acc_scr.shape, F32)
    lane = lax.broadcasted_iota(jnp.int32, (tq, LANES), 1)
    ones_rows = {tk: (lax.broadcasted_iota(jnp.int32, (DIFF_VT_ROWS - DIFF_V, tk), 0) == 0).astype(BF16)
                 for tk in (tq, 2 * tq)}
    hsl = [slice(h * LANES, (h + 1) * LANES) for h in range(DIFF_HEADS)]
    heads = list(range(DIFF_HEADS))
    q_l = []
    for hs in hsl:
        qp = q_ref[:, hs]
        q_l.append(jnp.concatenate([jnp.where(lane < DIFF_QK, qp, jnp.zeros_like(qp)),
                                    jnp.where(lane >= DIFF_QK, qp, jnp.zeros_like(qp))], axis=0))

    def tile(off, tk, bias_idx, mask):
        s_l = [_dot_nt(k_ref[pl.ds(off, tk), hsl[h]], q_l[h]) for h in heads]
        if bias_idx is not None:
            s_l = [s + jnp.concatenate([bias_scr[bias_idx, h]] * 2, axis=1) for h, s in enumerate(s_l)]
        if mask:
            mask2 = jnp.concatenate([causal, causal], axis=1)
            s_l = [jnp.where(mask2, s, NEG) for s in s_l]

        def pv(c, p):
            vt = jnp.concatenate([vt_ref[hsl[c], pl.ds(off, tk)], ones_rows[tk]], axis=0)
            return jnp.concatenate([_dot(vt, p[:, :tq]), _dot(vt, p[:, tq:])], axis=1)

        _softmax_stage_t(s_l, pv, m_scr, acc_scr, heads, s_scr.at[:, 0:tk, :])

    def far_pair_body(j, carry):
        tile(pl.multiple_of(j * 2 * tq, 2 * tq), 2 * tq, None, False)
        return carry

    def far_body(j, carry):
        tile(pl.multiple_of(j * tq, tq), tq, None, False)
        return carry

    def near_body(j, carry):
        tile(pl.multiple_of(j * tq, tq), tq, 1, False)
        return carry

    n_far = jnp.maximum(i - 1, 0)
    n_pairs = n_far // 2 if KEY_TILES_PER_STEP == 2 else 0
    if KEY_TILES_PER_STEP == 2:
        lax.fori_loop(0, n_pairs, far_pair_body, 0)
    lax.fori_loop(2 * n_pairs, n_far, far_body, 0)
    lax.fori_loop(n_far, i, near_body, 0)
    tile(pl.multiple_of(i * tq, tq), tq, 0, True)

    lam = lam_ref[...]
    lam_full = (jnp.exp(jnp.sum(lam[0:1] * lam[1:2], axis=-1, keepdims=True))
                - jnp.exp(jnp.sum(lam[2:3] * lam[3:4], axis=-1, keepdims=True)) + lambda_init)
    for h in heads:
        acc = acc_scr[h]
        on = acc[:DIFF_V, :] / acc[DIFF_V:DIFF_V + 1, :]
        ot = on[:, :tq] - lam_full * on[:, tq:]
        ot = ot * lax.rsqrt(jnp.mean(ot * ot, axis=0, keepdims=True) + SUBLN_EPS)
        o_ref[:, hsl[h]] = (jnp.transpose(ot) * subln_ref[...] * (1.0 - lambda_init)).astype(BF16)


def _diff_attn(p_qk, p_vt, rel_bias, lam, subln, batch, seq, layer_idx):
    t = p_qk.shape[0]
    nq = seq // TQ
    w = DIFF_HEADS * LANES
    lambda_init = 0.8 - 0.6 * math.exp(-0.3 * layer_idx)
    kern = functools.partial(_diff_attn_kernel, lambda_init=lambda_init)
    specs, params = _const_operands(lam, subln)
    return pl.pallas_call(
        kern,
        grid=(batch, nq),
        in_specs=[pl.BlockSpec(memory_space=pltpu.SMEM)] + specs + [
                  pl.BlockSpec((TQ, w), lambda b, i: (b * nq + i, 0)),
                  pl.BlockSpec((seq, w), lambda b, i: (b, 1)),
                  pl.BlockSpec((w, seq), lambda b, i: (0, b))],
        out_specs=pl.BlockSpec((TQ, w), lambda b, i: (b * nq + i, 0)),
        out_shape=jax.ShapeDtypeStruct((t, w), BF16),
        scratch_shapes=[pltpu.VMEM((2, DIFF_HEADS, TQ, TQ), F32), pltpu.VMEM((DIFF_HEADS, 8, 2 * TQ), F32),
                        pltpu.VMEM((DIFF_HEADS, DIFF_VT_ROWS, 2 * TQ), F32),
                        pltpu.VMEM((DIFF_HEADS, KEY_TILES_PER_STEP * TQ, 2 * TQ), F32)],
        compiler_params=_params2(),
        name="diff_attn",
    )(rel_bias, *params, p_qk, p_qk, p_vt)


def _merge_kernel(x_ref, pg_ref, bg_ref, or_ref, om_ref, od_ref, wr_ref, wm_ref, wd_ref, wo_ref, gf_ref, x_out, h_out):
    halves = [slice(s * (TM_MERGE // 2), (s + 1) * (TM_MERGE // 2)) for s in range(2)]
    branch = [[_dot(o_ref[rs, :], w_ref[...]) for o_ref, w_ref in ((or_ref, wr_ref), (om_ref, wm_ref), (od_ref, wd_ref))]
              for rs in halves]
    for rs, d in zip(halves, branch):
        merged = None
        for idx in range(3):
            cs = slice(idx * D_MODEL, (idx + 1) * D_MODEL)
            term = jax.nn.sigmoid(pg_ref[rs, cs].astype(F32) + bg_ref[:, cs]) * d[idx]
            merged = term if merged is None else merged + term
        x1 = x_ref[rs, :] + _dot(merged.astype(BF16), wo_ref[...])
        x_out[rs, :] = x1
        h_out[rs, :] = _rms(x1, gf_ref[...], NORM_EPS).astype(BF16)


def _merge(x, p_gate, b_gate, o_r, o_m, o_d, w_r, w_m, w_d, w_o, g_ffn):
    t = x.shape[0]
    tm = TM_MERGE
    (bg_spec,), (bg,) = _const_operands(b_gate)
    specs, params = _const_operands(w_r, w_m, w_d, w_o, g_ffn)
    return pl.pallas_call(
        _merge_kernel,
        grid=(t // tm,),
        in_specs=[_row_spec(tm, D_MODEL), _row_spec(tm, GATE_COLS), bg_spec,
                  _row_spec(tm, 512), _row_spec(tm, 512), _row_spec(tm, 512)] + specs,
        out_specs=[_row_spec(tm, D_MODEL), _row_spec(tm, D_MODEL)],
        out_shape=[jax.ShapeDtypeStruct((t, D_MODEL), F32), jax.ShapeDtypeStruct((t, D_MODEL), BF16)],
        compiler_params=_params(),
        name="merge",
    )(x, p_gate, bg, o_r, o_m, o_d, *params)


def _ffn_kernel(x_ref, h_ref, wup_ref, cw_ref, cb_ref, wdn_ref, gfin_ref, o_ref, carry_ref, *, tiles_per_seq, final_norm):
    tm = TM_FFN

    @pl.when(pl.program_id(0) % tiles_per_seq == 0)
    def _():
        carry_ref[0:8, :] = jnp.zeros((8, carry_ref.shape[1]), F32)

    h = h_ref[...]

    def conv(u, cols):
        carry_ref[8:, cols] = u
        u1 = carry_ref[7:7 + tm, cols]
        u2 = carry_ref[6:6 + tm, cols]
        carry_ref[0:8, cols] = u[tm - 8:tm, :]
        return cw_ref[0:1, cols] * u2 + cw_ref[1:2, cols] * u1 + cw_ref[2:3, cols] * u + cb_ref[:, cols]

    def up(ck):
        gc = slice(ck * FF_CHUNK, (ck + 1) * FF_CHUNK)
        vc = slice(D_FF + ck * FF_CHUNK, D_FF + (ck + 1) * FF_CHUNK)
        return _dot(h, wup_ref[:, gc]), _dot(h, wup_ref[:, vc])

    n_chunks = D_FF // FF_CHUNK
    nxt = up(0)
    acts = []
    for ck in range(n_chunks):
        ug, uv = nxt
        if ck + 1 < n_chunks:
            nxt = up(ck + 1)
        gc = slice(ck * FF_CHUNK, (ck + 1) * FF_CHUNK)
        vc = slice(D_FF + ck * FF_CHUNK, D_FF + (ck + 1) * FF_CHUNK)
        gate = conv(ug, gc)
        val = conv(uv, vc)
        acts.append((gate * jax.nn.sigmoid(gate) * val).astype(BF16))
    out = x_ref[...] + _dot(jnp.concatenate(acts, axis=-1), wdn_ref[...])
    if final_norm:
        out = _rms(out, gfin_ref[...], NORM_EPS)
    o_ref[...] = out


def _ffn(x1, h2, w_up, conv_w, conv_b, w_down, g_final, seq, final_norm):
    t = x1.shape[0]
    tm = TM_FFN
    kern = functools.partial(_ffn_kernel, tiles_per_seq=seq // tm, final_norm=final_norm)
    specs, params = _const_operands(w_up, conv_w, conv_b, w_down, g_final)
    return pl.pallas_call(
        kern,
        grid=(t // tm,),
        in_specs=[_row_spec(tm, D_MODEL), _row_spec(tm, D_MODEL)] + specs,
        out_specs=_row_spec(tm, D_MODEL),
        out_shape=jax.ShapeDtypeStruct((t, D_MODEL), F32),
        scratch_shapes=[pltpu.VMEM((8 + tm, 2 * D_FF), F32)],
        compiler_params=_params(),
        name="conv_ffn",
    )(x1, h2, *params)


def _mla_weights(w_uq, w_ukv):
    qd = MLA_NOPE + MLA_ROPE
    half = MLA_ROPE // 2
    wq = w_uq.reshape(MLA_Q_LORA, MLA_HEADS, qd)
    zq = jnp.zeros((MLA_Q_LORA, MLA_HEADS, LANES - qd), F32)
    wq_main = jnp.concatenate([wq, zq], axis=-1)
    x1 = wq[:, :, MLA_NOPE:MLA_NOPE + half]
    x2 = wq[:, :, MLA_NOPE + half:]
    wq_rot = jnp.concatenate([jnp.zeros((MLA_Q_LORA, MLA_HEADS, MLA_NOPE), F32), -x2, x1, zq], axis=-1)
    wkv = w_ukv.reshape(MLA_KV_LORA, MLA_HEADS, MLA_NOPE + MLA_V)
    zkv = jnp.zeros((MLA_KV_LORA, MLA_HEADS, LANES - MLA_NOPE), F32)
    wk = jnp.concatenate([wkv[:, :, :MLA_NOPE], zkv], axis=-1)
    wv = jnp.concatenate([wkv[:, :, MLA_NOPE:], jnp.zeros((MLA_KV_LORA, MLA_HEADS, MLA_VT_ROWS - MLA_V), F32)], axis=-1)
    flat = lambda w: w.reshape(w.shape[0], -1).astype(BF16)
    return flat(wq_main), flat(wq_rot), flat(wk), flat(wv).T


def kernel(x, positions, rel_bias, norm_mix, w_in, b_gate, rwkv_mu, rwkv_w0, rwkv_w2, rwkv_a0, rwkv_a2, rwkv_g2, rwkv_k_k, rwkv_k_a, rwkv_r_k, rwkv_ln_w, rwkv_ln_b, mla_q_norm, mla_w_uq, mla_kv_norm, mla_w_ukv, diff_lambda, diff_subln, w_branch_rwkv, w_branch_mla, w_branch_diff, w_o, norm_ffn, ffn_w_up, ffn_conv_w, ffn_conv_b, ffn_w_down, norm_final):
    batch, seq, _ = x.shape
    depth = w_in.shape[0]
    t = batch * seq
    assert seq % TQ == 0 and seq % TM == 0 and seq % TM_FFN == 0 and seq % RWKV_TILE == 0 and RWKV_TILE % CHUNK == 0
    assert (batch * seq) % TM_MERGE == 0
    xf = x.reshape(t, D_MODEL)
    pos_col = positions.reshape(t, 1)
    inv_freq = ROPE_BASE ** (-jnp.arange(0, MLA_ROPE, 2, dtype=F32) / MLA_ROPE)
    freq = jnp.concatenate([jnp.zeros((MLA_NOPE,), F32), inv_freq, inv_freq,
                            jnp.zeros((LANES - MLA_NOPE - MLA_ROPE,), F32)]).reshape(1, LANES)
    vec = lambda v: v.reshape(v.shape[0], 1, -1)
    bf = lambda w: w.astype(BF16)
    diff_scale = jnp.concatenate([jnp.full((512,), DIFF_QK ** -0.5 * LOG2E, F32), jnp.ones((512,), F32)])
    s0, s1, s2 = RWKV_COLS, RWKV_COLS + MLA_COLS, RWKV_COLS + MLA_COLS + DIFF_COLS
    sv = s1 + DIFF_QK_COLS
    st = {
        "norm_mix": vec(norm_mix),
        "w_r": bf(w_in[:, :, :s0]),
        "w_m": bf(jnp.pad(w_in[:, :, s0:s1], ((0, 0), (0, 0), (0, MLA_PAD - MLA_COLS)))),
        "w_qk": bf(w_in[:, :, s1:sv] * diff_scale),
        "w_vt": bf(jnp.swapaxes(w_in[:, :, sv:s2], 1, 2)),
        "w_g": bf(w_in[:, :, s2:]),
        "q_norm": vec(mla_q_norm), "kv_norm": vec(mla_kv_norm),
        "mu": vec(rwkv_mu), "w0": vec(rwkv_w0), "w2": bf(rwkv_w2), "a0": vec(rwkv_a0), "a2": bf(rwkv_a2),
        "g2": bf(rwkv_g2), "k_k": vec(rwkv_k_k), "k_a": vec(rwkv_k_a), "r_k": vec(rwkv_r_k),
        "ln_w": vec(rwkv_ln_w), "ln_b": vec(rwkv_ln_b),
        "lam": diff_lambda, "subln": vec(diff_subln),
        "b_gate": vec(b_gate), "wb_r": bf(w_branch_rwkv), "wb_m": bf(w_branch_mla), "wb_d": bf(w_branch_diff),
        "w_o": bf(w_o), "norm_ffn": vec(norm_ffn),
        "w_up": bf(ffn_w_up), "conv_w": ffn_conv_w, "conv_b": vec(ffn_conv_b), "w_down": bf(ffn_w_down),
    }
    g_final = norm_final.reshape(1, -1)

    for l in range(depth):
        p = {name: _Layer(a, l) for name, a in st.items()}
        wq, wqr, wk, wv = _mla_weights(mla_w_uq[l], mla_w_ukv[l])
        p_rwkv, q_m, k_m, v_m, p_qk, p_vt, p_gate = _in_proj(
            xf, p["norm_mix"], p["w_r"], p["w_m"], p["w_qk"], p["w_g"], p["w_vt"], pos_col, freq, p["q_norm"],
            p["kv_norm"], wq, wqr, wk, wv)

        o_r = _rwkv(p_rwkv, seq, p["mu"], p["w0"], p["w2"], p["a0"], p["a2"], p["g2"], p["k_k"], p["k_a"], p["r_k"],
                    p["ln_w"], p["ln_b"])

        o_m = _mla_attn(q_m, k_m, v_m, batch, seq)

        o_d = _diff_attn(p_qk, p_vt, rel_bias, p["lam"], p["subln"], batch, seq, l)

        x1, h2 = _merge(xf, p_gate, p["b_gate"], o_r, o_m, o_d, p["wb_r"], p["wb_m"], p["wb_d"], p["w_o"], p["norm_ffn"])
        xf = _ffn(x1, h2, p["w_up"], p["conv_w"], p["conv_b"], p["w_down"], g_final, seq, final_norm=(l == depth - 1))
    return xf.reshape(batch, seq, D_MODEL)
```

```python
import functools
import math

import numpy as np
import jax
import jax.numpy as jnp
from jax import lax
from jax.experimental import pallas as pl
from jax.experimental.pallas import tpu as pltpu

F32 = jnp.float32
BF16 = jnp.bfloat16

D_MODEL = 1024
RWKV_HEADS = 8
RWKV_N = 64
RWKV_DIM = 512
RWKV_W_LORA = 64
RWKV_A_LORA = 64
RWKV_G_LORA = 128
RWKV_COLS = 3 * RWKV_DIM + RWKV_W_LORA + RWKV_A_LORA + RWKV_G_LORA
RWKV_GN_EPS = 64e-5
MLA_HEADS = 8
MLA_Q_LORA = 256
MLA_KV_LORA = 128
MLA_NOPE = 64
MLA_ROPE = 32
MLA_V = 64
MLA_VT_ROWS = MLA_V + 16
MLA_COLS = 416
MLA_PAD = 512
ROPE_BASE = 10000.0
DIFF_HEADS = 4
DIFF_QK = 64
DIFF_V = 128
DIFF_COLS = 1536
DIFF_QK_COLS = 1024
DIFF_VT_ROWS = DIFF_V + 16
REL_BUCKETS = 32
REL_MAX_DISTANCE = 128
D_FF = 2816
GATE_COLS = 3072
NORM_EPS = 1e-6
SUBLN_EPS = 1e-5

LANES = 128
VMEM_LIMIT = 58 * 1024 * 1024
TM = 512
TM_FFN = 256
TM_MERGE = 1024
CHUNK = 64
RWKV_TILE = 256
TQ = 512
KEY_TILES_PER_STEP = 2
FF_CHUNK = 256
NEG = -1e30
LOG2E = 1.4426950408889634

_NT = (((1,), (1,)), ((), ()))
_TN = (((0,), (0,)), ((), ()))


def _dot(a, b):
    return jnp.dot(a, b, preferred_element_type=F32)


def _dot_nt(a, b):
    return lax.dot_general(a, b, _NT, preferred_element_type=F32)


def _dot_tn(a, b):
    return lax.dot_general(a, b, _TN, preferred_element_type=F32)


def _rms(x, g, eps):
    return x * lax.rsqrt(jnp.mean(x * x, axis=-1, keepdims=True) + eps) * g


def _params():
    return pltpu.CompilerParams(dimension_semantics=("arbitrary",), vmem_limit_bytes=VMEM_LIMIT)


def _params2():
    return pltpu.CompilerParams(dimension_semantics=("arbitrary", "arbitrary"), vmem_limit_bytes=VMEM_LIMIT)


def _const_spec(shape):
    return pl.BlockSpec(shape, lambda *_: (0,) * len(shape), pipeline_mode=pl.Buffered(1))


class _Layer:
    def __init__(self, stacked, layer):
        self.stacked, self.layer = stacked, layer


def _const_operands(*params):
    specs, arrays = [], []
    for p in params:
        if isinstance(p, _Layer):
            shape = p.stacked.shape[1:]
            specs.append(pl.BlockSpec((None,) + shape, lambda *_, l=p.layer, n=len(shape): (l,) + (0,) * n,
                                      pipeline_mode=pl.Buffered(1)))
            arrays.append(p.stacked)
        else:
            specs.append(_const_spec(p.shape))
            arrays.append(p)
    return specs, arrays


def _row_spec(tm, cols, col_block=0):
    return pl.BlockSpec((tm, cols), lambda i: (i, col_block))


def _in_proj_kernel(x_ref, g_ref, wr_ref, wm_ref, wqk_ref, wg_ref, wvt_ref, pos_ref, freq_ref, qn_ref, kvn_ref, wq_ref,
                    wqr_ref, wk_ref, wv_ref, pr_ref, mq_ref, mk_ref, mvt_ref, pqk_ref, pvt_ref, pg_ref):
    h = _rms(x_ref[...], g_ref[...], NORM_EPS).astype(BF16)
    p_mla = _dot(h, wm_ref[...])
    pqk_ref[...] = _dot(h, wqk_ref[...]).astype(BF16)
    pg_ref[...] = _dot(h, wg_ref[...]).astype(BF16)
    _mla_prep_body(p_mla, pos_ref, freq_ref, qn_ref, kvn_ref, wq_ref, wqr_ref, wk_ref, wv_ref, mq_ref, mk_ref, mvt_ref)
    pr_ref[...] = _dot(h, wr_ref[...])
    pvt_ref[...] = _dot_nt(wvt_ref[...], h).astype(BF16)


def _in_proj(x, g, w_r, w_m, w_qk, w_g, w_vt, pos_col, freq, q_norm, kv_norm, wq, wqr, wk, wv):
    t = x.shape[0]
    vw = DIFF_HEADS * DIFF_V
    hw = MLA_HEADS * LANES
    mvw = MLA_HEADS * MLA_VT_ROWS
    c1, a1 = _const_operands(g, w_r, w_m, w_qk, w_g, w_vt)
    c2, a2 = _const_operands(freq, q_norm, kv_norm, wq, wqr, wk, wv)
    return pl.pallas_call(
        _in_proj_kernel,
        grid=(t // TM,),
        in_specs=[_row_spec(TM, D_MODEL)] + c1 + [_row_spec(TM, 1)] + c2,
        out_specs=[_row_spec(TM, RWKV_COLS), _row_spec(TM, hw), _row_spec(TM, hw),
                   pl.BlockSpec((mvw, TM), lambda i: (0, i)), _row_spec(TM, DIFF_QK_COLS),
                   pl.BlockSpec((vw, TM), lambda i: (0, i)), _row_spec(TM, GATE_COLS)],
        out_shape=[jax.ShapeDtypeStruct((t, RWKV_COLS), F32), jax.ShapeDtypeStruct((t, hw), BF16),
                   jax.ShapeDtypeStruct((t, hw), BF16), jax.ShapeDtypeStruct((mvw, t), BF16),
                   jax.ShapeDtypeStruct((t, DIFF_QK_COLS), BF16), jax.ShapeDtypeStruct((vw, t), BF16),
                   jax.ShapeDtypeStruct((t, GATE_COLS), BF16)],
        compiler_params=_params(),
        name="in_proj",
    )(x, *a1, pos_col, *a2)


def _split3(a):
    hi = a.astype(BF16)
    r1 = a - hi.astype(F32)
    mid = r1.astype(BF16)
    lo = (r1 - mid.astype(F32)).astype(BF16)
    return hi, mid, lo


def _softplus(z):
    return jnp.maximum(z, 0.0) + jnp.log(1.0 + jnp.exp(-jnp.abs(z)))


def _rwkv_kernel(p_ref, mu_ref, w0_ref, w2_ref, a0_ref, a2_ref, g2_ref, kk_ref, ka_ref, rk_ref, lnw_ref, lnb_ref,
                 o_ref, carry_ref, state_ref, *, tiles_per_seq):
    c = CHUNK
    n = RWKV_N
    ts = RWKV_TILE
    nc = ts // c

    @pl.when(pl.program_id(0) % tiles_per_seq == 0)
    def _():
        carry_ref[...] = jnp.zeros_like(carry_ref)
        state_ref[...] = jnp.zeros_like(state_ref)

    p = p_ref[...]
    row = lax.broadcasted_iota(jnp.int32, (ts, 1), 0)
    shifted = jnp.where(row == 0, carry_ref[...], pltpu.roll(p, 1, 0))
    carry_ref[...] = p[ts - 1:ts, :]
    pm = p + (shifted - p) * mu_ref[...]

    d = RWKV_DIM
    r = pm[:, 0:d]
    k = pm[:, d:2 * d]
    v = pm[:, 2 * d:3 * d]
    pw = pm[:, 3 * d:3 * d + RWKV_W_LORA]
    pa = pm[:, 3 * d + RWKV_W_LORA:3 * d + RWKV_W_LORA + RWKV_A_LORA]
    pg = pm[:, 3 * d + RWKV_W_LORA + RWKV_A_LORA:RWKV_COLS]

    w_log = -_softplus(-(w0_ref[...] + _dot(jnp.tanh(pw).astype(BF16), w2_ref[...]))) - 0.5
    logd = -jnp.exp(w_log)
    a = jax.nn.sigmoid(a0_ref[...] + _dot(pa.astype(BF16), a2_ref[...]))
    g = _dot(jax.nn.sigmoid(pg).astype(BF16), g2_ref[...])

    tr = lax.broadcasted_iota(jnp.int32, (ts, ts), 0)
    tc = lax.broadcasted_iota(jnp.int32, (ts, ts), 1)
    tri = ((tr >= tc) & (tr // c == tc // c)).astype(BF16)
    hi, mid, lo = _split3(logd)
    cs = _dot(tri, hi) + _dot(tri, mid) + _dot(tri, lo)
    total = jnp.concatenate([jnp.broadcast_to(cs[(m + 1) * c - 1:(m + 1) * c, :], (c, RWKV_DIM)) for m in range(nc)], axis=0)
    e_in = jnp.exp(cs)
    e_ex = jnp.exp(cs - logd)
    e_inv = jnp.exp(-cs)
    e_end = jnp.exp(total - cs)
    g_end = jnp.exp(total)

    n_pairs = RWKV_HEADS // 2
    lo_t = lax.broadcasted_iota(jnp.int32, (ts, LANES), 1) < n
    lo_c = lax.broadcasted_iota(jnp.int32, (c, LANES), 1) < n

    def head_sum(x):
        blocks = []
        for q in range(n_pairs):
            xb = x[:, q * LANES:(q + 1) * LANES]
            s_lo = jnp.sum(jnp.where(lo_t, xb, 0.0), axis=-1, keepdims=True)
            s_hi = jnp.sum(jnp.where(lo_t, 0.0, xb), axis=-1, keepdims=True)
            blocks.append(jnp.where(lo_t, s_lo, s_hi))
        return jnp.concatenate(blocks, axis=-1)

    k2 = k * (1.0 + (a - 1.0) * ka_ref[...])
    kku = k * kk_ref[...]
    kkn = kku / jnp.maximum(jnp.sqrt(head_sum(kku * kku)), 1e-12)
    b = kkn * a
    full = {"at": -kkn * e_ex, "rt": r * e_in, "bt": b * e_inv, "kt": k2 * e_inv, "v": v,
            "be": b * e_end, "ke": k2 * e_end}

    ti = lax.broadcasted_iota(jnp.int32, (2 * c, 4 * c), 0) % c
    si = lax.broadcasted_iota(jnp.int32, (2 * c, 4 * c), 1) % c
    strict = ti > si
    incl = ti >= si
    eye = (lax.broadcasted_iota(jnp.int32, (2 * c, 2 * c), 0)
           == lax.broadcasted_iota(jnp.int32, (2 * c, 2 * c), 1)).astype(F32)

    items = [(m, q) for m in range(nc) for q in range(n_pairs)]

    def stacked(name, m, q):
        xb = full[name][m * c:(m + 1) * c, q * LANES:(q + 1) * LANES]
        return jnp.concatenate([jnp.where(lo_c, xb, 0.0), jnp.where(lo_c, 0.0, xb)], axis=0).astype(BF16)

    st_ops = {name: [stacked(name, m, q) for m, q in items] for name in full}
    big_l = [_dot_nt(jnp.concatenate([at, rtb], axis=0), jnp.concatenate([bt, kt], axis=0))
             for at, rtb, bt, kt in zip(st_ops["at"], st_ops["rt"], st_ops["bt"], st_ops["kt"])]
    top_l = [jnp.where(strict, big[:2 * c, :], 0.0) for big in big_l]
    bot_l = [jnp.where(incl, big[2 * c:, :], 0.0).astype(BF16) for big in big_l]
    lakv_l = [_dot(top[:, 2 * c:].astype(BF16), vb) for top, vb in zip(top_l, st_ops["v"])]

    x_l = [top[:, :2 * c] for top in top_l]
    tinv_l = [eye + x for x in x_l]
    xb_l = [x.astype(BF16) for x in x_l]
    for _ in range(int(math.log2(c)) - 1):
        x_l = [_dot(xb, xb) for xb in xb_l]
        xb_l = [x.astype(BF16) for x in x_l]
        tinv_l = [tinv + _dot(tinv.astype(BF16), xb) for tinv, xb in zip(tinv_l, xb_l)]
    tinvb_l = [tinv.astype(BF16) for tinv in tinv_l]
    abar_l = [_dot(tb, at).astype(BF16) for tb, at in zip(tinvb_l, st_ops["at"])]
    vbar_l = [_dot(tb, lv.astype(BF16)).astype(BF16) for tb, lv in zip(tinvb_l, lakv_l)]
    rhat_l = [(rtb.astype(F32) + _dot(bot[:, :2 * c], ab)).astype(BF16)
              for rtb, bot, ab in zip(st_ops["rt"], bot_l, abar_l)]
    uv_l = [jnp.concatenate([vbar, vb], axis=0) for vbar, vb in zip(vbar_l, st_ops["v"])]
    y0_l = [_dot(bot, uv) for bot, uv in zip(bot_l, uv_l)]
    p_l = [_dot_tn(ab, be).astype(BF16) for ab, be in zip(abar_l, st_ops["be"])]
    q_l = [_dot_tn(uv, jnp.concatenate([be, ke], axis=0)) for uv, be, ke in zip(uv_l, st_ops["be"], st_ops["ke"])]

    st_l = [state_ref[q] for q in range(n_pairs)]
    y_rows = []
    for m in range(nc):
        y_blocks = []
        for q in range(n_pairs):
            idx = m * n_pairs + q
            st = st_l[q]
            st_b = st.astype(BF16)
            y_s = _dot_nt(rhat_l[idx], st_b) + y0_l[idx]
            y_blocks.append(y_s[:c] + y_s[c:])
            decay = g_end[(m + 1) * c - 1:(m + 1) * c, q * LANES:(q + 1) * LANES]
            st_l[q] = st * decay + _dot(st_b, p_l[idx]) + q_l[idx]
        y_rows.append(jnp.concatenate(y_blocks, axis=-1))
    for q in range(n_pairs):
        state_ref[q] = st_l[q]
    y = jnp.concatenate(y_rows, axis=0)

    inv_n = 1.0 / n
    yc = y - head_sum(y) * inv_n
    var = head_sum(yc * yc) * inv_n
    yn = yc * lax.rsqrt(var + RWKV_GN_EPS) * lnw_ref[...] + lnb_ref[...]
    bonus = head_sum(r * k2 * rk_ref[...]) * v
    o_ref[...] = ((yn + bonus) * g).astype(BF16)


def _rwkv(p_rwkv, seq, mu, w0, w2, a0, a2, g2, k_k, k_a, r_k, ln_w, ln_b):
    t = p_rwkv.shape[0]
    ts = RWKV_TILE
    kern = functools.partial(_rwkv_kernel, tiles_per_seq=seq // ts)
    specs, params = _const_operands(mu, w0, w2, a0, a2, g2, k_k, k_a, r_k, ln_w, ln_b)
    return pl.pallas_call(
        kern,
        grid=(t // ts,),
        in_specs=[_row_spec(ts, RWKV_COLS)] + specs,
        out_specs=_row_spec(ts, RWKV_DIM),
        out_shape=jax.ShapeDtypeStruct((t, RWKV_DIM), BF16),
        scratch_shapes=[pltpu.VMEM((1, RWKV_COLS), F32), pltpu.VMEM((RWKV_HEADS // 2, LANES, LANES), F32)],
        compiler_params=_params(),
        name="rwkv7",
    )(p_rwkv, *params)


def _mla_prep_body(p, pos_ref, freq_ref, qn_ref, kvn_ref, wq_ref, wqr_ref, wk_ref, wv_ref, q_out, k_out, v_out):
    hq = _rms(p[:, 0:MLA_Q_LORA], qn_ref[...], NORM_EPS).astype(BF16)
    hkv = _rms(p[:, MLA_Q_LORA:MLA_Q_LORA + MLA_KV_LORA], kvn_ref[...], NORM_EPS).astype(BF16)
    c_rope = MLA_Q_LORA + MLA_KV_LORA
    blk = p[:, c_rope:c_rope + LANES]
    ang = pos_ref[...].astype(F32) * freq_ref[...]
    cos = jnp.cos(ang)
    sin = jnp.sin(ang)
    scale = (MLA_NOPE + MLA_ROPE) ** -0.5 * LOG2E
    qa = _dot(hq, wq_ref[...])
    qr = _dot(hq, wqr_ref[...])
    lane = lax.broadcasted_iota(jnp.int32, blk.shape, 1)
    half = MLA_ROPE // 2
    kr = pltpu.roll(blk, MLA_NOPE, 1)
    rot = jnp.where(lane < MLA_NOPE + half, -pltpu.roll(blk, MLA_NOPE - half, 1), pltpu.roll(blk, MLA_NOPE + half, 1))
    rot = jnp.where((lane >= MLA_NOPE) & (lane < MLA_NOPE + MLA_ROPE), rot, 0.0)
    krope = kr * cos + rot * sin
    kn = _dot(hkv, wk_ref[...])
    for h in range(MLA_HEADS):
        hs = slice(h * LANES, (h + 1) * LANES)
        q_out[:, hs] = ((qa[:, hs] * cos + qr[:, hs] * sin) * scale).astype(BF16)
        k_out[:, hs] = (kn[:, hs] + krope).astype(BF16)
    vt = _dot_nt(wv_ref[...], hkv)
    rowi = lax.broadcasted_iota(jnp.int32, vt.shape, 0)
    v_out[...] = (vt + (rowi % MLA_VT_ROWS == MLA_V).astype(F32)).astype(BF16)


def _softmax_stage_t(s_l, pv_fn, m_scr, acc_scr, idx, s_scr=None, l_scr=None):
    n = len(s_l)
    m_old = [m_scr[c][0:1, :] for c in idx]
    m_new = [jnp.maximum(m, jnp.max(s, axis=0, keepdims=True)) for m, s in zip(m_old, s_l)]
    alpha = [jnp.exp2(mo - mn) for mo, mn in zip(m_old, m_new)]
    if s_scr is not None:
        for c in range(n):
            s_scr[c] = s_l[c]
        s_l = [s_scr[c] for c in range(n)]
    pv_l = []
    for c in range(n):
        p = jnp.exp2(s_l[c] - m_new[c])
        if l_scr is not None:
            l_new = alpha[c] * l_scr[idx[c]][0:1, :] + jnp.sum(p, axis=0, keepdims=True)
            l_scr[idx[c]] = jnp.broadcast_to(l_new, l_scr.shape[1:])
        pv_l.append(pv_fn(c, p.astype(BF16)))
    for c in range(n):
        m_scr[idx[c]] = jnp.broadcast_to(m_new[c], m_scr.shape[1:])
        acc_scr[idx[c]] = alpha[c] * acc_scr[idx[c]] + pv_l[c]


def _mla_attn_kernel(q_ref, k_ref, vt_ref, o_ref, m_scr, acc_scr, s_scr):
    tq = TQ
    i = pl.program_id(1)
    causal = lax.broadcasted_iota(jnp.int32, (tq, tq), 0) <= lax.broadcasted_iota(jnp.int32, (tq, tq), 1)
    m_scr[...] = jnp.full(m_scr.shape, NEG, F32)
    acc_scr[...] = jnp.zeros(acc_scr.shape, F32)
    hsl = [slice(h * LANES, (h + 1) * LANES) for h in range(MLA_HEADS)]
    vsl =[slice(h * MLA_VT_ROWS, (h + 1) * MLA_VT_ROWS) for h in range(MLA_HEADS)]
    heads = list(range(MLA_HEADS))

    def tile(off, tk, mask):
        s_l = [_dot_nt(k_ref[pl.ds(off, tk), hs], q_ref[:, hs]) for hs in hsl]
        if mask:
            s_l = [jnp.where(causal, s, NEG) for s in s_l]
        _softmax_stage_t(s_l, lambda c, p: _dot(vt_ref[vsl[c], pl.ds(off, tk)], p), m_scr, acc_scr, heads,
                         s_scr.at[:, 0:tk, :])

    def pair_body(j, carry):
        tile(pl.multiple_of(j * 2 * tq, 2 * tq), 2 * tq, False)
        return carry

    def single_body(j, carry):
        tile(pl.multiple_of(j * tq, tq), tq, False)
        return carry

    n_pairs = i // 2 if KEY_TILES_PER_STEP == 2 else 0
    if KEY_TILES_PER_STEP == 2:
        lax.fori_loop(0, n_pairs, pair_body, 0)
    lax.fori_loop(2 * n_pairs, i, single_body, 0)
    tile(pl.multiple_of(i * tq, tq), tq, True)
    for pr in range(MLA_HEADS // 2):
        accs = [acc_scr[2 * pr + hh] for hh in range(2)]
        ot = jnp.concatenate([acc[:MLA_V, :] / acc[MLA_V:MLA_V + 1, :] for acc in accs], axis=0)
        o_ref[:, pr * LANES:(pr + 1) * LANES] = jnp.transpose(ot).astype(BF16)


def _mla_attn(q, k, vt, batch, seq):
    t = q.shape[0]
    nq = seq // TQ
    hw = MLA_HEADS * LANES
    vw = MLA_HEADS * MLA_V
    return pl.pallas_call(
        _mla_attn_kernel,
        grid=(batch, nq),
        in_specs=[pl.BlockSpec((TQ, hw), lambda b, i: (b * nq + i, 0)),
                  pl.BlockSpec((seq, hw), lambda b, i: (b, 0)),
                  pl.BlockSpec((MLA_HEADS * MLA_VT_ROWS, seq), lambda b, i: (0, b))],
        out_specs=pl.BlockSpec((TQ, vw), lambda b, i: (b * nq + i, 0)),
        out_shape=jax.ShapeDtypeStruct((t, vw), BF16),
        scratch_shapes=[pltpu.VMEM((MLA_HEADS, 8, TQ), F32), pltpu.VMEM((MLA_HEADS, MLA_VT_ROWS, TQ), F32),
                        pltpu.VMEM((MLA_HEADS, KEY_TILES_PER_STEP * TQ, TQ), F32)],
        compiler_params=_params2(),
        name="mla_attn",
    )(q, k, vt)


def _t5_bucket_table():
    n = np.arange(0, REL_MAX_DISTANCE + 1)
    max_exact = REL_BUCKETS // 2
    nf = np.maximum(n, 1).astype(np.float32)
    ratio = np.log(nf / np.float32(max_exact)) / np.float32(math.log(REL_MAX_DISTANCE / max_exact))
    large = max_exact + (ratio * np.float32(REL_BUCKETS - max_exact)).astype(np.int32)
    large = np.minimum(large, REL_BUCKETS - 1)
    return np.where(n < max_exact, n, large)


_BUCKETS = _t5_bucket_table()
_FAR_BUCKET = int(_BUCKETS[REL_MAX_DISTANCE])
assert _FAR_BUCKET == REL_BUCKETS - 1 and np.all(np.diff(_BUCKETS) >= 0)
_BUCKET_STARTS = [int(np.argmax(_BUCKETS >= b)) for b in range(REL_BUCKETS // 2 + 1, REL_BUCKETS)]


def _diff_attn_kernel(rb_ref, lam_ref, subln_ref, q_ref, k_ref, vt_ref, o_ref, bias_scr, m_scr, acc_scr, s_scr, *,
                      lambda_init):
    tq = TQ
    i = pl.program_id(1)
    r_i = lax.broadcasted_iota(jnp.int32, (tq, tq), 0)
    c_i = lax.broadcasted_iota(jnp.int32, (tq, tq), 1)
    causal = r_i <= c_i

    @pl.when((pl.program_id(0) == 0) & (i == 0))
    def _():
        for t_idx, delta in enumerate((0, tq)):
            d = jnp.maximum(c_i - r_i + delta, 0)
            log_b = REL_BUCKETS // 2
            for start in _BUCKET_STARTS:
                log_b = log_b + (d >= start).astype(jnp.int32)
            bucket = jnp.where(d < REL_BUCKETS // 2, d, log_b)
            for h in range(DIFF_HEADS):
                bias = jnp.zeros((tq, tq), F32)
                for b in range(REL_BUCKETS):
                    bias = jnp.where(bucket == b, rb_ref[b, h], bias)
                bias_scr[t_idx, h] = (bias - rb_ref[_FAR_BUCKET, h]) * LOG2E

    m_scr[...] = jnp.full(m_scr.shape, NEG, F32)
    acc_scr[...] = jnp.zeros(acc_scr.shape, F32)
    lane = lax.broadcasted_iota(jnp.int32, (tq, LANES), 1)
    ones_rows = {tk: (lax.broadcasted_iota(jnp.int32, (DIFF_VT_ROWS - DIFF_V, tk), 0) == 0).astype(BF16)
                 for tk in (tq, 2 * tq)}
    hsl = [slice(h * LANES, (h + 1) * LANES) for h in range(DIFF_HEADS)]
    heads = list(range(DIFF_HEADS))
    q_l = []
    for hs in hsl:
        qp = q_ref[:, hs]
        q_l.append(jnp.concatenate([jnp.where(lane < DIFF_QK, qp, jnp.zeros_like(qp)),
                                    jnp.where(lane >= DIFF_QK, qp, jnp.zeros_like(qp))], axis=0))

    def tile(off, tk, bias_idx, mask):
        s_l = [_dot_nt(k_ref[pl.ds(off, tk), hsl[h]], q_l[h]) for h in heads]
        if bias_idx is not None:
            s_l = [s + jnp.concatenate([bias_scr[bias_idx, h]] * 2, axis=1) for h, s in enumerate(s_l)]
        if mask:
            mask2 = jnp.concatenate([causal, causal], axis=1)
            s_l = [jnp.where(mask2, s, NEG) for s in s_l]

        def pv(c, p):
            vt = jnp.concatenate([vt_ref[hsl[c], pl.ds(off, tk)], ones_rows[tk]], axis=0)
            return jnp.concatenate([_dot(vt, p[:, :tq]), _dot(vt, p[:, tq:])], axis=1)

        _softmax_stage_t(s_l, pv, m_scr, acc_scr, heads, s_scr.at[:, 0:tk, :])

    def far_pair_body(j, carry):
        tile(pl.multiple_of(j * 2 * tq, 2 * tq), 2 * tq, None, False)
        return carry

    def far_body(j, carry):
        tile(pl.multiple_of(j * tq, tq), tq, None, False)
        return carry

    def near_body(j, carry):
        tile(pl.multiple_of(j * tq, tq), tq, 1, False)
        return carry

    n_far = jnp.maximum(i - 1, 0)
    n_pairs = n_far // 2 if KEY_TILES_PER_STEP == 2 else 0
    if KEY_TILES_PER_STEP == 2:
        lax.fori_loop(0, n_pairs, far_pair_body, 0)
    lax.fori_loop(2 * n_pairs, n_far, far_body, 0)
    lax.fori_loop(n_far, i, near_body, 0)
    tile(pl.multiple_of(i * tq, tq), tq, 0, True)

    lam = lam_ref[...]
    lam_full = (jnp.exp(jnp.sum(lam[0:1] * lam[1:2], axis=-1, keepdims=True))
                - jnp.exp(jnp.sum(lam[2:3] * lam[3:4], axis=-1, keepdims=True)) + lambda_init)
    for h in heads:
        acc = acc_scr[h]
        on = acc[:DIFF_V, :] / acc[DIFF_V:DIFF_V + 1, :]
        ot = on[:, :tq] - lam_full * on[:, tq:]
        ot = ot * lax.rsqrt(jnp.mean(ot * ot, axis=0, keepdims=True) + SUBLN_EPS)
        o_ref[:, hsl[h]] = (jnp.transpose(ot) * subln_ref[...] * (1.0 - lambda_init)).astype(BF16)


def _diff_attn(p_qk, p_vt, rel_bias, lam, subln, batch, seq, layer_idx):
    t = p_qk.shape[0]
    nq = seq // TQ
    w = DIFF_HEADS * LANES
    lambda_init = 0.8 - 0.6 * math.exp(-0.3 * layer_idx)
    kern = functools.partial(_diff_attn_kernel, lambda_init=lambda_init)
    specs, params = _const_operands(lam, subln)
    return pl.pallas_call(
        kern,
        grid=(batch, nq),
        in_specs=[pl.BlockSpec(memory_space=pltpu.SMEM)] + specs + [
                  pl.BlockSpec((TQ, w), lambda b, i: (b * nq + i, 0)),
                  pl.BlockSpec((seq, w), lambda b, i: (b, 1)),
                  pl.BlockSpec((w, seq), lambda b, i: (0, b))],
        out_specs=pl.BlockSpec((TQ, w), lambda b, i: (b * nq + i, 0)),
        out_shape=jax.ShapeDtypeStruct((t, w), BF16),
        scratch_shapes=[pltpu.VMEM((2, DIFF_HEADS, TQ, TQ), F32), pltpu.VMEM((DIFF_HEADS, 8, 2 * TQ), F32),
                        pltpu.VMEM((DIFF_HEADS, DIFF_VT_ROWS, 2 * TQ), F32),
                        pltpu.VMEM((DIFF_HEADS, KEY_TILES_PER_STEP * TQ, 2 * TQ), F32)],
        compiler_params=_params2(),
        name="diff_attn",
    )(rel_bias, *params, p_qk, p_qk, p_vt)


def _merge_kernel(x_ref, pg_ref, bg_ref, or_ref, om_ref, od_ref, wr_ref, wm_ref, wd_ref, wo_ref, gf_ref, x_out, h_out):
    halves = [slice(s * (TM_MERGE // 2), (s + 1) * (TM_MERGE // 2)) for s in range(2)]
    branch = [[_dot(o_ref[rs, :], w_ref[...]) for o_ref, w_ref in ((or_ref, wr_ref), (om_ref, wm_ref), (od_ref, wd_ref))]
              for rs in halves]
    for rs, d in zip(halves, branch):
        merged = None
        for idx in range(3):
            cs = slice(idx * D_MODEL, (idx + 1) * D_MODEL)
            term = jax.nn.sigmoid(pg_ref[rs, cs].astype(F32) + bg_ref[:, cs]) * d[idx]
            merged = term if merged is None else merged + term
        x1 = x_ref[rs, :] + _dot(merged.astype(BF16), wo_ref[...])
        x_out[rs, :] = x1
        h_out[rs, :] = _rms(x1, gf_ref[...], NORM_EPS).astype(BF16)


def _merge(x, p_gate, b_gate, o_r, o_m, o_d, w_r, w_m, w_d, w_o, g_ffn):
    t = x.shape[0]
    tm = TM_MERGE
    (bg_spec,), (bg,) = _const_operands(b_gate)
    specs, params = _const_operands(w_r, w_m, w_d, w_o, g_ffn)
    return pl.pallas_call(
        _merge_kernel,
        grid=(t // tm,),
        in_specs=[_row_spec(tm, D_MODEL), _row_spec(tm, GATE_COLS), bg_spec,
                  _row_spec(tm, RWKV_DIM), _row_spec(tm, MLA_HEADS * MLA_V), _row_spec(tm, DIFF_HEADS * DIFF_V)] + specs,
        out_specs=[_row_spec(tm, D_MODEL), _row_spec(tm, D_MODEL)],
        out_shape=[jax.ShapeDtypeStruct((t, D_MODEL), F32), jax.ShapeDtypeStruct((t, D_MODEL), BF16)],
        compiler_params=_params(),
        name="merge",
    )(x, p_gate, bg, o_r, o_m, o_d, *params)


def _ffn_kernel(x_ref, h_ref, wup_ref, cw_ref, cb_ref, wdn_ref, gfin_ref, o_ref, carry_ref, *, tiles_per_seq, final_norm):
    tm = TM_FFN

    @pl.when(pl.program_id(0) % tiles_per_seq == 0)
    def _():
        carry_ref[0:8, :] = jnp.zeros((8, carry_ref.shape[1]), F32)

    h = h_ref[...]

    def conv(u, cols):
        carry_ref[8:, cols] = u
        u1 = carry_ref[7:7 + tm, cols]
        u2 = carry_ref[6:6 + tm, cols]
        carry_ref[0:8, cols] = u[tm - 8:tm, :]
        return cw_ref[0:1, cols] * u2 + cw_ref[1:2, cols] * u1 + cw_ref[2:3, cols] * u + cb_ref[:, cols]

    def up(ck):
        gc = slice(ck * FF_CHUNK, (ck + 1) * FF_CHUNK)
        vc = slice(D_FF + ck * FF_CHUNK, D_FF + (ck + 1) * FF_CHUNK)
        return _dot(h, wup_ref[:, gc]), _dot(h, wup_ref[:, vc])

    n_chunks = D_FF // FF_CHUNK
    nxt = up(0)
    acts = []
    for ck in range(n_chunks):
        ug, uv = nxt
        if ck + 1 < n_chunks:
            nxt = up(ck + 1)
        gc = slice(ck * FF_CHUNK, (ck + 1) * FF_CHUNK)
        vc = slice(D_FF + ck * FF_CHUNK, D_FF + (ck + 1) * FF_CHUNK)
        gate = conv(ug, gc)
        val = conv(uv, vc)
        acts.append((gate * jax.nn.sigmoid(gate) * val).astype(BF16))
    out = x_ref[...] + _dot(jnp.concatenate(acts, axis=-1), wdn_ref[...])
    if final_norm:
        out = _rms(out, gfin_ref[...], NORM_EPS)
    o_ref[...] = out


def _ffn(x1, h2, w_up, conv_w, conv_b, w_down, g_final, seq, final_norm):
    t = x1.shape[0]
    tm = TM_FFN
    kern = functools.partial(_ffn_kernel, tiles_per_seq=seq // tm, final_norm=final_norm)
    specs, params = _const_operands(w_up, conv_w, conv_b, w_down, g_final)
    return pl.pallas_call(
        kern,
        grid=(t // tm,),
        in_specs=[_row_spec(tm, D_MODEL), _row_spec(tm, D_MODEL)] + specs,
        out_specs=_row_spec(tm, D_MODEL),
        out_shape=jax.ShapeDtypeStruct((t, D_MODEL), F32),
        scratch_shapes=[pltpu.VMEM((8 + tm, 2 * D_FF), F32)],
        compiler_params=_params(),
        name="conv_ffn",
    )(x1, h2, *params)


def _mla_weights(w_uq, w_ukv):
    qd = MLA_NOPE + MLA_ROPE
    half = MLA_ROPE // 2
    wq = w_uq.reshape(MLA_Q_LORA, MLA_HEADS, qd)
    zq = jnp.zeros((MLA_Q_LORA, MLA_HEADS, LANES - qd), F32)
    wq_main = jnp.concatenate([wq, zq], axis=-1)
    x1 = wq[:, :, MLA_NOPE:MLA_NOPE + half]
    x2 = wq[:, :, MLA_NOPE + half:]
    wq_rot = jnp.concatenate([jnp.zeros((MLA_Q_LORA, MLA_HEADS, MLA_NOPE), F32), -x2, x1, zq], axis=-1)
    wkv = w_ukv.reshape(MLA_KV_LORA, MLA_HEADS, MLA_NOPE + MLA_V)
    zkv = jnp.zeros((MLA_KV_LORA, MLA_HEADS, LANES - MLA_NOPE), F32)
    wk = jnp.concatenate([wkv[:, :, :MLA_NOPE], zkv], axis=-1)
    wv = jnp.concatenate([wkv[:, :, MLA_NOPE:], jnp.zeros((MLA_KV_LORA, MLA_HEADS, MLA_VT_ROWS - MLA_V), F32)], axis=-1)
    flat = lambda w: w.reshape(w.shape[0], -1).astype(BF16)
    return flat(wq_main), flat(wq_rot), flat(wk), flat(wv).T


def kernel(x, positions, rel_bias, norm_mix, w_in, b_gate, rwkv_mu, rwkv_w0, rwkv_w2, rwkv_a0, rwkv_a2, rwkv_g2, rwkv_k_k, rwkv_k_a, rwkv_r_k, rwkv_ln_w, rwkv_ln_b, mla_q_norm, mla_w_uq, mla_kv_norm, mla_w_ukv, diff_lambda, diff_subln, w_branch_rwkv, w_branch_mla, w_branch_diff, w_o, norm_ffn, ffn_w_up, ffn_conv_w, ffn_conv_b, ffn_w_down, norm_final):
    batch, seq, _ = x.shape
    depth = w_in.shape[0]
    t = batch * seq
    assert seq % TQ == 0 and seq % TM == 0 and seq % TM_FFN == 0 and seq % RWKV_TILE == 0 and RWKV_TILE % CHUNK == 0
    assert (batch * seq) % TM_MERGE == 0
    xf = x.reshape(t, D_MODEL)
    pos_col = positions.reshape(t, 1)
    inv_freq = ROPE_BASE ** (-jnp.arange(0, MLA_ROPE, 2, dtype=F32) / MLA_ROPE)
    freq = jnp.concatenate([jnp.zeros((MLA_NOPE,), F32), inv_freq, inv_freq,
                            jnp.zeros((LANES - MLA_NOPE - MLA_ROPE,), F32)]).reshape(1, LANES)
    vec = lambda v: v.reshape(v.shape[0], 1, -1)
    bf = lambda w: w.astype(BF16)
    n_q = DIFF_QK_COLS // 2
    diff_scale = jnp.concatenate([jnp.full((n_q,), DIFF_QK ** -0.5 * LOG2E, F32), jnp.ones((n_q,), F32)])
    s0, s1, s2 = RWKV_COLS, RWKV_COLS + MLA_COLS, RWKV_COLS + MLA_COLS + DIFF_COLS
    sv = s1 + DIFF_QK_COLS
    st = {
        "norm_mix": vec(norm_mix),
        "w_r": bf(w_in[:, :, :s0]),
        "w_m": bf(jnp.pad(w_in[:, :, s0:s1], ((0, 0), (0, 0), (0, MLA_PAD - MLA_COLS)))),
        "w_qk": bf(w_in[:, :, s1:sv] * diff_scale),
        "w_vt": bf(jnp.swapaxes(w_in[:, :, sv:s2], 1, 2)),
        "w_g": bf(w_in[:, :, s2:]),
        "q_norm": vec(mla_q_norm), "kv_norm": vec(mla_kv_norm),
        "mu": vec(rwkv_mu), "w0": vec(rwkv_w0), "w2": bf(rwkv_w2), "a0": vec(rwkv_a0), "a2": bf(rwkv_a2),
        "g2": bf(rwkv_g2), "k_k": vec(rwkv_k_k), "k_a": vec(rwkv_k_a), "r_k": vec(rwkv_r_k),
        "ln_w": vec(rwkv_ln_w), "ln_b": vec(rwkv_ln_b),
        "lam": diff_lambda, "subln": vec(diff_subln),
        "b_gate": vec(b_gate), "wb_r": bf(w_branch_rwkv), "wb_m": bf(w_branch_mla), "wb_d": bf(w_branch_diff),
        "w_o": bf(w_o), "norm_ffn": vec(norm_ffn),
        "w_up": bf(ffn_w_up), "conv_w": ffn_conv_w, "conv_b": vec(ffn_conv_b), "w_down": bf(ffn_w_down),
    }
    g_final = norm_final.reshape(1, -1)

    for l in range(depth):
        p = {name: _Layer(a, l) for name, a in st.items()}
        wq, wqr, wk, wv = _mla_weights(mla_w_uq[l], mla_w_ukv[l])
        p_rwkv, q_m, k_m, v_m, p_qk, p_vt, p_gate = _in_proj(
            xf, p["norm_mix"], p["w_r"], p["w_m"], p["w_qk"], p["w_g"], p["w_vt"], pos_col, freq, p["q_norm"],
            p["kv_norm"], wq, wqr, wk, wv)

        o_r = _rwkv(p_rwkv, seq, p["mu"], p["w0"], p["w2"], p["a0"], p["a2"], p["g2"], p["k_k"], p["k_a"], p["r_k"],
                    p["ln_w"], p["ln_b"])

        o_m = _mla_attn(q_m, k_m, v_m, batch, seq)

        o_d = _diff_attn(p_qk, p_vt, rel_bias, p["lam"], p["subln"], batch, seq, l)

        x1, h2 = _merge(xf, p_gate, p["b_gate"], o_r, o_m, o_d, p["wb_r"], p["wb_m"], p["wb_d"], p["w_o"], p["norm_ffn"])
        xf = _ffn(x1, h2, p["w_up"], p["conv_w"], p["conv_b"], p["w_down"], g_final, seq, final_norm=(l == depth - 1))
    return xf.reshape(batch, seq, D_MODEL)
```

```python
import functools
import math

import numpy as np
import jax
import jax.numpy as jnp
from jax import lax
from jax.experimental import pallas as pl
from jax.experimental.pallas import tpu as pltpu

F32 = jnp.float32
BF16 = jnp.bfloat16

D_MODEL = 1024
RWKV_HEADS = 8
RWKV_N = 64
RWKV_DIM = 512
RWKV_W_LORA = 64
RWKV_A_LORA = 64
RWKV_G_LORA = 128
RWKV_COLS = 3 * RWKV_DIM + RWKV_W_LORA + RWKV_A_LORA + RWKV_G_LORA
RWKV_GN_EPS = 64e-5
MLA_HEADS = 8
MLA_Q_LORA = 256
MLA_KV_LORA = 128
MLA_NOPE = 64
MLA_ROPE = 32
MLA_V = 64
MLA_VT_ROWS = MLA_V + 16
MLA_COLS = 416
MLA_PAD = 512
ROPE_BASE = 10000.0
DIFF_HEADS = 4
DIFF_QK = 64
DIFF_V = 128
DIFF_COLS = 1536
DIFF_QK_COLS = 1024
DIFF_VT_ROWS = DIFF_V + 16
REL_BUCKETS = 32
REL_MAX_DISTANCE = 128
D_FF = 2816
GATE_COLS = 3072
NORM_EPS = 1e-6
SUBLN_EPS = 1e-5

LANES = 128
VMEM_LIMIT = 58 * 1024 * 1024
TM = 512
TM_FFN = 256
TM_MERGE = 1024
CHUNK = 64
RWKV_TILE = 512
TQ = 512
KEY_TILES_PER_STEP = 2
FF_CHUNK = 512
NEG = -1e30
LOG2E = 1.4426950408889634

_NT = (((1,), (1,)), ((), ()))
_TN = (((0,), (0,)), ((), ()))


def _dot(a, b):
    return jnp.dot(a, b, preferred_element_type=F32)


def _dot_nt(a, b):
    return lax.dot_general(a, b, _NT, preferred_element_type=F32)


def _dot_tn(a, b):
    return lax.dot_general(a, b, _TN, preferred_element_type=F32)


def _rms(x, g, eps):
    return x * lax.rsqrt(jnp.mean(x * x, axis=-1, keepdims=True) + eps) * g


def _params():
    return pltpu.CompilerParams(dimension_semantics=("arbitrary",), vmem_limit_bytes=VMEM_LIMIT)


def _params2():
    return pltpu.CompilerParams(dimension_semantics=("arbitrary", "arbitrary"), vmem_limit_bytes=VMEM_LIMIT)


def _const_spec(shape):
    return pl.BlockSpec(shape, lambda *_: (0,) * len(shape), pipeline_mode=pl.Buffered(1))


class _Layer:
    def __init__(self, stacked, layer):
        self.stacked, self.layer = stacked, layer


def _const_operands(*params):
    specs, arrays = [], []
    for p in params:
        if isinstance(p, _Layer):
            shape = p.stacked.shape[1:]
            specs.append(pl.BlockSpec((None,) + shape, lambda *_, l=p.layer, n=len(shape): (l,) + (0,) * n,
                                      pipeline_mode=pl.Buffered(1)))
            arrays.append(p.stacked)
        else:
            specs.append(_const_spec(p.shape))
            arrays.append(p)
    return specs, arrays


def _row_spec(tm, cols, col_block=0):
    return pl.BlockSpec((tm, cols), lambda i: (i, col_block))


def _in_proj_kernel(x_ref, g_ref, wr_ref, wm_ref, wqk_ref, wg_ref, wvt_ref, pos_ref, freq_ref, qn_ref, kvn_ref, wq_ref,
                    wqr_ref, wk_ref, wv_ref, pr_ref, mq_ref, mk_ref, mvt_ref, pqk_ref, pvt_ref, pg_ref):
    h = _rms(x_ref[...], g_ref[...], NORM_EPS).astype(BF16)
    p_mla = _dot(h, wm_ref[...])
    pqk_ref[...] = _dot(h, wqk_ref[...]).astype(BF16)
    pg_ref[...] = _dot(h, wg_ref[...]).astype(BF16)
    _mla_prep_body(p_mla, pos_ref, freq_ref, qn_ref, kvn_ref, wq_ref, wqr_ref, wk_ref, wv_ref, mq_ref, mk_ref, mvt_ref)
    pr_ref[...] = _dot(h, wr_ref[...])
    pvt_ref[...] = _dot_nt(wvt_ref[...], h).astype(BF16)


def _in_proj(x, g, w_r, w_m, w_qk, w_g, w_vt, pos_col, freq, q_norm, kv_norm, wq, wqr, wk, wv):
    t = x.shape[0]
    vw = DIFF_HEADS * DIFF_V
    hw = MLA_HEADS * LANES
    mvw = MLA_HEADS * MLA_VT_ROWS
    c1, a1 = _const_operands(g, w_r, w_m, w_qk, w_g, w_vt)
    c2, a2 = _const_operands(freq, q_norm, kv_norm, wq, wqr, wk, wv)
    return pl.pallas_call(
        _in_proj_kernel,
        grid=(t // TM,),
        in_specs=[_row_spec(TM, D_MODEL)] + c1 + [_row_spec(TM, 1)] + c2,
        out_specs=[_row_spec(TM, RWKV_COLS), _row_spec(TM, hw), _row_spec(TM, hw),
                   pl.BlockSpec((mvw, TM), lambda i: (0, i)), _row_spec(TM, DIFF_QK_COLS),
                   pl.BlockSpec((vw, TM), lambda i: (0, i)), _row_spec(TM, GATE_COLS)],
        out_shape=[jax.ShapeDtypeStruct((t, RWKV_COLS), F32), jax.ShapeDtypeStruct((t, hw), BF16),
                   jax.ShapeDtypeStruct((t, hw), BF16), jax.ShapeDtypeStruct((mvw, t), BF16),
                   jax.ShapeDtypeStruct((t, DIFF_QK_COLS), BF16), jax.ShapeDtypeStruct((vw, t), BF16),
                   jax.ShapeDtypeStruct((t, GATE_COLS), BF16)],
        compiler_params=_params(),
        name="in_proj",
    )(x, *a1, pos_col, *a2)


def _split3(a):
    hi = a.astype(BF16)
    r1 = a - hi.astype(F32)
    mid = r1.astype(BF16)
    lo = (r1 - mid.astype(F32)).astype(BF16)
    return hi, mid, lo


def _softplus(z):
    return jnp.maximum(z, 0.0) + jnp.log(1.0 + jnp.exp(-jnp.abs(z)))


def _rwkv_kernel(p_ref, mu_ref, w0_ref, w2_ref, a0_ref, a2_ref, g2_ref, kk_ref, ka_ref, rk_ref, lnw_ref, lnb_ref,
                 o_ref, carry_ref, state_ref, *, tiles_per_seq):
    c = CHUNK
    n = RWKV_N
    ts = RWKV_TILE
    nc = ts // c

    @pl.when(pl.program_id(0) % tiles_per_seq == 0)
    def _():
        carry_ref[...] = jnp.zeros_like(carry_ref)
        state_ref[...] = jnp.zeros_like(state_ref)

    p = p_ref[...]
    row = lax.broadcasted_iota(jnp.int32, (ts, 1), 0)
    shifted = jnp.where(row == 0, carry_ref[...], pltpu.roll(p, 1, 0))
    carry_ref[...] = p[ts - 1:ts, :]
    pm = p + (shifted - p) * mu_ref[...]

    d = RWKV_DIM
    r = pm[:, 0:d]
    k = pm[:, d:2 * d]
    v = pm[:, 2 * d:3 * d]
    pw = pm[:, 3 * d:3 * d + RWKV_W_LORA]
    pa = pm[:, 3 * d + RWKV_W_LORA:3 * d + RWKV_W_LORA + RWKV_A_LORA]
    pg = pm[:, 3 * d + RWKV_W_LORA + RWKV_A_LORA:RWKV_COLS]

    w_log = -_softplus(-(w0_ref[...] + _dot(jnp.tanh(pw).astype(BF16), w2_ref[...]))) - 0.5
    logd = -jnp.exp(w_log)
    a = jax.nn.sigmoid(a0_ref[...] + _dot(pa.astype(BF16), a2_ref[...]))
    g = _dot(jax.nn.sigmoid(pg).astype(BF16), g2_ref[...])

    tr = lax.broadcasted_iota(jnp.int32, (ts, ts), 0)
    tc = lax.broadcasted_iota(jnp.int32, (ts, ts), 1)
    tri = ((tr >= tc) & (tr // c == tc // c)).astype(BF16)
    hi, mid, lo = _split3(logd)
    cs = _dot(tri, hi) + _dot(tri, mid) + _dot(tri, lo)
    total = jnp.concatenate([jnp.broadcast_to(cs[(m + 1) * c - 1:(m + 1) * c, :], (c, RWKV_DIM)) for m in range(nc)], axis=0)
    e_in = jnp.exp(cs)
    e_ex = jnp.exp(cs - logd)
    e_inv = jnp.exp(-cs)
    e_end = jnp.exp(total - cs)
    g_end = jnp.exp(total)

    n_pairs = RWKV_HEADS // 2
    lo_t = lax.broadcasted_iota(jnp.int32, (ts, LANES), 1) < n
    lo_c = lax.broadcasted_iota(jnp.int32, (c, LANES), 1) < n

    def head_sum(x):
        blocks = []
        for q in range(n_pairs):
            xb = x[:, q * LANES:(q + 1) * LANES]
            s_lo = jnp.sum(jnp.where(lo_t, xb, 0.0), axis=-1, keepdims=True)
            s_hi = jnp.sum(jnp.where(lo_t, 0.0, xb), axis=-1, keepdims=True)
            blocks.append(jnp.where(lo_t, s_lo, s_hi))
        return jnp.concatenate(blocks, axis=-1)

    k2 = k * (1.0 + (a - 1.0) * ka_ref[...])
    kku = k * kk_ref[...]
    kkn = kku / jnp.maximum(jnp.sqrt(head_sum(kku * kku)), 1e-12)
    b = kkn * a
    full = {"at": -kkn * e_ex, "rt": r * e_in, "bt": b * e_inv, "kt": k2 * e_inv, "v": v,
            "be": b * e_end, "ke": k2 * e_end}

    ti = lax.broadcasted_iota(jnp.int32, (2 * c, 4 * c), 0) % c
    si = lax.broadcasted_iota(jnp.int32, (2 * c, 4 * c), 1) % c
    strict = ti > si
    incl = ti >= si
    eye = (lax.broadcasted_iota(jnp.int32, (2 * c, 2 * c), 0)
           == lax.broadcasted_iota(jnp.int32, (2 * c, 2 * c), 1)).astype(F32)

    items = [(m, q) for m in range(nc) for q in range(n_pairs)]

    def stacked(name, m, q):
        xb = full[name][m * c:(m + 1) * c, q * LANES:(q + 1) * LANES]
        return jnp.concatenate([jnp.where(lo_c, xb, 0.0), jnp.where(lo_c, 0.0, xb)], axis=0).astype(BF16)

    st_ops = {name: [stacked(name, m, q) for m, q in items] for name in full}
    big_l = [_dot_nt(jnp.concatenate([at, rtb], axis=0), jnp.concatenate([bt, kt], axis=0))
             for at, rtb, bt, kt in zip(st_ops["at"], st_ops["rt"], st_ops["bt"], st_ops["kt"])]
    top_l = [jnp.where(strict, big[:2 * c, :], 0.0) for big in big_l]
    bot_l = [jnp.where(incl, big[2 * c:, :], 0.0).astype(BF16) for big in big_l]
    lakv_l = [_dot(top[:, 2 * c:].astype(BF16), vb) for top, vb in zip(top_l, st_ops["v"])]

    x_l = [top[:, :2 * c] for top in top_l]
    tinv_l = [eye + x for x in x_l]
    xb_l = [x.astype(BF16) for x in x_l]
    for _ in range(int(math.log2(c)) - 1):
        x_l = [_dot(xb, xb) for xb in xb_l]
        xb_l = [x.astype(BF16) for x in x_l]
        tinv_l = [tinv + _dot(tinv.astype(BF16), xb) for tinv, xb in zip(tinv_l, xb_l)]
    tinvb_l = [tinv.astype(BF16) for tinv in tinv_l]
    abar_l = [_dot(tb, at).astype(BF16) for tb, at in zip(tinvb_l, st_ops["at"])]
    vbar_l = [_dot(tb, lv.astype(BF16)).astype(BF16) for tb, lv in zip(tinvb_l, lakv_l)]
    rhat_l = [(rtb.astype(F32) + _dot(bot[:, :2 * c], ab)).astype(BF16)
              for rtb, bot, ab in zip(st_ops["rt"], bot_l, abar_l)]
    uv_l = [jnp.concatenate([vbar, vb], axis=0) for vbar, vb in zip(vbar_l, st_ops["v"])]
    y0_l = [_dot(bot, uv) for bot, uv in zip(bot_l, uv_l)]
    p_l = [_dot_tn(ab, be).astype(BF16) for ab, be in zip(abar_l, st_ops["be"])]
    q_l = [_dot_tn(uv, jnp.concatenate([be, ke], axis=0)) for uv, be, ke in zip(uv_l, st_ops["be"], st_ops["ke"])]

    st_l = [state_ref[q] for q in range(n_pairs)]
    y_rows = []
    for m in range(nc):
        y_blocks = []
        for q in range(n_pairs):
            idx = m * n_pairs + q
            st = st_l[q]
            st_b = st.astype(BF16)
            y_s = _dot_nt(rhat_l[idx], st_b) + y0_l[idx]
            y_blocks.append(y_s[:c] + y_s[c:])
            decay = g_end[(m + 1) * c - 1:(m + 1) * c, q * LANES:(q + 1) * LANES]
            st_l[q] = st * decay + _dot(st_b, p_l[idx]) + q_l[idx]
        y_rows.append(jnp.concatenate(y_blocks, axis=-1))
    for q in range(n_pairs):
        state_ref[q] = st_l[q]
    y = jnp.concatenate(y_rows, axis=0)

    inv_n = 1.0 / n
    yc = y - head_sum(y) * inv_n
    var = head_sum(yc * yc) * inv_n
    yn = yc * lax.rsqrt(var + RWKV_GN_EPS) * lnw_ref[...] + lnb_ref[...]
    bonus = head_sum(r * k2 * rk_ref[...]) * v
    o_ref[...] = ((yn + bonus) * g).astype(BF16)


def _rwkv(p_rwkv, seq, mu, w0, w2, a0, a2, g2, k_k, k_a, r_k, ln_w, ln_b):
    t = p_rwkv.shape[0]
    ts = RWKV_TILE
    kern = functools.partial(_rwkv_kernel, tiles_per_seq=seq // ts)
    specs, params = _const_operands(mu, w0, w2, a0, a2, g2, k_k, k_a, r_k, ln_w, ln_b)
    return pl.pallas_call(
        kern,
        grid=(t // ts,),
        in_specs=[_row_spec(ts, RWKV_COLS)] + specs,
        out_specs=_row_spec(ts, RWKV_DIM),
        out_shape=jax.ShapeDtypeStruct((t, RWKV_DIM), BF16),
        scratch_shapes=[pltpu.VMEM((1, RWKV_COLS), F32), pltpu.VMEM((RWKV_HEADS // 2, LANES, LANES), F32)],
        compiler_params=_params(),
        name="rwkv7",
    )(p_rwkv, *params)


def _mla_prep_body(p, pos_ref, freq_ref, qn_ref, kvn_ref, wq_ref, wqr_ref, wk_ref, wv_ref, q_out, k_out, v_out):
    hq = _rms(p[:, 0:MLA_Q_LORA], qn_ref[...], NORM_EPS).astype(BF16)
    hkv = _rms(p[:, MLA_Q_LORA:MLA_Q_LORA + MLA_KV_LORA], kvn_ref[...], NORM_EPS).astype(BF16)
    c_rope = MLA_Q_LORA + MLA_KV_LORA
    blk = p[:, c_rope:c_rope + LANES]
    ang = pos_ref[...].astype(F32) * freq_ref[...]
    cos = jnp.cos(ang)
    sin = jnp.sin(ang)
    scale = (MLA_NOPE + MLA_ROPE) ** -0.5 * LOG2E
    qa = _dot(hq, wq_ref[...])
    qr = _dot(hq, wqr_ref[...])
    lane = lax.broadcasted_iota(jnp.int32, blk.shape, 1)
    half = MLA_ROPE // 2
    kr = pltpu.roll(blk, MLA_NOPE, 1)
    rot = jnp.where(lane < MLA_NOPE + half, -pltpu.roll(blk, MLA_NOPE - half, 1), pltpu.roll(blk, MLA_NOPE + half, 1))
    rot = jnp.where((lane >= MLA_NOPE) & (lane < MLA_NOPE + MLA_ROPE), rot, 0.0)
    krope = kr * cos + rot * sin
    kn = _dot(hkv, wk_ref[...])
    for h in range(MLA_HEADS):
        hs = slice(h * LANES, (h + 1) * LANES)
        q_out[:, hs] = ((qa[:, hs] * cos + qr[:, hs] * sin) * scale).astype(BF16)
        k_out[:, hs] = (kn[:, hs] + krope).astype(BF16)
    vt = _dot_nt(wv_ref[...], hkv)
    rowi = lax.broadcasted_iota(jnp.int32, vt.shape, 0)
    v_out[...] = (vt + (rowi % MLA_VT_ROWS == MLA_V).astype(F32)).astype(BF16)


def _softmax_stage_t(s_l, pv_fn, m_scr, acc_scr, idx, s_scr=None, l_scr=None):
    n = len(s_l)
    m_old = [m_scr[c][0:1, :] for c in idx]
    m_new = [jnp.maximum(m, jnp.max(s, axis=0, keepdims=True)) for m, s in zip(m_old, s_l)]
    alpha = [jnp.exp2(mo - mn) for mo, mn in zip(m_old, m_new)]
    if s_scr is not None:
        for c in range(n):
            s_scr[c] = s_l[c]
        s_l = [s_scr[c] for c in range(n)]
    pv_l = []
    for c in range(n):
        p = jnp.exp2(s_l[c] - m_new[c])
        if l_scr is not None:
            l_new = alpha[c] * l_scr[idx[c]][0:1, :] + jnp.sum(p, axis=0, keepdims=True)
            l_scr[idx[c]] = jnp.broadcast_to(l_new, l_scr.shape[1:])
        pv_l.append(pv_fn(c, p.astype(BF16)))
    for c in range(n):
        m_scr[idx[c]] = jnp.broadcast_to(m_new[c], m_scr.shape[1:])
        acc_scr[idx[c]] = alpha[c] * acc_scr[idx[c]] + pv_l[c]


def _mla_attn_kernel(q_ref, k_ref, vt_ref, o_ref, m_scr, acc_scr, s_scr):
    tq = TQ
    i = pl.program_id(1)
    causal = lax.broadcasted_iota(jnp.int32, (tq, tq), 0) <= lax.broadcasted_iota(jnp.int32, (tq, tq), 1)
    m_scr[...] = jnp.full(m_scr.shape, NEG, F32)
    acc_scr[...] = jnp.zeros(acc_scr.shape, F32)
    hsl = [slice(h * LANES, (h + 1) * LANES) for h in range(MLA_HEADS)]
    vsl =[slice(h * MLA_VT_ROWS, (h + 1) * MLA_VT_ROWS) for h in range(MLA_HEADS)]
    heads = list(range(MLA_HEADS))

    def tile(off, tk, mask):
        s_l = [_dot_nt(k_ref[pl.ds(off, tk), hs], q_ref[:, hs]) for hs in hsl]
        if mask:
            s_l = [jnp.where(causal, s, NEG) for s in s_l]
        _softmax_stage_t(s_l, lambda c, p: _dot(vt_ref[vsl[c], pl.ds(off, tk)], p), m_scr, acc_scr, heads,
                         s_scr.at[:, 0:tk, :])

    def pair_body(j, carry):
        tile(pl.multiple_of(j * 2 * tq, 2 * tq), 2 * tq, False)
        return carry

    def single_body(j, carry):
        tile(pl.multiple_of(j * tq, tq), tq, False)
        return carry

    n_pairs = i // 2 if KEY_TILES_PER_STEP == 2 else 0
    if KEY_TILES_PER_STEP == 2:
        lax.fori_loop(0, n_pairs, pair_body, 0)
    lax.fori_loop(2 * n_pairs, i, single_body, 0)
    tile(pl.multiple_of(i * tq, tq), tq, True)
    for pr in range(MLA_HEADS // 2):
        accs = [acc_scr[2 * pr + hh] for hh in range(2)]
        ot = jnp.concatenate([acc[:MLA_V, :] / acc[MLA_V:MLA_V + 1, :] for acc in accs], axis=0)
        o_ref[:, pr * LANES:(pr + 1) * LANES] = jnp.transpose(ot).astype(BF16)


def _mla_attn(q, k, vt, batch, seq):
    t = q.shape[0]
    nq = seq // TQ
    hw = MLA_HEADS * LANES
    vw = MLA_HEADS * MLA_V
    return pl.pallas_call(
        _mla_attn_kernel,
        grid=(batch, nq),
        in_specs=[pl.BlockSpec((TQ, hw), lambda b, i: (b * nq + i, 0)),
                  pl.BlockSpec((seq, hw), lambda b, i: (b, 0)),
                  pl.BlockSpec((MLA_HEADS * MLA_VT_ROWS, seq), lambda b, i: (0, b))],
        out_specs=pl.BlockSpec((TQ, vw), lambda b, i: (b * nq + i, 0)),
        out_shape=jax.ShapeDtypeStruct((t, vw), BF16),
        scratch_shapes=[pltpu.VMEM((MLA_HEADS, 8, TQ), F32), pltpu.VMEM((MLA_HEADS, MLA_VT_ROWS, TQ), F32),
                        pltpu.VMEM((MLA_HEADS, KEY_TILES_PER_STEP * TQ, TQ), F32)],
        compiler_params=_params2(),
        name="mla_attn",
    )(q, k, vt)


def _t5_bucket_table():
    n = np.arange(0, REL_MAX_DISTANCE + 1)
    max_exact = REL_BUCKETS // 2
    nf = np.maximum(n, 1).astype(np.float32)
    ratio = np.log(nf / np.float32(max_exact)) / np.float32(math.log(REL_MAX_DISTANCE / max_exact))
    large = max_exact + (ratio * np.float32(REL_BUCKETS - max_exact)).astype(np.int32)
    large = np.minimum(large, REL_BUCKETS - 1)
    return np.where(n < max_exact, n, large)


_BUCKETS = _t5_bucket_table()
_FAR_BUCKET = int(_BUCKETS[REL_MAX_DISTANCE])
assert _FAR_BUCKET == REL_BUCKETS - 1 and np.all(np.diff(_BUCKETS) >= 0)
_BUCKET_STARTS = [int(np.argmax(_BUCKETS >= b)) for b in range(REL_BUCKETS // 2 + 1, REL_BUCKETS)]


def _diff_attn_kernel(rb_ref, lam_ref, subln_ref, q_ref, k_ref, vt_ref, o_ref, bias_scr, m_scr, acc_scr, s_scr, *,
                      lambda_init):
    tq = TQ
    i = pl.program_id(1)
    r_i = lax.broadcasted_iota(jnp.int32, (tq, tq), 0)
    c_i = lax.broadcasted_iota(jnp.int32, (tq, tq), 1)
    causal = r_i <= c_i

    @pl.when((pl.program_id(0) == 0) & (i == 0))
    def _():
        for t_idx, delta in enumerate((0, tq)):
            d = jnp.maximum(c_i - r_i + delta, 0)
            log_b = REL_BUCKETS // 2
            for start in _BUCKET_STARTS:
                log_b = log_b + (d >= start).astype(jnp.int32)
            bucket = jnp.where(d < REL_BUCKETS // 2, d, log_b)
            for h in range(DIFF_HEADS):
                bias = jnp.zeros((tq, tq), F32)
                for b in range(REL_BUCKETS):
                    bias = jnp.where(bucket == b, rb_ref[b, h], bias)
                bias_scr[t_idx, h] = (bias - rb_ref[_FAR_BUCKET, h]) * LOG2E

    m_scr[...] = jnp.full(m_scr.shape, NEG, F32)
    acc_scr[...] = jnp.zeros(acc_scr.shape, F32)
    lane = lax.broadcasted_iota(jnp.int32, (tq, LANES), 1)
    ones_rows = {tk: (lax.broadcasted_iota(jnp.int32, (DIFF_VT_ROWS - DIFF_V, tk), 0) == 0).astype(BF16)
                 for tk in (tq, 2 * tq)}
    hsl = [slice(h * LANES, (h + 1) * LANES) for h in range(DIFF_HEADS)]
    heads = list(range(DIFF_HEADS))
    q_l = []
    for hs in hsl:
        qp = q_ref[:, hs]
        q_l.append(jnp.concatenate([jnp.where(lane < DIFF_QK, qp, jnp.zeros_like(qp)),
                                    jnp.where(lane >= DIFF_QK, qp, jnp.zeros_like(qp))], axis=0))

    def tile(off, tk, bias_idx, mask):
        s_l = [_dot_nt(k_ref[pl.ds(off, tk), hsl[h]], q_l[h]) for h in heads]
        if bias_idx is not None:
            s_l = [s + jnp.concatenate([bias_scr[bias_idx, h]] * 2, axis=1) for h, s in enumerate(s_l)]
        if mask:
            mask2 = jnp.concatenate([causal, causal], axis=1)
            s_l = [jnp.where(mask2, s, NEG) for s in s_l]

        def pv(c, p):
            vt = jnp.concatenate([vt_ref[hsl[c], pl.ds(off, tk)], ones_rows[tk]], axis=0)
            return jnp.concatenate([_dot(vt, p[:, :tq]), _dot(vt, p[:, tq:])], axis=1)

        _softmax_stage_t(s_l, pv, m_scr, acc_scr, heads, s_scr.at[:, 0:tk, :])

    def far_pair_body(j, carry):
        tile(pl.multiple_of(j * 2 * tq, 2 * tq), 2 * tq, None, False)
        return carry

    def far_body(j, carry):
        tile(pl.multiple_of(j * tq, tq), tq, None, False)
        return carry

    def near_body(j, carry):
        tile(pl.multiple_of(j * tq, tq), tq, 1, False)
        return carry

    n_far = jnp.maximum(i - 1, 0)
    n_pairs = n_far // 2 if KEY_TILES_PER_STEP == 2 else 0
    if KEY_TILES_PER_STEP == 2:
        lax.fori_loop(0, n_pairs, far_pair_body, 0)
    lax.fori_loop(2 * n_pairs, n_far, far_body, 0)
    lax.fori_loop(n_far, i, near_body, 0)
    tile(pl.multiple_of(i * tq, tq), tq, 0, True)

    lam = lam_ref[...]
    lam_full = (jnp.exp(jnp.sum(lam[0:1] * lam[1:2], axis=-1, keepdims=True))
                - jnp.exp(jnp.sum(lam[2:3] * lam[3:4], axis=-1, keepdims=True)) + lambda_init)
    for h in heads:
        acc = acc_scr[h]
        on = acc[:DIFF_V, :] / acc[DIFF_V:DIFF_V + 1, :]
        ot = on[:, :tq] - lam_full * on[:, tq:]
        ot = ot * lax.rsqrt(jnp.mean(ot * ot, axis=0, keepdims=True) + SUBLN_EPS)
        o_ref[:, hsl[h]] = (jnp.transpose(ot) * subln_ref[...] * (1.0 - lambda_init)).astype(BF16)


def _diff_attn(p_qk, p_vt, rel_bias, lam, subln, batch, seq, layer_idx):
    t = p_qk.shape[0]
    nq = seq // TQ
    w = DIFF_HEADS * LANES
    lambda_init = 0.8 - 0.6 * math.exp(-0.3 * layer_idx)
    kern = functools.partial(_diff_attn_kernel, lambda_init=lambda_init)
    specs, params = _const_operands(lam, subln)
    return pl.pallas_call(
        kern,
        grid=(batch, nq),
        in_specs=[pl.BlockSpec(memory_space=pltpu.SMEM)] + specs + [
                  pl.BlockSpec((TQ, w), lambda b, i: (b * nq + i, 0)),
                  pl.BlockSpec((seq, w), lambda b, i: (b, 1)),
                  pl.BlockSpec((w, seq), lambda b, i: (0, b))],
        out_specs=pl.BlockSpec((TQ, w), lambda b, i: (b * nq + i, 0)),
        out_shape=jax.ShapeDtypeStruct((t, w), BF16),
        scratch_shapes=[pltpu.VMEM((2, DIFF_HEADS, TQ, TQ), F32), pltpu.VMEM((DIFF_HEADS, 8, 2 * TQ), F32),
                        pltpu.VMEM((DIFF_HEADS, DIFF_VT_ROWS, 2 * TQ), F32),
                        pltpu.VMEM((DIFF_HEADS, KEY_TILES_PER_STEP * TQ, 2 * TQ), F32)],
        compiler_params=_params2(),
        name="diff_attn",
    )(rel_bias, *params, p_qk, p_qk, p_vt)


def _merge_kernel(x_ref, pg_ref, bg_ref, or_ref, om_ref, od_ref, wr_ref, wm_ref, wd_ref, wo_ref, gf_ref, x_out, h_out):
    halves = [slice(s * (TM_MERGE // 2), (s + 1) * (TM_MERGE // 2)) for s in range(2)]
    branch = [[_dot(o_ref[rs, :], w_ref[...]) for o_ref, w_ref in ((or_ref, wr_ref), (om_ref, wm_ref), (od_ref, wd_ref))]
              for rs in halves]
    for rs, d in zip(halves, branch):
        merged = None
        for idx in range(3):
            cs = slice(idx * D_MODEL, (idx + 1) * D_MODEL)
            term = jax.nn.sigmoid(pg_ref[rs, cs].astype(F32) + bg_ref[:, cs]) * d[idx]
            merged = term if merged is None else merged + term
        x1 = x_ref[rs, :] + _dot(merged.astype(BF16), wo_ref[...])
        x_out[rs, :] = x1
        h_out[rs, :] = _rms(x1, gf_ref[...], NORM_EPS).astype(BF16)


def _merge(x, p_gate, b_gate, o_r, o_m, o_d, w_r, w_m, w_d, w_o, g_ffn):
    t = x.shape[0]
    tm = TM_MERGE
    (bg_spec,), (bg,) = _const_operands(b_gate)
    specs, params = _const_operands(w_r, w_m, w_d, w_o, g_ffn)
    return pl.pallas_call(
        _merge_kernel,
        grid=(t // tm,),
        in_specs=[_row_spec(tm, D_MODEL), _row_spec(tm, GATE_COLS), bg_spec,
                  _row_spec(tm, RWKV_DIM), _row_spec(tm, MLA_HEADS * MLA_V), _row_spec(tm, DIFF_HEADS * DIFF_V)] + specs,
        out_specs=[_row_spec(tm, D_MODEL), _row_spec(tm, D_MODEL)],
        out_shape=[jax.ShapeDtypeStruct((t, D_MODEL), F32), jax.ShapeDtypeStruct((t, D_MODEL), BF16)],
        compiler_params=_params(),
        name="merge",
    )(x, p_gate, bg, o_r, o_m, o_d, *params)


def _ffn_kernel(x_ref, h_ref, wup_ref, cw_ref, cb_ref, wdn_ref, gfin_ref, o_ref, carry_ref, *, tiles_per_seq, final_norm):
    tm = TM_FFN

    @pl.when(pl.program_id(0) % tiles_per_seq == 0)
    def _():
        carry_ref[0:8, :] = jnp.zeros((8, carry_ref.shape[1]), F32)

    h = h_ref[...]

    def conv(u, cols):
        carry_ref[8:, cols] = u
        u1 = carry_ref[7:7 + tm, cols]
        u2 = carry_ref[6:6 + tm, cols]
        carry_ref[0:8, cols] = u[tm - 8:tm, :]
        return cw_ref[0:1, cols] * u2 + cw_ref[1:2, cols] * u1 + cw_ref[2:3, cols] * u + cb_ref[:, cols]

    bounds = list(range(0, D_FF, FF_CHUNK)) + [D_FF]
    chunks = [slice(a, b) for a, b in zip(bounds[:-1], bounds[1:])]
    shift = lambda c: slice(D_FF + c.start, D_FF + c.stop)

    def up(gc):
        return _dot(h, wup_ref[:, gc]), _dot(h, wup_ref[:, shift(gc)])

    nxt = up(chunks[0])
    acts = []
    for ck, gc in enumerate(chunks):
        ug, uv = nxt
        if ck + 1 < len(chunks):
            nxt = up(chunks[ck + 1])
        gate = conv(ug, gc)
        val = conv(uv, shift(gc))
        acts.append((gate * jax.nn.sigmoid(gate) * val).astype(BF16))
    out = x_ref[...] + _dot(jnp.concatenate(acts, axis=-1), wdn_ref[...])
    if final_norm:
        out = _rms(out, gfin_ref[...], NORM_EPS)
    o_ref[...] = out


def _ffn(x1, h2, w_up, conv_w, conv_b, w_down, g_final, seq, final_norm):
    t = x1.shape[0]
    tm = TM_FFN
    kern = functools.partial(_ffn_kernel, tiles_per_seq=seq // tm, final_norm=final_norm)
    specs, params = _const_operands(w_up, conv_w, conv_b, w_down, g_final)
    return pl.pallas_call(
        kern,
        grid=(t // tm,),
        in_specs=[_row_spec(tm, D_MODEL), _row_spec(tm, D_MODEL)] + specs,
        out_specs=_row_spec(tm, D_MODEL),
        out_shape=jax.ShapeDtypeStruct((t, D_MODEL), F32),
        scratch_shapes=[pltpu.VMEM((8 + tm, 2 * D_FF), F32)],
        compiler_params=_params(),
        name="conv_ffn",
    )(x1, h2, *params)


def _mla_weights(w_uq, w_ukv):
    qd = MLA_NOPE + MLA_ROPE
    half = MLA_ROPE // 2
    wq = w_uq.reshape(MLA_Q_LORA, MLA_HEADS, qd)
    zq = jnp.zeros((MLA_Q_LORA, MLA_HEADS, LANES - qd), F32)
    wq_main = jnp.concatenate([wq, zq], axis=-1)
    x1 = wq[:, :, MLA_NOPE:MLA_NOPE + half]
    x2 = wq[:, :, MLA_NOPE + half:]
    wq_rot = jnp.concatenate([jnp.zeros((MLA_Q_LORA, MLA_HEADS, MLA_NOPE), F32), -x2, x1, zq], axis=-1)
    wkv = w_ukv.reshape(MLA_KV_LORA, MLA_HEADS, MLA_NOPE + MLA_V)
    zkv = jnp.zeros((MLA_KV_LORA, MLA_HEADS, LANES - MLA_NOPE), F32)
    wk = jnp.concatenate([wkv[:, :, :MLA_NOPE], zkv], axis=-1)
    wv = jnp.concatenate([wkv[:, :, MLA_NOPE:], jnp.zeros((MLA_KV_LORA, MLA_HEADS, MLA_VT_ROWS - MLA_V), F32)], axis=-1)
    flat = lambda w: w.reshape(w.shape[0], -1).astype(BF16)
    return flat(wq_main), flat(wq_rot), flat(wk), flat(wv).T


def kernel(x, positions, rel_bias, norm_mix, w_in, b_gate, rwkv_mu, rwkv_w0, rwkv_w2, rwkv_a0, rwkv_a2, rwkv_g2, rwkv_k_k, rwkv_k_a, rwkv_r_k, rwkv_ln_w, rwkv_ln_b, mla_q_norm, mla_w_uq, mla_kv_norm, mla_w_ukv, diff_lambda, diff_subln, w_branch_rwkv, w_branch_mla, w_branch_diff, w_o, norm_ffn, ffn_w_up, ffn_conv_w, ffn_conv_b, ffn_w_down, norm_final):
    batch, seq, _ = x.shape
    depth = w_in.shape[0]
    t = batch * seq
    assert seq % TQ == 0 and seq % TM == 0 and seq % TM_FFN == 0 and seq % RWKV_TILE == 0 and RWKV_TILE % CHUNK == 0
    assert (batch * seq) % TM_MERGE == 0
    xf = x.reshape(t, D_MODEL)
    pos_col = positions.reshape(t, 1)
    inv_freq = ROPE_BASE ** (-jnp.arange(0, MLA_ROPE, 2, dtype=F32) / MLA_ROPE)
    freq = jnp.concatenate([jnp.zeros((MLA_NOPE,), F32), inv_freq, inv_freq,
                            jnp.zeros((LANES - MLA_NOPE - MLA_ROPE,), F32)]).reshape(1, LANES)
    vec = lambda v: v.reshape(v.shape[0], 1, -1)
    bf = lambda w: w.astype(BF16)
    n_q = DIFF_QK_COLS // 2
    diff_scale = jnp.concatenate([jnp.full((n_q,), DIFF_QK ** -0.5 * LOG2E, F32), jnp.ones((n_q,), F32)])
    s0, s1, s2 = RWKV_COLS, RWKV_COLS + MLA_COLS, RWKV_COLS + MLA_COLS + DIFF_COLS
    sv = s1 + DIFF_QK_COLS
    st = {
        "norm_mix": vec(norm_mix),
        "w_r": bf(w_in[:, :, :s0]),
        "w_m": bf(jnp.pad(w_in[:, :, s0:s1], ((0, 0), (0, 0), (0, MLA_PAD - MLA_COLS)))),
        "w_qk": bf(w_in[:, :, s1:sv] * diff_scale),
        "w_vt": bf(jnp.swapaxes(w_in[:, :, sv:s2], 1, 2)),
        "w_g": bf(w_in[:, :, s2:]),
        "q_norm": vec(mla_q_norm), "kv_norm": vec(mla_kv_norm),
        "mu": vec(rwkv_mu), "w0": vec(rwkv_w0), "w2": bf(rwkv_w2), "a0": vec(rwkv_a0), "a2": bf(rwkv_a2),
        "g2": bf(rwkv_g2), "k_k": vec(rwkv_k_k), "k_a": vec(rwkv_k_a), "r_k": vec(rwkv_r_k),
        "ln_w": vec(rwkv_ln_w), "ln_b": vec(rwkv_ln_b),
        "lam": diff_lambda, "subln": vec(diff_subln),
        "b_gate": vec(b_gate), "wb_r": bf(w_branch_rwkv), "wb_m": bf(w_branch_mla), "wb_d": bf(w_branch_diff),
        "w_o": bf(w_o), "norm_ffn": vec(norm_ffn),
        "w_up": bf(ffn_w_up), "conv_w": ffn_conv_w, "conv_b": vec(ffn_conv_b), "w_down": bf(ffn_w_down),
    }
    g_final = norm_final.reshape(1, -1)

    for l in range(depth):
        p = {name: _Layer(a, l) for name, a in st.items()}
        wq, wqr, wk, wv = _mla_weights(mla_w_uq[l], mla_w_ukv[l])
        p_rwkv, q_m, k_m, v_m, p_qk, p_vt, p_gate = _in_proj(
            xf, p["norm_mix"], p["w_r"], p["w_m"], p["w_qk"], p["w_g"], p["w_vt"], pos_col, freq, p["q_norm"],
            p["kv_norm"], wq, wqr, wk, wv)

        o_r = _rwkv(p_rwkv, seq, p["mu"], p["w0"], p["w2"], p["a0"], p["a2"], p["g2"], p["k_k"], p["k_a"], p["r_k"],
                    p["ln_w"], p["ln_b"])

        o_m = _mla_attn(q_m, k_m, v_m, batch, seq)

        o_d = _diff_attn(p_qk, p_vt, rel_bias, p["lam"], p["subln"], batch, seq, l)

        x1, h2 = _merge(xf, p_gate, p["b_gate"], o_r, o_m, o_d, p["wb_r"], p["wb_m"], p["wb_d"], p["w_o"], p["norm_ffn"])
        xf = _ffn(x1, h2, p["w_up"], p["conv_w"], p["conv_b"], p["w_down"], g_final, seq, final_norm=(l == depth - 1))
    return xf.reshape(batch, seq, D_MODEL)
```

```python
import functools
import math

import numpy as np
import jax
import jax.numpy as jnp
from jax import lax
from jax.experimental import pallas as pl
from jax.experimental.pallas import tpu as pltpu

F32 = jnp.float32
BF16 = jnp.bfloat16

D_MODEL = 1024
RWKV_HEADS = 8
RWKV_N = 64
RWKV_DIM = 512
RWKV_W_LORA = 64
RWKV_A_LORA = 64
RWKV_G_LORA = 128
RWKV_COLS = 3 * RWKV_DIM + RWKV_W_LORA + RWKV_A_LORA + RWKV_G_LORA
RWKV_GN_EPS = 64e-5
MLA_HEADS = 8
MLA_Q_LORA = 256
MLA_KV_LORA = 128
MLA_NOPE = 64
MLA_ROPE = 32
MLA_V = 64
MLA_VT_ROWS = MLA_V + 16
MLA_COLS = 416
MLA_PAD = 512
ROPE_BASE = 10000.0
DIFF_HEADS = 4
DIFF_QK = 64
DIFF_V = 128
DIFF_COLS = 1536
DIFF_QK_COLS = 1024
DIFF_VT_ROWS = DIFF_V + 16
REL_BUCKETS = 32
REL_MAX_DISTANCE = 128
D_FF = 2816
GATE_COLS = 3072
NORM_EPS = 1e-6
SUBLN_EPS = 1e-5

LANES = 128
VMEM_LIMIT = 58 * 1024 * 1024
TM = 512
TM_FFN = 256
TM_MERGE = 1024
CHUNK = 64
RWKV_TILE = 256
TQ = 512
KEY_TILES_PER_STEP = 2
FF_CHUNK = 512
NEG = -1e30
LOG2E = 1.4426950408889634

_NT = (((1,), (1,)), ((), ()))
_TN = (((0,), (0,)), ((), ()))


def _dot(a, b):
    return jnp.dot(a, b, preferred_element_type=F32)


def _dot_nt(a, b):
    return lax.dot_general(a, b, _NT, preferred_element_type=F32)


def _dot_tn(a, b):
    return lax.dot_general(a, b, _TN, preferred_element_type=F32)


def _rms(x, g, eps):
    return x * lax.rsqrt(jnp.mean(x * x, axis=-1, keepdims=True) + eps) * g


def _params():
    return pltpu.CompilerParams(dimension_semantics=("arbitrary",), vmem_limit_bytes=VMEM_LIMIT)


def _params2():
    return pltpu.CompilerParams(dimension_semantics=("arbitrary", "arbitrary"), vmem_limit_bytes=VMEM_LIMIT)


def _const_spec(shape):
    return pl.BlockSpec(shape, lambda *_: (0,) * len(shape), pipeline_mode=pl.Buffered(1))


class _Layer:
    def __init__(self, stacked, layer):
        self.stacked, self.layer = stacked, layer


def _const_operands(*params):
    specs, arrays = [], []
    for p in params:
        if isinstance(p, _Layer):
            shape = p.stacked.shape[1:]
            specs.append(pl.BlockSpec((None,) + shape, lambda *_, l=p.layer, n=len(shape): (l,) + (0,) * n,
                                      pipeline_mode=pl.Buffered(1)))
            arrays.append(p.stacked)
        else:
            specs.append(_const_spec(p.shape))
            arrays.append(p)
    return specs, arrays


def _row_spec(tm, cols, col_block=0):
    return pl.BlockSpec((tm, cols), lambda i: (i, col_block))


def _in_proj_kernel(x_ref, g_ref, wr_ref, wm_ref, wqk_ref, wg_ref, wvt_ref, pos_ref, freq_ref, qn_ref, kvn_ref, wq_ref,
                    wqr_ref, wk_ref, wv_ref, pr_ref, mq_ref, mk_ref, mvt_ref, pqk_ref, pvt_ref, pg_ref):
    h = _rms(x_ref[...], g_ref[...], NORM_EPS).astype(BF16)
    p_mla = _dot(h, wm_ref[...])
    pqk_ref[...] = _dot(h, wqk_ref[...]).astype(BF16)
    pg_ref[...] = _dot(h, wg_ref[...]).astype(BF16)
    _mla_prep_body(p_mla, pos_ref, freq_ref, qn_ref, kvn_ref, wq_ref, wqr_ref, wk_ref, wv_ref, mq_ref, mk_ref, mvt_ref)
    pr_ref[...] = _dot(h, wr_ref[...])
    pvt_ref[...] = _dot_nt(wvt_ref[...], h).astype(BF16)


def _in_proj(x, g, w_r, w_m, w_qk, w_g, w_vt, pos_col, freq, q_norm, kv_norm, wq, wqr, wk, wv):
    t = x.shape[0]
    vw = DIFF_HEADS * DIFF_V
    hw = MLA_HEADS * LANES
    mvw = MLA_HEADS * MLA_VT_ROWS
    c1, a1 = _const_operands(g, w_r, w_m, w_qk, w_g, w_vt)
    c2, a2 = _const_operands(freq, q_norm, kv_norm, wq, wqr, wk, wv)
    return pl.pallas_call(
        _in_proj_kernel,
        grid=(t // TM,),
        in_specs=[_row_spec(TM, D_MODEL)] + c1 + [_row_spec(TM, 1)] + c2,
        out_specs=[_row_spec(TM, RWKV_COLS), _row_spec(TM, hw), _row_spec(TM, hw),
                   pl.BlockSpec((mvw, TM), lambda i: (0, i)), _row_spec(TM, DIFF_QK_COLS),
                   pl.BlockSpec((vw, TM), lambda i: (0, i)), _row_spec(TM, GATE_COLS)],
        out_shape=[jax.ShapeDtypeStruct((t, RWKV_COLS), F32), jax.ShapeDtypeStruct((t, hw), BF16),
                   jax.ShapeDtypeStruct((t, hw), BF16), jax.ShapeDtypeStruct((mvw, t), BF16),
                   jax.ShapeDtypeStruct((t, DIFF_QK_COLS), BF16), jax.ShapeDtypeStruct((vw, t), BF16),
                   jax.ShapeDtypeStruct((t, GATE_COLS), BF16)],
        compiler_params=_params(),
        name="in_proj",
    )(x, *a1, pos_col, *a2)


def _split3(a):
    hi = a.astype(BF16)
    r1 = a - hi.astype(F32)
    mid = r1.astype(BF16)
    lo = (r1 - mid.astype(F32)).astype(BF16)
    return hi, mid, lo


def _softplus(z):
    return jnp.maximum(z, 0.0) + jnp.log(1.0 + jnp.exp(-jnp.abs(z)))


def _rwkv_kernel(p_ref, mu_ref, w0_ref, w2_ref, a0_ref, a2_ref, g2_ref, kk_ref, ka_ref, rk_ref, lnw_ref, lnb_ref,
                 o_ref, carry_ref, state_ref, *, tiles_per_seq):
    c = CHUNK
    n = RWKV_N
    ts = RWKV_TILE
    nc = ts // c

    @pl.when(pl.program_id(0) % tiles_per_seq == 0)
    def _():
        carry_ref[...] = jnp.zeros_like(carry_ref)
        state_ref[...] = jnp.zeros_like(state_ref)

    p = p_ref[...]
    row = lax.broadcasted_iota(jnp.int32, (ts, 1), 0)
    shifted = jnp.where(row == 0, carry_ref[...], pltpu.roll(p, 1, 0))
    carry_ref[...] = p[ts - 1:ts, :]
    pm = p + (shifted - p) * mu_ref[...]

    d = RWKV_DIM
    r = pm[:, 0:d]
    k = pm[:, d:2 * d]
    v = pm[:, 2 * d:3 * d]
    pw = pm[:, 3 * d:3 * d + RWKV_W_LORA]
    pa = pm[:, 3 * d + RWKV_W_LORA:3 * d + RWKV_W_LORA + RWKV_A_LORA]
    pg = pm[:, 3 * d + RWKV_W_LORA + RWKV_A_LORA:RWKV_COLS]

    w_log = -_softplus(-(w0_ref[...] + _dot(jnp.tanh(pw).astype(BF16), w2_ref[...]))) - 0.5
    logd = -jnp.exp(w_log)
    a = jax.nn.sigmoid(a0_ref[...] + _dot(pa.astype(BF16), a2_ref[...]))
    g = _dot(jax.nn.sigmoid(pg).astype(BF16), g2_ref[...])

    tr = lax.broadcasted_iota(jnp.int32, (ts, ts), 0)
    tc = lax.broadcasted_iota(jnp.int32, (ts, ts), 1)
    tri = ((tr >= tc) & (tr // c == tc // c)).astype(BF16)
    hi, mid, lo = _split3(logd)
    cs = _dot(tri, hi) + _dot(tri, mid) + _dot(tri, lo)
    total = jnp.concatenate([jnp.broadcast_to(cs[(m + 1) * c - 1:(m + 1) * c, :], (c, RWKV_DIM)) for m in range(nc)], axis=0)
    e_in = jnp.exp(cs)
    e_ex = jnp.exp(cs - logd)
    e_inv = jnp.exp(-cs)
    e_end = jnp.exp(total - cs)
    g_end = jnp.exp(total)

    n_pairs = RWKV_HEADS // 2
    lo_t = lax.broadcasted_iota(jnp.int32, (ts, LANES), 1) < n
    lo_c = lax.broadcasted_iota(jnp.int32, (c, LANES), 1) < n

    def head_sum(x):
        blocks = []
        for q in range(n_pairs):
            xb = x[:, q * LANES:(q + 1) * LANES]
            s_lo = jnp.sum(jnp.where(lo_t, xb, 0.0), axis=-1, keepdims=True)
            s_hi = jnp.sum(jnp.where(lo_t, 0.0, xb), axis=-1, keepdims=True)
            blocks.append(jnp.where(lo_t, s_lo, s_hi))
        return jnp.concatenate(blocks, axis=-1)

    k2 = k * (1.0 + (a - 1.0) * ka_ref[...])
    kku = k * kk_ref[...]
    kkn = kku / jnp.maximum(jnp.sqrt(head_sum(kku * kku)), 1e-12)
    b = kkn * a
    full = {"at": -kkn * e_ex, "rt": r * e_in, "bt": b * e_inv, "kt": k2 * e_inv, "v": v,
            "be": b * e_end, "ke": k2 * e_end}

    ti = lax.broadcasted_iota(jnp.int32, (2 * c, 4 * c), 0) % c
    si = lax.broadcasted_iota(jnp.int32, (2 * c, 4 * c), 1) % c
    strict = ti > si
    incl = ti >= si
    eye = (lax.broadcasted_iota(jnp.int32, (2 * c, 2 * c), 0)
           == lax.broadcasted_iota(jnp.int32, (2 * c, 2 * c), 1)).astype(F32)

    items = [(m, q) for m in range(nc) for q in range(n_pairs)]

    def stacked(name, m, q):
        xb = full[name][m * c:(m + 1) * c, q * LANES:(q + 1) * LANES]
        return jnp.concatenate([jnp.where(lo_c, xb, 0.0), jnp.where(lo_c, 0.0, xb)], axis=0).astype(BF16)

    st_ops = {name: [stacked(name, m, q) for m, q in items] for name in full}
    big_l = [_dot_nt(jnp.concatenate([at, rtb], axis=0), jnp.concatenate([bt, kt], axis=0))
             for at, rtb, bt, kt in zip(st_ops["at"], st_ops["rt"], st_ops["bt"], st_ops["kt"])]
    top_l = [jnp.where(strict, big[:2 * c, :], 0.0) for big in big_l]
    bot_l = [jnp.where(incl, big[2 * c:, :], 0.0).astype(BF16) for big in big_l]
    lakv_l = [_dot(top[:, 2 * c:].astype(BF16), vb) for top, vb in zip(top_l, st_ops["v"])]

    x_l = [top[:, :2 * c] for top in top_l]
    tinv_l = [eye + x for x in x_l]
    xb_l = [x.astype(BF16) for x in x_l]
    for _ in range(int(math.log2(c)) - 1):
        x_l = [_dot(xb, xb) for xb in xb_l]
        xb_l = [x.astype(BF16) for x in x_l]
        tinv_l = [tinv + _dot(tinv.astype(BF16), xb) for tinv, xb in zip(tinv_l, xb_l)]
    tinvb_l = [tinv.astype(BF16) for tinv in tinv_l]
    abar_l = [_dot(tb, at).astype(BF16) for tb, at in zip(tinvb_l, st_ops["at"])]
    vbar_l = [_dot(tb, lv.astype(BF16)).astype(BF16) for tb, lv in zip(tinvb_l, lakv_l)]
    rhat_l = [(rtb.astype(F32) + _dot(bot[:, :2 * c], ab)).astype(BF16)
              for rtb, bot, ab in zip(st_ops["rt"], bot_l, abar_l)]
    uv_l = [jnp.concatenate([vbar, vb], axis=0) for vbar, vb in zip(vbar_l, st_ops["v"])]
    y0_l = [_dot(bot, uv) for bot, uv in zip(bot_l, uv_l)]
    p_l = [_dot_tn(ab, be).astype(BF16) for ab, be in zip(abar_l, st_ops["be"])]
    q_l = [_dot_tn(uv, jnp.concatenate([be, ke], axis=0)) for uv, be, ke in zip(uv_l, st_ops["be"], st_ops["ke"])]

    st_l = [state_ref[q] for q in range(n_pairs)]
    y_rows = []
    for m in range(nc):
        y_blocks = []
        for q in range(n_pairs):
            idx = m * n_pairs + q
            st = st_l[q]
            st_b = st.astype(BF16)
            y_s = _dot_nt(rhat_l[idx], st_b) + y0_l[idx]
            y_blocks.append(y_s[:c] + y_s[c:])
            decay = g_end[(m + 1) * c - 1:(m + 1) * c, q * LANES:(q + 1) * LANES]
            st_l[q] = st * decay + _dot(st_b, p_l[idx]) + q_l[idx]
        y_rows.append(jnp.concatenate(y_blocks, axis=-1))
    for q in range(n_pairs):
        state_ref[q] = st_l[q]
    y = jnp.concatenate(y_rows, axis=0)

    inv_n = 1.0 / n
    yc = y - head_sum(y) * inv_n
    var = head_sum(yc * yc) * inv_n
    yn = yc * lax.rsqrt(var + RWKV_GN_EPS) * lnw_ref[...] + lnb_ref[...]
    bonus = head_sum(r * k2 * rk_ref[...]) * v
    o_ref[...] = ((yn + bonus) * g).astype(BF16)


def _rwkv(p_rwkv, seq, mu, w0, w2, a0, a2, g2, k_k, k_a, r_k, ln_w, ln_b):
    t = p_rwkv.shape[0]
    ts = RWKV_TILE
    kern = functools.partial(_rwkv_kernel, tiles_per_seq=seq // ts)
    specs, params = _const_operands(mu, w0, w2, a0, a2, g2, k_k, k_a, r_k, ln_w, ln_b)
    return pl.pallas_call(
        kern,
        grid=(t // ts,),
        in_specs=[_row_spec(ts, RWKV_COLS)] + specs,
        out_specs=_row_spec(ts, RWKV_DIM),
        out_shape=jax.ShapeDtypeStruct((t, RWKV_DIM), BF16),
        scratch_shapes=[pltpu.VMEM((1, RWKV_COLS), F32), pltpu.VMEM((RWKV_HEADS // 2, LANES, LANES), F32)],
        compiler_params=_params(),
        name="rwkv7",
    )(p_rwkv, *params)


def _mla_prep_body(p, pos_ref, freq_ref, qn_ref, kvn_ref, wq_ref, wqr_ref, wk_ref, wv_ref, q_out, k_out, v_out):
    hq = _rms(p[:, 0:MLA_Q_LORA], qn_ref[...], NORM_EPS).astype(BF16)
    hkv = _rms(p[:, MLA_Q_LORA:MLA_Q_LORA + MLA_KV_LORA], kvn_ref[...], NORM_EPS).astype(BF16)
    c_rope = MLA_Q_LORA + MLA_KV_LORA
    blk = p[:, c_rope:c_rope + LANES]
    ang = pos_ref[...].astype(F32) * freq_ref[...]
    cos = jnp.cos(ang)
    sin = jnp.sin(ang)
    scale = (MLA_NOPE + MLA_ROPE) ** -0.5 * LOG2E
    qa = _dot(hq, wq_ref[...])
    qr = _dot(hq, wqr_ref[...])
    lane = lax.broadcasted_iota(jnp.int32, blk.shape, 1)
    half = MLA_ROPE // 2
    kr = pltpu.roll(blk, MLA_NOPE, 1)
    rot = jnp.where(lane < MLA_NOPE + half, -pltpu.roll(blk, MLA_NOPE - half, 1), pltpu.roll(blk, MLA_NOPE + half, 1))
    rot = jnp.where((lane >= MLA_NOPE) & (lane < MLA_NOPE + MLA_ROPE), rot, 0.0)
    krope = kr * cos + rot * sin
    kn = _dot(hkv, wk_ref[...])
    for h in range(MLA_HEADS):
        hs = slice(h * LANES, (h + 1) * LANES)
        q_out[:, hs] = ((qa[:, hs] * cos + qr[:, hs] * sin) * scale).astype(BF16)
        k_out[:, hs] = (kn[:, hs] + krope).astype(BF16)
    vt = _dot_nt(wv_ref[...], hkv)
    rowi = lax.broadcasted_iota(jnp.int32, vt.shape, 0)
    v_out[...] = (vt + (rowi % MLA_VT_ROWS == MLA_V).astype(F32)).astype(BF16)


def _softmax_stage_t(s_l, pv_fn, m_scr, acc_scr, idx, s_scr, lanes=None):
    n = len(s_l)

    def load(ref, c):
        return ref[c] if lanes is None else jnp.concatenate([ref[c, :, ls] for ls in lanes], axis=-1)

    def store(ref, c, val):
        if lanes is None:
            ref[c] = val
            return
        o = 0
        for ls in lanes:
            ref[c, :, ls] = val[:, o:o + ls.stop - ls.start]
            o += ls.stop - ls.start

    m_old = [load(m_scr, c)[0:1, :] for c in idx]
    m_new = [jnp.maximum(m, jnp.max(s, axis=0, keepdims=True)) for m, s in zip(m_old, s_l)]
    alpha = [jnp.exp2(mo - mn) for mo, mn in zip(m_old, m_new)]
    for c in range(n):
        s_scr[c] = s_l[c]
    pv_l = [pv_fn(c, jnp.exp2(s_scr[c] - m_new[c]).astype(BF16)) for c in range(n)]
    for c in range(n):
        store(m_scr, idx[c], jnp.broadcast_to(m_new[c], (m_scr.shape[1], m_new[c].shape[1])))
        store(acc_scr, idx[c], alpha[c] * load(acc_scr, idx[c]) + pv_l[c])


def _mla_attn_kernel(q_ref, k_ref, vt_ref, o_ref, m_scr, acc_scr, s_scr):
    tq = TQ
    i = pl.program_id(1)
    th = tq // 2
    m_scr[...] = jnp.full(m_scr.shape, NEG, F32)
    acc_scr[...] = jnp.zeros(acc_scr.shape, F32)
    hsl = [slice(h * LANES, (h + 1) * LANES) for h in range(MLA_HEADS)]
    vsl = [slice(h * MLA_VT_ROWS, (h + 1) * MLA_VT_ROWS) for h in range(MLA_HEADS)]
    heads = list(range(MLA_HEADS))

    def tile(off, tk, q_lo=0, masked=False):
        nq = tq - q_lo
        s_l = [_dot_nt(k_ref[pl.ds(off, tk), hs], q_ref[q_lo:, hs]) for hs in hsl]
        if masked:
            vis = lax.broadcasted_iota(jnp.int32, (tk, nq), 0) <= lax.broadcasted_iota(jnp.int32, (tk, nq), 1)
            s_l = [jnp.where(vis, s, NEG) for s in s_l]
        _softmax_stage_t(s_l, lambda c, p: _dot(vt_ref[vsl[c], pl.ds(off, tk)], p), m_scr, acc_scr, heads,
                         s_scr.at[:, 0:tk, 0:nq], None if q_lo == 0 else [slice(q_lo, tq)])

    def pair_body(j, carry):
        tile(pl.multiple_of(j * 2 * tq, 2 * tq), 2 * tq)
        return carry

    def single_body(j, carry):
        tile(pl.multiple_of(j * tq, tq), tq)
        return carry

    n_pairs = i // 2 if KEY_TILES_PER_STEP == 2 else 0
    if KEY_TILES_PER_STEP == 2:
        lax.fori_loop(0, n_pairs, pair_body, 0)
    lax.fori_loop(2 * n_pairs, i, single_body, 0)
    diag = pl.multiple_of(i * tq, tq)
    tile(diag, th, 0, True)
    tile(pl.multiple_of(diag + th, th), th, th, True)
    for pr in range(MLA_HEADS // 2):
        accs = [acc_scr[2 * pr + hh] for hh in range(2)]
        ot = jnp.concatenate([acc[:MLA_V, :] / acc[MLA_V:MLA_V + 1, :] for acc in accs], axis=0)
        o_ref[:, pr * LANES:(pr + 1) * LANES] = jnp.transpose(ot).astype(BF16)


def _mla_attn(q, k, vt, batch, seq):
    t = q.shape[0]
    nq = seq // TQ
    hw = MLA_HEADS * LANES
    vw = MLA_HEADS * MLA_V
    return pl.pallas_call(
        _mla_attn_kernel,
        grid=(batch, nq),
        in_specs=[pl.BlockSpec((TQ, hw), lambda b, i: (b * nq + i, 0)),
                  pl.BlockSpec((seq, hw), lambda b, i: (b, 0)),
                  pl.BlockSpec((MLA_HEADS * MLA_VT_ROWS, seq), lambda b, i: (0, b))],
        out_specs=pl.BlockSpec((TQ, vw), lambda b, i: (b * nq + i, 0)),
        out_shape=jax.ShapeDtypeStruct((t, vw), BF16),
        scratch_shapes=[pltpu.VMEM((MLA_HEADS, 8, TQ), F32), pltpu.VMEM((MLA_HEADS, MLA_VT_ROWS, TQ), F32),
                        pltpu.VMEM((MLA_HEADS, KEY_TILES_PER_STEP * TQ, TQ), F32)],
        compiler_params=_params2(),
        name="mla_attn",
    )(q, k, vt)


def _t5_bucket_table():
    n = np.arange(0, REL_MAX_DISTANCE + 1)
    max_exact = REL_BUCKETS // 2
    nf = np.maximum(n, 1).astype(np.float32)
    ratio = np.log(nf / np.float32(max_exact)) / np.float32(math.log(REL_MAX_DISTANCE / max_exact))
    large = max_exact + (ratio * np.float32(REL_BUCKETS - max_exact)).astype(np.int32)
    large = np.minimum(large, REL_BUCKETS - 1)
    return np.where(n < max_exact, n, large)


_BUCKETS = _t5_bucket_table()
_FAR_BUCKET = int(_BUCKETS[REL_MAX_DISTANCE])
assert _FAR_BUCKET == REL_BUCKETS - 1 and np.all(np.diff(_BUCKETS) >= 0)
_BUCKET_STARTS = [int(np.argmax(_BUCKETS >= b)) for b in range(REL_BUCKETS // 2 + 1, REL_BUCKETS)]


def _diff_attn_kernel(rb_ref, lam_ref, subln_ref, q_ref, k_ref, vt_ref, o_ref, bias_scr, m_scr, acc_scr, s_scr, *,
                      lambda_init):
    tq = TQ
    i = pl.program_id(1)
    th = tq // 2
    r_i = lax.broadcasted_iota(jnp.int32, (tq, tq), 0)
    c_i = lax.broadcasted_iota(jnp.int32, (tq, tq), 1)

    @pl.when((pl.program_id(0) == 0) & (i == 0))
    def _():
        for t_idx, delta in enumerate((0, tq)):
            d = jnp.maximum(c_i - r_i + delta, 0)
            log_b = REL_BUCKETS // 2
            for start in _BUCKET_STARTS:
                log_b = log_b + (d >= start).astype(jnp.int32)
            bucket = jnp.where(d < REL_BUCKETS // 2, d, log_b)
            for h in range(DIFF_HEADS):
                bias = jnp.zeros((tq, tq), F32)
                for b in range(REL_BUCKETS):
                    bias = jnp.where(bucket == b, rb_ref[b, h], bias)
                bias_scr[t_idx, h] = (bias - rb_ref[_FAR_BUCKET, h]) * LOG2E

    m_scr[...] = jnp.full(m_scr.shape, NEG, F32)
    acc_scr[...] = jnp.zeros(acc_scr.shape, F32)
    lane = lax.broadcasted_iota(jnp.int32, (tq, LANES), 1)
    ones_rows = {tk: (lax.broadcasted_iota(jnp.int32, (DIFF_VT_ROWS - DIFF_V, tk), 0) == 0).astype(BF16)
                 for tk in (th, tq, 2 * tq)}
    hsl = [slice(h * LANES, (h + 1) * LANES) for h in range(DIFF_HEADS)]
    heads = list(range(DIFF_HEADS))
    q_l = []
    for hs in hsl:
        qp = q_ref[:, hs]
        q_l.append(jnp.concatenate([jnp.where(lane < DIFF_QK, qp, jnp.zeros_like(qp)),
                                    jnp.where(lane >= DIFF_QK, qp, jnp.zeros_like(qp))], axis=0))

    def tile(off, tk, bias_idx, q_lo=0, k_lo=0, masked=False):
        nq = tq - q_lo
        if q_lo == 0:
            qs_l = q_l
        else:
            qs_l = [jnp.concatenate([q[q_lo:tq], q[tq + q_lo:]], axis=0) for q in q_l]
        s_l = [_dot_nt(k_ref[pl.ds(off, tk), hsl[h]], qs_l[h]) for h in heads]
        if bias_idx is not None:
            s_l = [s + jnp.concatenate([bias_scr[bias_idx, h, k_lo:k_lo + tk, q_lo:tq]] * 2, axis=1)
                   for h, s in enumerate(s_l)]
        if masked:
            vis = lax.broadcasted_iota(jnp.int32, (tk, nq), 0) <= lax.broadcasted_iota(jnp.int32, (tk, nq), 1)
            vis2 = jnp.concatenate([vis, vis], axis=1)
            s_l = [jnp.where(vis2, s, NEG) for s in s_l]

        def pv(c, p):
            vt = jnp.concatenate([vt_ref[hsl[c], pl.ds(off, tk)], ones_rows[tk]], axis=0)
            return jnp.concatenate([_dot(vt, p[:, :nq]), _dot(vt, p[:, nq:])], axis=1)

        lanes = None if q_lo == 0 else [slice(q_lo, tq), slice(tq + q_lo, 2 * tq)]
        _softmax_stage_t(s_l, pv, m_scr, acc_scr, heads, s_scr.at[:, 0:tk, 0:2 * nq], lanes)

    def far_pair_body(j, carry):
        tile(pl.multiple_of(j * 2 * tq, 2 * tq), 2 * tq, None)
        return carry

    def far_body(j, carry):
        tile(pl.multiple_of(j * tq, tq), tq, None)
        return carry

    def near_body(j, carry):
        tile(pl.multiple_of(j * tq, tq), tq, 1)
        return carry

    n_far = jnp.maximum(i - 1, 0)
    n_pairs = n_far // 2 if KEY_TILES_PER_STEP == 2 else 0
    if KEY_TILES_PER_STEP == 2:
        lax.fori_loop(0, n_pairs, far_pair_body, 0)
    lax.fori_loop(2 * n_pairs, n_far, far_body, 0)
    lax.fori_loop(n_far, i, near_body, 0)
    diag = pl.multiple_of(i * tq, tq)
    tile(diag, th, 0, 0, 0, True)
    tile(pl.multiple_of(diag + th, th), th, 0, th, th, True)

    lam = lam_ref[...]
    lam_full = (jnp.exp(jnp.sum(lam[0:1] * lam[1:2], axis=-1, keepdims=True))
                - jnp.exp(jnp.sum(lam[2:3] * lam[3:4], axis=-1, keepdims=True)) + lambda_init)
    for h in heads:
        acc = acc_scr[h]
        on = acc[:DIFF_V, :] / acc[DIFF_V:DIFF_V + 1, :]
        ot = on[:, :tq] - lam_full * on[:, tq:]
        ot = ot * lax.rsqrt(jnp.mean(ot * ot, axis=0, keepdims=True) + SUBLN_EPS)
        o_ref[:, hsl[h]] = (jnp.transpose(ot) * subln_ref[...] * (1.0 - lambda_init)).astype(BF16)


def _diff_attn(p_qk, p_vt, rel_bias, lam, subln, batch, seq, layer_idx):
    t = p_qk.shape[0]
    nq = seq // TQ
    w = DIFF_HEADS * LANES
    lambda_init = 0.8 - 0.6 * math.exp(-0.3 * layer_idx)
    kern = functools.partial(_diff_attn_kernel, lambda_init=lambda_init)
    specs, params = _const_operands(lam, subln)
    return pl.pallas_call(
        kern,
        grid=(batch, nq),
        in_specs=[pl.BlockSpec(memory_space=pltpu.SMEM)] + specs + [
                  pl.BlockSpec((TQ, w), lambda b, i: (b * nq + i, 0)),
                  pl.BlockSpec((seq, w), lambda b, i: (b, 1)),
                  pl.BlockSpec((w, seq), lambda b, i: (0, b))],
        out_specs=pl.BlockSpec((TQ, w), lambda b, i: (b * nq + i, 0)),
        out_shape=jax.ShapeDtypeStruct((t, w), BF16),
        scratch_shapes=[pltpu.VMEM((2, DIFF_HEADS, TQ, TQ), F32), pltpu.VMEM((DIFF_HEADS, 8, 2 * TQ), F32),
                        pltpu.VMEM((DIFF_HEADS, DIFF_VT_ROWS, 2 * TQ), F32),
                        pltpu.VMEM((DIFF_HEADS, KEY_TILES_PER_STEP * TQ, 2 * TQ), F32)],
        compiler_params=_params2(),
        name="diff_attn",
    )(rel_bias, *params, p_qk, p_qk, p_vt)


def _merge_kernel(x_ref, pg_ref, bg_ref, or_ref, om_ref, od_ref, wr_ref, wm_ref, wd_ref, wo_ref, gf_ref, x_out, h_out):
    halves = [slice(s * (TM_MERGE // 2), (s + 1) * (TM_MERGE // 2)) for s in range(2)]
    branch = [[_dot(o_ref[rs, :], w_ref[...]) for o_ref, w_ref in ((or_ref, wr_ref), (om_ref, wm_ref), (od_ref, wd_ref))]
              for rs in halves]
    for rs, d in zip(halves, branch):
        merged = None
        for idx in range(3):
            cs = slice(idx * D_MODEL, (idx + 1) * D_MODEL)
            term = jax.nn.sigmoid(pg_ref[rs, cs].astype(F32) + bg_ref[:, cs]) * d[idx]
            merged = term if merged is None else merged + term
        x1 = x_ref[rs, :] + _dot(merged.astype(BF16), wo_ref[...])
        x_out[rs, :] = x1
        h_out[rs, :] = _rms(x1, gf_ref[...], NORM_EPS).astype(BF16)


def _merge(x, p_gate, b_gate, o_r, o_m, o_d, w_r, w_m, w_d, w_o, g_ffn):
    t = x.shape[0]
    tm = TM_MERGE
    (bg_spec,), (bg,) = _const_operands(b_gate)
    specs, params = _const_operands(w_r, w_m, w_d, w_o, g_ffn)
    return pl.pallas_call(
        _merge_kernel,
        grid=(t // tm,),
        in_specs=[_row_spec(tm, D_MODEL), _row_spec(tm, GATE_COLS), bg_spec,
                  _row_spec(tm, RWKV_DIM), _row_spec(tm, MLA_HEADS * MLA_V), _row_spec(tm, DIFF_HEADS * DIFF_V)] + specs,
        out_specs=[_row_spec(tm, D_MODEL), _row_spec(tm, D_MODEL)],
        out_shape=[jax.ShapeDtypeStruct((t, D_MODEL), F32), jax.ShapeDtypeStruct((t, D_MODEL), BF16)],
        compiler_params=_params(),
        name="merge",
    )(x, p_gate, bg, o_r, o_m, o_d, *params)


def _ffn_kernel(x_ref, h_ref, wup_ref, cw_ref, cb_ref, wdn_ref, gfin_ref, o_ref, carry_ref, *, tiles_per_seq, final_norm):
    tm = TM_FFN

    @pl.when(pl.program_id(0) % tiles_per_seq == 0)
    def _():
        carry_ref[0:8, :] = jnp.zeros((8, carry_ref.shape[1]), F32)

    h = h_ref[...]

    def conv(u, cols):
        carry_ref[8:, cols] = u
        u1 = carry_ref[7:7 + tm, cols]
        u2 = carry_ref[6:6 + tm, cols]
        carry_ref[0:8, cols] = u[tm - 8:tm, :]
        return cw_ref[0:1, cols] * u2 + cw_ref[1:2, cols] * u1 + cw_ref[2:3, cols] * u + cb_ref[:, cols]

    bounds = list(range(0, D_FF, FF_CHUNK)) + [D_FF]
    chunks = [slice(a, b) for a, b in zip(bounds[:-1], bounds[1:])]
    shift = lambda c: slice(D_FF + c.start, D_FF + c.stop)

    def up(gc):
        return _dot(h, wup_ref[:, gc]), _dot(h, wup_ref[:, shift(gc)])

    nxt = up(chunks[0])
    acts = []
    for ck, gc in enumerate(chunks):
        ug, uv = nxt
        if ck + 1 < len(chunks):
            nxt = up(chunks[ck + 1])
        gate = conv(ug, gc)
        val = conv(uv, shift(gc))
        acts.append((gate * jax.nn.sigmoid(gate) * val).astype(BF16))
    out = x_ref[...] + _dot(jnp.concatenate(acts, axis=-1), wdn_ref[...])
    if final_norm:
        out = _rms(out, gfin_ref[...], NORM_EPS)
    o_ref[...] = out


def _ffn(x1, h2, w_up, conv_w, conv_b, w_down, g_final, seq, final_norm):
    t = x1.shape[0]
    tm = TM_FFN
    kern = functools.partial(_ffn_kernel, tiles_per_seq=seq // tm, final_norm=final_norm)
    specs, params = _const_operands(w_up, conv_w, conv_b, w_down, g_final)
    return pl.pallas_call(
        kern,
        grid=(t // tm,),
        in_specs=[_row_spec(tm, D_MODEL), _row_spec(tm, D_MODEL)] + specs,
        out_specs=_row_spec(tm, D_MODEL),
        out_shape=jax.ShapeDtypeStruct((t, D_MODEL), F32),
        scratch_shapes=[pltpu.VMEM((8 + tm, 2 * D_FF), F32)],
        compiler_params=_params(),
        name="conv_ffn",
    )(x1, h2, *params)


def _mla_weights(w_uq, w_ukv):
    qd = MLA_NOPE + MLA_ROPE
    half = MLA_ROPE // 2
    wq = w_uq.reshape(MLA_Q_LORA, MLA_HEADS, qd)
    zq = jnp.zeros((MLA_Q_LORA, MLA_HEADS, LANES - qd), F32)
    wq_main = jnp.concatenate([wq, zq], axis=-1)
    x1 = wq[:, :, MLA_NOPE:MLA_NOPE + half]
    x2 = wq[:, :, MLA_NOPE + half:]
    wq_rot = jnp.concatenate([jnp.zeros((MLA_Q_LORA, MLA_HEADS, MLA_NOPE), F32), -x2, x1, zq], axis=-1)
    wkv = w_ukv.reshape(MLA_KV_LORA, MLA_HEADS, MLA_NOPE + MLA_V)
    zkv = jnp.zeros((MLA_KV_LORA, MLA_HEADS, LANES - MLA_NOPE), F32)
    wk = jnp.concatenate([wkv[:, :, :MLA_NOPE], zkv], axis=-1)
    wv = jnp.concatenate([wkv[:, :, MLA_NOPE:], jnp.zeros((MLA_KV_LORA, MLA_HEADS, MLA_VT_ROWS - MLA_V), F32)], axis=-1)
    flat = lambda w: w.reshape(w.shape[0], -1).astype(BF16)
    return flat(wq_main), flat(wq_rot), flat(wk), flat(wv).T


def kernel(x, positions, rel_bias, norm_mix, w_in, b_gate, rwkv_mu, rwkv_w0, rwkv_w2, rwkv_a0, rwkv_a2, rwkv_g2, rwkv_k_k, rwkv_k_a, rwkv_r_k, rwkv_ln_w, rwkv_ln_b, mla_q_norm, mla_w_uq, mla_kv_norm, mla_w_ukv, diff_lambda, diff_subln, w_branch_rwkv, w_branch_mla, w_branch_diff, w_o, norm_ffn, ffn_w_up, ffn_conv_w, ffn_conv_b, ffn_w_down, norm_final):
    batch, seq, _ = x.shape
    depth = w_in.shape[0]
    t = batch * seq
    assert seq % TQ == 0 and seq % TM == 0 and seq % TM_FFN == 0 and seq % RWKV_TILE == 0 and RWKV_TILE % CHUNK == 0
    assert (batch * seq) % TM_MERGE == 0
    xf = x.reshape(t, D_MODEL)
    pos_col = positions.reshape(t, 1)
    inv_freq = ROPE_BASE ** (-jnp.arange(0, MLA_ROPE, 2, dtype=F32) / MLA_ROPE)
    freq = jnp.concatenate([jnp.zeros((MLA_NOPE,), F32), inv_freq, inv_freq,
                            jnp.zeros((LANES - MLA_NOPE - MLA_ROPE,), F32)]).reshape(1, LANES)
    vec = lambda v: v.reshape(v.shape[0], 1, -1)
    bf = lambda w: w.astype(BF16)
    n_q = DIFF_QK_COLS // 2
    diff_scale = jnp.concatenate([jnp.full((n_q,), DIFF_QK ** -0.5 * LOG2E, F32), jnp.ones((n_q,), F32)])
    s0, s1, s2 = RWKV_COLS, RWKV_COLS + MLA_COLS, RWKV_COLS + MLA_COLS + DIFF_COLS
    sv = s1 + DIFF_QK_COLS
    st = {
        "norm_mix": vec(norm_mix),
        "w_r": bf(w_in[:, :, :s0]),
        "w_m": bf(jnp.pad(w_in[:, :, s0:s1], ((0, 0), (0, 0), (0, MLA_PAD - MLA_COLS)))),
        "w_qk": bf(w_in[:, :, s1:sv] * diff_scale),
        "w_vt": bf(jnp.swapaxes(w_in[:, :, sv:s2], 1, 2)),
        "w_g": bf(w_in[:, :, s2:]),
        "q_norm": vec(mla_q_norm), "kv_norm": vec(mla_kv_norm),
        "mu": vec(rwkv_mu), "w0": vec(rwkv_w0), "w2": bf(rwkv_w2), "a0": vec(rwkv_a0), "a2": bf(rwkv_a2),
        "g2": bf(rwkv_g2), "k_k": vec(rwkv_k_k), "k_a": vec(rwkv_k_a), "r_k": vec(rwkv_r_k),
        "ln_w": vec(rwkv_ln_w), "ln_b": vec(rwkv_ln_b),
        "lam": diff_lambda, "subln": vec(diff_subln),
        "b_gate": vec(b_gate), "wb_r": bf(w_branch_rwkv), "wb_m": bf(w_branch_mla), "wb_d": bf(w_branch_diff),
        "w_o": bf(w_o), "norm_ffn": vec(norm_ffn),
        "w_up": bf(ffn_w_up), "conv_w": ffn_conv_w, "conv_b": vec(ffn_conv_b), "w_down": bf(ffn_w_down),
    }
    g_final = norm_final.reshape(1, -1)

    for l in range(depth):
        p = {name: _Layer(a, l) for name, a in st.items()}
        wq, wqr, wk, wv = _mla_weights(mla_w_uq[l], mla_w_ukv[l])
        p_rwkv, q_m, k_m, v_m, p_qk, p_vt, p_gate = _in_proj(
            xf, p["norm_mix"], p["w_r"], p["w_m"], p["w_qk"], p["w_g"], p["w_vt"], pos_col, freq, p["q_norm"],
            p["kv_norm"], wq, wqr, wk, wv)

        o_r = _rwkv(p_rwkv, seq, p["mu"], p["w0"], p["w2"], p["a0"], p["a2"], p["g2"], p["k_k"], p["k_a"], p["r_k"],
                    p["ln_w"], p["ln_b"])

        o_m = _mla_attn(q_m, k_m, v_m, batch, seq)

        o_d = _diff_attn(p_qk, p_vt, rel_bias, p["lam"], p["subln"], batch, seq, l)

        x1, h2 = _merge(xf, p_gate, p["b_gate"], o_r, o_m, o_d, p["wb_r"], p["wb_m"], p["wb_d"], p["w_o"], p["norm_ffn"])
        xf = _ffn(x1, h2, p["w_up"], p["conv_w"], p["conv_b"], p["w_down"], g_final, seq, final_norm=(l == depth - 1))
    return xf.reshape(batch, seq, D_MODEL)
```

```python
import functools
import math

import numpy as np
import jax
import jax.numpy as jnp
from jax import lax
from jax.experimental import pallas as pl
from jax.experimental.pallas import tpu as pltpu

F32 = jnp.float32
BF16 = jnp.bfloat16

D_MODEL = 1024
RWKV_HEADS = 8
RWKV_N = 64
RWKV_DIM = 512
RWKV_W_LORA = 64
RWKV_A_LORA = 64
RWKV_G_LORA = 128
RWKV_COLS = 3 * RWKV_DIM + RWKV_W_LORA + RWKV_A_LORA + RWKV_G_LORA
RWKV_GN_EPS = 64e-5
MLA_HEADS = 8
MLA_Q_LORA = 256
MLA_KV_LORA = 128
MLA_NOPE = 64
MLA_ROPE = 32
MLA_V = 64
MLA_VT_ROWS = MLA_V + 16
MLA_COLS = 416
MLA_PAD = 512
ROPE_BASE = 10000.0
DIFF_HEADS = 4
DIFF_QK = 64
DIFF_V = 128
DIFF_COLS = 1536
DIFF_QK_COLS = 1024
DIFF_VT_ROWS = DIFF_V + 16
REL_BUCKETS = 32
REL_MAX_DISTANCE = 128
D_FF = 2816
GATE_COLS = 3072
NORM_EPS = 1e-6
SUBLN_EPS = 1e-5

LANES = 128
VMEM_LIMIT = 58 * 1024 * 1024
TM = 512
TM_FFN = 256
TM_MERGE = 1024
CHUNK = 64
RWKV_TILE = 256
TQ = 512
KEY_TILES_PER_STEP = 2
FF_CHUNK = 512
NEG = -1e30
LOG2E = 1.4426950408889634

_NT = (((1,), (1,)), ((), ()))
_TN = (((0,), (0,)), ((), ()))


def _dot(a, b):
    return jnp.dot(a, b, preferred_element_type=F32)


def _dot_nt(a, b):
    return lax.dot_general(a, b, _NT, preferred_element_type=F32)


def _dot_tn(a, b):
    return lax.dot_general(a, b, _TN, preferred_element_type=F32)


def _rms(x, g, eps):
    return x * lax.rsqrt(jnp.mean(x * x, axis=-1, keepdims=True) + eps) * g


def _params():
    return pltpu.CompilerParams(dimension_semantics=("arbitrary",), vmem_limit_bytes=VMEM_LIMIT)


def _params2():
    return pltpu.CompilerParams(dimension_semantics=("arbitrary", "arbitrary"), vmem_limit_bytes=VMEM_LIMIT)


def _const_spec(shape):
    return pl.BlockSpec(shape, lambda *_: (0,) * len(shape), pipeline_mode=pl.Buffered(1))


class _Layer:
    def __init__(self, stacked, layer):
        self.stacked, self.layer = stacked, layer


def _const_operands(*params):
    specs, arrays = [], []
    for p in params:
        if isinstance(p, _Layer):
            shape = p.stacked.shape[1:]
            specs.append(pl.BlockSpec((None,) + shape, lambda *_, l=p.layer, n=len(shape): (l,) + (0,) * n,
                                      pipeline_mode=pl.Buffered(1)))
            arrays.append(p.stacked)
        else:
            specs.append(_const_spec(p.shape))
            arrays.append(p)
    return specs, arrays


def _row_spec(tm, cols, col_block=0):
    return pl.BlockSpec((tm, cols), lambda i: (i, col_block))


def _in_proj_kernel(x_ref, g_ref, wr_ref, wm_ref, wqk_ref, wg_ref, wvt_ref, pos_ref, freq_ref, qn_ref, kvn_ref, wq_ref,
                    wqr_ref, wk_ref, wv_ref, pr_ref, mq_ref, mk_ref, mvt_ref, pqk_ref, pvt_ref, pg_ref):
    h = _rms(x_ref[...], g_ref[...], NORM_EPS).astype(BF16)
    p_mla = _dot(h, wm_ref[...])
    pqk_ref[...] = _dot(h, wqk_ref[...]).astype(BF16)
    pg_ref[...] = _dot(h, wg_ref[...]).astype(BF16)
    _mla_prep_body(p_mla, pos_ref, freq_ref, qn_ref, kvn_ref, wq_ref, wqr_ref, wk_ref, wv_ref, mq_ref, mk_ref, mvt_ref)
    pr_ref[...] = _dot(h, wr_ref[...])
    pvt_ref[...] = _dot_nt(wvt_ref[...], h).astype(BF16)


def _in_proj(x, g, w_r, w_m, w_qk, w_g, w_vt, pos_col, freq, q_norm, kv_norm, wq, wqr, wk, wv):
    t = x.shape[0]
    vw = DIFF_HEADS * DIFF_V
    hw = MLA_HEADS * LANES
    mvw = MLA_HEADS * MLA_VT_ROWS
    c1, a1 = _const_operands(g, w_r, w_m, w_qk, w_g, w_vt)
    c2, a2 = _const_operands(freq, q_norm, kv_norm, wq, wqr, wk, wv)
    return pl.pallas_call(
        _in_proj_kernel,
        grid=(t // TM,),
        in_specs=[_row_spec(TM, D_MODEL)] + c1 + [_row_spec(TM, 1)] + c2,
        out_specs=[_row_spec(TM, RWKV_COLS), _row_spec(TM, hw), _row_spec(TM, hw),
                   pl.BlockSpec((mvw, TM), lambda i: (0, i)), _row_spec(TM, DIFF_QK_COLS),
                   pl.BlockSpec((vw, TM), lambda i: (0, i)), _row_spec(TM, GATE_COLS)],
        out_shape=[jax.ShapeDtypeStruct((t, RWKV_COLS), F32), jax.ShapeDtypeStruct((t, hw), BF16),
                   jax.ShapeDtypeStruct((t, hw), BF16), jax.ShapeDtypeStruct((mvw, t), BF16),
                   jax.ShapeDtypeStruct((t, DIFF_QK_COLS), BF16), jax.ShapeDtypeStruct((vw, t), BF16),
                   jax.ShapeDtypeStruct((t, GATE_COLS), BF16)],
        compiler_params=_params(),
        name="in_proj",
    )(x, *a1, pos_col, *a2)


def _split3(a):
    hi = a.astype(BF16)
    r1 = a - hi.astype(F32)
    mid = r1.astype(BF16)
    lo = (r1 - mid.astype(F32)).astype(BF16)
    return hi, mid, lo


def _softplus(z):
    return jnp.maximum(z, 0.0) + jnp.log(1.0 + jnp.exp(-jnp.abs(z)))


def _rwkv_kernel(p_ref, mu_ref, w0_ref, w2_ref, a0_ref, a2_ref, g2_ref, kk_ref, ka_ref, rk_ref, lnw_ref, lnb_ref,
                 o_ref, carry_ref, state_ref, *, tiles_per_seq):
    c = CHUNK
    n = RWKV_N
    ts = RWKV_TILE
    nc = ts // c

    @pl.when(pl.program_id(0) % tiles_per_seq == 0)
    def _():
        carry_ref[...] = jnp.zeros_like(carry_ref)
        state_ref[...] = jnp.zeros_like(state_ref)

    p = p_ref[...]
    row = lax.broadcasted_iota(jnp.int32, (ts, 1), 0)
    shifted = jnp.where(row == 0, carry_ref[...], pltpu.roll(p, 1, 0))
    carry_ref[...] = p[ts - 1:ts, :]
    pm = p + (shifted - p) * mu_ref[...]

    d = RWKV_DIM
    r = pm[:, 0:d]
    k = pm[:, d:2 * d]
    v = pm[:, 2 * d:3 * d]
    pw = pm[:, 3 * d:3 * d + RWKV_W_LORA]
    pa = pm[:, 3 * d + RWKV_W_LORA:3 * d + RWKV_W_LORA + RWKV_A_LORA]
    pg = pm[:, 3 * d + RWKV_W_LORA + RWKV_A_LORA:RWKV_COLS]

    w_log = -_softplus(-(w0_ref[...] + _dot(jnp.tanh(pw).astype(BF16), w2_ref[...]))) - 0.5
    logd = -jnp.exp(w_log)
    a = jax.nn.sigmoid(a0_ref[...] + _dot(pa.astype(BF16), a2_ref[...]))
    g = _dot(jax.nn.sigmoid(pg).astype(BF16), g2_ref[...])

    tr = lax.broadcasted_iota(jnp.int32, (ts, ts), 0)
    tc = lax.broadcasted_iota(jnp.int32, (ts, ts), 1)
    tri = ((tr >= tc) & (tr // c == tc // c)).astype(BF16)
    hi, mid, lo = _split3(logd)
    cs = _dot(tri, hi) + _dot(tri, mid) + _dot(tri, lo)
    total = jnp.concatenate([jnp.broadcast_to(cs[(m + 1) * c - 1:(m + 1) * c, :], (c, RWKV_DIM)) for m in range(nc)], axis=0)
    e_in = jnp.exp(cs)
    e_ex = jnp.exp(cs - logd)
    e_inv = jnp.exp(-cs)
    e_end = jnp.exp(total - cs)
    g_end = jnp.exp(total)

    n_pairs = RWKV_HEADS // 2
    lo_t = lax.broadcasted_iota(jnp.int32, (ts, LANES), 1) < n
    lo_c = lax.broadcasted_iota(jnp.int32, (c, LANES), 1) < n

    def head_sum(x):
        blocks = []
        for q in range(n_pairs):
            xb = x[:, q * LANES:(q + 1) * LANES]
            s_lo = jnp.sum(jnp.where(lo_t, xb, 0.0), axis=-1, keepdims=True)
            s_hi = jnp.sum(jnp.where(lo_t, 0.0, xb), axis=-1, keepdims=True)
            blocks.append(jnp.where(lo_t, s_lo, s_hi))
        return jnp.concatenate(blocks, axis=-1)

    k2 = k * (1.0 + (a - 1.0) * ka_ref[...])
    kku = k * kk_ref[...]
    kkn = kku * jnp.minimum(lax.rsqrt(head_sum(kku * kku)), 1e12)
    b = kkn * a
    full = {"at": -kkn * e_ex, "rt": r * e_in, "bt": b * e_inv, "kt": k2 * e_inv, "v": v,
            "be": b * e_end, "ke": k2 * e_end}

    ti = lax.broadcasted_iota(jnp.int32, (2 * c, 4 * c), 0) % c
    si = lax.broadcasted_iota(jnp.int32, (2 * c, 4 * c), 1) % c
    strict = ti > si
    incl = ti >= si
    eye = (lax.broadcasted_iota(jnp.int32, (2 * c, 2 * c), 0)
           == lax.broadcasted_iota(jnp.int32, (2 * c, 2 * c), 1)).astype(F32)

    items = [(m, q) for m in range(nc) for q in range(n_pairs)]

    def stacked(name, m, q):
        xb = full[name][m * c:(m + 1) * c, q * LANES:(q + 1) * LANES]
        return jnp.concatenate([jnp.where(lo_c, xb, 0.0), jnp.where(lo_c, 0.0, xb)], axis=0).astype(BF16)

    st_ops = {name: [stacked(name, m, q) for m, q in items] for name in full}
    big_l = [_dot_nt(jnp.concatenate([at, rtb], axis=0), jnp.concatenate([bt, kt], axis=0))
             for at, rtb, bt, kt in zip(st_ops["at"], st_ops["rt"], st_ops["bt"], st_ops["kt"])]
    top_l = [jnp.where(strict, big[:2 * c, :], 0.0) for big in big_l]
    bot_l = [jnp.where(incl, big[2 * c:, :], 0.0).astype(BF16) for big in big_l]
    lakv_l = [_dot(top[:, 2 * c:].astype(BF16), vb) for top, vb in zip(top_l, st_ops["v"])]

    x_l = [top[:, :2 * c] for top in top_l]
    tinv_l = [eye + x for x in x_l]
    xb_l = [x.astype(BF16) for x in x_l]
    for _ in range(int(math.log2(c)) - 1):
        x_l = [_dot(xb, xb) for xb in xb_l]
        xb_l = [x.astype(BF16) for x in x_l]
        tinv_l = [tinv + _dot(tinv.astype(BF16), xb) for tinv, xb in zip(tinv_l, xb_l)]
    tinvb_l = [tinv.astype(BF16) for tinv in tinv_l]
    abar_l = [_dot(tb, at).astype(BF16) for tb, at in zip(tinvb_l, st_ops["at"])]
    vbar_l = [_dot(tb, lv.astype(BF16)).astype(BF16) for tb, lv in zip(tinvb_l, lakv_l)]
    rhat_l = [(rtb.astype(F32) + _dot(bot[:, :2 * c], ab)).astype(BF16)
              for rtb, bot, ab in zip(st_ops["rt"], bot_l, abar_l)]
    uv_l = [jnp.concatenate([vbar, vb], axis=0) for vbar, vb in zip(vbar_l, st_ops["v"])]
    y0_l = [_dot(bot, uv) for bot, uv in zip(bot_l, uv_l)]
    p_l = [_dot_tn(ab, be).astype(BF16) for ab, be in zip(abar_l, st_ops["be"])]
    q_l = [_dot_tn(uv, jnp.concatenate([be, ke], axis=0)) for uv, be, ke in zip(uv_l, st_ops["be"], st_ops["ke"])]

    st_l = [state_ref[q] for q in range(n_pairs)]
    y_rows = []
    for m in range(nc):
        y_blocks = []
        for q in range(n_pairs):
            idx = m * n_pairs + q
            st = st_l[q]
            st_b = st.astype(BF16)
            y_s = _dot_nt(rhat_l[idx], st_b) + y0_l[idx]
            y_blocks.append(y_s[:c] + y_s[c:])
            decay = g_end[(m + 1) * c - 1:(m + 1) * c, q * LANES:(q + 1) * LANES]
            st_l[q] = st * decay + _dot(st_b, p_l[idx]) + q_l[idx]
        y_rows.append(jnp.concatenate(y_blocks, axis=-1))
    for q in range(n_pairs):
        state_ref[q] = st_l[q]
    y = jnp.concatenate(y_rows, axis=0)

    inv_n = 1.0 / n
    yc = y - head_sum(y) * inv_n
    var = head_sum(yc * yc) * inv_n
    yn = yc * lax.rsqrt(var + RWKV_GN_EPS) * lnw_ref[...] + lnb_ref[...]
    bonus = head_sum(r * k2 * rk_ref[...]) * v
    o_ref[...] = ((yn + bonus) * g).astype(BF16)


def _rwkv(p_rwkv, seq, mu, w0, w2, a0, a2, g2, k_k, k_a, r_k, ln_w, ln_b):
    t = p_rwkv.shape[0]
    ts = RWKV_TILE
    kern = functools.partial(_rwkv_kernel, tiles_per_seq=seq // ts)
    specs, params = _const_operands(mu, w0, w2, a0, a2, g2, k_k, k_a, r_k, ln_w, ln_b)
    return pl.pallas_call(
        kern,
        grid=(t // ts,),
        in_specs=[_row_spec(ts, RWKV_COLS)] + specs,
        out_specs=_row_spec(ts, RWKV_DIM),
        out_shape=jax.ShapeDtypeStruct((t, RWKV_DIM), BF16),
        scratch_shapes=[pltpu.VMEM((1, RWKV_COLS), F32), pltpu.VMEM((RWKV_HEADS // 2, LANES, LANES), F32)],
        compiler_params=_params(),
        name="rwkv7",
    )(p_rwkv, *params)


def _mla_prep_body(p, pos_ref, freq_ref, qn_ref, kvn_ref, wq_ref, wqr_ref, wk_ref, wv_ref, q_out, k_out, v_out):
    hq = _rms(p[:, 0:MLA_Q_LORA], qn_ref[...], NORM_EPS).astype(BF16)
    hkv = _rms(p[:, MLA_Q_LORA:MLA_Q_LORA + MLA_KV_LORA], kvn_ref[...], NORM_EPS).astype(BF16)
    c_rope = MLA_Q_LORA + MLA_KV_LORA
    blk = p[:, c_rope:c_rope + LANES]
    ang = pos_ref[...].astype(F32) * freq_ref[...]
    cos = jnp.cos(ang)
    sin = jnp.sin(ang)
    scale = (MLA_NOPE + MLA_ROPE) ** -0.5 * LOG2E
    qa = _dot(hq, wq_ref[...])
    qr = _dot(hq, wqr_ref[...])
    lane = lax.broadcasted_iota(jnp.int32, blk.shape, 1)
    half = MLA_ROPE // 2
    kr = pltpu.roll(blk, MLA_NOPE, 1)
    rot = jnp.where(lane < MLA_NOPE + half, -pltpu.roll(blk, MLA_NOPE - half, 1), pltpu.roll(blk, MLA_NOPE + half, 1))
    rot = jnp.where((lane >= MLA_NOPE) & (lane < MLA_NOPE + MLA_ROPE), rot, 0.0)
    krope = kr * cos + rot * sin
    kn = _dot(hkv, wk_ref[...])
    for h in range(MLA_HEADS):
        hs = slice(h * LANES, (h + 1) * LANES)
        q_out[:, hs] = ((qa[:, hs] * cos + qr[:, hs] * sin) * scale).astype(BF16)
        k_out[:, hs] = (kn[:, hs] + krope).astype(BF16)
    vt = _dot_nt(wv_ref[...], hkv)
    rowi = lax.broadcasted_iota(jnp.int32, vt.shape, 0)
    v_out[...] = (vt + (rowi % MLA_VT_ROWS == MLA_V).astype(F32)).astype(BF16)


def _softmax_stage_t(s_l, pv_fn, m_scr, acc_scr, idx, s_scr, lanes=None):
    n = len(s_l)

    def load(ref, c):
        return ref[c] if lanes is None else jnp.concatenate([ref[c, :, ls] for ls in lanes], axis=-1)

    def store(ref, c, val):
        if lanes is None:
            ref[c] = val
            return
        o = 0
        for ls in lanes:
            ref[c, :, ls] = val[:, o:o + ls.stop - ls.start]
            o += ls.stop - ls.start

    m_old = [load(m_scr, c)[0:1, :] for c in idx]
    m_new = [jnp.maximum(m, jnp.max(s, axis=0, keepdims=True)) for m, s in zip(m_old, s_l)]
    alpha = [jnp.exp2(mo - mn) for mo, mn in zip(m_old, m_new)]
    for c in range(n):
        s_scr[c] = s_l[c]
    pv_l = [pv_fn(c, jnp.exp2(s_scr[c] - m_new[c]).astype(BF16)) for c in range(n)]
    for c in range(n):
        store(m_scr, idx[c], jnp.broadcast_to(m_new[c], (m_scr.shape[1], m_new[c].shape[1])))
        store(acc_scr, idx[c], alpha[c] * load(acc_scr, idx[c]) + pv_l[c])


def _mla_attn_kernel(q_ref, k_ref, vt_ref, o_ref, m_scr, acc_scr, s_scr):
    tq = TQ
    i = pl.program_id(1)
    th = tq // 2
    m_scr[...] = jnp.full(m_scr.shape, NEG, F32)
    acc_scr[...] = jnp.zeros(acc_scr.shape, F32)
    hsl = [slice(h * LANES, (h + 1) * LANES) for h in range(MLA_HEADS)]
    vsl = [slice(h * MLA_VT_ROWS, (h + 1) * MLA_VT_ROWS) for h in range(MLA_HEADS)]
    heads = list(range(MLA_HEADS))

    def tile(off, tk, q_lo=0, masked=False):
        nq = tq - q_lo
        s_l = [_dot_nt(k_ref[pl.ds(off, tk), hs], q_ref[q_lo:, hs]) for hs in hsl]
        if masked:
            vis = lax.broadcasted_iota(jnp.int32, (tk, nq), 0) <= lax.broadcasted_iota(jnp.int32, (tk, nq), 1)
            s_l = [jnp.where(vis, s, NEG) for s in s_l]
        _softmax_stage_t(s_l, lambda c, p: _dot(vt_ref[vsl[c], pl.ds(off, tk)], p), m_scr, acc_scr, heads,
                         s_scr.at[:, 0:tk, 0:nq], None if q_lo == 0 else [slice(q_lo, tq)])

    def pair_body(j, carry):
        tile(pl.multiple_of(j * 2 * tq, 2 * tq), 2 * tq)
        return carry

    def single_body(j, carry):
        tile(pl.multiple_of(j * tq, tq), tq)
        return carry

    n_pairs = i // 2 if KEY_TILES_PER_STEP == 2 else 0
    if KEY_TILES_PER_STEP == 2:
        lax.fori_loop(0, n_pairs, pair_body, 0)
    lax.fori_loop(2 * n_pairs, i, single_body, 0)
    diag = pl.multiple_of(i * tq, tq)
    tile(diag, th, 0, True)
    tile(pl.multiple_of(diag + th, th), th, th, True)
    for pr in range(MLA_HEADS // 2):
        accs = [acc_scr[2 * pr + hh] for hh in range(2)]
        ot = jnp.concatenate([acc[:MLA_V, :] * (1.0 / acc[MLA_V:MLA_V + 1, :]) for acc in accs], axis=0)
        o_ref[:, pr * LANES:(pr + 1) * LANES] = jnp.transpose(ot).astype(BF16)


def _mla_attn(q, k, vt, batch, seq):
    t = q.shape[0]
    nq = seq // TQ
    hw = MLA_HEADS * LANES
    vw = MLA_HEADS * MLA_V
    return pl.pallas_call(
        _mla_attn_kernel,
        grid=(batch, nq),
        in_specs=[pl.BlockSpec((TQ, hw), lambda b, i: (b * nq + i, 0)),
                  pl.BlockSpec((seq, hw), lambda b, i: (b, 0)),
                  pl.BlockSpec((MLA_HEADS * MLA_VT_ROWS, seq), lambda b, i: (0, b))],
        out_specs=pl.BlockSpec((TQ, vw), lambda b, i: (b * nq + i, 0)),
        out_shape=jax.ShapeDtypeStruct((t, vw), BF16),
        scratch_shapes=[pltpu.VMEM((MLA_HEADS, 8, TQ), F32), pltpu.VMEM((MLA_HEADS, MLA_VT_ROWS, TQ), F32),
                        pltpu.VMEM((MLA_HEADS, KEY_TILES_PER_STEP * TQ, TQ), F32)],
        compiler_params=_params2(),
        name="mla_attn",
    )(q, k, vt)


def _t5_bucket_table():
    n = np.arange(0, REL_MAX_DISTANCE + 1)
    max_exact = REL_BUCKETS // 2
    nf = np.maximum(n, 1).astype(np.float32)
    ratio = np.log(nf / np.float32(max_exact)) / np.float32(math.log(REL_MAX_DISTANCE / max_exact))
    large = max_exact + (ratio * np.float32(REL_BUCKETS - max_exact)).astype(np.int32)
    large = np.minimum(large, REL_BUCKETS - 1)
    return np.where(n < max_exact, n, large)


_BUCKETS = _t5_bucket_table()
_FAR_BUCKET = int(_BUCKETS[REL_MAX_DISTANCE])
assert _FAR_BUCKET == REL_BUCKETS - 1 and np.all(np.diff(_BUCKETS) >= 0)
_BUCKET_STARTS = [int(np.argmax(_BUCKETS >= b)) for b in range(REL_BUCKETS // 2 + 1, REL_BUCKETS)]


def _diff_attn_kernel(rb_ref, lam_ref, subln_ref, q_ref, k_ref, vt_ref, o_ref, bias_scr, m_scr, acc_scr, s_scr, *,
                      lambda_init):
    tq = TQ
    i = pl.program_id(1)
    th = tq // 2
    r_i = lax.broadcasted_iota(jnp.int32, (tq, tq), 0)
    c_i = lax.broadcasted_iota(jnp.int32, (tq, tq), 1)

    @pl.when((pl.program_id(0) == 0) & (i == 0))
    def _():
        for t_idx, delta in enumerate((0, tq)):
            d = jnp.maximum(c_i - r_i + delta, 0)
            log_b = REL_BUCKETS // 2
            for start in _BUCKET_STARTS:
                log_b = log_b + (d >= start).astype(jnp.int32)
            bucket = jnp.where(d < REL_BUCKETS // 2, d, log_b)
            for h in range(DIFF_HEADS):
                bias = jnp.zeros((tq, tq), F32)
                for b in range(REL_BUCKETS):
                    bias = jnp.where(bucket == b, rb_ref[b, h], bias)
                bias_scr[t_idx, h] = (bias - rb_ref[_FAR_BUCKET, h]) * LOG2E

    m_scr[...] = jnp.full(m_scr.shape, NEG, F32)
    acc_scr[...] = jnp.zeros(acc_scr.shape, F32)
    lane = lax.broadcasted_iota(jnp.int32, (tq, LANES), 1)
    ones_rows = {tk: (lax.broadcasted_iota(jnp.int32, (DIFF_VT_ROWS - DIFF_V, tk), 0) == 0).astype(BF16)
                 for tk in (th, tq, 2 * tq)}
    hsl = [slice(h * LANES, (h + 1) * LANES) for h in range(DIFF_HEADS)]
    heads = list(range(DIFF_HEADS))
    q_l = []
    for hs in hsl:
        qp = q_ref[:, hs]
        q_l.append(jnp.concatenate([jnp.where(lane < DIFF_QK, qp, jnp.zeros_like(qp)),
                                    jnp.where(lane >= DIFF_QK, qp, jnp.zeros_like(qp))], axis=0))

    def tile(off, tk, bias_idx, q_lo=0, k_lo=0, masked=False):
        nq = tq - q_lo
        if q_lo == 0:
            qs_l = q_l
        else:
            qs_l = [jnp.concatenate([q[q_lo:tq], q[tq + q_lo:]], axis=0) for q in q_l]
        s_l = [_dot_nt(k_ref[pl.ds(off, tk), hsl[h]], qs_l[h]) for h in heads]
        if bias_idx is not None:
            s_l = [s + jnp.concatenate([bias_scr[bias_idx, h, k_lo:k_lo + tk, q_lo:tq]] * 2, axis=1)
                   for h, s in enumerate(s_l)]
        if masked:
            vis = lax.broadcasted_iota(jnp.int32, (tk, nq), 0) <= lax.broadcasted_iota(jnp.int32, (tk, nq), 1)
            vis2 = jnp.concatenate([vis, vis], axis=1)
            s_l = [jnp.where(vis2, s, NEG) for s in s_l]

        def pv(c, p):
            vt = jnp.concatenate([vt_ref[hsl[c], pl.ds(off, tk)], ones_rows[tk]], axis=0)
            return jnp.concatenate([_dot(vt, p[:, :nq]), _dot(vt, p[:, nq:])], axis=1)

        lanes = None if q_lo == 0 else [slice(q_lo, tq), slice(tq + q_lo, 2 * tq)]
        _softmax_stage_t(s_l, pv, m_scr, acc_scr, heads, s_scr.at[:, 0:tk, 0:2 * nq], lanes)

    def far_pair_body(j, carry):
        tile(pl.multiple_of(j * 2 * tq, 2 * tq), 2 * tq, None)
        return carry

    def far_body(j, carry):
        tile(pl.multiple_of(j * tq, tq), tq, None)
        return carry

    def near_body(j, carry):
        tile(pl.multiple_of(j * tq, tq), tq, 1)
        return carry

    n_far = jnp.maximum(i - 1, 0)
    n_pairs = n_far // 2 if KEY_TILES_PER_STEP == 2 else 0
    if KEY_TILES_PER_STEP == 2:
        lax.fori_loop(0, n_pairs, far_pair_body, 0)
    lax.fori_loop(2 * n_pairs, n_far, far_body, 0)
    lax.fori_loop(n_far, i, near_body, 0)
    diag = pl.multiple_of(i * tq, tq)
    tile(diag, th, 0, 0, 0, True)
    tile(pl.multiple_of(diag + th, th), th, 0, th, th, True)

    lam = lam_ref[...]
    lam_full = (jnp.exp(jnp.sum(lam[0:1] * lam[1:2], axis=-1, keepdims=True))
                - jnp.exp(jnp.sum(lam[2:3] * lam[3:4], axis=-1, keepdims=True)) + lambda_init)
    for h in heads:
        acc = acc_scr[h]
        on = acc[:DIFF_V, :] * (1.0 / acc[DIFF_V:DIFF_V + 1, :])
        ot = on[:, :tq] - lam_full * on[:, tq:]
        ot = ot * lax.rsqrt(jnp.mean(ot * ot, axis=0, keepdims=True) + SUBLN_EPS)
        o_ref[:, hsl[h]] = (jnp.transpose(ot) * subln_ref[...] * (1.0 - lambda_init)).astype(BF16)


def _diff_attn(p_qk, p_vt, rel_bias, lam, subln, batch, seq, layer_idx):
    t = p_qk.shape[0]
    nq = seq // TQ
    w = DIFF_HEADS * LANES
    lambda_init = 0.8 - 0.6 * math.exp(-0.3 * layer_idx)
    kern = functools.partial(_diff_attn_kernel, lambda_init=lambda_init)
    specs, params = _const_operands(lam, subln)
    return pl.pallas_call(
        kern,
        grid=(batch, nq),
        in_specs=[pl.BlockSpec(memory_space=pltpu.SMEM)] + specs + [
                  pl.BlockSpec((TQ, w), lambda b, i: (b * nq + i, 0)),
                  pl.BlockSpec((seq, w), lambda b, i: (b, 1)),
                  pl.BlockSpec((w, seq), lambda b, i: (0, b))],
        out_specs=pl.BlockSpec((TQ, w), lambda b, i: (b * nq + i, 0)),
        out_shape=jax.ShapeDtypeStruct((t, w), BF16),
        scratch_shapes=[pltpu.VMEM((2, DIFF_HEADS, TQ, TQ), F32), pltpu.VMEM((DIFF_HEADS, 8, 2 * TQ), F32),
                        pltpu.VMEM((DIFF_HEADS, DIFF_VT_ROWS, 2 * TQ), F32),
                        pltpu.VMEM((DIFF_HEADS, KEY_TILES_PER_STEP * TQ, 2 * TQ), F32)],
        compiler_params=_params2(),
        name="diff_attn",
    )(rel_bias, *params, p_qk, p_qk, p_vt)


def _merge_kernel(x_ref, pg_ref, bg_ref, or_ref, om_ref, od_ref, wr_ref, wm_ref, wd_ref, wo_ref, gf_ref, x_out, h_out):
    halves = [slice(s * (TM_MERGE // 2), (s + 1) * (TM_MERGE // 2)) for s in range(2)]
    branch = [[_dot(o_ref[rs, :], w_ref[...]) for o_ref, w_ref in ((or_ref, wr_ref), (om_ref, wm_ref), (od_ref, wd_ref))]
              for rs in halves]
    for rs, d in zip(halves, branch):
        merged = None
        for idx in range(3):
            cs = slice(idx * D_MODEL, (idx + 1) * D_MODEL)
            term = jax.nn.sigmoid(pg_ref[rs, cs].astype(F32) + bg_ref[:, cs]) * d[idx]
            merged = term if merged is None else merged + term
        x1 = x_ref[rs, :] + _dot(merged.astype(BF16), wo_ref[...])
        x_out[rs, :] = x1
        h_out[rs, :] = _rms(x1, gf_ref[...], NORM_EPS).astype(BF16)


def _merge(x, p_gate, b_gate, o_r, o_m, o_d, w_r, w_m, w_d, w_o, g_ffn):
    t = x.shape[0]
    tm = TM_MERGE
    (bg_spec,), (bg,) = _const_operands(b_gate)
    specs, params = _const_operands(w_r, w_m, w_d, w_o, g_ffn)
    return pl.pallas_call(
        _merge_kernel,
        grid=(t // tm,),
        in_specs=[_row_spec(tm, D_MODEL), _row_spec(tm, GATE_COLS), bg_spec,
                  _row_spec(tm, RWKV_DIM), _row_spec(tm, MLA_HEADS * MLA_V), _row_spec(tm, DIFF_HEADS * DIFF_V)] + specs,
        out_specs=[_row_spec(tm, D_MODEL), _row_spec(tm, D_MODEL)],
        out_shape=[jax.ShapeDtypeStruct((t, D_MODEL), F32), jax.ShapeDtypeStruct((t, D_MODEL), BF16)],
        compiler_params=_params(),
        name="merge",
    )(x, p_gate, bg, o_r, o_m, o_d, *params)


def _ffn_kernel(x_ref, h_ref, wup_ref, cw_ref, cb_ref, wdn_ref, gfin_ref, o_ref, carry_ref, *, tiles_per_seq, final_norm):
    tm = TM_FFN

    @pl.when(pl.program_id(0) % tiles_per_seq == 0)
    def _():
        carry_ref[0:8, :] = jnp.zeros((8, carry_ref.shape[1]), F32)

    h = h_ref[...]

    def conv(u, cols):
        carry_ref[8:, cols] = u
        u1 = carry_ref[7:7 + tm, cols]
        u2 = carry_ref[6:6 + tm, cols]
        carry_ref[0:8, cols] = u[tm - 8:tm, :]
        return cw_ref[0:1, cols] * u2 + cw_ref[1:2, cols] * u1 + cw_ref[2:3, cols] * u + cb_ref[:, cols]

    bounds = list(range(0, D_FF, FF_CHUNK)) + [D_FF]
    chunks = [slice(a, b) for a, b in zip(bounds[:-1], bounds[1:])]
    shift = lambda c: slice(D_FF + c.start, D_FF + c.stop)

    def up(gc):
        return _dot(h, wup_ref[:, gc]), _dot(h, wup_ref[:, shift(gc)])

    nxt = up(chunks[0])
    acts = []
    for ck, gc in enumerate(chunks):
        ug, uv = nxt
        if ck + 1 < len(chunks):
            nxt = up(chunks[ck + 1])
        gate = conv(ug, gc)
        val = conv(uv, shift(gc))
        acts.append((gate * jax.nn.sigmoid(gate) * val).astype(BF16))
    out = x_ref[...] + _dot(jnp.concatenate(acts, axis=-1), wdn_ref[...])
    if final_norm:
        out = _rms(out, gfin_ref[...], NORM_EPS)
    o_ref[...] = out


def _ffn(x1, h2, w_up, conv_w, conv_b, w_down, g_final, seq, final_norm):
    t = x1.shape[0]
    tm = TM_FFN
    kern = functools.partial(_ffn_kernel, tiles_per_seq=seq // tm, final_norm=final_norm)
    specs, params = _const_operands(w_up, conv_w, conv_b, w_down, g_final)
    return pl.pallas_call(
        kern,
        grid=(t // tm,),
        in_specs=[_row_spec(tm, D_MODEL), _row_spec(tm, D_MODEL)] + specs,
        out_specs=_row_spec(tm, D_MODEL),
        out_shape=jax.ShapeDtypeStruct((t, D_MODEL), F32),
        scratch_shapes=[pltpu.VMEM((8 + tm, 2 * D_FF), F32)],
        compiler_params=_params(),
        name="conv_ffn",
    )(x1, h2, *params)


def _mla_weights(w_uq, w_ukv):
    qd = MLA_NOPE + MLA_ROPE
    half = MLA_ROPE // 2
    wq = w_uq.reshape(MLA_Q_LORA, MLA_HEADS, qd)
    zq = jnp.zeros((MLA_Q_LORA, MLA_HEADS, LANES - qd), F32)
    wq_main = jnp.concatenate([wq, zq], axis=-1)
    x1 = wq[:, :, MLA_NOPE:MLA_NOPE + half]
    x2 = wq[:, :, MLA_NOPE + half:]
    wq_rot = jnp.concatenate([jnp.zeros((MLA_Q_LORA, MLA_HEADS, MLA_NOPE), F32), -x2, x1, zq], axis=-1)
    wkv = w_ukv.reshape(MLA_KV_LORA, MLA_HEADS, MLA_NOPE + MLA_V)
    zkv = jnp.zeros((MLA_KV_LORA, MLA_HEADS, LANES - MLA_NOPE), F32)
    wk = jnp.concatenate([wkv[:, :, :MLA_NOPE], zkv], axis=-1)
    wv = jnp.concatenate([wkv[:, :, MLA_NOPE:], jnp.zeros((MLA_KV_LORA, MLA_HEADS, MLA_VT_ROWS - MLA_V), F32)], axis=-1)
    flat = lambda w: w.reshape(w.shape[0], -1).astype(BF16)
    return flat(wq_main), flat(wq_rot), flat(wk), flat(wv).T


def kernel(x, positions, rel_bias, norm_mix, w_in, b_gate, rwkv_mu, rwkv_w0, rwkv_w2, rwkv_a0, rwkv_a2, rwkv_g2, rwkv_k_k, rwkv_k_a, rwkv_r_k, rwkv_ln_w, rwkv_ln_b, mla_q_norm, mla_w_uq, mla_kv_norm, mla_w_ukv, diff_lambda, diff_subln, w_branch_rwkv, w_branch_mla, w_branch_diff, w_o, norm_ffn, ffn_w_up, ffn_conv_w, ffn_conv_b, ffn_w_down, norm_final):
    batch, seq, _ = x.shape
    depth = w_in.shape[0]
    t = batch * seq
    assert seq % TQ == 0 and seq % TM == 0 and seq % TM_FFN == 0 and seq % RWKV_TILE == 0 and RWKV_TILE % CHUNK == 0
    assert (batch * seq) % TM_MERGE == 0
    xf = x.reshape(t, D_MODEL)
    pos_col = positions.reshape(t, 1)
    inv_freq = ROPE_BASE ** (-jnp.arange(0, MLA_ROPE, 2, dtype=F32) / MLA_ROPE)
    freq = jnp.concatenate([jnp.zeros((MLA_NOPE,), F32), inv_freq, inv_freq,
                            jnp.zeros((LANES - MLA_NOPE - MLA_ROPE,), F32)]).reshape(1, LANES)
    vec = lambda v: v.reshape(v.shape[0], 1, -1)
    bf = lambda w: w.astype(BF16)
    n_q = DIFF_QK_COLS // 2
    diff_scale = jnp.concatenate([jnp.full((n_q,), DIFF_QK ** -0.5 * LOG2E, F32), jnp.ones((n_q,), F32)])
    s0, s1, s2 = RWKV_COLS, RWKV_COLS + MLA_COLS, RWKV_COLS + MLA_COLS + DIFF_COLS
    sv = s1 + DIFF_QK_COLS
    st = {
        "norm_mix": vec(norm_mix),
        "w_r": bf(w_in[:, :, :s0]),
        "w_m": bf(jnp.pad(w_in[:, :, s0:s1], ((0, 0), (0, 0), (0, MLA_PAD - MLA_COLS)))),
        "w_qk": bf(w_in[:, :, s1:sv] * diff_scale),
        "w_vt": bf(jnp.swapaxes(w_in[:, :, sv:s2], 1, 2)),
        "w_g": bf(w_in[:, :, s2:]),
        "q_norm": vec(mla_q_norm), "kv_norm": vec(mla_kv_norm),
        "mu": vec(rwkv_mu), "w0": vec(rwkv_w0), "w2": bf(rwkv_w2), "a0": vec(rwkv_a0), "a2": bf(rwkv_a2),
        "g2": bf(rwkv_g2), "k_k": vec(rwkv_k_k), "k_a": vec(rwkv_k_a), "r_k": vec(rwkv_r_k),
        "ln_w": vec(rwkv_ln_w), "ln_b": vec(rwkv_ln_b),
        "lam": diff_lambda, "subln": vec(diff_subln),
        "b_gate": vec(b_gate), "wb_r": bf(w_branch_rwkv), "wb_m": bf(w_branch_mla), "wb_d": bf(w_branch_diff),
        "w_o": bf(w_o), "norm_ffn": vec(norm_ffn),
        "w_up": bf(ffn_w_up), "conv_w": ffn_conv_w, "conv_b": vec(ffn_conv_b), "w_down": bf(ffn_w_down),
    }
    g_final = norm_final.reshape(1, -1)

    for l in range(depth):
        p = {name: _Layer(a, l) for name, a in st.items()}
        wq, wqr, wk, wv = _mla_weights(mla_w_uq[l], mla_w_ukv[l])
        p_rwkv, q_m, k_m, v_m, p_qk, p_vt, p_gate = _in_proj(
            xf, p["norm_mix"], p["w_r"], p["w_m"], p["w_qk"], p["w_g"], p["w_vt"], pos_col, freq, p["q_norm"],
            p["kv_norm"], wq, wqr, wk, wv)

        o_r = _rwkv(p_rwkv, seq, p["mu"], p["w0"], p["w2"], p["a0"], p["a2"], p["g2"], p["k_k"], p["k_a"], p["r_k"],
                    p["ln_w"], p["ln_b"])

        o_m = _mla_attn(q_m, k_m, v_m, batch, seq)

        o_d = _diff_attn(p_qk, p_vt, rel_bias, p["lam"], p["subln"], batch, seq, l)

        x1, h2 = _merge(xf, p_gate, p["b_gate"], o_r, o_m, o_d, p["wb_r"], p["wb_m"], p["wb_d"], p["w_o"], p["norm_ffn"])
        xf = _ffn(x1, h2, p["w_up"], p["conv_w"], p["conv_b"], p["w_down"], g_final, seq, final_norm=(l == depth - 1))
    return xf.reshape(batch, seq, D_MODEL)
```

```python
import functools
import math

import numpy as np
import jax
import jax.numpy as jnp
from jax import lax
from jax.experimental import pallas as pl
from jax.experimental.pallas import tpu as pltpu

F32 = jnp.float32
BF16 = jnp.bfloat16

D_MODEL = 1024
RWKV_HEADS = 8
RWKV_N = 64
RWKV_DIM = 512
RWKV_W_LORA = 64
RWKV_A_LORA = 64
RWKV_G_LORA = 128
RWKV_COLS = 3 * RWKV_DIM + RWKV_W_LORA + RWKV_A_LORA + RWKV_G_LORA
RWKV_GN_EPS = 64e-5
MLA_HEADS = 8
MLA_Q_LORA = 256
MLA_KV_LORA = 128
MLA_NOPE = 64
MLA_ROPE = 32
MLA_V = 64
MLA_VT_ROWS = MLA_V + 16
MLA_COLS = 416
MLA_PAD = 512
ROPE_BASE = 10000.0
DIFF_HEADS = 4
DIFF_QK = 64
DIFF_V = 128
DIFF_COLS = 1536
DIFF_QK_COLS = 1024
DIFF_VT_ROWS = DIFF_V + 16
REL_BUCKETS = 32
REL_MAX_DISTANCE = 128
D_FF = 2816
GATE_COLS = 3072
NORM_EPS = 1e-6
SUBLN_EPS = 1e-5

LANES = 128
VMEM_LIMIT = 58 * 1024 * 1024
TM = 512
TM_FFN = 512
TM_MERGE = 1024
CHUNK = 64
RWKV_TILE = 256
TQ = 512
KEY_TILES_PER_STEP = 2
FF_CHUNK = 512
NEG = -1e30
LOG2E = 1.4426950408889634

_NT = (((1,), (1,)), ((), ()))
_TN = (((0,), (0,)), ((), ()))


def _dot(a, b):
    return jnp.dot(a, b, preferred_element_type=F32)


def _dot_nt(a, b):
    return lax.dot_general(a, b, _NT, preferred_element_type=F32)


def _dot_tn(a, b):
    return lax.dot_general(a, b, _TN, preferred_element_type=F32)


def _rms(x, g, eps):
    return x * lax.rsqrt(jnp.mean(x * x, axis=-1, keepdims=True) + eps) * g


def _params():
    return pltpu.CompilerParams(dimension_semantics=("arbitrary",), vmem_limit_bytes=VMEM_LIMIT)


def _params2():
    return pltpu.CompilerParams(dimension_semantics=("arbitrary", "arbitrary"), vmem_limit_bytes=VMEM_LIMIT)


def _const_spec(shape):
    return pl.BlockSpec(shape, lambda *_: (0,) * len(shape), pipeline_mode=pl.Buffered(1))


class _Layer:
    def __init__(self, stacked, layer):
        self.stacked, self.layer = stacked, layer


def _const_operands(*params):
    specs, arrays = [], []
    for p in params:
        if isinstance(p, _Layer):
            shape = p.stacked.shape[1:]
            specs.append(pl.BlockSpec((None,) + shape, lambda *_, l=p.layer, n=len(shape): (l,) + (0,) * n,
                                      pipeline_mode=pl.Buffered(1)))
            arrays.append(p.stacked)
        else:
            specs.append(_const_spec(p.shape))
            arrays.append(p)
    return specs, arrays


def _row_spec(tm, cols, col_block=0):
    return pl.BlockSpec((tm, cols), lambda i: (i, col_block))


def _in_proj_kernel(x_ref, g_ref, wr_ref, wm_ref, wqk_ref, wg_ref, wvt_ref, pos_ref, freq_ref, qn_ref, kvn_ref, wq_ref,
                    wqr_ref, wk_ref, wv_ref, pr_ref, mq_ref, mk_ref, mvt_ref, pqk_ref, pvt_ref, pg_ref):
    h = _rms(x_ref[...], g_ref[...], NORM_EPS).astype(BF16)
    p_mla = _dot(h, wm_ref[...])
    pqk_ref[...] = _dot(h, wqk_ref[...]).astype(BF16)
    pg_ref[...] = _dot(h, wg_ref[...]).astype(BF16)
    _mla_prep_body(p_mla, pos_ref, freq_ref, qn_ref, kvn_ref, wq_ref, wqr_ref, wk_ref, wv_ref, mq_ref, mk_ref, mvt_ref)
    pr_ref[...] = _dot(h, wr_ref[...])
    pvt_ref[...] = _dot_nt(wvt_ref[...], h).astype(BF16)


def _in_proj(x, g, w_r, w_m, w_qk, w_g, w_vt, pos_col, freq, q_norm, kv_norm, wq, wqr, wk, wv):
    t = x.shape[0]
    vw = DIFF_HEADS * DIFF_V
    hw = MLA_HEADS * LANES
    mvw = MLA_HEADS * MLA_VT_ROWS
    c1, a1 = _const_operands(g, w_r, w_m, w_qk, w_g, w_vt)
    c2, a2 = _const_operands(freq, q_norm, kv_norm, wq, wqr, wk, wv)
    return pl.pallas_call(
        _in_proj_kernel,
        grid=(t // TM,),
        in_specs=[_row_spec(TM, D_MODEL)] + c1 + [_row_spec(TM, 1)] + c2,
        out_specs=[_row_spec(TM, RWKV_COLS), _row_spec(TM, hw), _row_spec(TM, hw),
                   pl.BlockSpec((mvw, TM), lambda i: (0, i)), _row_spec(TM, DIFF_QK_COLS),
                   pl.BlockSpec((vw, TM), lambda i: (0, i)), _row_spec(TM, GATE_COLS)],
        out_shape=[jax.ShapeDtypeStruct((t, RWKV_COLS), F32), jax.ShapeDtypeStruct((t, hw), BF16),
                   jax.ShapeDtypeStruct((t, hw), BF16), jax.ShapeDtypeStruct((mvw, t), BF16),
                   jax.ShapeDtypeStruct((t, DIFF_QK_COLS), BF16), jax.ShapeDtypeStruct((vw, t), BF16),
                   jax.ShapeDtypeStruct((t, GATE_COLS), BF16)],
        compiler_params=_params(),
        name="in_proj",
    )(x, *a1, pos_col, *a2)


def _split3(a):
    hi = a.astype(BF16)
    r1 = a - hi.astype(F32)
    mid = r1.astype(BF16)
    lo = (r1 - mid.astype(F32)).astype(BF16)
    return hi, mid, lo


def _softplus(z):
    return jnp.maximum(z, 0.0) + jnp.log(1.0 + jnp.exp(-jnp.abs(z)))


def _rwkv_kernel(p_ref, mu_ref, w0_ref, w2_ref, a0_ref, a2_ref, g2_ref, kk_ref, ka_ref, rk_ref, lnw_ref, lnb_ref,
                 o_ref, carry_ref, state_ref, *, tiles_per_seq):
    c = CHUNK
    n = RWKV_N
    ts = RWKV_TILE
    nc = ts // c

    @pl.when(pl.program_id(0) % tiles_per_seq == 0)
    def _():
        carry_ref[...] = jnp.zeros_like(carry_ref)
        state_ref[...] = jnp.zeros_like(state_ref)

    p = p_ref[...]
    row = lax.broadcasted_iota(jnp.int32, (ts, 1), 0)
    shifted = jnp.where(row == 0, carry_ref[...], pltpu.roll(p, 1, 0))
    carry_ref[...] = p[ts - 1:ts, :]
    pm = p + (shifted - p) * mu_ref[...]

    d = RWKV_DIM
    r = pm[:, 0:d]
    k = pm[:, d:2 * d]
    v = pm[:, 2 * d:3 * d]
    pw = pm[:, 3 * d:3 * d + RWKV_W_LORA]
    pa = pm[:, 3 * d + RWKV_W_LORA:3 * d + RWKV_W_LORA + RWKV_A_LORA]
    pg = pm[:, 3 * d + RWKV_W_LORA + RWKV_A_LORA:RWKV_COLS]

    w_log = -_softplus(-(w0_ref[...] + _dot(jnp.tanh(pw).astype(BF16), w2_ref[...]))) - 0.5
    logd = -jnp.exp(w_log)
    a = jax.nn.sigmoid(a0_ref[...] + _dot(pa.astype(BF16), a2_ref[...]))
    g = _dot(jax.nn.sigmoid(pg).astype(BF16), g2_ref[...])

    tr = lax.broadcasted_iota(jnp.int32, (ts, ts), 0)
    tc = lax.broadcasted_iota(jnp.int32, (ts, ts), 1)
    tri = ((tr >= tc) & (tr // c == tc // c)).astype(BF16)
    hi, mid, lo = _split3(logd)
    cs = _dot(tri, hi) + _dot(tri, mid) + _dot(tri, lo)
    total = jnp.concatenate([jnp.broadcast_to(cs[(m + 1) * c - 1:(m + 1) * c, :], (c, RWKV_DIM)) for m in range(nc)], axis=0)
    e_in = jnp.exp(cs)
    e_ex = jnp.exp(cs - logd)
    e_inv = jnp.exp(-cs)
    e_end = jnp.exp(total - cs)
    g_end = jnp.exp(total)

    n_pairs = RWKV_HEADS // 2
    lo_t = lax.broadcasted_iota(jnp.int32, (ts, LANES), 1) < n
    lo_c = lax.broadcasted_iota(jnp.int32, (c, LANES), 1) < n

    def head_sum(x):
        blocks = []
        for q in range(n_pairs):
            xb = x[:, q * LANES:(q + 1) * LANES]
            s_lo = jnp.sum(jnp.where(lo_t, xb, 0.0), axis=-1, keepdims=True)
            s_hi = jnp.sum(jnp.where(lo_t, 0.0, xb), axis=-1, keepdims=True)
            blocks.append(jnp.where(lo_t, s_lo, s_hi))
        return jnp.concatenate(blocks, axis=-1)

    k2 = k * (1.0 + (a - 1.0) * ka_ref[...])
    kku = k * kk_ref[...]
    kkn = kku * jnp.minimum(lax.rsqrt(head_sum(kku * kku)), 1e12)
    b = kkn * a
    full = {"at": -kkn * e_ex, "rt": r * e_in, "bt": b * e_inv, "kt": k2 * e_inv, "v": v,
            "be": b * e_end, "ke": k2 * e_end}

    ti = lax.broadcasted_iota(jnp.int32, (2 * c, 4 * c), 0) % c
    si = lax.broadcasted_iota(jnp.int32, (2 * c, 4 * c), 1) % c
    strict = ti > si
    incl = ti >= si
    eye = (lax.broadcasted_iota(jnp.int32, (2 * c, 2 * c), 0)
           == lax.broadcasted_iota(jnp.int32, (2 * c, 2 * c), 1)).astype(F32)

    items = [(m, q) for m in range(nc) for q in range(n_pairs)]

    def stacked(name, m, q):
        xb = full[name][m * c:(m + 1) * c, q * LANES:(q + 1) * LANES]
        return jnp.concatenate([jnp.where(lo_c, xb, 0.0), jnp.where(lo_c, 0.0, xb)], axis=0).astype(BF16)

    st_ops = {name: [stacked(name, m, q) for m, q in items] for name in full}
    big_l = [_dot_nt(jnp.concatenate([at, rtb], axis=0), jnp.concatenate([bt, kt], axis=0))
             for at, rtb, bt, kt in zip(st_ops["at"], st_ops["rt"], st_ops["bt"], st_ops["kt"])]
    top_l = [jnp.where(strict, big[:2 * c, :], 0.0) for big in big_l]
    bot_l = [jnp.where(incl, big[2 * c:, :], 0.0).astype(BF16) for big in big_l]
    lakv_l = [_dot(top[:, 2 * c:].astype(BF16), vb) for top, vb in zip(top_l, st_ops["v"])]

    x_l = [top[:, :2 * c] for top in top_l]
    tinv_l = [eye + x for x in x_l]
    xb_l = [x.astype(BF16) for x in x_l]
    for _ in range(int(math.log2(c)) - 1):
        x_l = [_dot(xb, xb) for xb in xb_l]
        xb_l = [x.astype(BF16) for x in x_l]
        tinv_l = [tinv + _dot(tinv.astype(BF16), xb) for tinv, xb in zip(tinv_l, xb_l)]
    tinvb_l = [tinv.astype(BF16) for tinv in tinv_l]
    abar_l = [_dot(tb, at).astype(BF16) for tb, at in zip(tinvb_l, st_ops["at"])]
    vbar_l = [_dot(tb, lv.astype(BF16)).astype(BF16) for tb, lv in zip(tinvb_l, lakv_l)]
    rhat_l = [(rtb.astype(F32) + _dot(bot[:, :2 * c], ab)).astype(BF16)
              for rtb, bot, ab in zip(st_ops["rt"], bot_l, abar_l)]
    uv_l = [jnp.concatenate([vbar, vb], axis=0) for vbar, vb in zip(vbar_l, st_ops["v"])]
    y0_l = [_dot(bot, uv) for bot, uv in zip(bot_l, uv_l)]
    p_l = [_dot_tn(ab, be).astype(BF16) for ab, be in zip(abar_l, st_ops["be"])]
    q_l = [_dot_tn(uv, jnp.concatenate([be, ke], axis=0)) for uv, be, ke in zip(uv_l, st_ops["be"], st_ops["ke"])]

    st_l = [state_ref[q] for q in range(n_pairs)]
    y_rows = []
    for m in range(nc):
        y_blocks = []
        for q in range(n_pairs):
            idx = m * n_pairs + q
            st = st_l[q]
            st_b = st.astype(BF16)
            y_s = _dot_nt(rhat_l[idx], st_b) + y0_l[idx]
            y_blocks.append(y_s[:c] + y_s[c:])
            decay = g_end[(m + 1) * c - 1:(m + 1) * c, q * LANES:(q + 1) * LANES]
            st_l[q] = st * decay + _dot(st_b, p_l[idx]) + q_l[idx]
        y_rows.append(jnp.concatenate(y_blocks, axis=-1))
    for q in range(n_pairs):
        state_ref[q] = st_l[q]
    y = jnp.concatenate(y_rows, axis=0)

    inv_n = 1.0 / n
    yc = y - head_sum(y) * inv_n
    var = head_sum(yc * yc) * inv_n
    yn = yc * lax.rsqrt(var + RWKV_GN_EPS) * lnw_ref[...] + lnb_ref[...]
    bonus = head_sum(r * k2 * rk_ref[...]) * v
    o_ref[...] = ((yn + bonus) * g).astype(BF16)


def _rwkv(p_rwkv, seq, mu, w0, w2, a0, a2, g2, k_k, k_a, r_k, ln_w, ln_b):
    t = p_rwkv.shape[0]
    ts = RWKV_TILE
    kern = functools.partial(_rwkv_kernel, tiles_per_seq=seq // ts)
    specs, params = _const_operands(mu, w0, w2, a0, a2, g2, k_k, k_a, r_k, ln_w, ln_b)
    return pl.pallas_call(
        kern,
        grid=(t // ts,),
        in_specs=[_row_spec(ts, RWKV_COLS)] + specs,
        out_specs=_row_spec(ts, RWKV_DIM),
        out_shape=jax.ShapeDtypeStruct((t, RWKV_DIM), BF16),
        scratch_shapes=[pltpu.VMEM((1, RWKV_COLS), F32), pltpu.VMEM((RWKV_HEADS // 2, LANES, LANES), F32)],
        compiler_params=_params(),
        name="rwkv7",
    )(p_rwkv, *params)


def _mla_prep_body(p, pos_ref, freq_ref, qn_ref, kvn_ref, wq_ref, wqr_ref, wk_ref, wv_ref, q_out, k_out, v_out):
    hq = _rms(p[:, 0:MLA_Q_LORA], qn_ref[...], NORM_EPS).astype(BF16)
    hkv = _rms(p[:, MLA_Q_LORA:MLA_Q_LORA + MLA_KV_LORA], kvn_ref[...], NORM_EPS).astype(BF16)
    c_rope = MLA_Q_LORA + MLA_KV_LORA
    blk = p[:, c_rope:c_rope + LANES]
    ang = pos_ref[...].astype(F32) * freq_ref[...]
    cos = jnp.cos(ang)
    sin = jnp.sin(ang)
    scale = (MLA_NOPE + MLA_ROPE) ** -0.5 * LOG2E
    qa = _dot(hq, wq_ref[...])
    qr = _dot(hq, wqr_ref[...])
    lane = lax.broadcasted_iota(jnp.int32, blk.shape, 1)
    half = MLA_ROPE // 2
    kr = pltpu.roll(blk, MLA_NOPE, 1)
    rot = jnp.where(lane < MLA_NOPE + half, -pltpu.roll(blk, MLA_NOPE - half, 1), pltpu.roll(blk, MLA_NOPE + half, 1))
    rot = jnp.where((lane >= MLA_NOPE) & (lane < MLA_NOPE + MLA_ROPE), rot, 0.0)
    krope = kr * cos + rot * sin
    kn = _dot(hkv, wk_ref[...])
    for h in range(MLA_HEADS):
        hs = slice(h * LANES, (h + 1) * LANES)
        q_out[:, hs] = ((qa[:, hs] * cos + qr[:, hs] * sin) * scale).astype(BF16)
        k_out[:, hs] = (kn[:, hs] + krope).astype(BF16)
    vt = _dot_nt(wv_ref[...], hkv)
    rowi = lax.broadcasted_iota(jnp.int32, vt.shape, 0)
    v_out[...] = (vt + (rowi % MLA_VT_ROWS == MLA_V).astype(F32)).astype(BF16)


def _softmax_stage_t(s_l, pv_fn, m_scr, acc_scr, idx, s_scr, lanes=None):
    n = len(s_l)

    def load(ref, c):
        return ref[c] if lanes is None else jnp.concatenate([ref[c, :, ls] for ls in lanes], axis=-1)

    def store(ref, c, val):
        if lanes is None:
            ref[c] = val
            return
        o = 0
        for ls in lanes:
            ref[c, :, ls] = val[:, o:o + ls.stop - ls.start]
            o += ls.stop - ls.start

    m_old = [load(m_scr, c)[0:1, :] for c in idx]
    m_new = [jnp.maximum(m, jnp.max(s, axis=0, keepdims=True)) for m, s in zip(m_old, s_l)]
    alpha = [jnp.exp2(mo - mn) for mo, mn in zip(m_old, m_new)]
    for c in range(n):
        s_scr[c] = s_l[c]
    pv_l = [pv_fn(c, jnp.exp2(s_scr[c] - m_new[c]).astype(BF16)) for c in range(n)]
    for c in range(n):
        store(m_scr, idx[c], jnp.broadcast_to(m_new[c], (m_scr.shape[1], m_new[c].shape[1])))
        store(acc_scr, idx[c], alpha[c] * load(acc_scr, idx[c]) + pv_l[c])


def _mla_attn_kernel(q_ref, k_ref, vt_ref, o_ref, m_scr, acc_scr, s_scr):
    tq = TQ
    i = pl.program_id(1)
    th = tq // 2
    m_scr[...] = jnp.full(m_scr.shape, NEG, F32)
    acc_scr[...] = jnp.zeros(acc_scr.shape, F32)
    hsl = [slice(h * LANES, (h + 1) * LANES) for h in range(MLA_HEADS)]
    vsl = [slice(h * MLA_VT_ROWS, (h + 1) * MLA_VT_ROWS) for h in range(MLA_HEADS)]
    heads = list(range(MLA_HEADS))

    def tile(off, tk, q_lo=0, masked=False):
        nq = tq - q_lo
        s_l = [_dot_nt(k_ref[pl.ds(off, tk), hs], q_ref[q_lo:, hs]) for hs in hsl]
        if masked:
            vis = lax.broadcasted_iota(jnp.int32, (tk, nq), 0) <= lax.broadcasted_iota(jnp.int32, (tk, nq), 1)
            s_l = [jnp.where(vis, s, NEG) for s in s_l]
        _softmax_stage_t(s_l, lambda c, p: _dot(vt_ref[vsl[c], pl.ds(off, tk)], p), m_scr, acc_scr, heads,
                         s_scr.at[:, 0:tk, 0:nq], None if q_lo == 0 else [slice(q_lo, tq)])

    def pair_body(j, carry):
        tile(pl.multiple_of(j * 2 * tq, 2 * tq), 2 * tq)
        return carry

    def single_body(j, carry):
        tile(pl.multiple_of(j * tq, tq), tq)
        return carry

    n_pairs = i // 2 if KEY_TILES_PER_STEP == 2 else 0
    if KEY_TILES_PER_STEP == 2:
        lax.fori_loop(0, n_pairs, pair_body, 0)
    lax.fori_loop(2 * n_pairs, i, single_body, 0)
    diag = pl.multiple_of(i * tq, tq)
    tile(diag, th, 0, True)
    tile(pl.multiple_of(diag + th, th), th, th, True)
    for pr in range(MLA_HEADS // 2):
        accs = [acc_scr[2 * pr + hh] for hh in range(2)]
        ot = jnp.concatenate([acc[:MLA_V, :] * (1.0 / acc[MLA_V:MLA_V + 1, :]) for acc in accs], axis=0)
        o_ref[:, pr * LANES:(pr + 1) * LANES] = jnp.transpose(ot).astype(BF16)


def _mla_attn(q, k, vt, batch, seq):
    t = q.shape[0]
    nq = seq // TQ
    hw = MLA_HEADS * LANES
    vw = MLA_HEADS * MLA_V
    return pl.pallas_call(
        _mla_attn_kernel,
        grid=(batch, nq),
        in_specs=[pl.BlockSpec((TQ, hw), lambda b, i: (b * nq + i, 0)),
                  pl.BlockSpec((seq, hw), lambda b, i: (b, 0)),
                  pl.BlockSpec((MLA_HEADS * MLA_VT_ROWS, seq), lambda b, i: (0, b))],
        out_specs=pl.BlockSpec((TQ, vw), lambda b, i: (b * nq + i, 0)),
        out_shape=jax.ShapeDtypeStruct((t, vw), BF16),
        scratch_shapes=[pltpu.VMEM((MLA_HEADS, 8, TQ), F32), pltpu.VMEM((MLA_HEADS, MLA_VT_ROWS, TQ), F32),
                        pltpu.VMEM((MLA_HEADS, KEY_TILES_PER_STEP * TQ, TQ), F32)],
        compiler_params=_params2(),
        name="mla_attn",
    )(q, k, vt)


def _t5_bucket_table():
    n = np.arange(0, REL_MAX_DISTANCE + 1)
    max_exact = REL_BUCKETS // 2
    nf = np.maximum(n, 1).astype(np.float32)
    ratio = np.log(nf / np.float32(max_exact)) / np.float32(math.log(REL_MAX_DISTANCE / max_exact))
    large = max_exact + (ratio * np.float32(REL_BUCKETS - max_exact)).astype(np.int32)
    large = np.minimum(large, REL_BUCKETS - 1)
    return np.where(n < max_exact, n, large)


_BUCKETS = _t5_bucket_table()
_FAR_BUCKET = int(_BUCKETS[REL_MAX_DISTANCE])
assert _FAR_BUCKET == REL_BUCKETS - 1 and np.all(np.diff(_BUCKETS) >= 0)
_BUCKET_STARTS = [int(np.argmax(_BUCKETS >= b)) for b in range(REL_BUCKETS // 2 + 1, REL_BUCKETS)]


def _diff_attn_kernel(rb_ref, lam_ref, subln_ref, q_ref, k_ref, vt_ref, o_ref, bias_scr, m_scr, acc_scr, s_scr, *,
                      lambda_init):
    tq = TQ
    i = pl.program_id(1)
    th = tq // 2
    r_i = lax.broadcasted_iota(jnp.int32, (tq, tq), 0)
    c_i = lax.broadcasted_iota(jnp.int32, (tq, tq), 1)

    @pl.when((pl.program_id(0) == 0) & (i == 0))
    def _():
        for t_idx, delta in enumerate((0, tq)):
            d = jnp.maximum(c_i - r_i + delta, 0)
            log_b = REL_BUCKETS // 2
            for start in _BUCKET_STARTS:
                log_b = log_b + (d >= start).astype(jnp.int32)
            bucket = jnp.where(d < REL_BUCKETS // 2, d, log_b)
            for h in range(DIFF_HEADS):
                bias = jnp.zeros((tq, tq), F32)
                for b in range(REL_BUCKETS):
                    bias = jnp.where(bucket == b, rb_ref[b, h], bias)
                bias_scr[t_idx, h] = (bias - rb_ref[_FAR_BUCKET, h]) * LOG2E

    m_scr[...] = jnp.full(m_scr.shape, NEG, F32)
    acc_scr[...] = jnp.zeros(acc_scr.shape, F32)
    lane = lax.broadcasted_iota(jnp.int32, (tq, LANES), 1)
    ones_rows = {tk: (lax.broadcasted_iota(jnp.int32, (DIFF_VT_ROWS - DIFF_V, tk), 0) == 0).astype(BF16)
                 for tk in (th, tq, 2 * tq)}
    hsl = [slice(h * LANES, (h + 1) * LANES) for h in range(DIFF_HEADS)]
    heads = list(range(DIFF_HEADS))
    q_l = []
    for hs in hsl:
        qp = q_ref[:, hs]
        q_l.append(jnp.concatenate([jnp.where(lane < DIFF_QK, qp, jnp.zeros_like(qp)),
                                    jnp.where(lane >= DIFF_QK, qp, jnp.zeros_like(qp))], axis=0))

    def tile(off, tk, bias_idx, q_lo=0, k_lo=0, masked=False):
        nq = tq - q_lo
        if q_lo == 0:
            qs_l = q_l
        else:
            qs_l = [jnp.concatenate([q[q_lo:tq], q[tq + q_lo:]], axis=0) for q in q_l]
        s_l = [_dot_nt(k_ref[pl.ds(off, tk), hsl[h]], qs_l[h]) for h in heads]
        if bias_idx is not None:
            s_l = [s + jnp.concatenate([bias_scr[bias_idx, h, k_lo:k_lo + tk, q_lo:tq]] * 2, axis=1)
                   for h, s in enumerate(s_l)]
        if masked:
            vis = lax.broadcasted_iota(jnp.int32, (tk, nq), 0) <= lax.broadcasted_iota(jnp.int32, (tk, nq), 1)
            vis2 = jnp.concatenate([vis, vis], axis=1)
            s_l = [jnp.where(vis2, s, NEG) for s in s_l]

        def pv(c, p):
            vt = jnp.concatenate([vt_ref[hsl[c], pl.ds(off, tk)], ones_rows[tk]], axis=0)
            return jnp.concatenate([_dot(vt, p[:, :nq]), _dot(vt, p[:, nq:])], axis=1)

        lanes = None if q_lo == 0 else [slice(q_lo, tq), slice(tq + q_lo, 2 * tq)]
        _softmax_stage_t(s_l, pv, m_scr, acc_scr, heads, s_scr.at[:, 0:tk, 0:2 * nq], lanes)

    def far_pair_body(j, carry):
        tile(pl.multiple_of(j * 2 * tq, 2 * tq), 2 * tq, None)
        return carry

    def far_body(j, carry):
        tile(pl.multiple_of(j * tq, tq), tq, None)
        return carry

    def near_body(j, carry):
        tile(pl.multiple_of(j * tq, tq), tq, 1)
        return carry

    n_far = jnp.maximum(i - 1, 0)
    n_pairs = n_far // 2 if KEY_TILES_PER_STEP == 2 else 0
    if KEY_TILES_PER_STEP == 2:
        lax.fori_loop(0, n_pairs, far_pair_body, 0)
    lax.fori_loop(2 * n_pairs, n_far, far_body, 0)
    lax.fori_loop(n_far, i, near_body, 0)
    diag = pl.multiple_of(i * tq, tq)
    tile(diag, th, 0, 0, 0, True)
    tile(pl.multiple_of(diag + th, th), th, 0, th, th, True)

    lam = lam_ref[...]
    lam_full = (jnp.exp(jnp.sum(lam[0:1] * lam[1:2], axis=-1, keepdims=True))
                - jnp.exp(jnp.sum(lam[2:3] * lam[3:4], axis=-1, keepdims=True)) + lambda_init)
    for h in heads:
        acc = acc_scr[h]
        on = acc[:DIFF_V, :] * (1.0 / acc[DIFF_V:DIFF_V + 1, :])
        ot = on[:, :tq] - lam_full * on[:, tq:]
        ot = ot * lax.rsqrt(jnp.mean(ot * ot, axis=0, keepdims=True) + SUBLN_EPS)
        o_ref[:, hsl[h]] = (jnp.transpose(ot) * subln_ref[...] * (1.0 - lambda_init)).astype(BF16)


def _diff_attn(p_qk, p_vt, rel_bias, lam, subln, batch, seq, layer_idx):
    t = p_qk.shape[0]
    nq = seq // TQ
    w = DIFF_HEADS * LANES
    lambda_init = 0.8 - 0.6 * math.exp(-0.3 * layer_idx)
    kern = functools.partial(_diff_attn_kernel, lambda_init=lambda_init)
    specs, params = _const_operands(lam, subln)
    return pl.pallas_call(
        kern,
        grid=(batch, nq),
        in_specs=[pl.BlockSpec(memory_space=pltpu.SMEM)] + specs + [
                  pl.BlockSpec((TQ, w), lambda b, i: (b * nq + i, 0)),
                  pl.BlockSpec((seq, w), lambda b, i: (b, 1)),
                  pl.BlockSpec((w, seq), lambda b, i: (0, b))],
        out_specs=pl.BlockSpec((TQ, w), lambda b, i: (b * nq + i, 0)),
        out_shape=jax.ShapeDtypeStruct((t, w), BF16),
        scratch_shapes=[pltpu.VMEM((2, DIFF_HEADS, TQ, TQ), F32), pltpu.VMEM((DIFF_HEADS, 8, 2 * TQ), F32),
                        pltpu.VMEM((DIFF_HEADS, DIFF_VT_ROWS, 2 * TQ), F32),
                        pltpu.VMEM((DIFF_HEADS, KEY_TILES_PER_STEP * TQ, 2 * TQ), F32)],
        compiler_params=_params2(),
        name="diff_attn",
    )(rel_bias, *params, p_qk, p_qk, p_vt)


def _merge_kernel(x_ref, pg_ref, bg_ref, or_ref, om_ref, od_ref, wr_ref, wm_ref, wd_ref, wo_ref, gf_ref, x_out, h_out):
    halves = [slice(s * (TM_MERGE // 2), (s + 1) * (TM_MERGE // 2)) for s in range(2)]
    branch = [[_dot(o_ref[rs, :], w_ref[...]) for o_ref, w_ref in ((or_ref, wr_ref), (om_ref, wm_ref), (od_ref, wd_ref))]
              for rs in halves]
    for rs, d in zip(halves, branch):
        merged = None
        for idx in range(3):
            cs = slice(idx * D_MODEL, (idx + 1) * D_MODEL)
            term = jax.nn.sigmoid(pg_ref[rs, cs].astype(F32) + bg_ref[:, cs]) * d[idx]
            merged = term if merged is None else merged + term
        x1 = x_ref[rs, :] + _dot(merged.astype(BF16), wo_ref[...])
        x_out[rs, :] = x1
        h_out[rs, :] = _rms(x1, gf_ref[...], NORM_EPS).astype(BF16)


def _merge(x, p_gate, b_gate, o_r, o_m, o_d, w_r, w_m, w_d, w_o, g_ffn):
    t = x.shape[0]
    tm = TM_MERGE
    (bg_spec,), (bg,) = _const_operands(b_gate)
    specs, params = _const_operands(w_r, w_m, w_d, w_o, g_ffn)
    return pl.pallas_call(
        _merge_kernel,
        grid=(t // tm,),
        in_specs=[_row_spec(tm, D_MODEL), _row_spec(tm, GATE_COLS), bg_spec,
                  _row_spec(tm, RWKV_DIM), _row_spec(tm, MLA_HEADS * MLA_V), _row_spec(tm, DIFF_HEADS * DIFF_V)] + specs,
        out_specs=[_row_spec(tm, D_MODEL), _row_spec(tm, D_MODEL)],
        out_shape=[jax.ShapeDtypeStruct((t, D_MODEL), F32), jax.ShapeDtypeStruct((t, D_MODEL), BF16)],
        compiler_params=_params(),
        name="merge",
    )(x, p_gate, bg, o_r, o_m, o_d, *params)


def _ffn_kernel(x_ref, h_ref, wup_ref, cw_ref, cb_ref, wdn_ref, gfin_ref, o_ref, carry_ref, *, tiles_per_seq, final_norm):
    tm = TM_FFN

    @pl.when(pl.program_id(0) % tiles_per_seq == 0)
    def _():
        carry_ref[0:8, :] = jnp.zeros((8, carry_ref.shape[1]), F32)

    h = h_ref[...]

    def conv(u, cols):
        carry_ref[8:, cols] = u
        u1 = carry_ref[7:7 + tm, cols]
        u2 = carry_ref[6:6 + tm, cols]
        carry_ref[0:8, cols] = u[tm - 8:tm, :]
        return cw_ref[0:1, cols] * u2 + cw_ref[1:2, cols] * u1 + cw_ref[2:3, cols] * u + cb_ref[:, cols]

    bounds = list(range(0, D_FF, FF_CHUNK)) + [D_FF]
    chunks = [slice(a, b) for a, b in zip(bounds[:-1], bounds[1:])]
    shift = lambda c: slice(D_FF + c.start, D_FF + c.stop)

    def up(gc):
        return _dot(h, wup_ref[:, gc]), _dot(h, wup_ref[:, shift(gc)])

    nxt = up(chunks[0])
    acts = []
    for ck, gc in enumerate(chunks):
        ug, uv = nxt
        if ck + 1 < len(chunks):
            nxt = up(chunks[ck + 1])
        gate = conv(ug, gc)
        val = conv(uv, shift(gc))
        acts.append((gate * jax.nn.sigmoid(gate) * val).astype(BF16))
    out = x_ref[...] + _dot(jnp.concatenate(acts, axis=-1), wdn_ref[...])
    if final_norm:
        out = _rms(out, gfin_ref[...], NORM_EPS)
    o_ref[...] = out


def _ffn(x1, h2, w_up, conv_w, conv_b, w_down, g_final, seq, final_norm):
    t = x1.shape[0]
    tm = TM_FFN
    kern = functools.partial(_ffn_kernel, tiles_per_seq=seq // tm, final_norm=final_norm)
    specs, params = _const_operands(w_up, conv_w, conv_b, w_down, g_final)
    return pl.pallas_call(
        kern,
        grid=(t // tm,),
        in_specs=[_row_spec(tm, D_MODEL), _row_spec(tm, D_MODEL)] + specs,
        out_specs=_row_spec(tm, D_MODEL),
        out_shape=jax.ShapeDtypeStruct((t, D_MODEL), F32),
        scratch_shapes=[pltpu.VMEM((8 + tm, 2 * D_FF), F32)],
        compiler_params=_params(),
        name="conv_ffn",
    )(x1, h2, *params)


def _mla_weights(w_uq, w_ukv):
    qd = MLA_NOPE + MLA_ROPE
    half = MLA_ROPE // 2
    wq = w_uq.reshape(MLA_Q_LORA, MLA_HEADS, qd)
    zq = jnp.zeros((MLA_Q_LORA, MLA_HEADS, LANES - qd), F32)
    wq_main = jnp.concatenate([wq, zq], axis=-1)
    x1 = wq[:, :, MLA_NOPE:MLA_NOPE + half]
    x2 = wq[:, :, MLA_NOPE + half:]
    wq_rot = jnp.concatenate([jnp.zeros((MLA_Q_LORA, MLA_HEADS, MLA_NOPE), F32), -x2, x1, zq], axis=-1)
    wkv = w_ukv.reshape(MLA_KV_LORA, MLA_HEADS, MLA_NOPE + MLA_V)
    zkv = jnp.zeros((MLA_KV_LORA, MLA_HEADS, LANES - MLA_NOPE), F32)
    wk = jnp.concatenate([wkv[:, :, :MLA_NOPE], zkv], axis=-1)
    wv = jnp.concatenate([wkv[:, :, MLA_NOPE:], jnp.zeros((MLA_KV_LORA, MLA_HEADS, MLA_VT_ROWS - MLA_V), F32)], axis=-1)
    flat = lambda w: w.reshape(w.shape[0], -1).astype(BF16)
    return flat(wq_main), flat(wq_rot), flat(wk), flat(wv).T


def kernel(x, positions, rel_bias, norm_mix, w_in, b_gate, rwkv_mu, rwkv_w0, rwkv_w2, rwkv_a0, rwkv_a2, rwkv_g2, rwkv_k_k, rwkv_k_a, rwkv_r_k, rwkv_ln_w, rwkv_ln_b, mla_q_norm, mla_w_uq, mla_kv_norm, mla_w_ukv, diff_lambda, diff_subln, w_branch_rwkv, w_branch_mla, w_branch_diff, w_o, norm_ffn, ffn_w_up, ffn_conv_w, ffn_conv_b, ffn_w_down, norm_final):
    batch, seq, _ = x.shape
    depth = w_in.shape[0]
    t = batch * seq
    assert seq % TQ == 0 and seq % TM == 0 and seq % TM_FFN == 0 and seq % RWKV_TILE == 0 and RWKV_TILE % CHUNK == 0
    assert (batch * seq) % TM_MERGE == 0
    xf = x.reshape(t, D_MODEL)
    pos_col = positions.reshape(t, 1)
    inv_freq = ROPE_BASE ** (-jnp.arange(0, MLA_ROPE, 2, dtype=F32) / MLA_ROPE)
    freq = jnp.concatenate([jnp.zeros((MLA_NOPE,), F32), inv_freq, inv_freq,
                            jnp.zeros((LANES - MLA_NOPE - MLA_ROPE,), F32)]).reshape(1, LANES)
    vec = lambda v: v.reshape(v.shape[0], 1, -1)
    bf = lambda w: w.astype(BF16)
    n_q = DIFF_QK_COLS // 2
    diff_scale = jnp.concatenate([jnp.full((n_q,), DIFF_QK ** -0.5 * LOG2E, F32), jnp.ones((n_q,), F32)])
    s0, s1, s2 = RWKV_COLS, RWKV_COLS + MLA_COLS, RWKV_COLS + MLA_COLS + DIFF_COLS
    sv = s1 + DIFF_QK_COLS
    st = {
        "norm_mix": vec(norm_mix),
        "w_r": bf(w_in[:, :, :s0]),
        "w_m": bf(jnp.pad(w_in[:, :, s0:s1], ((0, 0), (0, 0), (0, MLA_PAD - MLA_COLS)))),
        "w_qk": bf(w_in[:, :, s1:sv] * diff_scale),
        "w_vt": bf(jnp.swapaxes(w_in[:, :, sv:s2], 1, 2)),
        "w_g": bf(w_in[:, :, s2:]),
        "q_norm": vec(mla_q_norm), "kv_norm": vec(mla_kv_norm),
        "mu": vec(rwkv_mu), "w0": vec(rwkv_w0), "w2": bf(rwkv_w2), "a0": vec(rwkv_a0), "a2": bf(rwkv_a2),
        "g2": bf(rwkv_g2), "k_k": vec(rwkv_k_k), "k_a": vec(rwkv_k_a), "r_k": vec(rwkv_r_k),
        "ln_w": vec(rwkv_ln_w), "ln_b": vec(rwkv_ln_b),
        "lam": diff_lambda, "subln": vec(diff_subln),
        "b_gate": vec(b_gate), "wb_r": bf(w_branch_rwkv), "wb_m": bf(w_branch_mla), "wb_d": bf(w_branch_diff),
        "w_o": bf(w_o), "norm_ffn": vec(norm_ffn),
        "w_up": bf(ffn_w_up), "conv_w": ffn_conv_w, "conv_b": vec(ffn_conv_b), "w_down": bf(ffn_w_down),
    }
    g_final = norm_final.reshape(1, -1)

    for l in range(depth):
        p = {name: _Layer(a, l) for name, a in st.items()}
        wq, wqr, wk, wv = _mla_weights(mla_w_uq[l], mla_w_ukv[l])
        p_rwkv, q_m, k_m, v_m, p_qk, p_vt, p_gate = _in_proj(
            xf, p["norm_mix"], p["w_r"], p["w_m"], p["w_qk"], p["w_g"], p["w_vt"], pos_col, freq, p["q_norm"],
            p["kv_norm"], wq, wqr, wk, wv)

        o_r = _rwkv(p_rwkv, seq, p["mu"], p["w0"], p["w2"], p["a0"], p["a2"], p["g2"], p["k_k"], p["k_a"], p["r_k"],
                    p["ln_w"], p["ln_b"])

        o_m = _mla_attn(q_m, k_m, v_m, batch, seq)

        o_d = _diff_attn(p_qk, p_vt, rel_bias, p["lam"], p["subln"], batch, seq, l)

        x1, h2 = _merge(xf, p_gate, p["b_gate"], o_r, o_m, o_d, p["wb_r"], p["wb_m"], p["wb_d"], p["w_o"], p["norm_ffn"])
        xf = _ffn(x1, h2, p["w_up"], p["conv_w"], p["conv_b"], p["w_down"], g_final, seq, final_norm=(l == depth - 1))
    return xf.reshape(batch, seq, D_MODEL)
```

```python
import functools
import math

import numpy as np
import jax
import jax.numpy as jnp
from jax import lax
from jax.experimental import pallas as pl
from jax.experimental.pallas import tpu as pltpu

F32 = jnp.float32
BF16 = jnp.bfloat16

D_MODEL = 1024
RWKV_HEADS = 8
RWKV_N = 64
RWKV_DIM = 512
RWKV_W_LORA = 64
RWKV_A_LORA = 64
RWKV_G_LORA = 128
RWKV_COLS = 3 * RWKV_DIM + RWKV_W_LORA + RWKV_A_LORA + RWKV_G_LORA
RWKV_GN_EPS = 64e-5
MLA_HEADS = 8
MLA_Q_LORA = 256
MLA_KV_LORA = 128
MLA_NOPE = 64
MLA_ROPE = 32
MLA_V = 64
MLA_VT_ROWS = MLA_V + 16
MLA_COLS = 416
MLA_PAD = 512
ROPE_BASE = 10000.0
DIFF_HEADS = 4
DIFF_QK = 64
DIFF_V = 128
DIFF_COLS = 1536
DIFF_QK_COLS = 1024
DIFF_VT_ROWS = DIFF_V + 16
REL_BUCKETS = 32
REL_MAX_DISTANCE = 128
D_FF = 2816
GATE_COLS = 3072
NORM_EPS = 1e-6
SUBLN_EPS = 1e-5

LANES = 128
VMEM_LIMIT = 58 * 1024 * 1024
TM = 512
TM_FFN = 512
TM_MERGE = 1024
CHUNK = 64
RWKV_TILE = 512
RWKV_SPLIT = 2
TQ = 512
KEY_TILES_PER_STEP = 2
FF_CHUNK = 512
NEG = -1e30
LOG2E = 1.4426950408889634

_NT = (((1,), (1,)), ((), ()))
_TN = (((0,), (0,)), ((), ()))


def _dot(a, b):
    return jnp.dot(a, b, preferred_element_type=F32)


def _dot_nt(a, b):
    return lax.dot_general(a, b, _NT, preferred_element_type=F32)


def _dot_tn(a, b):
    return lax.dot_general(a, b, _TN, preferred_element_type=F32)


def _rms(x, g, eps):
    return x * lax.rsqrt(jnp.mean(x * x, axis=-1, keepdims=True) + eps) * g


def _params():
    return pltpu.CompilerParams(dimension_semantics=("arbitrary",), vmem_limit_bytes=VMEM_LIMIT)


def _params2():
    return pltpu.CompilerParams(dimension_semantics=("arbitrary", "arbitrary"), vmem_limit_bytes=VMEM_LIMIT)


def _const_spec(shape):
    return pl.BlockSpec(shape, lambda *_: (0,) * len(shape), pipeline_mode=pl.Buffered(1))


class _Layer:
    def __init__(self, stacked, layer):
        self.stacked, self.layer = stacked, layer


def _const_operands(*params):
    specs, arrays = [], []
    for p in params:
        if isinstance(p, _Layer):
            shape = p.stacked.shape[1:]
            specs.append(pl.BlockSpec((None,) + shape, lambda *_, l=p.layer, n=len(shape): (l,) + (0,) * n,
                                      pipeline_mode=pl.Buffered(1)))
            arrays.append(p.stacked)
        else:
            specs.append(_const_spec(p.shape))
            arrays.append(p)
    return specs, arrays


def _row_spec(tm, cols, col_block=0):
    return pl.BlockSpec((tm, cols), lambda i: (i, col_block))


def _in_proj_kernel(x_ref, g_ref, wr_ref, wm_ref, wqk_ref, wg_ref, wvt_ref, pos_ref, freq_ref, qn_ref, kvn_ref, wq_ref,
                    wqr_ref, wk_ref, wv_ref, pr_ref, mq_ref, mk_ref, mvt_ref, pqk_ref, pvt_ref, pg_ref):
    h = _rms(x_ref[...], g_ref[...], NORM_EPS).astype(BF16)
    p_mla = _dot(h, wm_ref[...])
    pqk_ref[...] = _dot(h, wqk_ref[...]).astype(BF16)
    pg_ref[...] = _dot(h, wg_ref[...]).astype(BF16)
    _mla_prep_body(p_mla, pos_ref, freq_ref, qn_ref, kvn_ref, wq_ref, wqr_ref, wk_ref, wv_ref, mq_ref, mk_ref, mvt_ref)
    pr_ref[...] = _dot(h, wr_ref[...])
    pvt_ref[...] = _dot_nt(wvt_ref[...], h).astype(BF16)


def _in_proj(x, g, w_r, w_m, w_qk, w_g, w_vt, pos_col, freq, q_norm, kv_norm, wq, wqr, wk, wv):
    t = x.shape[0]
    vw = DIFF_HEADS * DIFF_V
    hw = MLA_HEADS * LANES
    mvw = MLA_HEADS * MLA_VT_ROWS
    c1, a1 = _const_operands(g, w_r, w_m, w_qk, w_g, w_vt)
    c2, a2 = _const_operands(freq, q_norm, kv_norm, wq, wqr, wk, wv)
    return pl.pallas_call(
        _in_proj_kernel,
        grid=(t // TM,),
        in_specs=[_row_spec(TM, D_MODEL)] + c1 + [_row_spec(TM, 1)] + c2,
        out_specs=[_row_spec(TM, RWKV_COLS), _row_spec(TM, hw), _row_spec(TM, hw),
                   pl.BlockSpec((mvw, TM), lambda i: (0, i)), _row_spec(TM, DIFF_QK_COLS),
                   pl.BlockSpec((vw, TM), lambda i: (0, i)), _row_spec(TM, GATE_COLS)],
        out_shape=[jax.ShapeDtypeStruct((t, RWKV_COLS), F32), jax.ShapeDtypeStruct((t, hw), BF16),
                   jax.ShapeDtypeStruct((t, hw), BF16), jax.ShapeDtypeStruct((mvw, t), BF16),
                   jax.ShapeDtypeStruct((t, DIFF_QK_COLS), BF16), jax.ShapeDtypeStruct((vw, t), BF16),
                   jax.ShapeDtypeStruct((t, GATE_COLS), BF16)],
        compiler_params=_params(),
        name="in_proj",
    )(x, *a1, pos_col, *a2)


def _split3(a):
    hi = a.astype(BF16)
    r1 = a - hi.astype(F32)
    mid = r1.astype(BF16)
    lo = (r1 - mid.astype(F32)).astype(BF16)
    return hi, mid, lo


def _softplus(z):
    return jnp.maximum(z, 0.0) + jnp.log(1.0 + jnp.exp(-jnp.abs(z)))


def _rwkv_kernel(p_ref, *refs, tiles_per_seq):
    param_refs, (o_ref, carry_ref, state_ref) = refs[:-3], refs[-3:]

    @pl.when(pl.program_id(0) % tiles_per_seq == 0)
    def _():
        carry_ref[...] = jnp.zeros_like(carry_ref)
        state_ref[...] = jnp.zeros_like(state_ref)

    sub = RWKV_TILE // RWKV_SPLIT
    for s in range(RWKV_SPLIT):
        rows = slice(s * sub, (s + 1) * sub)
        _rwkv_subtile(p_ref.at[rows, :], *param_refs, o_ref.at[rows, :], carry_ref, state_ref)


def _rwkv_subtile(p_ref, mu_ref, w0_ref, w2_ref, a0_ref, a2_ref, g2_ref, kk_ref, ka_ref, rk_ref, lnw_ref, lnb_ref,
                  o_ref, carry_ref, state_ref):
    c = CHUNK
    n = RWKV_N
    ts = p_ref.shape[0]
    nc = ts // c

    p = p_ref[...]
    row = lax.broadcasted_iota(jnp.int32, (ts, 1), 0)
    shifted = jnp.where(row == 0, carry_ref[...], pltpu.roll(p, 1, 0))
    carry_ref[...] = p[ts - 1:ts, :]
    pm = p + (shifted - p) * mu_ref[...]

    d = RWKV_DIM
    r = pm[:, 0:d]
    k = pm[:, d:2 * d]
    v = pm[:, 2 * d:3 * d]
    pw = pm[:, 3 * d:3 * d + RWKV_W_LORA]
    pa = pm[:, 3 * d + RWKV_W_LORA:3 * d + RWKV_W_LORA + RWKV_A_LORA]
    pg = pm[:, 3 * d + RWKV_W_LORA + RWKV_A_LORA:RWKV_COLS]

    w_log = -_softplus(-(w0_ref[...] + _dot(jnp.tanh(pw).astype(BF16), w2_ref[...]))) - 0.5
    logd = -jnp.exp(w_log)
    a = jax.nn.sigmoid(a0_ref[...] + _dot(pa.astype(BF16), a2_ref[...]))
    g = _dot(jax.nn.sigmoid(pg).astype(BF16), g2_ref[...])

    tr = lax.broadcasted_iota(jnp.int32, (ts, ts), 0)
    tc = lax.broadcasted_iota(jnp.int32, (ts, ts), 1)
    tri = ((tr >= tc) & (tr // c == tc // c)).astype(BF16)
    hi, mid, lo = _split3(logd)
    cs = _dot(tri, hi) + _dot(tri, mid) + _dot(tri, lo)
    total = jnp.concatenate([jnp.broadcast_to(cs[(m + 1) * c - 1:(m + 1) * c, :], (c, RWKV_DIM)) for m in range(nc)], axis=0)
    e_in = jnp.exp(cs)
    e_ex = jnp.exp(cs - logd)
    e_inv = jnp.exp(-cs)
    e_end = jnp.exp(total - cs)
    g_end = jnp.exp(total)

    n_pairs = RWKV_HEADS // 2
    lo_t = lax.broadcasted_iota(jnp.int32, (ts, LANES), 1) < n
    lo_c = lax.broadcasted_iota(jnp.int32, (c, LANES), 1) < n

    def head_sum(x):
        blocks = []
        for q in range(n_pairs):
            xb = x[:, q * LANES:(q + 1) * LANES]
            s_lo = jnp.sum(jnp.where(lo_t, xb, 0.0), axis=-1, keepdims=True)
            s_hi = jnp.sum(jnp.where(lo_t, 0.0, xb), axis=-1, keepdims=True)
            blocks.append(jnp.where(lo_t, s_lo, s_hi))
        return jnp.concatenate(blocks, axis=-1)

    k2 = k * (1.0 + (a - 1.0) * ka_ref[...])
    kku = k * kk_ref[...]
    kkn = kku * jnp.minimum(lax.rsqrt(head_sum(kku * kku)), 1e12)
    b = kkn * a
    full = {"at": -kkn * e_ex, "rt": r * e_in, "bt": b * e_inv, "kt": k2 * e_inv, "v": v,
            "be": b * e_end, "ke": k2 * e_end}

    ti = lax.broadcasted_iota(jnp.int32, (2 * c, 4 * c), 0) % c
    si = lax.broadcasted_iota(jnp.int32, (2 * c, 4 * c), 1) % c
    strict = ti > si
    incl = ti >= si
    eye = (lax.broadcasted_iota(jnp.int32, (2 * c, 2 * c), 0)
           == lax.broadcasted_iota(jnp.int32, (2 * c, 2 * c), 1)).astype(F32)

    items = [(m, q) for m in range(nc) for q in range(n_pairs)]

    def stacked(name, m, q):
        xb = full[name][m * c:(m + 1) * c, q * LANES:(q + 1) * LANES]
        return jnp.concatenate([jnp.where(lo_c, xb, 0.0), jnp.where(lo_c, 0.0, xb)], axis=0).astype(BF16)

    st_ops = {name: [stacked(name, m, q) for m, q in items] for name in full}
    big_l = [_dot_nt(jnp.concatenate([at, rtb], axis=0), jnp.concatenate([bt, kt], axis=0))
             for at, rtb, bt, kt in zip(st_ops["at"], st_ops["rt"], st_ops["bt"], st_ops["kt"])]
    top_l = [jnp.where(strict, big[:2 * c, :], 0.0) for big in big_l]
    bot_l = [jnp.where(incl, big[2 * c:, :], 0.0).astype(BF16) for big in big_l]
    lakv_l = [_dot(top[:, 2 * c:].astype(BF16), vb) for top, vb in zip(top_l, st_ops["v"])]

    x_l = [top[:, :2 * c] for top in top_l]
    tinv_l = [eye + x for x in x_l]
    xb_l = [x.astype(BF16) for x in x_l]
    for _ in range(int(math.log2(c)) - 1):
        x_l = [_dot(xb, xb) for xb in xb_l]
        xb_l = [x.astype(BF16) for x in x_l]
        tinv_l = [tinv + _dot(tinv.astype(BF16), xb) for tinv, xb in zip(tinv_l, xb_l)]
    tinvb_l = [tinv.astype(BF16) for tinv in tinv_l]
    abar_l = [_dot(tb, at).astype(BF16) for tb, at in zip(tinvb_l, st_ops["at"])]
    vbar_l = [_dot(tb, lv.astype(BF16)).astype(BF16) for tb, lv in zip(tinvb_l, lakv_l)]
    rhat_l = [(rtb.astype(F32) + _dot(bot[:, :2 * c], ab)).astype(BF16)
              for rtb, bot, ab in zip(st_ops["rt"], bot_l, abar_l)]
    uv_l = [jnp.concatenate([vbar, vb], axis=0) for vbar, vb in zip(vbar_l, st_ops["v"])]
    y0_l = [_dot(bot, uv) for bot, uv in zip(bot_l, uv_l)]
    p_l = [_dot_tn(ab, be).astype(BF16) for ab, be in zip(abar_l, st_ops["be"])]
    q_l = [_dot_tn(uv, jnp.concatenate([be, ke], axis=0)) for uv, be, ke in zip(uv_l, st_ops["be"], st_ops["ke"])]

    st_l = [state_ref[q] for q in range(n_pairs)]
    y_rows = []
    for m in range(nc):
        y_blocks = []
        for q in range(n_pairs):
            idx = m * n_pairs + q
            st = st_l[q]
            st_b = st.astype(BF16)
            y_s = _dot_nt(rhat_l[idx], st_b) + y0_l[idx]
            y_blocks.append(y_s[:c] + y_s[c:])
            decay = g_end[(m + 1) * c - 1:(m + 1) * c, q * LANES:(q + 1) * LANES]
            st_l[q] = st * decay + _dot(st_b, p_l[idx]) + q_l[idx]
        y_rows.append(jnp.concatenate(y_blocks, axis=-1))
    for q in range(n_pairs):
        state_ref[q] = st_l[q]
    y = jnp.concatenate(y_rows, axis=0)

    inv_n = 1.0 / n
    yc = y - head_sum(y) * inv_n
    var = head_sum(yc * yc) * inv_n
    yn = yc * lax.rsqrt(var + RWKV_GN_EPS) * lnw_ref[...] + lnb_ref[...]
    bonus = head_sum(r * k2 * rk_ref[...]) * v
    o_ref[...] = ((yn + bonus) * g).astype(BF16)


def _rwkv(p_rwkv, seq, mu, w0, w2, a0, a2, g2, k_k, k_a, r_k, ln_w, ln_b):
    t = p_rwkv.shape[0]
    ts = RWKV_TILE
    kern = functools.partial(_rwkv_kernel, tiles_per_seq=seq // ts)
    specs, params = _const_operands(mu, w0, w2, a0, a2, g2, k_k, k_a, r_k, ln_w, ln_b)
    return pl.pallas_call(
        kern,
        grid=(t // ts,),
        in_specs=[_row_spec(ts, RWKV_COLS)] + specs,
        out_specs=_row_spec(ts, RWKV_DIM),
        out_shape=jax.ShapeDtypeStruct((t, RWKV_DIM), BF16),
        scratch_shapes=[pltpu.VMEM((1, RWKV_COLS), F32), pltpu.VMEM((RWKV_HEADS // 2, LANES, LANES), F32)],
        compiler_params=_params(),
        name="rwkv7",
    )(p_rwkv, *params)


def _mla_prep_body(p, pos_ref, freq_ref, qn_ref, kvn_ref, wq_ref, wqr_ref, wk_ref, wv_ref, q_out, k_out, v_out):
    hq = _rms(p[:, 0:MLA_Q_LORA], qn_ref[...], NORM_EPS).astype(BF16)
    hkv = _rms(p[:, MLA_Q_LORA:MLA_Q_LORA + MLA_KV_LORA], kvn_ref[...], NORM_EPS).astype(BF16)
    c_rope = MLA_Q_LORA + MLA_KV_LORA
    blk = p[:, c_rope:c_rope + LANES]
    ang = pos_ref[...].astype(F32) * freq_ref[...]
    cos = jnp.cos(ang)
    sin = jnp.sin(ang)
    scale = (MLA_NOPE + MLA_ROPE) ** -0.5 * LOG2E
    qa = _dot(hq, wq_ref[...])
    qr = _dot(hq, wqr_ref[...])
    lane = lax.broadcasted_iota(jnp.int32, blk.shape, 1)
    half = MLA_ROPE // 2
    kr = pltpu.roll(blk, MLA_NOPE, 1)
    rot = jnp.where(lane < MLA_NOPE + half, -pltpu.roll(blk, MLA_NOPE - half, 1), pltpu.roll(blk, MLA_NOPE + half, 1))
    rot = jnp.where((lane >= MLA_NOPE) & (lane < MLA_NOPE + MLA_ROPE), rot, 0.0)
    krope = kr * cos + rot * sin
    kn = _dot(hkv, wk_ref[...])
    for h in range(MLA_HEADS):
        hs = slice(h * LANES, (h + 1) * LANES)
        q_out[:, hs] = ((qa[:, hs] * cos + qr[:, hs] * sin) * scale).astype(BF16)
        k_out[:, hs] = (kn[:, hs] + krope).astype(BF16)
    vt = _dot_nt(wv_ref[...], hkv)
    rowi = lax.broadcasted_iota(jnp.int32, vt.shape, 0)
    v_out[...] = (vt + (rowi % MLA_VT_ROWS == MLA_V).astype(F32)).astype(BF16)


def _softmax_stage_t(s_l, pv_fn, m_scr, acc_scr, idx, s_scr, lanes=None):
    n = len(s_l)

    def load(ref, c):
        return ref[c] if lanes is None else jnp.concatenate([ref[c, :, ls] for ls in lanes], axis=-1)

    def store(ref, c, val):
        if lanes is None:
            ref[c] = val
            return
        o = 0
        for ls in lanes:
            ref[c, :, ls] = val[:, o:o + ls.stop - ls.start]
            o += ls.stop - ls.start

    m_old = [load(m_scr, c)[0:1, :] for c in idx]
    m_new = [jnp.maximum(m, jnp.max(s, axis=0, keepdims=True)) for m, s in zip(m_old, s_l)]
    alpha = [jnp.exp2(mo - mn) for mo, mn in zip(m_old, m_new)]
    for c in range(n):
        s_scr[c] = s_l[c]
    pv_l = [pv_fn(c, jnp.exp2(s_scr[c] - m_new[c]).astype(BF16)) for c in range(n)]
    for c in range(n):
        store(m_scr, idx[c], jnp.broadcast_to(m_new[c], (m_scr.shape[1], m_new[c].shape[1])))
        store(acc_scr, idx[c], alpha[c] * load(acc_scr, idx[c]) + pv_l[c])


def _mla_attn_kernel(q_ref, k_ref, vt_ref, o_ref, m_scr, acc_scr, s_scr):
    tq = TQ
    i = pl.program_id(1)
    th = tq // 2
    m_scr[...] = jnp.full(m_scr.shape, NEG, F32)
    acc_scr[...] = jnp.zeros(acc_scr.shape, F32)
    hsl = [slice(h * LANES, (h + 1) * LANES) for h in range(MLA_HEADS)]
    vsl = [slice(h * MLA_VT_ROWS, (h + 1) * MLA_VT_ROWS) for h in range(MLA_HEADS)]
    heads = list(range(MLA_HEADS))

    def tile(off, tk, q_lo=0, masked=False):
        nq = tq - q_lo
        s_l = [_dot_nt(k_ref[pl.ds(off, tk), hs], q_ref[q_lo:, hs]) for hs in hsl]
        if masked:
            vis = lax.broadcasted_iota(jnp.int32, (tk, nq), 0) <= lax.broadcasted_iota(jnp.int32, (tk, nq), 1)
            s_l = [jnp.where(vis, s, NEG) for s in s_l]
        _softmax_stage_t(s_l, lambda c, p: _dot(vt_ref[vsl[c], pl.ds(off, tk)], p), m_scr, acc_scr, heads,
                         s_scr.at[:, 0:tk, 0:nq], None if q_lo == 0 else [slice(q_lo, tq)])

    def pair_body(j, carry):
        tile(pl.multiple_of(j * 2 * tq, 2 * tq), 2 * tq)
        return carry

    def single_body(j, carry):
        tile(pl.multiple_of(j * tq, tq), tq)
        return carry

    n_pairs = i // 2 if KEY_TILES_PER_STEP == 2 else 0
    if KEY_TILES_PER_STEP == 2:
        lax.fori_loop(0, n_pairs, pair_body, 0)
    lax.fori_loop(2 * n_pairs, i, single_body, 0)
    diag = pl.multiple_of(i * tq, tq)
    tile(diag, th, 0, True)
    tile(pl.multiple_of(diag + th, th), th, th, True)
    for pr in range(MLA_HEADS // 2):
        accs = [acc_scr[2 * pr + hh] for hh in range(2)]
        ot = jnp.concatenate([acc[:MLA_V, :] * (1.0 / acc[MLA_V:MLA_V + 1, :]) for acc in accs], axis=0)
        o_ref[:, pr * LANES:(pr + 1) * LANES] = jnp.transpose(ot).astype(BF16)


def _mla_attn(q, k, vt, batch, seq):
    t = q.shape[0]
    nq = seq // TQ
    hw = MLA_HEADS * LANES
    vw = MLA_HEADS * MLA_V
    return pl.pallas_call(
        _mla_attn_kernel,
        grid=(batch, nq),
        in_specs=[pl.BlockSpec((TQ, hw), lambda b, i: (b * nq + i, 0)),
                  pl.BlockSpec((seq, hw), lambda b, i: (b, 0)),
                  pl.BlockSpec((MLA_HEADS * MLA_VT_ROWS, seq), lambda b, i: (0, b))],
        out_specs=pl.BlockSpec((TQ, vw), lambda b, i: (b * nq + i, 0)),
        out_shape=jax.ShapeDtypeStruct((t, vw), BF16),
        scratch_shapes=[pltpu.VMEM((MLA_HEADS, 8, TQ), F32), pltpu.VMEM((MLA_HEADS, MLA_VT_ROWS, TQ), F32),
                        pltpu.VMEM((MLA_HEADS, KEY_TILES_PER_STEP * TQ, TQ), F32)],
        compiler_params=_params2(),
        name="mla_attn",
    )(q, k, vt)


def _t5_bucket_table():
    n = np.arange(0, REL_MAX_DISTANCE + 1)
    max_exact = REL_BUCKETS // 2
    nf = np.maximum(n, 1).astype(np.float32)
    ratio = np.log(nf / np.float32(max_exact)) / np.float32(math.log(REL_MAX_DISTANCE / max_exact))
    large = max_exact + (ratio * np.float32(REL_BUCKETS - max_exact)).astype(np.int32)
    large = np.minimum(large, REL_BUCKETS - 1)
    return np.where(n < max_exact, n, large)


_BUCKETS = _t5_bucket_table()
_FAR_BUCKET = int(_BUCKETS[REL_MAX_DISTANCE])
assert _FAR_BUCKET == REL_BUCKETS - 1 and np.all(np.diff(_BUCKETS) >= 0)
_BUCKET_STARTS = [int(np.argmax(_BUCKETS >= b)) for b in range(REL_BUCKETS // 2 + 1, REL_BUCKETS)]


def _diff_attn_kernel(rb_ref, lam_ref, subln_ref, q_ref, k_ref, vt_ref, o_ref, bias_scr, m_scr, acc_scr, s_scr, *,
                      lambda_init):
    tq = TQ
    i = pl.program_id(1)
    th = tq // 2
    r_i = lax.broadcasted_iota(jnp.int32, (tq, tq), 0)
    c_i = lax.broadcasted_iota(jnp.int32, (tq, tq), 1)

    @pl.when((pl.program_id(0) == 0) & (i == 0))
    def _():
        for t_idx, delta in enumerate((0, tq)):
            d = jnp.maximum(c_i - r_i + delta, 0)
            log_b = REL_BUCKETS // 2
            for start in _BUCKET_STARTS:
                log_b = log_b + (d >= start).astype(jnp.int32)
            bucket = jnp.where(d < REL_BUCKETS // 2, d, log_b)
            for h in range(DIFF_HEADS):
                bias = jnp.zeros((tq, tq), F32)
                for b in range(REL_BUCKETS):
                    bias = jnp.where(bucket == b, rb_ref[b, h], bias)
                bias_scr[t_idx, h] = (bias - rb_ref[_FAR_BUCKET, h]) * LOG2E

    m_scr[...] = jnp.full(m_scr.shape, NEG, F32)
    acc_scr[...] = jnp.zeros(acc_scr.shape, F32)
    lane = lax.broadcasted_iota(jnp.int32, (tq, LANES), 1)
    ones_rows = {tk: (lax.broadcasted_iota(jnp.int32, (DIFF_VT_ROWS - DIFF_V, tk), 0) == 0).astype(BF16)
                 for tk in (th, tq, 2 * tq)}
    hsl = [slice(h * LANES, (h + 1) * LANES) for h in range(DIFF_HEADS)]
    heads = list(range(DIFF_HEADS))
    q_l = []
    for hs in hsl:
        qp = q_ref[:, hs]
        q_l.append(jnp.concatenate([jnp.where(lane < DIFF_QK, qp, jnp.zeros_like(qp)),
                                    jnp.where(lane >= DIFF_QK, qp, jnp.zeros_like(qp))], axis=0))

    def tile(off, tk, bias_idx, q_lo=0, k_lo=0, masked=False):
        nq = tq - q_lo
        if q_lo == 0:
            qs_l = q_l
        else:
            qs_l = [jnp.concatenate([q[q_lo:tq], q[tq + q_lo:]], axis=0) for q in q_l]
        s_l = [_dot_nt(k_ref[pl.ds(off, tk), hsl[h]], qs_l[h]) for h in heads]
        if bias_idx is not None:
            s_l = [s + jnp.concatenate([bias_scr[bias_idx, h, k_lo:k_lo + tk, q_lo:tq]] * 2, axis=1)
                   for h, s in enumerate(s_l)]
        if masked:
            vis = lax.broadcasted_iota(jnp.int32, (tk, nq), 0) <= lax.broadcasted_iota(jnp.int32, (tk, nq), 1)
            vis2 = jnp.concatenate([vis, vis], axis=1)
            s_l = [jnp.where(vis2, s, NEG) for s in s_l]

        def pv(c, p):
            vt = jnp.concatenate([vt_ref[hsl[c], pl.ds(off, tk)], ones_rows[tk]], axis=0)
            return jnp.concatenate([_dot(vt, p[:, :nq]), _dot(vt, p[:, nq:])], axis=1)

        lanes = None if q_lo == 0 else [slice(q_lo, tq), slice(tq + q_lo, 2 * tq)]
        _softmax_stage_t(s_l, pv, m_scr, acc_scr, heads, s_scr.at[:, 0:tk, 0:2 * nq], lanes)

    def far_pair_body(j, carry):
        tile(pl.multiple_of(j * 2 * tq, 2 * tq), 2 * tq, None)
        return carry

    def far_body(j, carry):
        tile(pl.multiple_of(j * tq, tq), tq, None)
        return carry

    def near_body(j, carry):
        tile(pl.multiple_of(j * tq, tq), tq, 1)
        return carry

    n_far = jnp.maximum(i - 1, 0)
    n_pairs = n_far // 2 if KEY_TILES_PER_STEP == 2 else 0
    if KEY_TILES_PER_STEP == 2:
        lax.fori_loop(0, n_pairs, far_pair_body, 0)
    lax.fori_loop(2 * n_pairs, n_far, far_body, 0)
    lax.fori_loop(n_far, i, near_body, 0)
    diag = pl.multiple_of(i * tq, tq)
    tile(diag, th, 0, 0, 0, True)
    tile(pl.multiple_of(diag + th, th), th, 0, th, th, True)

    lam = lam_ref[...]
    lam_full = (jnp.exp(jnp.sum(lam[0:1] * lam[1:2], axis=-1, keepdims=True))
                - jnp.exp(jnp.sum(lam[2:3] * lam[3:4], axis=-1, keepdims=True)) + lambda_init)
    for h in heads:
        acc = acc_scr[h]
        on = acc[:DIFF_V, :] * (1.0 / acc[DIFF_V:DIFF_V + 1, :])
        ot = on[:, :tq] - lam_full * on[:, tq:]
        ot = ot * lax.rsqrt(jnp.mean(ot * ot, axis=0, keepdims=True) + SUBLN_EPS)
        o_ref[:, hsl[h]] = (jnp.transpose(ot) * subln_ref[...] * (1.0 - lambda_init)).astype(BF16)


def _diff_attn(p_qk, p_vt, rel_bias, lam, subln, batch, seq, layer_idx):
    t = p_qk.shape[0]
    nq = seq // TQ
    w = DIFF_HEADS * LANES
    lambda_init = 0.8 - 0.6 * math.exp(-0.3 * layer_idx)
    kern = functools.partial(_diff_attn_kernel, lambda_init=lambda_init)
    specs, params = _const_operands(lam, subln)
    return pl.pallas_call(
        kern,
        grid=(batch, nq),
        in_specs=[pl.BlockSpec(memory_space=pltpu.SMEM)] + specs + [
                  pl.BlockSpec((TQ, w), lambda b, i: (b * nq + i, 0)),
                  pl.BlockSpec((seq, w), lambda b, i: (b, 1)),
                  pl.BlockSpec((w, seq), lambda b, i: (0, b))],
        out_specs=pl.BlockSpec((TQ, w), lambda b, i: (b * nq + i, 0)),
        out_shape=jax.ShapeDtypeStruct((t, w), BF16),
        scratch_shapes=[pltpu.VMEM((2, DIFF_HEADS, TQ, TQ), F32), pltpu.VMEM((DIFF_HEADS, 8, 2 * TQ), F32),
                        pltpu.VMEM((DIFF_HEADS, DIFF_VT_ROWS, 2 * TQ), F32),
                        pltpu.VMEM((DIFF_HEADS, KEY_TILES_PER_STEP * TQ, 2 * TQ), F32)],
        compiler_params=_params2(),
        name="diff_attn",
    )(rel_bias, *params, p_qk, p_qk, p_vt)


def _merge_kernel(x_ref, pg_ref, bg_ref, or_ref, om_ref, od_ref, wr_ref, wm_ref, wd_ref, wo_ref, gf_ref, x_out, h_out):
    halves = [slice(s * (TM_MERGE // 2), (s + 1) * (TM_MERGE // 2)) for s in range(2)]
    branch = [[_dot(o_ref[rs, :], w_ref[...]) for o_ref, w_ref in ((or_ref, wr_ref), (om_ref, wm_ref), (od_ref, wd_ref))]
              for rs in halves]
    for rs, d in zip(halves, branch):
        merged = None
        for idx in range(3):
            cs = slice(idx * D_MODEL, (idx + 1) * D_MODEL)
            term = jax.nn.sigmoid(pg_ref[rs, cs].astype(F32) + bg_ref[:, cs]) * d[idx]
            merged = term if merged is None else merged + term
        x1 = x_ref[rs, :] + _dot(merged.astype(BF16), wo_ref[...])
        x_out[rs, :] = x1
        h_out[rs, :] = _rms(x1, gf_ref[...], NORM_EPS).astype(BF16)


def _merge(x, p_gate, b_gate, o_r, o_m, o_d, w_r, w_m, w_d, w_o, g_ffn):
    t = x.shape[0]
    tm = TM_MERGE
    (bg_spec,), (bg,) = _const_operands(b_gate)
    specs, params = _const_operands(w_r, w_m, w_d, w_o, g_ffn)
    return pl.pallas_call(
        _merge_kernel,
        grid=(t // tm,),
        in_specs=[_row_spec(tm, D_MODEL), _row_spec(tm, GATE_COLS), bg_spec,
                  _row_spec(tm, RWKV_DIM), _row_spec(tm, MLA_HEADS * MLA_V), _row_spec(tm, DIFF_HEADS * DIFF_V)] + specs,
        out_specs=[_row_spec(tm, D_MODEL), _row_spec(tm, D_MODEL)],
        out_shape=[jax.ShapeDtypeStruct((t, D_MODEL), F32), jax.ShapeDtypeStruct((t, D_MODEL), BF16)],
        compiler_params=_params(),
        name="merge",
    )(x, p_gate, bg, o_r, o_m, o_d, *params)


def _ffn_kernel(x_ref, h_ref, wup_ref, cw_ref, cb_ref, wdn_ref, gfin_ref, o_ref, carry_ref, *, tiles_per_seq, final_norm):
    tm = TM_FFN

    @pl.when(pl.program_id(0) % tiles_per_seq == 0)
    def _():
        carry_ref[0:8, :] = jnp.zeros((8, carry_ref.shape[1]), F32)

    h = h_ref[...]

    def conv(u, cols):
        carry_ref[8:, cols] = u
        u1 = carry_ref[7:7 + tm, cols]
        u2 = carry_ref[6:6 + tm, cols]
        carry_ref[0:8, cols] = u[tm - 8:tm, :]
        return cw_ref[0:1, cols] * u2 + cw_ref[1:2, cols] * u1 + cw_ref[2:3, cols] * u + cb_ref[:, cols]

    bounds = list(range(0, D_FF, FF_CHUNK)) + [D_FF]
    chunks = [slice(a, b) for a, b in zip(bounds[:-1], bounds[1:])]
    shift = lambda c: slice(D_FF + c.start, D_FF + c.stop)

    def up(gc):
        return _dot(h, wup_ref[:, gc]), _dot(h, wup_ref[:, shift(gc)])

    nxt = up(chunks[0])
    acts = []
    for ck, gc in enumerate(chunks):
        ug, uv = nxt
        if ck + 1 < len(chunks):
            nxt = up(chunks[ck + 1])
        gate = conv(ug, gc)
        val = conv(uv, shift(gc))
        acts.append((gate * jax.nn.sigmoid(gate) * val).astype(BF16))
    out = x_ref[...] + _dot(jnp.concatenate(acts, axis=-1), wdn_ref[...])
    if final_norm:
        out = _rms(out, gfin_ref[...], NORM_EPS)
    o_ref[...] = out


def _ffn(x1, h2, w_up, conv_w, conv_b, w_down, g_final, seq, final_norm):
    t = x1.shape[0]
    tm = TM_FFN
    kern = functools.partial(_ffn_kernel, tiles_per_seq=seq // tm, final_norm=final_norm)
    specs, params = _const_operands(w_up, conv_w, conv_b, w_down, g_final)
    return pl.pallas_call(
        kern,
        grid=(t // tm,),
        in_specs=[_row_spec(tm, D_MODEL), _row_spec(tm, D_MODEL)] + specs,
        out_specs=_row_spec(tm, D_MODEL),
        out_shape=jax.ShapeDtypeStruct((t, D_MODEL), F32),
        scratch_shapes=[pltpu.VMEM((8 + tm, 2 * D_FF), F32)],
        compiler_params=_params(),
        name="conv_ffn",
    )(x1, h2, *params)


def _mla_weights(w_uq, w_ukv):
    qd = MLA_NOPE + MLA_ROPE
    half = MLA_ROPE // 2
    wq = w_uq.reshape(MLA_Q_LORA, MLA_HEADS, qd)
    zq = jnp.zeros((MLA_Q_LORA, MLA_HEADS, LANES - qd), F32)
    wq_main = jnp.concatenate([wq, zq], axis=-1)
    x1 = wq[:, :, MLA_NOPE:MLA_NOPE + half]
    x2 = wq[:, :, MLA_NOPE + half:]
    wq_rot = jnp.concatenate([jnp.zeros((MLA_Q_LORA, MLA_HEADS, MLA_NOPE), F32), -x2, x1, zq], axis=-1)
    wkv = w_ukv.reshape(MLA_KV_LORA, MLA_HEADS, MLA_NOPE + MLA_V)
    zkv = jnp.zeros((MLA_KV_LORA, MLA_HEADS, LANES - MLA_NOPE), F32)
    wk = jnp.concatenate([wkv[:, :, :MLA_NOPE], zkv], axis=-1)
    wv = jnp.concatenate([wkv[:, :, MLA_NOPE:], jnp.zeros((MLA_KV_LORA, MLA_HEADS, MLA_VT_ROWS - MLA_V), F32)], axis=-1)
    flat = lambda w: w.reshape(w.shape[0], -1).astype(BF16)
    return flat(wq_main), flat(wq_rot), flat(wk), flat(wv).T


def kernel(x, positions, rel_bias, norm_mix, w_in, b_gate, rwkv_mu, rwkv_w0, rwkv_w2, rwkv_a0, rwkv_a2, rwkv_g2, rwkv_k_k, rwkv_k_a, rwkv_r_k, rwkv_ln_w, rwkv_ln_b, mla_q_norm, mla_w_uq, mla_kv_norm, mla_w_ukv, diff_lambda, diff_subln, w_branch_rwkv, w_branch_mla, w_branch_diff, w_o, norm_ffn, ffn_w_up, ffn_conv_w, ffn_conv_b, ffn_w_down, norm_final):
    batch, seq, _ = x.shape
    depth = w_in.shape[0]
    t = batch * seq
    assert seq % TQ == 0 and seq % TM == 0 and seq % TM_FFN == 0 and seq % RWKV_TILE == 0 and RWKV_TILE % CHUNK == 0
    assert (batch * seq) % TM_MERGE == 0
    xf = x.reshape(t, D_MODEL)
    pos_col = positions.reshape(t, 1)
    inv_freq = ROPE_BASE ** (-jnp.arange(0, MLA_ROPE, 2, dtype=F32) / MLA_ROPE)
    freq = jnp.concatenate([jnp.zeros((MLA_NOPE,), F32), inv_freq, inv_freq,
                            jnp.zeros((LANES - MLA_NOPE - MLA_ROPE,), F32)]).reshape(1, LANES)
    vec = lambda v: v.reshape(v.shape[0], 1, -1)
    bf = lambda w: w.astype(BF16)
    n_q = DIFF_QK_COLS // 2
    diff_scale = jnp.concatenate([jnp.full((n_q,), DIFF_QK ** -0.5 * LOG2E, F32), jnp.ones((n_q,), F32)])
    s0, s1, s2 = RWKV_COLS, RWKV_COLS + MLA_COLS, RWKV_COLS + MLA_COLS + DIFF_COLS
    sv = s1 + DIFF_QK_COLS
    st = {
        "norm_mix": vec(norm_mix),
        "w_r": bf(w_in[:, :, :s0]),
        "w_m": bf(jnp.pad(w_in[:, :, s0:s1], ((0, 0), (0, 0), (0, MLA_PAD - MLA_COLS)))),
        "w_qk": bf(w_in[:, :, s1:sv] * diff_scale),
        "w_vt": bf(jnp.swapaxes(w_in[:, :, sv:s2], 1, 2)),
        "w_g": bf(w_in[:, :, s2:]),
        "q_norm": vec(mla_q_norm), "kv_norm": vec(mla_kv_norm),
        "mu": vec(rwkv_mu), "w0": vec(rwkv_w0), "w2": bf(rwkv_w2), "a0": vec(rwkv_a0), "a2": bf(rwkv_a2),
        "g2": bf(rwkv_g2), "k_k": vec(rwkv_k_k), "k_a": vec(rwkv_k_a), "r_k": vec(rwkv_r_k),
        "ln_w": vec(rwkv_ln_w), "ln_b": vec(rwkv_ln_b),
        "lam": diff_lambda, "subln": vec(diff_subln),
        "b_gate": vec(b_gate), "wb_r": bf(w_branch_rwkv), "wb_m": bf(w_branch_mla), "wb_d": bf(w_branch_diff),
        "w_o": bf(w_o), "norm_ffn": vec(norm_ffn),
        "w_up": bf(ffn_w_up), "conv_w": ffn_conv_w, "conv_b": vec(ffn_conv_b), "w_down": bf(ffn_w_down),
    }
    g_final = norm_final.reshape(1, -1)

    for l in range(depth):
        p = {name: _Layer(a, l) for name, a in st.items()}
        wq, wqr, wk, wv = _mla_weights(mla_w_uq[l], mla_w_ukv[l])
        p_rwkv, q_m, k_m, v_m, p_qk, p_vt, p_gate = _in_proj(
            xf, p["norm_mix"], p["w_r"], p["w_m"], p["w_qk"], p["w_g"], p["w_vt"], pos_col, freq, p["q_norm"],
            p["kv_norm"], wq, wqr, wk, wv)

        o_r = _rwkv(p_rwkv, seq, p["mu"], p["w0"], p["w2"], p["a0"], p["a2"], p["g2"], p["k_k"], p["k_a"], p["r_k"],
                    p["ln_w"], p["ln_b"])

        o_m = _mla_attn(q_m, k_m, v_m, batch, seq)

        o_d = _diff_attn(p_qk, p_vt, rel_bias, p["lam"], p["subln"], batch, seq, l)

        x1, h2 = _merge(xf, p_gate, p["b_gate"], o_r, o_m, o_d, p["wb_r"], p["wb_m"], p["wb_d"], p["w_o"], p["norm_ffn"])
        xf = _ffn(x1, h2, p["w_up"], p["conv_w"], p["conv_b"], p["w_down"], g_final, seq, final_norm=(l == depth - 1))
    return xf.reshape(batch, seq, D_MODEL)
```
